```python
import math
import jax
import jax.numpy as jnp
from jax import lax
import numpy as np

D_MODEL = 1024
BATCH = 8
SEQ = 2048
DEPTH = 2

CTX_LEN = 256
GRID_W = 64
N_MIXERS = 4
GROUP = D_MODEL // N_MIXERS
HEAD_DIM = 64
GROUP_HEADS = GROUP // HEAD_DIM
D_FF = ((8 * D_MODEL // 3 + 255) // 256) * 256
N_MOD = 9
NORM_EPS = 1e-6

HY_ORDER = 2
HY_SHORT_W = 3
HY_BANDS = 16
HY_EMB = 1 + 2 * HY_BANDS
HY_FILTER_HIDDEN = 64
HY_TARGET = 1e-2
HY_SHORT_DECAY_PCT = 0.3
HY_LONG_DECAY_PCT = 1.5

NA_WIN_ROWS = 8
NA_WIN_COLS = 16

DN_SHORT_W = 3
DN_CHUNK = 64

RW_DECAY_RANK = 32
RW_AAA_RANK = 32
RW_GATE_RANK = 64
RW_LN_EPS = 64e-5

HY_COLS = 3 * GROUP
NA_COLS = 3 * GROUP
DN_COLS = 4 * GROUP + 4 * GROUP_HEADS
RW_COLS = 3 * GROUP + RW_DECAY_RANK + RW_AAA_RANK + RW_GATE_RANK
P_TOTAL = HY_COLS + NA_COLS + DN_COLS + RW_COLS

kernel_name = 'hybrid_hyena_natten_deltanet_rwkv7_dit_block'


def rms_norm(x, w, eps=NORM_EPS):
    xf = x.astype(jnp.float32)
    y = xf * lax.rsqrt(jnp.mean(xf * xf, axis=-1, keepdims=True) + eps)
    return (y * w.astype(jnp.float32)).astype(x.dtype)


def l2_normalize(x, eps=1e-6):
    xf = x.astype(jnp.float32)
    return xf * lax.rsqrt(jnp.sum(xf * xf, axis=-1, keepdims=True) + eps)


def _rev(t):
    return jnp.flip(t, axis=1)


def _same(t):
    return t


def centred_dwconv(u, w):
    k_w = w.shape[0]
    pad = k_w // 2
    n = u.shape[1]
    up = jnp.pad(u, ((0, 0), (pad, pad), (0, 0)))
    out = up[:, 0:n] * w[0]
    for i in range(1, k_w):
        out = out + up[:, i:i + n] * w[i]
    return out


def token_shift_centred(u, mu):
    prev = jnp.pad(u, ((0, 0), (1, 0), (0, 0)))[:, :-1]
    nxt = jnp.pad(u, ((0, 0), (0, 1), (0, 0)))[:, 1:]
    return u + mu[0] * (prev - u) + mu[1] * (nxt - u)


def swiglu(h, w_gu, w_down):
    gate, up = jnp.split(h @ w_gu, 2, axis=-1)
    return (jax.nn.silu(gate) * up) @ w_down


def modulation(cond, w_mod, b_mod):
    m = jax.nn.silu(cond) @ w_mod + b_mod
    return m.reshape(cond.shape[:-1] + (N_MOD, D_MODEL))


def adaln(x, norm_w, mod, i):
    return rms_norm(x, norm_w) * (1.0 + mod[:, 3 * i + 1, None]) + mod[:, 3 * i, None]


def gate_of(mod, i):
    return mod[:, 3 * i + 2, None]


def hyena_filters(n, f_w1, f_b1, f_w2, f_b2, f_w3, f_freq):
    f32 = jnp.float32
    t = jnp.linspace(0.0, 1.0, n, dtype=f32)[:, None]
    ang = 2.0 * math.pi * jnp.arange(n, dtype=f32)[:, None] / n
    bands = jnp.linspace(1e-4, HY_BANDS - 1, HY_BANDS, dtype=f32)[None]
    z = jnp.concatenate([t, jnp.cos(bands * ang), -jnp.sin(bands * ang)], axis=-1)
    freq = f_freq.astype(f32)
    hid = jnp.sin(freq * (z @ f_w1.astype(f32) + f_b1.astype(f32)))
    hid = jnp.sin(freq * (hid @ f_w2.astype(f32) + f_b2.astype(f32)))
    h = (hid @ f_w3.astype(f32)).reshape(n, 2, HY_ORDER, GROUP)
    max_decay = math.log(HY_TARGET) / HY_SHORT_DECAY_PCT
    min_decay = math.log(HY_TARGET) / HY_LONG_DECAY_PCT
    deltas = jnp.linspace(min_decay, max_decay, HY_ORDER * GROUP, dtype=f32).reshape(HY_ORDER, GROUP)
    h = h * jnp.exp(-t[:, :, None, None] * jnp.abs(deltas))
    k = jnp.concatenate([h[:, 0], jnp.zeros((1, HY_ORDER, GROUP), f32), h[:0:-1, 1]], axis=0)
    return k / jnp.sum(jnp.abs(k), axis=0, keepdims=True)


def fft_longconv(u, k, bias):
    n = u.shape[1]
    uf = jnp.fft.rfft(u.astype(jnp.float32), n=2 * n, axis=1)
    kf = jnp.fft.rfft(k, n=2 * n, axis=0)
    y = jnp.fft.irfft(uf * kf[None], n=2 * n, axis=1)[:, :n]
    return (y + u.astype(jnp.float32) * bias.astype(jnp.float32)).astype(u.dtype)


def hyena_mixer(slab, conv_w, f_w1, f_b1, f_w2, f_b2, f_w3, f_freq, bias):
    u = centred_dwconv(slab, conv_w)
    v, x1, x2 = jnp.split(u, 3, axis=-1)
    k = hyena_filters(slab.shape[1], f_w1, f_b1, f_w2, f_b2, f_w3, f_freq)
    z = x1 * fft_longconv(v, k[:, 0], bias[0])
    return x2 * fft_longconv(z, k[:, 1], bias[1])


def na_qkv(slab, q_norm, k_norm):
    b, n, _ = slab.shape
    q, k, v = (t.reshape(b, n, GROUP_HEADS, HEAD_DIM) for t in jnp.split(slab, 3, axis=-1))
    return rms_norm(q, q_norm), rms_norm(k, k_norm), v


def na_latent(q, k, v, kc, vc, rpb):
    b, n, h, d = q.shape
    rows = n // GRID_W
    wr = min(NA_WIN_ROWS, rows)
    scale = d ** -0.5
    qg = q.reshape(b, rows, GRID_W, h, d)
    kg = k.reshape(b, rows, GRID_W, h, d)
    vg = v.reshape(b, rows, GRID_W, h, d)
    r = jnp.arange(rows)
    row_idx = jnp.clip(r - wr // 2, 0, rows - wr)[:, None] + jnp.arange(wr)[None]
    k_blk = kg[:, row_idx]
    v_blk = vg[:, row_idx]
    col = jnp.arange(GRID_W)
    col_start = jnp.clip(col - NA_WIN_COLS // 2, 0, GRID_W - NA_WIN_COLS)
    in_win = (col[None, :] >= col_start[:, None]) & (col[None, :] < col_start[:, None] + NA_WIN_COLS)
    d_row = row_idx - r[:, None]
    d_col = jnp.clip(col[None, :] - col[:, None], 1 - NA_WIN_COLS, NA_WIN_COLS - 1)
    bias = rpb[:, (d_row + NA_WIN_ROWS - 1)[:, None, :, None], (d_col + NA_WIN_COLS - 1)[None, :, None, :]]
    s_loc = jnp.einsum('brchd,brwjhd->bhrcwj', qg, k_blk).astype(jnp.float32) * scale
    s_loc = jnp.where(in_win[:, None, :], s_loc + bias[None].astype(jnp.float32), -jnp.inf)
    s_ctx = jnp.einsum('brchd,bmhd->bhrcm', qg, kc).astype(jnp.float32) * scale
    n_loc = wr * GRID_W
    p = jax.nn.softmax(jnp.concatenate([s_loc.reshape(b, h, rows, GRID_W, n_loc), s_ctx], axis=-1), axis=-1)
    p = p.astype(v.dtype)
    p_loc = p[..., :n_loc].reshape(b, h, rows, GRID_W, wr, GRID_W)
    o = jnp.einsum('bhrcwj,brwjhd->brchd', p_loc, v_blk) + jnp.einsum('bhrcm,bmhd->brchd', p[..., n_loc:], vc)
    return o.reshape(b, n, h * d)


def na_context(qc, kc, vc):
    b, m, h, d = qc.shape
    s = jnp.einsum('bqhd,bkhd->bhqk', qc, kc).astype(jnp.float32) * d ** -0.5
    p = jax.nn.softmax(s, axis=-1).astype(vc.dtype)
    return jnp.einsum('bhqk,bkhd->bqhd', p, vc).reshape(b, m, h * d)


def gated_delta_chunked(q, k, v, beta, g, s0):
    f32 = jnp.float32
    q, k, v, beta, g = (t.astype(f32) for t in (q, k, v, beta, g))
    b, n, h, dk = q.shape
    dv = v.shape[-1]
    n_chunks = n // DN_CHUNK

    def blocks(t):
        t = t.reshape((b, n_chunks, DN_CHUNK) + t.shape[2:])
        return jnp.moveaxis(t, 3, 1)

    q = blocks(q) * dk ** -0.5
    k, v, beta, g = blocks(k), blocks(v), blocks(beta), blocks(g)
    gcum = jnp.cumsum(g, axis=-1)
    lower = jnp.tril(jnp.ones((DN_CHUNK, DN_CHUNK), bool))
    strict = jnp.tril(jnp.ones((DN_CHUNK, DN_CHUNK), bool), -1)
    decay = jnp.exp(jnp.where(lower, gcum[..., :, None] - gcum[..., None, :], -jnp.inf))
    kb = k * beta[..., None]
    m = jnp.where(strict, jnp.einsum('bhnik,bhnjk->bhnij', kb, k) * decay, 0.0)
    rhs = jnp.concatenate([v * beta[..., None], kb * jnp.exp(gcum)[..., None]], axis=-1)
    sol = lax.linalg.triangular_solve(m, rhs, left_side=True, lower=True, unit_diagonal=True)
    u, w = sol[..., :dv], sol[..., dv:]
    attn = jnp.einsum('bhnik,bhnjk->bhnij', q, k) * decay
    q_dec = q * jnp.exp(gcum)[..., None]
    k_dec = k * jnp.exp(gcum[..., -1:] - gcum)[..., None]
    g_last = jnp.exp(gcum[..., -1])

    def step(s, xs):
        u_i, w_i, attn_i, q_i, k_i, gl_i = xs
        v_new = u_i - jnp.einsum('bhck,bhkv->bhcv', w_i, s)
        o_i = jnp.einsum('bhck,bhkv->bhcv', q_i, s) + jnp.einsum('bhij,bhjv->bhiv', attn_i, v_new)
        s = s * gl_i[..., None, None] + jnp.einsum('bhck,bhcv->bhkv', k_i, v_new)
        return s, o_i

    xs = tuple(jnp.moveaxis(t, 2, 0) for t in (u, w, attn, q_dec, k_dec, g_last))
    s_final, o = lax.scan(step, s0.astype(f32), xs)
    o = jnp.moveaxis(o, 0, 2)
    return jnp.moveaxis(o, 1, 3).reshape(b, n, h, dv), s_final


def deltanet_inputs(slab, conv_w, a_log, dt_bias):
    b, n, _ = slab.shape
    qkv = jax.nn.silu(centred_dwconv(slab[..., :3 * GROUP], conv_w))
    q, k, v = (t.reshape(b, n, GROUP_HEADS, HEAD_DIM) for t in jnp.split(qkv, 3, axis=-1))
    z = slab[..., 3 * GROUP:4 * GROUP]
    ba = slab[..., 4 * GROUP:].astype(jnp.float32).reshape(b, n, 2, 2, GROUP_HEADS)
    beta = jax.nn.sigmoid(ba[:, :, :, 0])
    g = -jnp.exp(a_log.astype(jnp.float32)) * jax.nn.softplus(ba[:, :, :, 1] + dt_bias.astype(jnp.float32))
    return l2_normalize(q), l2_normalize(k), v, z, beta, g


def deltanet_gated_out(o, z, norm_w):
    b, n = z.shape[:2]
    zh = z.reshape(b, n, GROUP_HEADS, HEAD_DIM).astype(jnp.float32)
    return (rms_norm(o, norm_w) * jax.nn.silu(zh)).reshape(b, n, GROUP)


def deltanet_mixer(slab, slab_c, conv_w, a_log, dt_bias, norm_w, need_ctx):
    q, k, v, z, beta, g = deltanet_inputs(slab, conv_w, a_log, dt_bias)
    qc, kc, vc, zc, betac, gc = deltanet_inputs(slab_c, conv_w, a_log, dt_bias)
    b = slab.shape[0]
    o_lat, o_ctx = [], []
    for d in range(2):
        orient = _rev if d == 1 else _same
        s0 = jnp.zeros((b, GROUP_HEADS, HEAD_DIM, HEAD_DIM), jnp.float32)
        oc_d, s_ctx = gated_delta_chunked(*(orient(t) for t in (qc, kc, vc, betac[:, :, d], gc[:, :, d])), s0)
        ol_d, _ = gated_delta_chunked(*(orient(t) for t in (q, k, v, beta[:, :, d], g[:, :, d])), s_ctx)
        o_lat.append(orient(ol_d))
        o_ctx.append(orient(oc_d))
    y = deltanet_gated_out(o_lat[0] + o_lat[1], z, norm_w)
    yc = deltanet_gated_out(o_ctx[0] + o_ctx[1], zc, norm_w) if need_ctx else None
    return y, yc


def rwkv7_scan(r, w, k, v, a, b, s0):
    def step(s, xs):
        r_t, w_t, k_t, v_t, a_t, b_t = xs
        s = (s * w_t[:, :, None, :]
             + jnp.einsum('bhvk,bhk->bhv', s, a_t)[..., None] * b_t[:, :, None, :]
             + v_t[..., None] * k_t[:, :, None, :])
        return s, jnp.einsum('bhvk,bhk->bhv', s, r_t)
    xs = tuple(jnp.moveaxis(t.astype(jnp.float32), 1, 0) for t in (r, w, k, v, a, b))
    s_final, y = lax.scan(step, s0, xs)
    return jnp.moveaxis(y, 0, 1), s_final


def rwkv_inputs(slab, mu, w0, w_up, a0, a_up, g_up, k_k, k_a):
    b, n, _ = slab.shape
    s = token_shift_centred(slab, mu).astype(jnp.float32)
    o1 = 3 * GROUP
    r, k, v, wd, ad, gd = jnp.split(
        s, [GROUP, 2 * GROUP, o1, o1 + RW_DECAY_RANK, o1 + RW_DECAY_RANK + RW_AAA_RANK], axis=-1)

    def heads(t):
        return t.reshape(b, n, GROUP_HEADS, HEAD_DIM)

    kk = l2_normalize(heads(k * k_k))
    dirs = []
    for d in range(2):
        w_log = -jax.nn.softplus(-(w0[d] + jnp.tanh(wd) @ w_up[d])) - 0.5
        a = jax.nn.sigmoid(a0[d] + ad @ a_up[d])
        dirs.append((heads(jnp.exp(-jnp.exp(w_log))), heads(k * (1.0 + (a - 1.0) * k_a)), heads(a)))
    g = jax.nn.sigmoid(gd) @ g_up
    return heads(r), heads(v), kk, g, dirs


def rwkv_output(y, r, v, k_dirs, g, r_k, ln_w, ln_b):
    b, n = y.shape[:2]
    mean = jnp.mean(y, axis=-1, keepdims=True)
    var = jnp.mean(jnp.square(y - mean), axis=-1, keepdims=True)
    yn = ((y - mean) * lax.rsqrt(var + RW_LN_EPS)).reshape(b, n, GROUP) * ln_w + ln_b
    bonus = (jnp.sum(r * (k_dirs[0] + k_dirs[1]) * r_k, axis=-1, keepdims=True) * v).reshape(b, n, GROUP)
    return (yn + bonus) * g


def rwkv_mixer(slab, slab_c, rw, need_ctx):
    mu, w0, w_up, a0, a_up, g_up, k_k, k_a, r_k, ln_w, ln_b = rw
    r, v, kk, g, dirs = rwkv_inputs(slab, mu, w0, w_up, a0, a_up, g_up, k_k, k_a)
    rc, vc, kkc, gc, dirs_c = rwkv_inputs(slab_c, mu, w0, w_up, a0, a_up, g_up, k_k, k_a)
    b = slab.shape[0]
    y_lat, y_ctx = [], []
    for d in range(2):
        orient = _rev if d == 1 else _same
        (w_l, k_l, a_l), (w_c, k_c, a_c) = dirs[d], dirs_c[d]
        s0 = jnp.zeros((b, GROUP_HEADS, HEAD_DIM, HEAD_DIM), jnp.float32)
        yc_d, s_ctx = rwkv7_scan(*(orient(t) for t in (rc, w_c, k_c, vc, -kkc, kkc * a_c)), s0)
        yl_d, _ = rwkv7_scan(*(orient(t) for t in (r, w_l, k_l, v, -kk, kk * a_l)), s_ctx)
        y_lat.append(orient(yl_d))
        y_ctx.append(orient(yc_d))
    y = rwkv_output(y_lat[0] + y_lat[1], r, v, [dirs[0][1], dirs[1][1]], g, r_k, ln_w, ln_b)
    if not need_ctx:
        return y, None
    yc = rwkv_output(y_ctx[0] + y_ctx[1], rc, vc, [dirs_c[0][1], dirs_c[1][1]], gc, r_k, ln_w, ln_b)
    return y, yc


def hybrid_mixer(h, hc, w_in, w_out, hy, na, dn, rw, need_ctx):
    cuts = [HY_COLS, HY_COLS + NA_COLS, HY_COLS + NA_COLS + DN_COLS]
    hy_s, na_s, dn_s, rw_s = jnp.split(h @ w_in, cuts, axis=-1)
    hy_c, na_c, dn_c, rw_c = jnp.split(hc @ w_in, cuts, axis=-1)
    q, k, v = na_qkv(na_s, na[0], na[1])
    qc, kc, vc = na_qkv(na_c, na[0], na[1])
    y_dn, yc_dn = deltanet_mixer(dn_s, dn_c, dn[0], dn[1], dn[2], dn[3], need_ctx)
    y_rw, yc_rw = rwkv_mixer(rw_s, rw_c, rw, need_ctx)
    groups = [hyena_mixer(hy_s, *hy), na_latent(q, k, v, kc, vc, na[2]), y_dn, y_rw]
    y = jnp.concatenate([t.astype(h.dtype) for t in groups], axis=-1) @ w_out
    if not need_ctx:
        return y, None
    groups_c = [hyena_mixer(hy_c, *hy), na_context(qc, kc, vc), yc_dn, yc_rw]
    yc = jnp.concatenate([t.astype(hc.dtype) for t in groups_c], axis=-1) @ w_out
    return y, yc


def setup_inputs(seed: int = 0) -> dict:
    key = jax.random.key(seed)
    keys = iter(jax.random.split(key, 48))
    f32 = jnp.float32

    def nrm(shape, scale):
        return scale * jax.random.normal(next(keys), shape, f32)

    def unif(shape, lo, hi):
        return jax.random.uniform(next(keys), shape, f32, lo, hi)

    nl, d, g, h, hd = DEPTH, D_MODEL, GROUP, GROUP_HEADS, HEAD_DIM
    dt = jnp.exp(unif((nl, 2, h), math.log(1e-3), math.log(1e-1)))
    return {
        'x': nrm((BATCH, SEQ, d), 1.0),
        'c': nrm((BATCH, d), 1.0),
        'ctx': nrm((BATCH, CTX_LEN, d), 1.0),
        'c_ctx': nrm((d,), 1.0),
        'w_mod': nrm((nl, d, N_MOD * d), 0.5 * d ** -0.5),
        'b_mod': nrm((nl, N_MOD * d), 0.02),
        'norm_w': 1.0 + nrm((nl, 3, d), 0.02),
        'ffn_w_gu': nrm((nl, 2, d, 2 * D_FF), d ** -0.5),
        'ffn_w_down': nrm((nl, 2, D_FF, d), D_FF ** -0.5),
        'w_in': nrm((nl, d, P_TOTAL), d ** -0.5),
        'w_out': nrm((nl, d, d), d ** -0.5),
        'hy_conv': nrm((nl, HY_SHORT_W, 3 * g), HY_SHORT_W ** -0.5),
        'hy_f_w1': nrm((nl, HY_EMB, HY_FILTER_HIDDEN), HY_EMB ** -0.5),
        'hy_f_b1': nrm((nl, HY_FILTER_HIDDEN), 0.02),
        'hy_f_w2': nrm((nl, HY_FILTER_HIDDEN, HY_FILTER_HIDDEN), HY_FILTER_HIDDEN ** -0.5),
        'hy_f_b2': nrm((nl, HY_FILTER_HIDDEN), 0.02),
        'hy_f_w3': nrm((nl, HY_FILTER_HIDDEN, 2 * HY_ORDER * g), HY_FILTER_HIDDEN ** -0.5),
        'hy_f_freq': 1.0 + nrm((nl, HY_FILTER_HIDDEN), 0.02),
        'hy_bias': nrm((nl, HY_ORDER, g), 0.1),
        'na_q_norm': 1.0 + nrm((nl, hd), 0.02),
        'na_k_norm': 1.0 + nrm((nl, hd), 0.02),
        'na_rpb': nrm((nl, h, 2 * NA_WIN_ROWS - 1, 2 * NA_WIN_COLS - 1), 0.1),
        'dn_conv': nrm((nl, DN_SHORT_W, 3 * g), DN_SHORT_W ** -0.5),
        'dn_a_log': jnp.log(unif((nl, 2, h), 1.0, 16.0)),
        'dn_dt_bias': dt + jnp.log(-jnp.expm1(-dt)),
        'dn_norm': 1.0 + nrm((nl, hd), 0.02),
        'rw_mu': unif((nl, 2, RW_COLS), 0.0, 0.5),
        'rw_w0': unif((nl, 2, g), -6.5, -1.5),
        'rw_w_up': nrm((nl, 2, RW_DECAY_RANK, g), 0.5 * RW_DECAY_RANK ** -0.5),
        'rw_a0': nrm((nl, 2, g), 0.1),
        'rw_a_up': nrm((nl, 2, RW_AAA_RANK, g), 0.5 * RW_AAA_RANK ** -0.5),
        'rw_g_up': nrm((nl, RW_GATE_RANK, g), RW_GATE_RANK ** -0.5),
        'rw_k_k': 0.85 + nrm((nl, g), 0.02),
        'rw_k_a': 1.0 + nrm((nl, g), 0.02),
        'rw_r_k': nrm((nl, h, hd), 0.1),
        'rw_ln_w': 1.0 + nrm((nl, g), 0.02),
        'rw_ln_b': nrm((nl, g), 0.02),
    }


def reference(x, c, ctx, c_ctx, w_mod, b_mod, norm_w, ffn_w_gu, ffn_w_down, w_in, w_out,
              hy_conv, hy_f_w1, hy_f_b1, hy_f_w2, hy_f_b2, hy_f_w3, hy_f_freq, hy_bias,
              na_q_norm, na_k_norm, na_rpb, dn_conv, dn_a_log, dn_dt_bias, dn_norm,
              rw_mu, rw_w0, rw_w_up, rw_a0, rw_a_up, rw_g_up, rw_k_k, rw_k_a, rw_r_k, rw_ln_w, rw_ln_b):
    xc = ctx
    for l in range(DEPTH):
        need_ctx = l < DEPTH - 1
        mod = modulation(c, w_mod[l], b_mod[l])
        mod_c = modulation(c_ctx, w_mod[l], b_mod[l])[None]
        x = x + 0.5 * gate_of(mod, 0) * swiglu(adaln(x, norm_w[l, 0], mod, 0), ffn_w_gu[l, 0], ffn_w_down[l, 0])
        xc = xc + 0.5 * gate_of(mod_c, 0) * swiglu(adaln(xc, norm_w[l, 0], mod_c, 0), ffn_w_gu[l, 0], ffn_w_down[l, 0])
        hy = (hy_conv[l], hy_f_w1[l], hy_f_b1[l], hy_f_w2[l], hy_f_b2[l], hy_f_w3[l], hy_f_freq[l], hy_bias[l])
        na = (na_q_norm[l], na_k_norm[l], na_rpb[l])
        dn = (dn_conv[l], dn_a_log[l], dn_dt_bias[l], dn_norm[l])
        rw = (rw_mu[l], rw_w0[l], rw_w_up[l], rw_a0[l], rw_a_up[l], rw_g_up[l], rw_k_k[l], rw_k_a[l],
              rw_r_k[l], rw_ln_w[l], rw_ln_b[l])
        y, yc = hybrid_mixer(adaln(x, norm_w[l, 1], mod, 1), adaln(xc, norm_w[l, 1], mod_c, 1),
                             w_in[l], w_out[l], hy, na, dn, rw, need_ctx)
        x = x + gate_of(mod, 1) * y
        x = x + 0.5 * gate_of(mod, 2) * swiglu(adaln(x, norm_w[l, 2], mod, 2), ffn_w_gu[l, 1], ffn_w_down[l, 1])
        if need_ctx:
            xc = xc + gate_of(mod_c, 1) * yc
            xc = xc + 0.5 * gate_of(mod_c, 2) * swiglu(adaln(xc, norm_w[l, 2], mod_c, 2), ffn_w_gu[l, 1], ffn_w_down[l, 1])
    return x
```

```python
import functools
import math

import numpy as np
import jax
import jax.numpy as jnp
from jax import lax
from jax.experimental import pallas as pl
from jax.experimental.pallas import tpu as pltpu

D_MODEL = 1024
SEQ = 2048
DEPTH = 2
CTX_LEN = 256
T_ALL = CTX_LEN + SEQ
GRID_W = 64
GROUP = 256
HEAD_DIM = 64
GROUP_HEADS = 4
D_FF = 2816
N_MOD = 9
NORM_EPS = 1e-6

HY_ORDER = 2
HY_BANDS = 16
HY_TARGET = 1e-2
HY_SHORT_DECAY_PCT = 0.3
HY_LONG_DECAY_PCT = 1.5

NA_WIN_ROWS = 8
NA_WIN_COLS = 16

CHUNK = 64
RW_DECAY_RANK = 32
RW_AAA_RANK = 32
RW_GATE_RANK = 64
RW_LN_EPS = 64e-5

DN_W = 4 * GROUP + 128
RW_W = 3 * GROUP + 128
P_PAD = 3 * GROUP + 3 * GROUP + DN_W + RW_W

TM = 768
TF = 1408
VMEM_LIMIT = 56 * 1024 * 1024

F32 = jnp.float32
BF16 = jnp.bfloat16


def _cparams(*sem):
    return pltpu.CompilerParams(dimension_semantics=sem, vmem_limit_bytes=VMEM_LIMIT)


def _silu(x):
    return x * (1.0 / (1.0 + jnp.exp(-x)))


def _sigmoid(x):
    return 1.0 / (1.0 + jnp.exp(-x))


def _softplus(x):
    return jnp.maximum(x, 0.0) + jnp.log(1.0 + jnp.exp(-jnp.abs(x)))


def _dot(a, b):
    return jnp.dot(a, b, preferred_element_type=F32)


def _dot_nt(a, b):
    return lax.dot_general(a, b, (((1,), (1,)), ((), ())), preferred_element_type=F32)


def _dot_tn(a, b):
    return lax.dot_general(a, b, (((0,), (0,)), ((), ())), preferred_element_type=F32)


def _dot_hi(a, b):
    return jnp.dot(a, b, preferred_element_type=F32, precision=lax.Precision.HIGHEST)


def _mod_kernel(cond_ref, w_ref, b_ref, o_ref):
    a = _silu(cond_ref[...]).astype(BF16)
    o_ref[0] = _dot(a, w_ref[0].astype(BF16)) + b_ref[0]


def modulation_all(cond, w_mod, b_mod):
    r = cond.shape[0]
    tn = 1024
    return pl.pallas_call(
        _mod_kernel,
        grid=(DEPTH, N_MOD * D_MODEL // tn),
        in_specs=[
            pl.BlockSpec((r, D_MODEL), lambda l, j: (0, 0)),
            pl.BlockSpec((1, D_MODEL, tn), lambda l, j: (l, 0, j)),
            pl.BlockSpec((1, 1, tn), lambda l, j: (l, 0, j)),
        ],
        out_specs=pl.BlockSpec((1, r, tn), lambda l, j: (l, 0, j)),
        out_shape=jax.ShapeDtypeStruct((DEPTH, r, N_MOD * D_MODEL), F32),
        compiler_params=_cparams("parallel", "parallel"),
        name="modulation",
    )(cond, w_mod, b_mod.reshape(DEPTH, 1, N_MOD * D_MODEL))


def _row_mod(modc_ref, modb_ref, tile, idx):
    row = lax.broadcasted_iota(jnp.int32, (TM, 1), 0) + (tile % (T_ALL // TM)) * TM
    return jnp.where(row < CTX_LEN, modc_ref[0, idx:idx + 1, :], modb_ref[0, idx:idx + 1, :])


def _adaln(x, nw, shift, scale):
    y = x * lax.rsqrt(jnp.mean(x * x, axis=-1, keepdims=True) + NORM_EPS)
    return y * nw * (1.0 + scale) + shift


def _ffn_kernel(sub, x_ref, modc_ref, modb_ref, nw_ref, wg_ref, wu_ref, wd_ref, o_ref, h_ref, acc_ref):
    i = pl.program_id(0)
    j = pl.program_id(1)

    @pl.when(j == 0)
    def _():
        shift = _row_mod(modc_ref, modb_ref, i, 3 * sub)
        scale = _row_mod(modc_ref, modb_ref, i, 3 * sub + 1)
        h_ref[...] = _adaln(x_ref[...], nw_ref[...], shift, scale).astype(BF16)
        acc_ref[...] = jnp.zeros_like(acc_ref)

    h = h_ref[...]
    a = (_silu(_dot(h, wg_ref[...])) * _dot(h, wu_ref[...])).astype(BF16)
    acc_ref[...] += _dot(a, wd_ref[...])

    @pl.when(j == pl.num_programs(1) - 1)
    def _():
        gate = _row_mod(modc_ref, modb_ref, i, 3 * sub + 2)
        o_ref[...] = x_ref[...] + 0.5 * gate * acc_ref[...]


def ffn_half_step(x, modc, modb, nw, w_gu, w_down, sub):
    n = x.shape[0]
    tiles_per_b = T_ALL // TM
    nj = D_FF // TF
    return pl.pallas_call(
        functools.partial(_ffn_kernel, sub),
        grid=(n // TM, nj),
        in_specs=[
            pl.BlockSpec((TM, D_MODEL), lambda i, j: (i, 0)),
            pl.BlockSpec((1, N_MOD, D_MODEL), lambda i, j: (0, 0, 0)),
            pl.BlockSpec((1, N_MOD, D_MODEL), lambda i, j: (i // tiles_per_b, 0, 0)),
            pl.BlockSpec((1, D_MODEL), lambda i, j: (0, 0)),
            pl.BlockSpec((D_MODEL, TF), lambda i, j: (0, j)),
            pl.BlockSpec((D_MODEL, TF), lambda i, j: (0, nj + j)),
            pl.BlockSpec((TF, D_MODEL), lambda i, j: (j, 0)),
        ],
        out_specs=pl.BlockSpec((TM, D_MODEL), lambda i, j: (i, 0)),
        out_shape=jax.ShapeDtypeStruct((n, D_MODEL), F32),
        scratch_shapes=[pltpu.VMEM((TM, D_MODEL), BF16), pltpu.VMEM((TM, D_MODEL), F32)],
        compiler_params=_cparams("parallel", "arbitrary"),
        name=f"ffn{sub}",
    )(x, modc, modb, nw.reshape(1, D_MODEL), w_gu, w_gu, w_down)


def _inproj_kernel(x_ref, modc_ref, modb_ref, nw_ref, w_ref, hy_ref, na_ref, dn_ref, rw_ref):
    i = pl.program_id(0)
    shift = _row_mod(modc_ref, modb_ref, i, 3)
    scale = _row_mod(modc_ref, modb_ref, i, 4)
    h = _adaln(x_ref[...], nw_ref[...], shift, scale).astype(BF16)
    o0 = 3 * GROUP
    o1 = 6 * GROUP
    o2 = o1 + DN_W
    hy_ref[...] = _dot(h, w_ref[:, 0:o0])
    na_ref[...] = _dot(h, w_ref[:, o0:o1])
    dn_ref[...] = _dot(h, w_ref[:, o1:o2])
    rw_ref[...] = _dot(h, w_ref[:, o2:P_PAD])


def input_projection(x, modc, modb, nw, w_in_p):
    n = x.shape[0]
    tiles_per_b = T_ALL // TM
    widths = (3 * GROUP, 3 * GROUP, DN_W, RW_W)
    return pl.pallas_call(
        _inproj_kernel,
        grid=(n // TM,),
        in_specs=[
            pl.BlockSpec((TM, D_MODEL), lambda i: (i, 0)),
            pl.BlockSpec((1, N_MOD, D_MODEL), lambda i: (0, 0, 0)),
            pl.BlockSpec((1, N_MOD, D_MODEL), lambda i: (i // tiles_per_b, 0, 0)),
            pl.BlockSpec((1, D_MODEL), lambda i: (0, 0)),
            pl.BlockSpec((D_MODEL, P_PAD), lambda i: (0, 0)),
        ],
        out_specs=[pl.BlockSpec((TM, w), lambda i: (i, 0)) for w in widths],
        out_shape=[jax.ShapeDtypeStruct((n, w), F32) for w in widths],
        compiler_params=_cparams("parallel"),
        name="inproj",
    )(x, modc, modb, nw.reshape(1, D_MODEL), w_in_p)


def _outproj_kernel(x_ref, modc_ref, modb_ref, g0_ref, g1_ref, g2_ref, g3_ref, w_ref, o_ref):
    i = pl.program_id(0)
    y = _dot(g0_ref[...].astype(BF16), w_ref[0:GROUP, :])
    y += _dot(g1_ref[...].astype(BF16), w_ref[GROUP:2 * GROUP, :])
    y += _dot(g2_ref[...].astype(BF16), w_ref[2 * GROUP:3 * GROUP, :])
    y += _dot(g3_ref[...].astype(BF16), w_ref[3 * GROUP:4 * GROUP, :])
    gate = _row_mod(modc_ref, modb_ref, i, 5)
    o_ref[...] = x_ref[...] + gate * y


def output_projection(x, modc, modb, groups, w_out):
    n = x.shape[0]
    tiles_per_b = T_ALL // TM
    return pl.pallas_call(
        _outproj_kernel,
        grid=(n // TM,),
        in_specs=[
            pl.BlockSpec((TM, D_MODEL), lambda i: (i, 0)),
            pl.BlockSpec((1, N_MOD, D_MODEL), lambda i: (0, 0, 0)),
            pl.BlockSpec((1, N_MOD, D_MODEL), lambda i: (i // tiles_per_b, 0, 0)),
        ] + [pl.BlockSpec((TM, GROUP), lambda i: (i, 0))] * 4 + [
            pl.BlockSpec((D_MODEL, D_MODEL), lambda i: (0, 0)),
        ],
        out_specs=pl.BlockSpec((TM, D_MODEL), lambda i: (i, 0)),
        out_shape=jax.ShapeDtypeStruct((n, D_MODEL), F32),
        compiler_params=_cparams("parallel"),
        name="outproj",
    )(x, modc, modb, *groups, w_out)


def _head_mean_matrix(scale):
    r = lax.broadcasted_iota(jnp.int32, (GROUP, GROUP), 0) // HEAD_DIM
    c = lax.broadcasted_iota(jnp.int32, (GROUP, GROUP), 1) // HEAD_DIM
    return jnp.where(r == c, scale, 0.0).astype(BF16)


def _dot_split(a, m_bf16):
    hi = a.astype(BF16)
    lo = (a - hi.astype(F32)).astype(BF16)
    return _dot(hi, m_bf16) + _dot(lo, m_bf16)


def _lane_head(width=GROUP):
    return lax.broadcasted_iota(jnp.int32, (1, width), 1) // HEAD_DIM


NA_ROWS = SEQ // GRID_W
NA_LOCAL = NA_WIN_ROWS * GRID_W
NA_NEG = -1e30
NA_BLK = 256


def na_bias_table(rpb):
    col = np.arange(GRID_W)
    col_start = np.clip(col - NA_WIN_COLS // 2, 0, GRID_W - NA_WIN_COLS)
    in_win = (col[None, :] >= col_start[:, None]) & (col[None, :] < col_start[:, None] + NA_WIN_COLS)
    d_col = np.clip(col[None, :] - col[:, None], 1 - NA_WIN_COLS, NA_WIN_COLS - 1) + NA_WIN_COLS - 1
    p = np.arange(NA_WIN_ROWS)
    w = np.arange(NA_WIN_ROWS)
    d_row = w[None, :] - p[:, None] + NA_WIN_ROWS - 1
    tab = rpb[:, d_row[:, None, :, None], d_col[None, :, None, :]]
    tab = jnp.where(jnp.asarray(in_win)[None, None, :, None, :], tab, NA_NEG)
    tab = jnp.transpose(tab, (1, 0, 2, 3, 4))
    return tab.reshape(NA_WIN_ROWS, GROUP_HEADS, GRID_W, NA_LOCAL).astype(F32)


def _na_kernel(need_ctx, slab_ref, qw_ref, kw_ref, bias_ref, o_ref, q_s, k_s, v_s):
    hm = _head_mean_matrix(1.0 / HEAD_DIM)
    qw = qw_ref[...] * (HEAD_DIM ** -0.5)
    kw = kw_ref[...]

    def prep(i, c):
        r0 = pl.multiple_of(i * NA_BLK, NA_BLK)
        q = slab_ref[pl.ds(r0, NA_BLK), 0:GROUP]
        k = slab_ref[pl.ds(r0, NA_BLK), GROUP:2 * GROUP]
        q_s[pl.ds(r0, NA_BLK), :] = (q * lax.rsqrt(_dot_split(q * q, hm) + NORM_EPS) * qw).astype(BF16)
        k_s[pl.ds(r0, NA_BLK), :] = (k * lax.rsqrt(_dot_split(k * k, hm) + NORM_EPS) * kw).astype(BF16)
        v_s[pl.ds(r0, NA_BLK), :] = slab_ref[pl.ds(r0, NA_BLK), 2 * GROUP:3 * GROUP].astype(BF16)
        return c

    lax.fori_loop(0, T_ALL // NA_BLK, prep, 0)

    lane_h = _lane_head()
    kc = k_s[0:CTX_LEN, :]
    vc = v_s[0:CTX_LEN, :]

    if need_ctx:
        qc = q_s[0:CTX_LEN, :]
        out = jnp.zeros((CTX_LEN, GROUP), F32)
        for h in range(GROUP_HEADS):
            mask = lane_h == h
            s = _dot_nt(jnp.where(mask, qc, jnp.zeros_like(qc)), kc)
            e = jnp.exp(s - jnp.max(s, axis=-1, keepdims=True))
            p = e * (1.0 / jnp.sum(e, axis=-1, keepdims=True))
            out = jnp.where(mask, _dot(p.astype(BF16), vc), out)
        o_ref[0:CTX_LEN, :] = out
    else:
        o_ref[0:CTX_LEN, :] = jnp.zeros((CTX_LEN, GROUP), F32)

    def row_body(r, c):
        start = jnp.clip(r - NA_WIN_ROWS // 2, 0, NA_ROWS - NA_WIN_ROWS)
        pat = r - start
        q0 = pl.multiple_of(CTX_LEN + r * GRID_W, GRID_W)
        k0 = pl.multiple_of(CTX_LEN + start * GRID_W, GRID_W)
        q = q_s[pl.ds(q0, GRID_W), :]
        kb = k_s[pl.ds(k0, NA_LOCAL), :]
        vb = v_s[pl.ds(k0, NA_LOCAL), :]
        out = jnp.zeros((GRID_W, GROUP), F32)
        for h in range(GROUP_HEADS):
            mask = lane_h == h
            qh = jnp.where(mask, q, jnp.zeros_like(q))
            s_loc = _dot_nt(qh, kb) + bias_ref[pat, h]
            s_ctx = _dot_nt(qh, kc)
            m = jnp.maximum(jnp.max(s_loc, axis=-1, keepdims=True), jnp.max(s_ctx, axis=-1, keepdims=True))
            e_loc = jnp.exp(s_loc - m)
            e_ctx = jnp.exp(s_ctx - m)
            inv = 1.0 / (jnp.sum(e_loc, axis=-1, keepdims=True) + jnp.sum(e_ctx, axis=-1, keepdims=True))
            o = _dot((e_loc * inv).astype(BF16), vb) + _dot((e_ctx * inv).astype(BF16), vc)
            out = jnp.where(mask, o, out)
        o_ref[pl.ds(q0, GRID_W), :] = out
        return c

    lax.fori_loop(0, NA_ROWS, row_body, 0)


def na_mixer(slab, q_norm, k_norm, bias_tab, need_ctx):
    n = slab.shape[0]
    tile4 = lambda w: jnp.tile(w, GROUP_HEADS).reshape(1, GROUP)
    return pl.pallas_call(
        functools.partial(_na_kernel, need_ctx),
        grid=(n // T_ALL,),
        in_specs=[
            pl.BlockSpec((T_ALL, 3 * GROUP), lambda b: (b, 0)),
            pl.BlockSpec((1, GROUP), lambda b: (0, 0)),
            pl.BlockSpec((1, GROUP), lambda b: (0, 0)),
            pl.BlockSpec((NA_WIN_ROWS, GROUP_HEADS, GRID_W, NA_LOCAL), lambda b: (0, 0, 0, 0)),
        ],
        out_specs=pl.BlockSpec((T_ALL, GROUP), lambda b: (b, 0)),
        out_shape=jax.ShapeDtypeStruct((n, GROUP), F32),
        scratch_shapes=[pltpu.VMEM((T_ALL, GROUP), BF16)] * 3,
        compiler_params=_cparams("parallel"),
        name="na_mixer",
    )(slab, tile4(q_norm), tile4(k_norm), bias_tab)


SEQ_BLK = 256
N_BLK = T_ALL // SEQ_BLK
N_CHUNK = T_ALL // CHUNK
CTX_CHUNKS = CTX_LEN // CHUNK


def _prev_cur_next(ref, i, c0, c1):
    r0 = pl.multiple_of(i * SEQ_BLK, SEQ_BLK)
    cur = ref[pl.ds(r0, SEQ_BLK), c0:c1]
    up0 = pl.multiple_of(jnp.maximum(r0 - 8, 0), 8)
    dn0 = pl.multiple_of(jnp.minimum(r0 + SEQ_BLK, T_ALL - 8), 8)
    up = ref[pl.ds(up0, 8), c0:c1][7:8, :]
    dn = ref[pl.ds(dn0, 8), c0:c1][0:1, :]
    up = jnp.where(i >= 2, up, 0.0)
    dn = jnp.where(jnp.logical_and(i >= 1, i <= N_BLK - 2), dn, 0.0)
    row = lax.broadcasted_iota(jnp.int32, (SEQ_BLK, 1), 0)
    prev = jnp.where(row == 0, up, pltpu.roll(cur, 1, 0))
    nxt = jnp.where(row == SEQ_BLK - 1, dn, pltpu.roll(cur, SEQ_BLK - 1, 0))
    return prev, cur, nxt


def _chunk_cumsum(x, reverse):
    pos = lax.broadcasted_iota(jnp.int32, (SEQ_BLK, 1), 0) % CHUNK
    s = 1
    while s < CHUNK:
        if reverse:
            x = x + jnp.where(pos < CHUNK - s, pltpu.roll(x, SEQ_BLK - s, 0), 0.0)
        else:
            x = x + jnp.where(pos >= s, pltpu.roll(x, s, 0), 0.0)
        s *= 2
    return x


def _split3(a):
    hi = a.astype(BF16)
    r1 = a - hi.astype(F32)
    mid = r1.astype(BF16)
    lo = (r1 - mid.astype(F32)).astype(BF16)
    return hi, mid, lo


def _dot_exact_rhs(a, m_bf16):
    hi, mid, lo = _split3(a)
    return _dot(hi, m_bf16) + _dot(mid, m_bf16) + _dot(lo, m_bf16)


def _expand_heads(x):
    lane_h = _lane_head()
    return jnp.concatenate([jnp.where(lane_h == h, x, 0.0) for h in range(GROUP_HEADS)], axis=0)


def _tile_heads(x):
    return jnp.concatenate([x] * GROUP_HEADS, axis=0)


def _fold_heads(x):
    return (x[0:CHUNK] + x[CHUNK:2 * CHUNK]) + (x[2 * CHUNK:3 * CHUNK] + x[3 * CHUNK:4 * CHUNK])


def _chunk_masks(reverse):
    n = GROUP_HEADS * CHUNK
    r = lax.broadcasted_iota(jnp.int32, (n, n), 0)
    c = lax.broadcasted_iota(jnp.int32, (n, n), 1)
    same = (r // CHUNK) == (c // CHUNK)
    ri = r % CHUNK
    ci = c % CHUNK
    if reverse:
        return jnp.logical_and(same, ri <= ci), jnp.logical_and(same, ri < ci)
    return jnp.logical_and(same, ri >= ci), jnp.logical_and(same, ri > ci)


def _chunk_of_step(n, reverse):
    if not reverse:
        return n
    return jnp.where(n < CTX_CHUNKS, CTX_CHUNKS - 1 - n, N_CHUNK + CTX_CHUNKS - 1 - n)


INV_BASE = 16


def _bdot(a, b):
    return _dot(a.astype(BF16), b.astype(BF16))


def _inverse_unit_triangular(n_mat):
    size = GROUP_HEADS * CHUNK
    r = lax.broadcasted_iota(jnp.int32, (size, size), 0)
    c = lax.broadcasted_iota(jnp.int32, (size, size), 1)
    inner = (r // INV_BASE) == (c // INV_BASE)
    nd = jnp.where(inner, n_mat, 0.0)
    x = jnp.where(r == c, 1.0, 0.0) - nd
    p = _bdot(nd, nd)
    k = 2
    while k < INV_BASE:
        x = x + _bdot(p, x)
        k *= 2
        if k < INV_BASE:
            p = _bdot(p, p)
    width = INV_BASE
    while width < CHUNK:
        outer = (r // (2 * width)) == (c // (2 * width))
        off = jnp.where(jnp.logical_and(outer, jnp.logical_not(inner)), n_mat, 0.0)
        x = x - _bdot(x, _bdot(off, x))
        inner = outer
        width *= 2
    return x


def _head_rows(gc, lane_onehot):
    hi, mid, lo = _split3(gc)
    t = _dot_nt(lane_onehot, hi) + _dot_nt(lane_onehot, mid) + _dot_nt(lane_onehot, lo)
    return jnp.concatenate([t[h:h + 1, :] for h in range(GROUP_HEADS)], axis=1)


def _head_cols(gc):
    return jnp.concatenate(
        [jnp.broadcast_to(gc[:, h * HEAD_DIM:h * HEAD_DIM + 1], (CHUNK, GROUP)) for h in range(GROUP_HEADS)], axis=0)


def _dn_kernel(slab_ref, conv_ref, alog_ref, dt_ref, nw_ref, o_ref, q_s, k_s, v_s, gc_s, beta_s, o_s):
    hsum = _head_mean_matrix(1.0)
    lane_h = _lane_head()
    col = lax.broadcasted_iota(jnp.int32, (128, GROUP), 0)
    lane = lax.broadcasted_iota(jnp.int32, (128, GROUP), 1) // HEAD_DIM
    neg_a = -jnp.exp(alog_ref[...])
    dtb = dt_ref[...]

    def prep(i, c):
        r0 = pl.multiple_of(i * SEQ_BLK, SEQ_BLK)
        for j, dst in enumerate((q_s, k_s, v_s)):
            prev, cur, nxt = _prev_cur_next(slab_ref, i, j * GROUP, (j + 1) * GROUP)
            w = conv_ref[:, j * GROUP:(j + 1) * GROUP]
            u = _silu(prev * w[0:1] + cur * w[1:2] + nxt * w[2:3])
            if j == 0:
                u = u * lax.rsqrt(_dot_split(u * u, hsum) + 1e-6) * (HEAD_DIM ** -0.5)
            elif j == 1:
                u = u * lax.rsqrt(_dot_split(u * u, hsum) + 1e-6)
            dst[pl.ds(r0, SEQ_BLK), :] = u
        ba = slab_ref[pl.ds(r0, SEQ_BLK), 4 * GROUP:4 * GROUP + 128]
        for d in range(2):
            e_b = jnp.where(col == 8 * d + lane, 1.0, 0.0).astype(BF16)
            e_a = jnp.where(col == 8 * d + 4 + lane, 1.0, 0.0).astype(BF16)
            beta_s[d, pl.ds(r0, SEQ_BLK), :] = _sigmoid(_dot_exact_rhs(ba, e_b))
            g = neg_a[d:d + 1] * _softplus(_dot_exact_rhs(ba, e_a) + dtb[d:d + 1])
            gc_s[d, pl.ds(r0, SEQ_BLK), :] = _chunk_cumsum(g, reverse=(d == 1))
        return c

    lax.fori_loop(0, N_BLK, prep, 0)

    onehot = jnp.where(
        lax.broadcasted_iota(jnp.int32, (8, GROUP), 1) == HEAD_DIM * lax.broadcasted_iota(jnp.int32, (8, GROUP), 0),
        1.0, 0.0).astype(BF16)

    for d in range(2):
        reverse = d == 1
        incl, strict = _chunk_masks(reverse)

        def step(n, s, d=d, reverse=reverse, incl=incl, strict=strict):
            r0 = pl.multiple_of(_chunk_of_step(n, reverse) * CHUNK, CHUNK)
            q = q_s[pl.ds(r0, CHUNK), :]
            k = k_s[pl.ds(r0, CHUNK), :]
            v = v_s[pl.ds(r0, CHUNK), :]
            gc = gc_s[d, pl.ds(r0, CHUNK), :]
            beta = _tile_heads(beta_s[d, pl.ds(r0, CHUNK), :])
            ge = _tile_heads(gc)
            k_e = _expand_heads(k)
            q_e = _expand_heads(q)
            kb_e = k_e * beta
            delta = _head_cols(gc) - _head_rows(gc, onehot)
            dec = jnp.where(incl, jnp.exp(jnp.where(incl, delta, 0.0)), 0.0)
            k_b = k_e.astype(BF16)
            m = jnp.where(strict, _dot_nt(kb_e.astype(BF16), k_b) * dec, 0.0)
            attn = _dot_nt(q_e.astype(BF16), k_b) * dec
            eg = jnp.exp(ge)
            rhs = jnp.concatenate([_expand_heads(v) * beta, kb_e * eg], axis=1)
            sol = _bdot(_inverse_unit_triangular(m), rhs)
            u = sol[:, 0:GROUP]
            w = sol[:, GROUP:2 * GROUP]
            g_last = gc[0:1, :] if reverse else gc[CHUNK - 1:CHUNK, :]
            s_b = s.astype(BF16)
            v_new = u - _dot(w.astype(BF16), s_b)
            v_nb = v_new.astype(BF16)
            o_e = _dot((q_e * eg).astype(BF16), s_b) + _dot(attn.astype(BF16), v_nb)
            k_dec = (k_e * jnp.exp(g_last - ge)).astype(BF16)
            o = _fold_heads(o_e)
            if reverse:
                o_s[pl.ds(r0, CHUNK), :] += o
            else:
                o_s[pl.ds(r0, CHUNK), :] = o
            return s * jnp.exp(g_last) + _dot_tn(k_dec, v_nb)

        lax.fori_loop(0, N_CHUNK, step, jnp.zeros((GROUP, GROUP), F32))

    hmean = _head_mean_matrix(1.0 / HEAD_DIM)
    nw = nw_ref[...]

    def finish(i, c):
        r0 = pl.multiple_of(i * SEQ_BLK, SEQ_BLK)
        o = o_s[pl.ds(r0, SEQ_BLK), :]
        z = slab_ref[pl.ds(r0, SEQ_BLK), 3 * GROUP:4 * GROUP]
        o_ref[pl.ds(r0, SEQ_BLK), :] = o * lax.rsqrt(_dot_split(o * o, hmean) + NORM_EPS) * nw * _silu(z)
        return c

    lax.fori_loop(0, N_BLK, finish, 0)


def deltanet_mixer(slab, conv_w, a_log, dt_bias, norm_w):
    n = slab.shape[0]
    lanes = lambda t: jnp.repeat(t, HEAD_DIM, axis=-1)
    seq = pltpu.VMEM((T_ALL, GROUP), F32)
    seq2 = pltpu.VMEM((2, T_ALL, GROUP), F32)
    return pl.pallas_call(
        _dn_kernel,
        grid=(n // T_ALL,),
        in_specs=[
            pl.BlockSpec((T_ALL, DN_W), lambda b: (b, 0)),
            pl.BlockSpec((3, 3 * GROUP), lambda b: (0, 0)),
            pl.BlockSpec((2, GROUP), lambda b: (0, 0)),
            pl.BlockSpec((2, GROUP), lambda b: (0, 0)),
            pl.BlockSpec((1, GROUP), lambda b: (0, 0)),
        ],
        out_specs=pl.BlockSpec((T_ALL, GROUP), lambda b: (b, 0)),
        out_shape=jax.ShapeDtypeStruct((n, GROUP), F32),
        scratch_shapes=[seq, seq, seq, seq2, seq2, seq],
        compiler_params=_cparams("parallel"),
        name="deltanet",
    )(slab, conv_w, lanes(a_log), lanes(dt_bias), jnp.tile(norm_w, GROUP_HEADS).reshape(1, GROUP))


RW_LR = RW_DECAY_RANK + RW_AAA_RANK + RW_GATE_RANK
RW_LR_OUT = 5 * GROUP


def _dot3(a, b_hi, b_lo):
    a_hi = a.astype(BF16)
    a_lo = (a - a_hi.astype(F32)).astype(BF16)
    return _dot(a_hi, b_hi) + (_dot(a_lo, b_hi) + _dot(a_hi, b_lo))


def _rw_kernel(slab_ref, mu_ref, pv_ref, wlr_ref, o_ref,
               at_s, rt_s, bh_s, kh_s, gl_s, v_s, bonus_s, g_s, y_s):
    hsum = _head_mean_matrix(1.0)
    hmean = _head_mean_matrix(1.0 / HEAD_DIM)
    wlr = wlr_ref[...]
    wlr_hi = wlr.astype(BF16)
    wlr_lo = (wlr - wlr_hi.astype(F32)).astype(BF16)
    pv = pv_ref[...]
    w0 = (pv[0:1], pv[1:2])
    a0 = (pv[2:3], pv[3:4])
    k_k, k_a, r_k = pv[4:5], pv[5:6], pv[6:7]
    lr_lane = lax.broadcasted_iota(jnp.int32, (1, RW_LR), 1)

    def shifted(i, c0, c1):
        prev, cur, nxt = _prev_cur_next(slab_ref, i, c0, c1)
        return cur + mu_ref[0:1, c0:c1] * (prev - cur) + mu_ref[1:2, c0:c1] * (nxt - cur)

    def prep(i, c):
        r0 = pl.multiple_of(i * SEQ_BLK, SEQ_BLK)
        rows = pl.ds(r0, SEQ_BLK)
        r = shifted(i, 0, GROUP)
        k = shifted(i, GROUP, 2 * GROUP)
        v = shifted(i, 2 * GROUP, 3 * GROUP)
        lr = shifted(i, 3 * GROUP, 3 * GROUP + RW_LR)
        t = jnp.where(lr_lane < RW_DECAY_RANK, jnp.tanh(lr),
                      jnp.where(lr_lane < RW_DECAY_RANK + RW_AAA_RANK, lr, _sigmoid(lr)))
        proj = _dot3(t, wlr_hi, wlr_lo)
        kq = k * k_k
        kk = kq * lax.rsqrt(_dot_split(kq * kq, hsum) + 1e-6)
        v_s[rows, :] = v.astype(BF16)
        g_s[rows, :] = proj[:, 4 * GROUP:5 * GROUP]
        ksum = jnp.zeros_like(k)
        for d in range(2):
            w_log = -_softplus(-(w0[d] + proj[:, d * GROUP:(d + 1) * GROUP])) - 0.5
            lw = -jnp.exp(w_log)
            a_gate = _sigmoid(a0[d] + proj[:, (2 + d) * GROUP:(3 + d) * GROUP])
            k_d = k * (1.0 + (a_gate - 1.0) * k_a)
            ksum = ksum + k_d
            cum = _chunk_cumsum(lw, reverse=(d == 1))
            inv = jnp.exp(-cum)
            at_s[d, rows, :] = (-kk * jnp.exp(cum - lw)).astype(BF16)
            rt_s[d, rows, :] = (r * jnp.exp(cum)).astype(BF16)
            bh_s[d, rows, :] = (kk * a_gate * inv).astype(BF16)
            kh_s[d, rows, :] = (k_d * inv).astype(BF16)
            for j in range(SEQ_BLK // CHUNK):
                last = j * CHUNK if d == 1 else (j + 1) * CHUNK - 1
                gl_s[d, pl.ds(i * (SEQ_BLK // CHUNK) + j, 1), :] = cum[last:last + 1, :]
        bonus_s[rows, :] = _dot_split(r * ksum * r_k, hsum) * v
        return c

    lax.fori_loop(0, N_BLK, prep, 0)

    for d in range(2):
        reverse = d == 1
        incl, strict = _chunk_masks(reverse)

        def step(n, s, d=d, reverse=reverse, incl=incl, strict=strict):
            ch = _chunk_of_step(n, reverse)
            r0 = pl.multiple_of(ch * CHUNK, CHUNK)
            rows = pl.ds(r0, CHUNK)
            at_e = _expand_heads(at_s[d, rows, :])
            rt_e = _expand_heads(rt_s[d, rows, :])
            bh_e = _expand_heads(bh_s[d, rows, :])
            kh_e = _expand_heads(kh_s[d, rows, :])
            v_e = _expand_heads(v_s[rows, :])
            gamma = jnp.exp(gl_s[d, pl.ds(ch, 1), :])
            x = _inverse_unit_triangular(jnp.where(strict, -_dot_nt(at_e, bh_e), 0.0))
            a_ak = jnp.where(strict, _dot_nt(at_e, kh_e), 0.0).astype(BF16)
            a_rb = jnp.where(incl, _dot_nt(rt_e, bh_e), 0.0).astype(BF16)
            a_rk = jnp.where(incl, _dot_nt(rt_e, kh_e), 0.0).astype(BF16)
            s_b = s.astype(BF16)
            sa = _dot(x.astype(BF16), (_dot_nt(at_e, s_b) + _dot(a_ak, v_e)).astype(BF16)).astype(BF16)
            y_e = _dot_nt(rt_e, s_b) + _dot(a_rb, sa) + _dot(a_rk, v_e)
            y = _fold_heads(y_e)
            if reverse:
                y_s[rows, :] += y
            else:
                y_s[rows, :] = y
            bg = (bh_e.astype(F32) * gamma).astype(BF16)
            kg = (kh_e.astype(F32) * gamma).astype(BF16)
            return s * gamma + _dot_tn(sa, bg) + _dot_tn(v_e, kg)

        lax.fori_loop(0, N_CHUNK, step, jnp.zeros((GROUP, GROUP), F32))

    ln_w, ln_b = pv[7:8], pv[8:9]

    def finish(i, c):
        rows = pl.ds(pl.multiple_of(i * SEQ_BLK, SEQ_BLK), SEQ_BLK)
        y = y_s[rows, :]
        yc = y - _dot_split(y, hmean)
        yn = yc * lax.rsqrt(_dot_split(yc * yc, hmean) + RW_LN_EPS) * ln_w + ln_b
        o_ref[rows, :] = (yn + bonus_s[rows, :]) * g_s[rows, :]
        return c

    lax.fori_loop(0, N_BLK, finish, 0)


def rwkv_lowrank_weights(w_up, a_up, g_up):
    w = jnp.zeros((RW_LR, RW_LR_OUT), F32)
    o1 = RW_DECAY_RANK
    o2 = o1 + RW_AAA_RANK
    for d in range(2):
        w = w.at[0:o1, d * GROUP:(d + 1) * GROUP].set(w_up[d])
        w = w.at[o1:o2, (2 + d) * GROUP:(3 + d) * GROUP].set(a_up[d])
    return w.at[o2:RW_LR, 4 * GROUP:5 * GROUP].set(g_up)


def rwkv_mixer(slab, mu, w0, w_up, a0, a_up, g_up, k_k, k_a, r_k, ln_w, ln_b):
    n = slab.shape[0]
    pv = jnp.concatenate([w0, a0, k_k[None], k_a[None], r_k.reshape(1, GROUP), ln_w[None], ln_b[None],
                          jnp.zeros((7, GROUP), F32)], axis=0)
    seq_b = pltpu.VMEM((2, T_ALL, GROUP), BF16)
    return pl.pallas_call(
        _rw_kernel,
        grid=(n // T_ALL,),
        in_specs=[
            pl.BlockSpec((T_ALL, RW_W), lambda b: (b, 0)),
            pl.BlockSpec((2, RW_W), lambda b: (0, 0)),
            pl.BlockSpec((16, GROUP), lambda b: (0, 0)),
            pl.BlockSpec((RW_LR, RW_LR_OUT), lambda b: (0, 0)),
        ],
        out_specs=pl.BlockSpec((T_ALL, GROUP), lambda b: (b, 0)),
        out_shape=jax.ShapeDtypeStruct((n, GROUP), F32),
        scratch_shapes=[seq_b, seq_b, seq_b, seq_b, pltpu.VMEM((2, 40, GROUP), F32),
                        pltpu.VMEM((T_ALL, GROUP), BF16), pltpu.VMEM((T_ALL, GROUP), F32),
                        pltpu.VMEM((T_ALL, GROUP), F32), pltpu.VMEM((T_ALL, GROUP), F32)],
        compiler_params=_cparams("parallel"),
        name="rwkv7",
    )(slab, mu, pv, rwkv_lowrank_weights(w_up, a_up, g_up))


DFT_SPLIT = 64
DFT_BLK = 256


def _dft_tables(n):
    big = 2 * n
    t = np.arange(n, dtype=np.int64)[:, None]
    k1 = np.arange(n // DFT_SPLIT, dtype=np.int64)[None, :]
    k2 = np.arange(DFT_SPLIT, dtype=np.int64)[None, :]
    alpha = 2.0 * np.pi * ((DFT_SPLIT * t * k1) % big) / big
    beta = 2.0 * np.pi * ((t * k2) % big) / big

    def pad(a):
        out = np.zeros((n, 128), np.float32)
        out[:, :a.shape[1]] = a
        return out

    return np.stack([pad(np.cos(alpha)), pad(np.sin(alpha)), pad(np.cos(beta)), pad(np.sin(beta))])


def _dft_gen_kernel(n, tab_ref, g_ref):
    k = lax.broadcasted_iota(jnp.int32, (128, n), 1)
    row = lax.broadcasted_iota(jnp.int32, (128, n), 0)
    e_a = jnp.where(k // DFT_SPLIT == row, 1.0, 0.0).astype(BF16)
    e_b = jnp.where(jnp.logical_and(k % DFT_SPLIT == row, row < DFT_SPLIT), 1.0, 0.0).astype(BF16)
    ca = _dot_split(tab_ref[0], e_a)
    sa = _dot_split(tab_ref[1], e_a)
    cb = _dot_split(tab_ref[2], e_b)
    sb = _dot_split(tab_ref[3], e_b)
    g_ref[:, 0:n] = (ca * cb - sa * sb).astype(BF16)
    g_ref[:, n:2 * n] = (-(sa * cb + ca * sb)).astype(BF16)


def dft_matrix(n):
    blk = min(DFT_BLK, n)
    return pl.pallas_call(
        functools.partial(_dft_gen_kernel, n),
        grid=(n // blk,),
        in_specs=[pl.BlockSpec((4, blk, 128), lambda i: (0, i, 0))],
        out_specs=pl.BlockSpec((blk, 2 * n), lambda i: (i, 0)),
        out_shape=jax.ShapeDtypeStruct((n, 2 * n), BF16),
        compiler_params=_cparams("parallel"),
        name=f"dft_matrix_{n}",
    )(jnp.asarray(_dft_tables(n)))


HY_COLS_F = 2 * HY_ORDER * GROUP
HY_OC = HY_ORDER * GROUP


def _hyena_filter_kernel(n, z_ref, w1_ref, b1_ref, w2_ref, b2_ref, w3_ref, freq_ref, dl_ref, hs_ref, hd_ref):
    blk = min(SEQ_BLK, n)
    freq = freq_ref[...]
    dl = dl_ref[...]

    def fill(i, norm):
        r0 = pl.multiple_of(i * blk, blk)
        z = z_ref[pl.ds(r0, blk), :]
        hid = jnp.sin(freq * (_dot_hi(z, w1_ref[...]) + b1_ref[...]))
        hid = jnp.sin(freq * (_dot_hi(hid, w2_ref[...]) + b2_ref[...]))
        t = z[:, 0:1]
        h = _dot_hi(hid, w3_ref[...]) * jnp.exp(-t * dl)
        lag = lax.broadcasted_iota(jnp.int32, (blk, 1), 0) + r0
        hf = h[:, 0:HY_OC]
        hb = jnp.where(lag == 0, 0.0, h[:, HY_OC:2 * HY_OC])
        hs_ref[pl.ds(r0, blk), :] = hf + hb
        hd_ref[pl.ds(r0, blk), :] = hf - hb
        return norm + jnp.sum(jnp.abs(hf) + jnp.abs(hb), axis=0, keepdims=True)

    norm = lax.fori_loop(0, n // blk, fill, jnp.zeros((1, HY_OC), F32))
    inv = 1.0 / norm

    def scale(i, c):
        rows = pl.ds(pl.multiple_of(i * blk, blk), blk)
        hs_ref[rows, :] = hs_ref[rows, :] * inv
        hd_ref[rows, :] = hd_ref[rows, :] * inv
        return c

    lax.fori_loop(0, n // blk, scale, 0)


def hyena_filter_taps(n, f_w1, f_b1, f_w2, f_b2, f_w3, f_freq):
    t = jnp.linspace(0.0, 1.0, n, dtype=F32)[:, None]
    ang = 2.0 * math.pi * jnp.arange(n, dtype=F32)[:, None] / n
    bands = jnp.linspace(1e-4, HY_BANDS - 1, HY_BANDS, dtype=F32)[None]
    z = jnp.concatenate([t, jnp.cos(bands * ang), -jnp.sin(bands * ang)], axis=-1)
    emb = z.shape[1]
    z = jnp.pad(z, ((0, 0), (0, 128 - emb)))
    w1 = jnp.pad(f_w1, ((0, 128 - emb), (0, 0)))
    max_decay = math.log(HY_TARGET) / HY_SHORT_DECAY_PCT
    min_decay = math.log(HY_TARGET) / HY_LONG_DECAY_PCT
    deltas = jnp.abs(jnp.linspace(min_decay, max_decay, HY_OC, dtype=F32))
    dl = jnp.tile(deltas, 2).reshape(1, HY_COLS_F)
    hid = f_w2.shape[0]
    out = jax.ShapeDtypeStruct((n, HY_OC), F32)
    return pl.pallas_call(
        functools.partial(_hyena_filter_kernel, n),
        out_shape=[out, out],
        compiler_params=pltpu.CompilerParams(vmem_limit_bytes=VMEM_LIMIT),
        name=f"hyena_filter_{n}",
    )(z, w1, f_b1.reshape(1, hid), f_w2, f_b2.reshape(1, hid), f_w3, f_freq.reshape(1, hid), dl)


def _hyena_spectrum_kernel(n, g_ref, hs_ref, hd_ref, kr_ref, ki_ref, kn_ref):
    blk = min(2 * SEQ_BLK, n)
    big = 2.0 * n

    def split(ref):
        x = ref[...]
        hi = x.astype(BF16)
        return hi, (x - hi.astype(F32)).astype(BF16)

    s_hi, s_lo = split(hs_ref)
    d_hi, d_lo = split(hd_ref)

    def body(i, c):
        r0 = pl.multiple_of(i * blk, blk)
        rows = pl.ds(r0, blk)
        k = lax.broadcasted_iota(jnp.int32, (blk, 1), 0) + r0
        wgt = jnp.where(k == 0, 1.0 / big, 2.0 / big)
        gc = g_ref[rows, 0:n]
        gs = g_ref[rows, n:2 * n]
        kr_ref[rows, :] = (_dot(gc, s_hi) + _dot(gc, s_lo)) * wgt
        ki_ref[rows, :] = (_dot(gs, d_hi) + _dot(gs, d_lo)) * wgt
        return c

    lax.fori_loop(0, n // blk, body, 0)
    t = lax.broadcasted_iota(jnp.int32, (n, 1), 0)
    sign = jnp.where(t % 2 == 0, 1.0, -1.0)
    kn_ref[...] = jnp.broadcast_to(jnp.sum(sign * hs_ref[...], axis=0, keepdims=True) * (1.0 / big), (8, HY_OC))


def hyena_spectrum(n, g, hs, hd):
    out = jax.ShapeDtypeStruct((n, HY_OC), F32)
    return pl.pallas_call(
        functools.partial(_hyena_spectrum_kernel, n),
        out_shape=[out, out, jax.ShapeDtypeStruct((8, HY_OC), F32)],
        compiler_params=pltpu.CompilerParams(vmem_limit_bytes=VMEM_LIMIT),
        name=f"hyena_spectrum_{n}",
    )(g, hs, hd)


def _hyena_conv_kernel(slab_ref, w_ref, o_ref):
    def body(i, c):
        rows = pl.ds(pl.multiple_of(i * SEQ_BLK, SEQ_BLK), SEQ_BLK)
        for j in range(3):
            prev, cur, nxt = _prev_cur_next(slab_ref, i, j * GROUP, (j + 1) * GROUP)
            w = w_ref[:, j * GROUP:(j + 1) * GROUP]
            o_ref[rows, j * GROUP:(j + 1) * GROUP] = prev * w[0:1] + cur * w[1:2] + nxt * w[2:3]
        return c

    lax.fori_loop(0, N_BLK, body, 0)


def hyena_short_conv(slab, conv_w):
    n = slab.shape[0]
    return pl.pallas_call(
        _hyena_conv_kernel,
        grid=(n // T_ALL,),
        in_specs=[pl.BlockSpec((T_ALL, 3 * GROUP), lambda b: (b, 0)),
                  pl.BlockSpec((3, 3 * GROUP), lambda b: (0, 0))],
        out_specs=pl.BlockSpec((T_ALL, 3 * GROUP), lambda b: (b, 0)),
        out_shape=jax.ShapeDtypeStruct((n, 3 * GROUP), F32),
        compiler_params=_cparams("parallel"),
        name="hyena_short_conv",
    )(slab, conv_w)


HY_FBLK = 512


def _alt_sign(n):
    t = lax.broadcasted_iota(jnp.int32, (n, 1), 0)
    return jnp.where(t % 2 == 0, 1.0, -1.0)


def _hyena_fwd_kernel(x_ref, gl_ref, gc_ref, krl_ref, kil_ref, knl_ref, krc_ref, kic_ref, knc_ref,
                      pl_ref, pc_ref, pn_ref):
    def transform(x, g_ref, kr_ref, ki_ref, kn_ref, p_ref, n, blk):
        xb = x.astype(BF16)

        def body(i, c):
            rows = pl.ds(pl.multiple_of(i * blk, blk), blk)
            zr = _dot(g_ref[rows, 0:n], xb)
            zi = _dot(g_ref[rows, n:2 * n], xb)
            kr = kr_ref[rows, :]
            ki = ki_ref[rows, :]
            p_ref[0, 0, rows, :] = (zr * kr - zi * ki).astype(BF16)
            p_ref[0, 1, rows, :] = (zr * ki + zi * kr).astype(BF16)
            return c

        lax.fori_loop(0, n // blk, body, 0)
        return jnp.sum(_alt_sign(n) * x, axis=0, keepdims=True) * kn_ref[0:1, :]

    nyq_c = transform(x_ref[0:CTX_LEN, :], gc_ref, krc_ref, kic_ref, knc_ref, pc_ref, CTX_LEN, CTX_LEN)
    nyq_l = transform(x_ref[CTX_LEN:T_ALL, :], gl_ref, krl_ref, kil_ref, knl_ref, pl_ref, SEQ, HY_FBLK)
    pn_ref[0] = jnp.concatenate([nyq_l, nyq_c, jnp.zeros((6, GROUP), F32)], axis=0)


def _resident(shape):
    return pl.BlockSpec(shape, lambda b: (0,) * len(shape))


def hyena_forward_transform(x, col, g_l, g_c, spec_l, spec_c, order):
    n = x.shape[0]
    nb = n // T_ALL
    kcol = lambda shape: pl.BlockSpec(shape, lambda b: (0, order))
    return pl.pallas_call(
        _hyena_fwd_kernel,
        grid=(nb,),
        in_specs=[
            pl.BlockSpec((T_ALL, GROUP), lambda b: (b, col)),
            _resident((SEQ, 2 * SEQ)), _resident((CTX_LEN, 2 * CTX_LEN)),
            kcol((SEQ, GROUP)), kcol((SEQ, GROUP)), kcol((8, GROUP)),
            kcol((CTX_LEN, GROUP)), kcol((CTX_LEN, GROUP)), kcol((8, GROUP)),
        ],
        out_specs=[
            pl.BlockSpec((1, 2, SEQ, GROUP), lambda b: (b, 0, 0, 0)),
            pl.BlockSpec((1, 2, CTX_LEN, GROUP), lambda b: (b, 0, 0, 0)),
            pl.BlockSpec((1, 8, GROUP), lambda b: (b, 0, 0)),
        ],
        out_shape=[
            jax.ShapeDtypeStruct((nb, 2, SEQ, GROUP), BF16),
            jax.ShapeDtypeStruct((nb, 2, CTX_LEN, GROUP), BF16),
            jax.ShapeDtypeStruct((nb, 8, GROUP), F32),
        ],
        compiler_params=_cparams("parallel"),
        name=f"hyena_fwd_{order}",
    )(x, g_l, g_c, *spec_l, *spec_c)


def _hyena_inv_kernel(pl_ref, pc_ref, pn_ref, gl_ref, gc_ref, u_ref, gate_ref, bias_ref, o_ref):
    bias = bias_ref[0]

    def inverse(p_ref, nyq, g_ref, n, blk, off):
        pr = p_ref[0, 0]
        pi = p_ref[0, 1]

        def body(i, c):
            r0 = pl.multiple_of(i * blk, blk)
            rows = pl.ds(r0, blk)
            orow = pl.ds(pl.multiple_of(off + r0, blk), blk)
            t = lax.broadcasted_iota(jnp.int32, (blk, 1), 0)
            sign = jnp.where(t % 2 == 0, 1.0, -1.0)
            y = _dot(g_ref[rows, 0:n], pr) + _dot(g_ref[rows, n:2 * n], pi) + sign * nyq
            o_ref[orow, :] = gate_ref[orow, :] * (y + u_ref[orow, :] * bias)
            return c

        lax.fori_loop(0, n // blk, body, 0)

    inverse(pc_ref, pn_ref[0, 1:2, :], gc_ref, CTX_LEN, CTX_LEN, 0)
    inverse(pl_ref, pn_ref[0, 0:1, :], gl_ref, SEQ, SEQ_BLK, CTX_LEN)


def hyena_inverse_transform(p_l, p_c, p_n, g_l, g_c, u, ucol, gate, gcol, bias):
    nb = p_l.shape[0]
    return pl.pallas_call(
        _hyena_inv_kernel,
        grid=(nb,),
        in_specs=[
            pl.BlockSpec((1, 2, SEQ, GROUP), lambda b: (b, 0, 0, 0)),
            pl.BlockSpec((1, 2, CTX_LEN, GROUP), lambda b: (b, 0, 0, 0)),
            pl.BlockSpec((1, 8, GROUP), lambda b: (b, 0, 0)),
            _resident((SEQ, 2 * SEQ)), _resident((CTX_LEN, 2 * CTX_LEN)),
            pl.BlockSpec((T_ALL, GROUP), lambda b: (b, ucol)),
            pl.BlockSpec((T_ALL, GROUP), lambda b: (b, gcol)),
            pl.BlockSpec((1, 1, GROUP), lambda b: (0, 0, 0)),
        ],
        out_specs=pl.BlockSpec((T_ALL, GROUP), lambda b: (b, 0)),
        out_shape=jax.ShapeDtypeStruct((nb * T_ALL, GROUP), F32),
        compiler_params=_cparams("parallel"),
        name="hyena_inv",
    )(p_l, p_c, p_n, g_l, g_c, u, gate, bias.reshape(1, 1, GROUP))


def hyena_mixer(slab, g_l, g_c, conv_w, f_w1, f_b1, f_w2, f_b2, f_w3, f_freq, bias):
    u = hyena_short_conv(slab, conv_w)
    spec_l = hyena_spectrum(SEQ, g_l, *hyena_filter_taps(SEQ, f_w1, f_b1, f_w2, f_b2, f_w3, f_freq))
    spec_c = hyena_spectrum(CTX_LEN, g_c, *hyena_filter_taps(CTX_LEN, f_w1, f_b1, f_w2, f_b2, f_w3, f_freq))
    p = hyena_forward_transform(u, 0, g_l, g_c, spec_l, spec_c, 0)
    z = hyena_inverse_transform(*p, g_l, g_c, u, 0, u, 1, bias[0])
    p = hyena_forward_transform(z, 0, g_l, g_c, spec_l, spec_c, 1)
    return hyena_inverse_transform(*p, g_l, g_c, z, 0, u, 2, bias[1])


def kernel(x, c, ctx, c_ctx, w_mod, b_mod, norm_w, ffn_w_gu, ffn_w_down, w_in, w_out,
           hy_conv, hy_f_w1, hy_f_b1, hy_f_w2, hy_f_b2, hy_f_w3, hy_f_freq, hy_bias,
           na_q_norm, na_k_norm, na_rpb, dn_conv, dn_a_log, dn_dt_bias, dn_norm,
           rw_mu, rw_w0, rw_w_up, rw_a0, rw_a_up, rw_g_up, rw_k_k, rw_k_a, rw_r_k, rw_ln_w, rw_ln_b):
    nb = x.shape[0]
    assert x.shape[1:] == (SEQ, D_MODEL) and ctx.shape[1:] == (CTX_LEN, D_MODEL) and nb + 1 <= 16
    s = jnp.concatenate([ctx, x], axis=1).reshape(nb * T_ALL, D_MODEL)
    cond = jnp.concatenate([c_ctx[None], c, jnp.zeros((15 - nb, D_MODEL), F32)], axis=0)
    mod = modulation_all(cond, w_mod, b_mod).reshape(DEPTH, 16, N_MOD, D_MODEL)
    g_l = dft_matrix(SEQ)
    g_c = dft_matrix(CTX_LEN)
    w_gu = ffn_w_gu.astype(BF16)
    w_down = ffn_w_down.astype(BF16)
    w_out_b = w_out.astype(BF16)
    dn_end = 6 * GROUP + 4 * GROUP + 4 * GROUP_HEADS
    w_in_p = jnp.concatenate(
        [w_in[:, :, :dn_end], jnp.zeros((DEPTH, D_MODEL, 6 * GROUP + DN_W - dn_end), F32), w_in[:, :, dn_end:]],
        axis=2).astype(BF16)
    for l in range(DEPTH):
        need_ctx = l < DEPTH - 1
        modc = mod[l, 0:1]
        modb = mod[l, 1:1 + nb]
        s = ffn_half_step(s, modc, modb, norm_w[l, 0], w_gu[l, 0], w_down[l, 0], 0)
        hy_s, na_s, dn_s, rw_s = input_projection(s, modc, modb, norm_w[l, 1], w_in_p[l])
        groups = (
            hyena_mixer(hy_s, g_l, g_c, hy_conv[l], hy_f_w1[l], hy_f_b1[l], hy_f_w2[l], hy_f_b2[l], hy_f_w3[l],
                        hy_f_freq[l], hy_bias[l]),
            na_mixer(na_s, na_q_norm[l], na_k_norm[l], na_bias_table(na_rpb[l]), need_ctx),
            deltanet_mixer(dn_s, dn_conv[l], dn_a_log[l], dn_dt_bias[l], dn_norm[l]),
            rwkv_mixer(rw_s, rw_mu[l], rw_w0[l], rw_w_up[l], rw_a0[l], rw_a_up[l], rw_g_up[l], rw_k_k[l],
                       rw_k_a[l], rw_r_k[l], rw_ln_w[l], rw_ln_b[l]),
        )
        s = output_projection(s, modc, modb, groups, w_out_b[l])
        s = ffn_half_step(s, modc, modb, norm_w[l, 2], w_gu[l, 1], w_down[l, 1], 2)
    return s.reshape(nb, T_ALL, D_MODEL)[:, CTX_LEN:]
```

```python
import functools
import math

import numpy as np
import jax
import jax.numpy as jnp
from jax import lax
from jax.experimental import pallas as pl
from jax.experimental.pallas import tpu as pltpu

D_MODEL = 1024
SEQ = 2048
DEPTH = 2
CTX_LEN = 256
T_ALL = CTX_LEN + SEQ
GRID_W = 64
GROUP = 256
HEAD_DIM = 64
GROUP_HEADS = 4
D_FF = 2816
N_MOD = 9
NORM_EPS = 1e-6

HY_ORDER = 2
HY_BANDS = 16
HY_TARGET = 1e-2
HY_SHORT_DECAY_PCT = 0.3
HY_LONG_DECAY_PCT = 1.5

NA_WIN_ROWS = 8
NA_WIN_COLS = 16

CHUNK = 64
RW_DECAY_RANK = 32
RW_AAA_RANK = 32
RW_GATE_RANK = 64
RW_LN_EPS = 64e-5

DN_W = 4 * GROUP + 128
RW_W = 3 * GROUP + 128
P_PAD = 3 * GROUP + 3 * GROUP + DN_W + RW_W

TM = 768
TF = 1408
VMEM_LIMIT = 56 * 1024 * 1024

F32 = jnp.float32
BF16 = jnp.bfloat16


def _cparams(*sem):
    return pltpu.CompilerParams(dimension_semantics=sem, vmem_limit_bytes=VMEM_LIMIT)


def _silu(x):
    return x * (1.0 / (1.0 + jnp.exp(-x)))


def _sigmoid(x):
    return 1.0 / (1.0 + jnp.exp(-x))


def _softplus(x):
    return jnp.maximum(x, 0.0) + jnp.log(1.0 + jnp.exp(-jnp.abs(x)))


def _dot(a, b):
    return jnp.dot(a, b, preferred_element_type=F32)


def _dot_nt(a, b):
    return lax.dot_general(a, b, (((1,), (1,)), ((), ())), preferred_element_type=F32)


def _dot_tn(a, b):
    return lax.dot_general(a, b, (((0,), (0,)), ((), ())), preferred_element_type=F32)


def _dot_hi(a, b):
    return jnp.dot(a, b, preferred_element_type=F32, precision=lax.Precision.HIGHEST)


def _mod_kernel(cond_ref, w_ref, b_ref, o_ref):
    a = _silu(cond_ref[...]).astype(BF16)
    o_ref[0] = _dot(a, w_ref[0].astype(BF16)) + b_ref[0]


def modulation_all(cond, w_mod, b_mod):
    r = cond.shape[0]
    tn = 1024
    return pl.pallas_call(
        _mod_kernel,
        grid=(DEPTH, N_MOD * D_MODEL // tn),
        in_specs=[
            pl.BlockSpec((r, D_MODEL), lambda l, j: (0, 0)),
            pl.BlockSpec((1, D_MODEL, tn), lambda l, j: (l, 0, j)),
            pl.BlockSpec((1, 1, tn), lambda l, j: (l, 0, j)),
        ],
        out_specs=pl.BlockSpec((1, r, tn), lambda l, j: (l, 0, j)),
        out_shape=jax.ShapeDtypeStruct((DEPTH, r, N_MOD * D_MODEL), F32),
        compiler_params=_cparams("parallel", "parallel"),
        name="modulation",
    )(cond, w_mod, b_mod.reshape(DEPTH, 1, N_MOD * D_MODEL))


def _row_mod(modc_ref, modb_ref, tile, idx):
    row = lax.broadcasted_iota(jnp.int32, (TM, 1), 0) + (tile % (T_ALL // TM)) * TM
    return jnp.where(row < CTX_LEN, modc_ref[0, idx:idx + 1, :], modb_ref[0, idx:idx + 1, :])


def _adaln(x, nw, shift, scale):
    y = x * lax.rsqrt(jnp.mean(x * x, axis=-1, keepdims=True) + NORM_EPS)
    return y * nw * (1.0 + scale) + shift


def _ffn_kernel(sub, x_ref, modc_ref, modb_ref, nw_ref, wg_ref, wu_ref, wd_ref, o_ref, h_ref, acc_ref):
    i = pl.program_id(0)
    j = pl.program_id(1)

    @pl.when(j == 0)
    def _():
        shift = _row_mod(modc_ref, modb_ref, i, 3 * sub)
        scale = _row_mod(modc_ref, modb_ref, i, 3 * sub + 1)
        h_ref[...] = _adaln(x_ref[...], nw_ref[...], shift, scale).astype(BF16)
        acc_ref[...] = jnp.zeros_like(acc_ref)

    h = h_ref[...]
    a = (_silu(_dot(h, wg_ref[...])) * _dot(h, wu_ref[...])).astype(BF16)
    acc_ref[...] += _dot(a, wd_ref[...])

    @pl.when(j == pl.num_programs(1) - 1)
    def _():
        gate = _row_mod(modc_ref, modb_ref, i, 3 * sub + 2)
        o_ref[...] = x_ref[...] + 0.5 * gate * acc_ref[...]


def ffn_half_step(x, modc, modb, nw, w_gu, w_down, sub):
    n = x.shape[0]
    tiles_per_b = T_ALL // TM
    nj = D_FF // TF
    return pl.pallas_call(
        functools.partial(_ffn_kernel, sub),
        grid=(n // TM, nj),
        in_specs=[
            pl.BlockSpec((TM, D_MODEL), lambda i, j: (i, 0)),
            pl.BlockSpec((1, N_MOD, D_MODEL), lambda i, j: (0, 0, 0)),
            pl.BlockSpec((1, N_MOD, D_MODEL), lambda i, j: (i // tiles_per_b, 0, 0)),
            pl.BlockSpec((1, D_MODEL), lambda i, j: (0, 0)),
            pl.BlockSpec((D_MODEL, TF), lambda i, j: (0, j)),
            pl.BlockSpec((D_MODEL, TF), lambda i, j: (0, nj + j)),
            pl.BlockSpec((TF, D_MODEL), lambda i, j: (j, 0)),
        ],
        out_specs=pl.BlockSpec((TM, D_MODEL), lambda i, j: (i, 0)),
        out_shape=jax.ShapeDtypeStruct((n, D_MODEL), F32),
        scratch_shapes=[pltpu.VMEM((TM, D_MODEL), BF16), pltpu.VMEM((TM, D_MODEL), F32)],
        compiler_params=_cparams("parallel", "arbitrary"),
        name=f"ffn{sub}",
    )(x, modc, modb, nw.reshape(1, D_MODEL), w_gu, w_gu, w_down)


def _inproj_kernel(x_ref, modc_ref, modb_ref, nw_ref, w_ref, hy_ref, na_ref, dn_ref, rw_ref):
    i = pl.program_id(0)
    shift = _row_mod(modc_ref, modb_ref, i, 3)
    scale = _row_mod(modc_ref, modb_ref, i, 4)
    h = _adaln(x_ref[...], nw_ref[...], shift, scale).astype(BF16)
    o0 = 3 * GROUP
    o1 = 6 * GROUP
    o2 = o1 + DN_W
    hy_ref[...] = _dot(h, w_ref[:, 0:o0])
    na_ref[...] = _dot(h, w_ref[:, o0:o1])
    dn_ref[...] = _dot(h, w_ref[:, o1:o2])
    rw_ref[...] = _dot(h, w_ref[:, o2:P_PAD])


def input_projection(x, modc, modb, nw, w_in_p):
    n = x.shape[0]
    tiles_per_b = T_ALL // TM
    widths = (3 * GROUP, 3 * GROUP, DN_W, RW_W)
    return pl.pallas_call(
        _inproj_kernel,
        grid=(n // TM,),
        in_specs=[
            pl.BlockSpec((TM, D_MODEL), lambda i: (i, 0)),
            pl.BlockSpec((1, N_MOD, D_MODEL), lambda i: (0, 0, 0)),
            pl.BlockSpec((1, N_MOD, D_MODEL), lambda i: (i // tiles_per_b, 0, 0)),
            pl.BlockSpec((1, D_MODEL), lambda i: (0, 0)),
            pl.BlockSpec((D_MODEL, P_PAD), lambda i: (0, 0)),
        ],
        out_specs=[pl.BlockSpec((TM, w), lambda i: (i, 0)) for w in widths],
        out_shape=[jax.ShapeDtypeStruct((n, w), F32) for w in widths],
        compiler_params=_cparams("parallel"),
        name="inproj",
    )(x, modc, modb, nw.reshape(1, D_MODEL), w_in_p)


def _outproj_kernel(x_ref, modc_ref, modb_ref, g0_ref, g1_ref, g2_ref, g3_ref, w_ref, o_ref):
    i = pl.program_id(0)
    y = _dot(g0_ref[...].astype(BF16), w_ref[0:GROUP, :])
    y += _dot(g1_ref[...].astype(BF16), w_ref[GROUP:2 * GROUP, :])
    y += _dot(g2_ref[...].astype(BF16), w_ref[2 * GROUP:3 * GROUP, :])
    y += _dot(g3_ref[...].astype(BF16), w_ref[3 * GROUP:4 * GROUP, :])
    gate = _row_mod(modc_ref, modb_ref, i, 5)
    o_ref[...] = x_ref[...] + gate * y


def output_projection(x, modc, modb, groups, w_out):
    n = x.shape[0]
    tiles_per_b = T_ALL // TM
    return pl.pallas_call(
        _outproj_kernel,
        grid=(n // TM,),
        in_specs=[
            pl.BlockSpec((TM, D_MODEL), lambda i: (i, 0)),
            pl.BlockSpec((1, N_MOD, D_MODEL), lambda i: (0, 0, 0)),
            pl.BlockSpec((1, N_MOD, D_MODEL), lambda i: (i // tiles_per_b, 0, 0)),
        ] + [pl.BlockSpec((TM, GROUP), lambda i: (i, 0))] * 4 + [
            pl.BlockSpec((D_MODEL, D_MODEL), lambda i: (0, 0)),
        ],
        out_specs=pl.BlockSpec((TM, D_MODEL), lambda i: (i, 0)),
        out_shape=jax.ShapeDtypeStruct((n, D_MODEL), F32),
        compiler_params=_cparams("parallel"),
        name="outproj",
    )(x, modc, modb, *groups, w_out)


def _head_mean_matrix(scale):
    r = lax.broadcasted_iota(jnp.int32, (GROUP, GROUP), 0) // HEAD_DIM
    c = lax.broadcasted_iota(jnp.int32, (GROUP, GROUP), 1) // HEAD_DIM
    return jnp.where(r == c, scale, 0.0).astype(BF16)


def _dot_split(a, m_bf16):
    hi = a.astype(BF16)
    lo = (a - hi.astype(F32)).astype(BF16)
    return _dot(hi, m_bf16) + _dot(lo, m_bf16)


def _lane_head(width=GROUP):
    return lax.broadcasted_iota(jnp.int32, (1, width), 1) // HEAD_DIM


NA_ROWS = SEQ // GRID_W
NA_LOCAL = NA_WIN_ROWS * GRID_W
NA_NEG = -1e30
NA_BLK = 256


def na_bias_table(rpb):
    n_dr = 2 * NA_WIN_ROWS
    rows = jnp.pad(rpb, ((0, 0), (0, 1), (0, 128 - rpb.shape[2]))).reshape(GROUP_HEADS * n_dr, 128)
    toep = pl.pallas_call(
        _na_bias_kernel,
        out_shape=jax.ShapeDtypeStruct((GROUP_HEADS * n_dr, GRID_W * GRID_W), F32),
        name="na_bias",
    )(rows).reshape(GROUP_HEADS, n_dr, GRID_W, GRID_W)
    tab = jnp.stack([toep[:, NA_WIN_ROWS - 1 - p:2 * NA_WIN_ROWS - 1 - p] for p in range(NA_WIN_ROWS)], axis=0)
    tab = jnp.transpose(tab, (0, 1, 3, 2, 4))
    return tab.reshape(NA_WIN_ROWS, GROUP_HEADS, GRID_W, NA_LOCAL)


def _na_bias_kernel(rpb_ref, o_ref):
    n = GRID_W * GRID_W
    d = lax.broadcasted_iota(jnp.int32, (128, n), 0)
    cj = lax.broadcasted_iota(jnp.int32, (128, n), 1)
    onehot = jnp.where((cj % GRID_W) - (cj // GRID_W) + NA_WIN_COLS - 1 == d, 1.0, 0.0).astype(BF16)
    cj1 = lax.broadcasted_iota(jnp.int32, (1, n), 1)
    c = cj1 // GRID_W
    j = cj1 % GRID_W
    start = jnp.clip(c - NA_WIN_COLS // 2, 0, GRID_W - NA_WIN_COLS)
    in_win = jnp.logical_and(j >= start, j < start + NA_WIN_COLS)
    o_ref[...] = jnp.where(in_win, _dot_exact_rhs(rpb_ref[...], onehot), NA_NEG)


def _na_kernel(need_ctx, slab_ref, qw_ref, kw_ref, bias_ref, o_ref, q_s, k_s, v_s):
    hm = _head_mean_matrix(1.0 / HEAD_DIM)
    qw = qw_ref[...] * (HEAD_DIM ** -0.5)
    kw = kw_ref[...]

    def prep(i, c):
        r0 = pl.multiple_of(i * NA_BLK, NA_BLK)
        q = slab_ref[pl.ds(r0, NA_BLK), 0:GROUP]
        k = slab_ref[pl.ds(r0, NA_BLK), GROUP:2 * GROUP]
        q_s[pl.ds(r0, NA_BLK), :] = (q * lax.rsqrt(_dot_split(q * q, hm) + NORM_EPS) * qw).astype(BF16)
        k_s[pl.ds(r0, NA_BLK), :] = (k * lax.rsqrt(_dot_split(k * k, hm) + NORM_EPS) * kw).astype(BF16)
        v_s[pl.ds(r0, NA_BLK), :] = slab_ref[pl.ds(r0, NA_BLK), 2 * GROUP:3 * GROUP].astype(BF16)
        return c

    lax.fori_loop(0, T_ALL // NA_BLK, prep, 0)

    lane_h = _lane_head()
    kc = k_s[0:CTX_LEN, :]
    vc = v_s[0:CTX_LEN, :]

    if need_ctx:
        qc = q_s[0:CTX_LEN, :]
        out = jnp.zeros((CTX_LEN, GROUP), F32)
        for h in range(GROUP_HEADS):
            mask = lane_h == h
            s = _dot_nt(jnp.where(mask, qc, jnp.zeros_like(qc)), kc)
            e = jnp.exp(s - jnp.max(s, axis=-1, keepdims=True))
            p = e * (1.0 / jnp.sum(e, axis=-1, keepdims=True))
            out = jnp.where(mask, _dot(p.astype(BF16), vc), out)
        o_ref[0:CTX_LEN, :] = out
    else:
        o_ref[0:CTX_LEN, :] = jnp.zeros((CTX_LEN, GROUP), F32)

    def row_body(r, c):
        start = jnp.clip(r - NA_WIN_ROWS // 2, 0, NA_ROWS - NA_WIN_ROWS)
        pat = r - start
        q0 = pl.multiple_of(CTX_LEN + r * GRID_W, GRID_W)
        k0 = pl.multiple_of(CTX_LEN + start * GRID_W, GRID_W)
        q = q_s[pl.ds(q0, GRID_W), :]
        kb = k_s[pl.ds(k0, NA_LOCAL), :]
        vb = v_s[pl.ds(k0, NA_LOCAL), :]
        out = jnp.zeros((GRID_W, GROUP), F32)
        for h in range(GROUP_HEADS):
            mask = lane_h == h
            qh = jnp.where(mask, q, jnp.zeros_like(q))
            s_loc = _dot_nt(qh, kb) + bias_ref[pat, h]
            s_ctx = _dot_nt(qh, kc)
            m = jnp.maximum(jnp.max(s_loc, axis=-1, keepdims=True), jnp.max(s_ctx, axis=-1, keepdims=True))
            e_loc = jnp.exp(s_loc - m)
            e_ctx = jnp.exp(s_ctx - m)
            inv = 1.0 / (jnp.sum(e_loc, axis=-1, keepdims=True) + jnp.sum(e_ctx, axis=-1, keepdims=True))
            o = _dot((e_loc * inv).astype(BF16), vb) + _dot((e_ctx * inv).astype(BF16), vc)
            out = jnp.where(mask, o, out)
        o_ref[pl.ds(q0, GRID_W), :] = out
        return c

    lax.fori_loop(0, NA_ROWS, row_body, 0)


def na_mixer(slab, q_norm, k_norm, bias_tab, need_ctx):
    n = slab.shape[0]
    tile4 = lambda w: jnp.tile(w, GROUP_HEADS).reshape(1, GROUP)
    return pl.pallas_call(
        functools.partial(_na_kernel, need_ctx),
        grid=(n // T_ALL,),
        in_specs=[
            pl.BlockSpec((T_ALL, 3 * GROUP), lambda b: (b, 0)),
            pl.BlockSpec((1, GROUP), lambda b: (0, 0)),
            pl.BlockSpec((1, GROUP), lambda b: (0, 0)),
            pl.BlockSpec((NA_WIN_ROWS, GROUP_HEADS, GRID_W, NA_LOCAL), lambda b: (0, 0, 0, 0)),
        ],
        out_specs=pl.BlockSpec((T_ALL, GROUP), lambda b: (b, 0)),
        out_shape=jax.ShapeDtypeStruct((n, GROUP), F32),
        scratch_shapes=[pltpu.VMEM((T_ALL, GROUP), BF16)] * 3,
        compiler_params=_cparams("parallel"),
        name="na_mixer",
    )(slab, tile4(q_norm), tile4(k_norm), bias_tab)


SEQ_BLK = 256
N_BLK = T_ALL // SEQ_BLK
N_CHUNK = T_ALL // CHUNK
CTX_CHUNKS = CTX_LEN // CHUNK


def _prev_cur_next(ref, i, c0, c1):
    r0 = pl.multiple_of(i * SEQ_BLK, SEQ_BLK)
    cur = ref[pl.ds(r0, SEQ_BLK), c0:c1]
    up0 = pl.multiple_of(jnp.maximum(r0 - 8, 0), 8)
    dn0 = pl.multiple_of(jnp.minimum(r0 + SEQ_BLK, T_ALL - 8), 8)
    up = ref[pl.ds(up0, 8), c0:c1][7:8, :]
    dn = ref[pl.ds(dn0, 8), c0:c1][0:1, :]
    up = jnp.where(i >= 2, up, 0.0)
    dn = jnp.where(jnp.logical_and(i >= 1, i <= N_BLK - 2), dn, 0.0)
    row = lax.broadcasted_iota(jnp.int32, (SEQ_BLK, 1), 0)
    prev = jnp.where(row == 0, up, pltpu.roll(cur, 1, 0))
    nxt = jnp.where(row == SEQ_BLK - 1, dn, pltpu.roll(cur, SEQ_BLK - 1, 0))
    return prev, cur, nxt


def _chunk_cumsum(x, reverse):
    pos = lax.broadcasted_iota(jnp.int32, (SEQ_BLK, 1), 0) % CHUNK
    s = 1
    while s < CHUNK:
        if reverse:
            x = x + jnp.where(pos < CHUNK - s, pltpu.roll(x, SEQ_BLK - s, 0), 0.0)
        else:
            x = x + jnp.where(pos >= s, pltpu.roll(x, s, 0), 0.0)
        s *= 2
    return x


def _split3(a):
    hi = a.astype(BF16)
    r1 = a - hi.astype(F32)
    mid = r1.astype(BF16)
    lo = (r1 - mid.astype(F32)).astype(BF16)
    return hi, mid, lo


def _dot_exact_rhs(a, m_bf16):
    hi, mid, lo = _split3(a)
    return _dot(hi, m_bf16) + _dot(mid, m_bf16) + _dot(lo, m_bf16)


def _expand_heads(x):
    lane_h = _lane_head()
    return jnp.concatenate([jnp.where(lane_h == h, x, 0.0) for h in range(GROUP_HEADS)], axis=0)


def _tile_heads(x):
    return jnp.concatenate([x] * GROUP_HEADS, axis=0)


def _fold_heads(x):
    return (x[0:CHUNK] + x[CHUNK:2 * CHUNK]) + (x[2 * CHUNK:3 * CHUNK] + x[3 * CHUNK:4 * CHUNK])


def _chunk_masks(reverse):
    n = GROUP_HEADS * CHUNK
    r = lax.broadcasted_iota(jnp.int32, (n, n), 0)
    c = lax.broadcasted_iota(jnp.int32, (n, n), 1)
    same = (r // CHUNK) == (c // CHUNK)
    ri = r % CHUNK
    ci = c % CHUNK
    if reverse:
        return jnp.logical_and(same, ri <= ci), jnp.logical_and(same, ri < ci)
    return jnp.logical_and(same, ri >= ci), jnp.logical_and(same, ri > ci)


def _chunk_of_step(n, reverse):
    if not reverse:
        return n
    return jnp.where(n < CTX_CHUNKS, CTX_CHUNKS - 1 - n, N_CHUNK + CTX_CHUNKS - 1 - n)


INV_BASE = 16


def _bdot(a, b):
    return _dot(a.astype(BF16), b.astype(BF16))


def _inverse_unit_triangular(mats):
    size = GROUP_HEADS * CHUNK
    r = lax.broadcasted_iota(jnp.int32, (size, size), 0)
    c = lax.broadcasted_iota(jnp.int32, (size, size), 1)
    inner = (r // INV_BASE) == (c // INV_BASE)
    eye = jnp.where(r == c, 1.0, 0.0)
    nd = [jnp.where(inner, n, 0.0) for n in mats]
    x = [eye - n for n in nd]
    p = [_bdot(n, n) for n in nd]
    k = 2
    while k < INV_BASE:
        x = [xi + _bdot(pi, xi) for xi, pi in zip(x, p)]
        k *= 2
        if k < INV_BASE:
            p = [_bdot(pi, pi) for pi in p]
    width = INV_BASE
    while width < CHUNK:
        outer = (r // (2 * width)) == (c // (2 * width))
        sel = jnp.logical_and(outer, jnp.logical_not(inner))
        t = [_bdot(jnp.where(sel, n, 0.0), xi) for n, xi in zip(mats, x)]
        x = [xi - _bdot(xi, ti) for xi, ti in zip(x, t)]
        inner = outer
        width *= 2
    return x


def _head_rows(gc, lane_onehot):
    hi, mid, lo = _split3(gc)
    t = _dot_nt(lane_onehot, hi) + _dot_nt(lane_onehot, mid) + _dot_nt(lane_onehot, lo)
    return jnp.concatenate([t[h:h + 1, :] for h in range(GROUP_HEADS)], axis=1)


def _head_cols(gc):
    return jnp.concatenate(
        [jnp.broadcast_to(gc[:, h * HEAD_DIM:h * HEAD_DIM + 1], (CHUNK, GROUP)) for h in range(GROUP_HEADS)], axis=0)


def _dn_kernel(slab_ref, conv_ref, alog_ref, dt_ref, nw_ref, o_ref, q_s, k_s, v_s, gc_s, beta_s, o_s):
    hsum = _head_mean_matrix(1.0)
    lane_h = _lane_head()
    col = lax.broadcasted_iota(jnp.int32, (128, GROUP), 0)
    lane = lax.broadcasted_iota(jnp.int32, (128, GROUP), 1) // HEAD_DIM
    neg_a = -jnp.exp(alog_ref[...])
    dtb = dt_ref[...]

    def prep(i, c):
        r0 = pl.multiple_of(i * SEQ_BLK, SEQ_BLK)
        for j, dst in enumerate((q_s, k_s, v_s)):
            prev, cur, nxt = _prev_cur_next(slab_ref, i, j * GROUP, (j + 1) * GROUP)
            w = conv_ref[:, j * GROUP:(j + 1) * GROUP]
            u = _silu(prev * w[0:1] + cur * w[1:2] + nxt * w[2:3])
            if j == 0:
                u = u * lax.rsqrt(_dot_split(u * u, hsum) + 1e-6) * (HEAD_DIM ** -0.5)
            elif j == 1:
                u = u * lax.rsqrt(_dot_split(u * u, hsum) + 1e-6)
            dst[pl.ds(r0, SEQ_BLK), :] = u
        ba = slab_ref[pl.ds(r0, SEQ_BLK), 4 * GROUP:4 * GROUP + 128]
        for d in range(2):
            e_b = jnp.where(col == 8 * d + lane, 1.0, 0.0).astype(BF16)
            e_a = jnp.where(col == 8 * d + 4 + lane, 1.0, 0.0).astype(BF16)
            beta_s[d, pl.ds(r0, SEQ_BLK), :] = _sigmoid(_dot_exact_rhs(ba, e_b))
            g = neg_a[d:d + 1] * _softplus(_dot_exact_rhs(ba, e_a) + dtb[d:d + 1])
            gc_s[d, pl.ds(r0, SEQ_BLK), :] = _chunk_cumsum(g, reverse=(d == 1))
        return c

    lax.fori_loop(0, N_BLK, prep, 0)

    onehot = jnp.where(
        lax.broadcasted_iota(jnp.int32, (8, GROUP), 1) == HEAD_DIM * lax.broadcasted_iota(jnp.int32, (8, GROUP), 0),
        1.0, 0.0).astype(BF16)

    masks = (_chunk_masks(False), _chunk_masks(True))

    def step(n, s):
        dirs = (0, 1)
        rows = [pl.ds(pl.multiple_of(_chunk_of_step(n, d == 1) * CHUNK, CHUNK), CHUNK) for d in dirs]
        gc = [gc_s[d, rows[d], :] for d in dirs]
        beta = [_tile_heads(beta_s[d, rows[d], :]) for d in dirs]
        ge = [_tile_heads(g) for g in gc]
        k_e = [_expand_heads(k_s[rows[d], :]) for d in dirs]
        q_e = [_expand_heads(q_s[rows[d], :]) for d in dirs]
        v_e = [_expand_heads(v_s[rows[d], :]) for d in dirs]
        kb_e = [k_e[d] * beta[d] for d in dirs]
        dec = []
        for d in dirs:
            incl = masks[d][0]
            delta = _head_cols(gc[d]) - _head_rows(gc[d], onehot)
            dec.append(jnp.where(incl, jnp.exp(jnp.where(incl, delta, 0.0)), 0.0))
        k_b = [k.astype(BF16) for k in k_e]
        m = [jnp.where(masks[d][1], _dot_nt(kb_e[d].astype(BF16), k_b[d]) * dec[d], 0.0) for d in dirs]
        attn = [(_dot_nt(q_e[d].astype(BF16), k_b[d]) * dec[d]).astype(BF16) for d in dirs]
        eg = [jnp.exp(g) for g in ge]
        rhs = [jnp.concatenate([v_e[d] * beta[d], kb_e[d] * eg[d]], axis=1) for d in dirs]
        x = _inverse_unit_triangular(m)
        sol = [_bdot(x[d], rhs[d]) for d in dirs]
        g_last = [gc[0][CHUNK - 1:CHUNK, :], gc[1][0:1, :]]
        s_b = [t.astype(BF16) for t in s]
        v_new = [(sol[d][:, 0:GROUP] - _dot(sol[d][:, GROUP:2 * GROUP].astype(BF16), s_b[d])).astype(BF16)
                 for d in dirs]
        o_e = [_dot((q_e[d] * eg[d]).astype(BF16), s_b[d]) + _dot(attn[d], v_new[d]) for d in dirs]
        k_dec = [(k_e[d] * jnp.exp(g_last[d] - ge[d])).astype(BF16) for d in dirs]
        for d in dirs:
            o_s[d, rows[d], :] = _fold_heads(o_e[d])
        return tuple(s[d] * jnp.exp(g_last[d]) + _dot_tn(k_dec[d], v_new[d]) for d in dirs)

    zero = jnp.zeros((GROUP, GROUP), F32)
    lax.fori_loop(0, N_CHUNK, step, (zero, zero))

    hmean = _head_mean_matrix(1.0 / HEAD_DIM)
    nw = nw_ref[...]

    def finish(i, c):
        r0 = pl.multiple_of(i * SEQ_BLK, SEQ_BLK)
        o = o_s[0, pl.ds(r0, SEQ_BLK), :] + o_s[1, pl.ds(r0, SEQ_BLK), :]
        z = slab_ref[pl.ds(r0, SEQ_BLK), 3 * GROUP:4 * GROUP]
        o_ref[pl.ds(r0, SEQ_BLK), :] = o * lax.rsqrt(_dot_split(o * o, hmean) + NORM_EPS) * nw * _silu(z)
        return c

    lax.fori_loop(0, N_BLK, finish, 0)


def deltanet_mixer(slab, conv_w, a_log, dt_bias, norm_w):
    n = slab.shape[0]
    lanes = lambda t: jnp.repeat(t, HEAD_DIM, axis=-1)
    seq = pltpu.VMEM((T_ALL, GROUP), F32)
    seq2 = pltpu.VMEM((2, T_ALL, GROUP), F32)
    return pl.pallas_call(
        _dn_kernel,
        grid=(n // T_ALL,),
        in_specs=[
            pl.BlockSpec((T_ALL, DN_W), lambda b: (b, 0)),
            pl.BlockSpec((3, 3 * GROUP), lambda b: (0, 0)),
            pl.BlockSpec((2, GROUP), lambda b: (0, 0)),
            pl.BlockSpec((2, GROUP), lambda b: (0, 0)),
            pl.BlockSpec((1, GROUP), lambda b: (0, 0)),
        ],
        out_specs=pl.BlockSpec((T_ALL, GROUP), lambda b: (b, 0)),
        out_shape=jax.ShapeDtypeStruct((n, GROUP), F32),
        scratch_shapes=[seq, seq, seq, seq2, seq2, seq2],
        compiler_params=_cparams("parallel"),
        name="deltanet",
    )(slab, conv_w, lanes(a_log), lanes(dt_bias), jnp.tile(norm_w, GROUP_HEADS).reshape(1, GROUP))


RW_LR = RW_DECAY_RANK + RW_AAA_RANK + RW_GATE_RANK
RW_LR_OUT = 5 * GROUP


def _dot3(a, b_hi, b_lo):
    a_hi = a.astype(BF16)
    a_lo = (a - a_hi.astype(F32)).astype(BF16)
    return _dot(a_hi, b_hi) + (_dot(a_lo, b_hi) + _dot(a_hi, b_lo))


def _rw_kernel(slab_ref, mu_ref, pv_ref, wlr_ref, o_ref,
               at_s, rt_s, bh_s, kh_s, gl_s, v_s, bonus_s, g_s, y_s):
    hsum = _head_mean_matrix(1.0)
    hmean = _head_mean_matrix(1.0 / HEAD_DIM)
    wlr = wlr_ref[...]
    wlr_hi = wlr.astype(BF16)
    wlr_lo = (wlr - wlr_hi.astype(F32)).astype(BF16)
    pv = pv_ref[...]
    w0 = (pv[0:1], pv[1:2])
    a0 = (pv[2:3], pv[3:4])
    k_k, k_a, r_k = pv[4:5], pv[5:6], pv[6:7]
    lr_lane = lax.broadcasted_iota(jnp.int32, (1, RW_LR), 1)

    def shifted(i, c0, c1):
        prev, cur, nxt = _prev_cur_next(slab_ref, i, c0, c1)
        return cur + mu_ref[0:1, c0:c1] * (prev - cur) + mu_ref[1:2, c0:c1] * (nxt - cur)

    def prep(i, c):
        r0 = pl.multiple_of(i * SEQ_BLK, SEQ_BLK)
        rows = pl.ds(r0, SEQ_BLK)
        r = shifted(i, 0, GROUP)
        k = shifted(i, GROUP, 2 * GROUP)
        v = shifted(i, 2 * GROUP, 3 * GROUP)
        lr = shifted(i, 3 * GROUP, 3 * GROUP + RW_LR)
        t = jnp.where(lr_lane < RW_DECAY_RANK, jnp.tanh(lr),
                      jnp.where(lr_lane < RW_DECAY_RANK + RW_AAA_RANK, lr, _sigmoid(lr)))
        proj = _dot3(t, wlr_hi, wlr_lo)
        kq = k * k_k
        kk = kq * lax.rsqrt(_dot_split(kq * kq, hsum) + 1e-6)
        v_s[rows, :] = v.astype(BF16)
        g_s[rows, :] = proj[:, 4 * GROUP:5 * GROUP]
        ksum = jnp.zeros_like(k)
        for d in range(2):
            w_log = -_softplus(-(w0[d] + proj[:, d * GROUP:(d + 1) * GROUP])) - 0.5
            lw = -jnp.exp(w_log)
            a_gate = _sigmoid(a0[d] + proj[:, (2 + d) * GROUP:(3 + d) * GROUP])
            k_d = k * (1.0 + (a_gate - 1.0) * k_a)
            ksum = ksum + k_d
            cum = _chunk_cumsum(lw, reverse=(d == 1))
            inv = jnp.exp(-cum)
            at_s[d, rows, :] = (-kk * jnp.exp(cum - lw)).astype(BF16)
            rt_s[d, rows, :] = (r * jnp.exp(cum)).astype(BF16)
            bh_s[d, rows, :] = (kk * a_gate * inv).astype(BF16)
            kh_s[d, rows, :] = (k_d * inv).astype(BF16)
            for j in range(SEQ_BLK // CHUNK):
                last = j * CHUNK if d == 1 else (j + 1) * CHUNK - 1
                gl_s[d, pl.ds(i * (SEQ_BLK // CHUNK) + j, 1), :] = cum[last:last + 1, :]
        bonus_s[rows, :] = _dot_split(r * ksum * r_k, hsum) * v
        return c

    lax.fori_loop(0, N_BLK, prep, 0)

    masks = (_chunk_masks(False), _chunk_masks(True))

    def step(n, s):
        dirs = (0, 1)
        ch = [_chunk_of_step(n, d == 1) for d in dirs]
        rows = [pl.ds(pl.multiple_of(c * CHUNK, CHUNK), CHUNK) for c in ch]
        at_e = [_expand_heads(at_s[d, rows[d], :]) for d in dirs]
        rt_e = [_expand_heads(rt_s[d, rows[d], :]) for d in dirs]
        bh_e = [_expand_heads(bh_s[d, rows[d], :]) for d in dirs]
        kh_e = [_expand_heads(kh_s[d, rows[d], :]) for d in dirs]
        v_e = [_expand_heads(v_s[rows[d], :]) for d in dirs]
        gamma = [jnp.exp(gl_s[d, pl.ds(ch[d], 1), :]) for d in dirs]
        x = _inverse_unit_triangular([jnp.where(masks[d][1], -_dot_nt(at_e[d], bh_e[d]), 0.0) for d in dirs])
        a_ak = [jnp.where(masks[d][1], _dot_nt(at_e[d], kh_e[d]), 0.0).astype(BF16) for d in dirs]
        a_rb = [jnp.where(masks[d][0], _dot_nt(rt_e[d], bh_e[d]), 0.0).astype(BF16) for d in dirs]
        a_rk = [jnp.where(masks[d][0], _dot_nt(rt_e[d], kh_e[d]), 0.0).astype(BF16) for d in dirs]
        s_b = [t.astype(BF16) for t in s]
        rhs = [(_dot_nt(at_e[d], s_b[d]) + _dot(a_ak[d], v_e[d])).astype(BF16) for d in dirs]
        sa = [_dot(x[d].astype(BF16), rhs[d]).astype(BF16) for d in dirs]
        y_e = [_dot_nt(rt_e[d], s_b[d]) + _dot(a_rb[d], sa[d]) + _dot(a_rk[d], v_e[d]) for d in dirs]
        for d in dirs:
            y_s[d, rows[d], :] = _fold_heads(y_e[d])
        bg = [(bh_e[d].astype(F32) * gamma[d]).astype(BF16) for d in dirs]
        kg = [(kh_e[d].astype(F32) * gamma[d]).astype(BF16) for d in dirs]
        return tuple(s[d] * gamma[d] + _dot_tn(sa[d], bg[d]) + _dot_tn(v_e[d], kg[d]) for d in dirs)

    zero = jnp.zeros((GROUP, GROUP), F32)
    lax.fori_loop(0, N_CHUNK, step, (zero, zero))

    ln_w, ln_b = pv[7:8], pv[8:9]

    def finish(i, c):
        rows = pl.ds(pl.multiple_of(i * SEQ_BLK, SEQ_BLK), SEQ_BLK)
        y = y_s[0, rows, :] + y_s[1, rows, :]
        yc = y - _dot_split(y, hmean)
        yn = yc * lax.rsqrt(_dot_split(yc * yc, hmean) + RW_LN_EPS) * ln_w + ln_b
        o_ref[rows, :] = (yn + bonus_s[rows, :]) * g_s[rows, :]
        return c

    lax.fori_loop(0, N_BLK, finish, 0)


def rwkv_lowrank_weights(w_up, a_up, g_up):
    w = jnp.zeros((RW_LR, RW_LR_OUT), F32)
    o1 = RW_DECAY_RANK
    o2 = o1 + RW_AAA_RANK
    for d in range(2):
        w = w.at[0:o1, d * GROUP:(d + 1) * GROUP].set(w_up[d])
        w = w.at[o1:o2, (2 + d) * GROUP:(3 + d) * GROUP].set(a_up[d])
    return w.at[o2:RW_LR, 4 * GROUP:5 * GROUP].set(g_up)


def rwkv_mixer(slab, mu, w0, w_up, a0, a_up, g_up, k_k, k_a, r_k, ln_w, ln_b):
    n = slab.shape[0]
    pv = jnp.concatenate([w0, a0, k_k[None], k_a[None], r_k.reshape(1, GROUP), ln_w[None], ln_b[None],
                          jnp.zeros((7, GROUP), F32)], axis=0)
    seq_b = pltpu.VMEM((2, T_ALL, GROUP), BF16)
    return pl.pallas_call(
        _rw_kernel,
        grid=(n // T_ALL,),
        in_specs=[
            pl.BlockSpec((T_ALL, RW_W), lambda b: (b, 0)),
            pl.BlockSpec((2, RW_W), lambda b: (0, 0)),
            pl.BlockSpec((16, GROUP), lambda b: (0, 0)),
            pl.BlockSpec((RW_LR, RW_LR_OUT), lambda b: (0, 0)),
        ],
        out_specs=pl.BlockSpec((T_ALL, GROUP), lambda b: (b, 0)),
        out_shape=jax.ShapeDtypeStruct((n, GROUP), F32),
        scratch_shapes=[seq_b, seq_b, seq_b, seq_b, pltpu.VMEM((2, 40, GROUP), F32),
                        pltpu.VMEM((T_ALL, GROUP), BF16), pltpu.VMEM((T_ALL, GROUP), F32),
                        pltpu.VMEM((T_ALL, GROUP), F32), pltpu.VMEM((2, T_ALL, GROUP), F32)],
        compiler_params=_cparams("parallel"),
        name="rwkv7",
    )(slab, mu, pv, rwkv_lowrank_weights(w_up, a_up, g_up))


DFT_SPLIT = 64
DFT_BLK = 256


def _dft_tables(n):
    big = 2 * n
    t = np.arange(n, dtype=np.int64)[:, None]
    k1 = np.arange(n // DFT_SPLIT, dtype=np.int64)[None, :]
    k2 = np.arange(DFT_SPLIT, dtype=np.int64)[None, :]
    alpha = 2.0 * np.pi * ((DFT_SPLIT * t * k1) % big) / big
    beta = 2.0 * np.pi * ((t * k2) % big) / big

    def pad(a):
        out = np.zeros((n, 128), np.float32)
        out[:, :a.shape[1]] = a
        return out

    return np.stack([pad(np.cos(alpha)), pad(np.sin(alpha)), pad(np.cos(beta)), pad(np.sin(beta))])


def _dft_gen_kernel(n, tab_ref, g_ref):
    k = lax.broadcasted_iota(jnp.int32, (128, n), 1)
    row = lax.broadcasted_iota(jnp.int32, (128, n), 0)
    e_a = jnp.where(k // DFT_SPLIT == row, 1.0, 0.0).astype(BF16)
    e_b = jnp.where(jnp.logical_and(k % DFT_SPLIT == row, row < DFT_SPLIT), 1.0, 0.0).astype(BF16)
    ca = _dot_split(tab_ref[0], e_a)
    sa = _dot_split(tab_ref[1], e_a)
    cb = _dot_split(tab_ref[2], e_b)
    sb = _dot_split(tab_ref[3], e_b)
    g_ref[:, 0:n] = (ca * cb - sa * sb).astype(BF16)
    g_ref[:, n:2 * n] = (-(sa * cb + ca * sb)).astype(BF16)


def dft_matrix(n):
    blk = min(DFT_BLK, n)
    return pl.pallas_call(
        functools.partial(_dft_gen_kernel, n),
        grid=(n // blk,),
        in_specs=[pl.BlockSpec((4, blk, 128), lambda i: (0, i, 0))],
        out_specs=pl.BlockSpec((blk, 2 * n), lambda i: (i, 0)),
        out_shape=jax.ShapeDtypeStruct((n, 2 * n), BF16),
        compiler_params=_cparams("parallel"),
        name=f"dft_matrix_{n}",
    )(jnp.asarray(_dft_tables(n)))


HY_COLS_F = 2 * HY_ORDER * GROUP
HY_OC = HY_ORDER * GROUP


def _hyena_filter_kernel(n, z_ref, w1_ref, b1_ref, w2_ref, b2_ref, w3_ref, freq_ref, dl_ref, hs_ref, hd_ref):
    blk = min(SEQ_BLK, n)
    freq = freq_ref[...]
    dl = dl_ref[...]

    def fill(i, norm):
        r0 = pl.multiple_of(i * blk, blk)
        z = z_ref[pl.ds(r0, blk), :]
        hid = jnp.sin(freq * (_dot_hi(z, w1_ref[...]) + b1_ref[...]))
        hid = jnp.sin(freq * (_dot_hi(hid, w2_ref[...]) + b2_ref[...]))
        t = z[:, 0:1]
        h = _dot_hi(hid, w3_ref[...]) * jnp.exp(-t * dl)
        lag = lax.broadcasted_iota(jnp.int32, (blk, 1), 0) + r0
        hf = h[:, 0:HY_OC]
        hb = jnp.where(lag == 0, 0.0, h[:, HY_OC:2 * HY_OC])
        hs_ref[pl.ds(r0, blk), :] = hf + hb
        hd_ref[pl.ds(r0, blk), :] = hf - hb
        return norm + jnp.sum(jnp.abs(hf) + jnp.abs(hb), axis=0, keepdims=True)

    norm = lax.fori_loop(0, n // blk, fill, jnp.zeros((1, HY_OC), F32))
    inv = 1.0 / norm

    def scale(i, c):
        rows = pl.ds(pl.multiple_of(i * blk, blk), blk)
        hs_ref[rows, :] = hs_ref[rows, :] * inv
        hd_ref[rows, :] = hd_ref[rows, :] * inv
        return c

    lax.fori_loop(0, n // blk, scale, 0)


def hyena_filter_taps(n, f_w1, f_b1, f_w2, f_b2, f_w3, f_freq):
    t = jnp.linspace(0.0, 1.0, n, dtype=F32)[:, None]
    ang = 2.0 * math.pi * jnp.arange(n, dtype=F32)[:, None] / n
    bands = jnp.linspace(1e-4, HY_BANDS - 1, HY_BANDS, dtype=F32)[None]
    z = jnp.concatenate([t, jnp.cos(bands * ang), -jnp.sin(bands * ang)], axis=-1)
    emb = z.shape[1]
    z = jnp.pad(z, ((0, 0), (0, 128 - emb)))
    w1 = jnp.pad(f_w1, ((0, 128 - emb), (0, 0)))
    max_decay = math.log(HY_TARGET) / HY_SHORT_DECAY_PCT
    min_decay = math.log(HY_TARGET) / HY_LONG_DECAY_PCT
    deltas = jnp.abs(jnp.linspace(min_decay, max_decay, HY_OC, dtype=F32))
    dl = jnp.tile(deltas, 2).reshape(1, HY_COLS_F)
    hid = f_w2.shape[0]
    out = jax.ShapeDtypeStruct((n, HY_OC), F32)
    return pl.pallas_call(
        functools.partial(_hyena_filter_kernel, n),
        out_shape=[out, out],
        compiler_params=pltpu.CompilerParams(vmem_limit_bytes=VMEM_LIMIT),
        name=f"hyena_filter_{n}",
    )(z, w1, f_b1.reshape(1, hid), f_w2, f_b2.reshape(1, hid), f_w3, f_freq.reshape(1, hid), dl)


def _hyena_spectrum_kernel(n, g_ref, hs_ref, hd_ref, kr_ref, ki_ref, kn_ref):
    blk = min(2 * SEQ_BLK, n)
    big = 2.0 * n

    def split(ref):
        x = ref[...]
        hi = x.astype(BF16)
        return hi, (x - hi.astype(F32)).astype(BF16)

    s_hi, s_lo = split(hs_ref)
    d_hi, d_lo = split(hd_ref)

    def body(i, c):
        r0 = pl.multiple_of(i * blk, blk)
        rows = pl.ds(r0, blk)
        k = lax.broadcasted_iota(jnp.int32, (blk, 1), 0) + r0
        wgt = jnp.where(k == 0, 1.0 / big, 2.0 / big)
        gc = g_ref[rows, 0:n]
        gs = g_ref[rows, n:2 * n]
        kr_ref[rows, :] = (_dot(gc, s_hi) + _dot(gc, s_lo)) * wgt
        ki_ref[rows, :] = (_dot(gs, d_hi) + _dot(gs, d_lo)) * wgt
        return c

    lax.fori_loop(0, n // blk, body, 0)
    t = lax.broadcasted_iota(jnp.int32, (n, 1), 0)
    sign = jnp.where(t % 2 == 0, 1.0, -1.0)
    kn_ref[...] = jnp.broadcast_to(jnp.sum(sign * hs_ref[...], axis=0, keepdims=True) * (1.0 / big), (8, HY_OC))


def hyena_spectrum(n, g, hs, hd):
    out = jax.ShapeDtypeStruct((n, HY_OC), F32)
    return pl.pallas_call(
        functools.partial(_hyena_spectrum_kernel, n),
        out_shape=[out, out, jax.ShapeDtypeStruct((8, HY_OC), F32)],
        compiler_params=pltpu.CompilerParams(vmem_limit_bytes=VMEM_LIMIT),
        name=f"hyena_spectrum_{n}",
    )(g, hs, hd)


def _hyena_conv_kernel(slab_ref, w_ref, o_ref):
    def body(i, c):
        rows = pl.ds(pl.multiple_of(i * SEQ_BLK, SEQ_BLK), SEQ_BLK)
        for j in range(3):
            prev, cur, nxt = _prev_cur_next(slab_ref, i, j * GROUP, (j + 1) * GROUP)
            w = w_ref[:, j * GROUP:(j + 1) * GROUP]
            o_ref[rows, j * GROUP:(j + 1) * GROUP] = prev * w[0:1] + cur * w[1:2] + nxt * w[2:3]
        return c

    lax.fori_loop(0, N_BLK, body, 0)


def hyena_short_conv(slab, conv_w):
    n = slab.shape[0]
    return pl.pallas_call(
        _hyena_conv_kernel,
        grid=(n // T_ALL,),
        in_specs=[pl.BlockSpec((T_ALL, 3 * GROUP), lambda b: (b, 0)),
                  pl.BlockSpec((3, 3 * GROUP), lambda b: (0, 0))],
        out_specs=pl.BlockSpec((T_ALL, 3 * GROUP), lambda b: (b, 0)),
        out_shape=jax.ShapeDtypeStruct((n, 3 * GROUP), F32),
        compiler_params=_cparams("parallel"),
        name="hyena_short_conv",
    )(slab, conv_w)


HY_FBLK = 512


def _alt_sign(n):
    t = lax.broadcasted_iota(jnp.int32, (n, 1), 0)
    return jnp.where(t % 2 == 0, 1.0, -1.0)


def _hyena_fwd_kernel(x_ref, gl_ref, gc_ref, krl_ref, kil_ref, knl_ref, krc_ref, kic_ref, knc_ref,
                      pl_ref, pc_ref, pn_ref):
    def transform(x, g_ref, kr_ref, ki_ref, kn_ref, p_ref, n, blk):
        xb = x.astype(BF16)

        def body(i, c):
            rows = pl.ds(pl.multiple_of(i * blk, blk), blk)
            zr = _dot(g_ref[rows, 0:n], xb)
            zi = _dot(g_ref[rows, n:2 * n], xb)
            kr = kr_ref[rows, :]
            ki = ki_ref[rows, :]
            p_ref[0, 0, rows, :] = (zr * kr - zi * ki).astype(BF16)
            p_ref[0, 1, rows, :] = (zr * ki + zi * kr).astype(BF16)
            return c

        lax.fori_loop(0, n // blk, body, 0)
        return jnp.sum(_alt_sign(n) * x, axis=0, keepdims=True) * kn_ref[0:1, :]

    nyq_c = transform(x_ref[0:CTX_LEN, :], gc_ref, krc_ref, kic_ref, knc_ref, pc_ref, CTX_LEN, CTX_LEN)
    nyq_l = transform(x_ref[CTX_LEN:T_ALL, :], gl_ref, krl_ref, kil_ref, knl_ref, pl_ref, SEQ, HY_FBLK)
    pn_ref[0] = jnp.concatenate([nyq_l, nyq_c, jnp.zeros((6, GROUP), F32)], axis=0)


def _resident(shape):
    return pl.BlockSpec(shape, lambda b: (0,) * len(shape))


def hyena_forward_transform(x, col, g_l, g_c, spec_l, spec_c, order):
    n = x.shape[0]
    nb = n // T_ALL
    kcol = lambda shape: pl.BlockSpec(shape, lambda b: (0, order))
    return pl.pallas_call(
        _hyena_fwd_kernel,
        grid=(nb,),
        in_specs=[
            pl.BlockSpec((T_ALL, GROUP), lambda b: (b, col)),
            _resident((SEQ, 2 * SEQ)), _resident((CTX_LEN, 2 * CTX_LEN)),
            kcol((SEQ, GROUP)), kcol((SEQ, GROUP)), kcol((8, GROUP)),
            kcol((CTX_LEN, GROUP)), kcol((CTX_LEN, GROUP)), kcol((8, GROUP)),
        ],
        out_specs=[
            pl.BlockSpec((1, 2, SEQ, GROUP), lambda b: (b, 0, 0, 0)),
            pl.BlockSpec((1, 2, CTX_LEN, GROUP), lambda b: (b, 0, 0, 0)),
            pl.BlockSpec((1, 8, GROUP), lambda b: (b, 0, 0)),
        ],
        out_shape=[
            jax.ShapeDtypeStruct((nb, 2, SEQ, GROUP), BF16),
            jax.ShapeDtypeStruct((nb, 2, CTX_LEN, GROUP), BF16),
            jax.ShapeDtypeStruct((nb, 8, GROUP), F32),
        ],
        compiler_params=_cparams("parallel"),
        name=f"hyena_fwd_{order}",
    )(x, g_l, g_c, *spec_l, *spec_c)


def _hyena_inv_kernel(pl_ref, pc_ref, pn_ref, gl_ref, gc_ref, u_ref, gate_ref, bias_ref, o_ref):
    bias = bias_ref[0]

    def inverse(p_ref, nyq, g_ref, n, blk, off):
        pr = p_ref[0, 0]
        pi = p_ref[0, 1]

        def body(i, c):
            r0 = pl.multiple_of(i * blk, blk)
            rows = pl.ds(r0, blk)
            orow = pl.ds(pl.multiple_of(off + r0, blk), blk)
            t = lax.broadcasted_iota(jnp.int32, (blk, 1), 0)
            sign = jnp.where(t % 2 == 0, 1.0, -1.0)
            y = _dot(g_ref[rows, 0:n], pr) + _dot(g_ref[rows, n:2 * n], pi) + sign * nyq
            o_ref[orow, :] = gate_ref[orow, :] * (y + u_ref[orow, :] * bias)
            return c

        lax.fori_loop(0, n // blk, body, 0)

    inverse(pc_ref, pn_ref[0, 1:2, :], gc_ref, CTX_LEN, CTX_LEN, 0)
    inverse(pl_ref, pn_ref[0, 0:1, :], gl_ref, SEQ, SEQ_BLK, CTX_LEN)


def hyena_inverse_transform(p_l, p_c, p_n, g_l, g_c, u, ucol, gate, gcol, bias):
    nb = p_l.shape[0]
    return pl.pallas_call(
        _hyena_inv_kernel,
        grid=(nb,),
        in_specs=[
            pl.BlockSpec((1, 2, SEQ, GROUP), lambda b: (b, 0, 0, 0)),
            pl.BlockSpec((1, 2, CTX_LEN, GROUP), lambda b: (b, 0, 0, 0)),
            pl.BlockSpec((1, 8, GROUP), lambda b: (b, 0, 0)),
            _resident((SEQ, 2 * SEQ)), _resident((CTX_LEN, 2 * CTX_LEN)),
            pl.BlockSpec((T_ALL, GROUP), lambda b: (b, ucol)),
            pl.BlockSpec((T_ALL, GROUP), lambda b: (b, gcol)),
            pl.BlockSpec((1, 1, GROUP), lambda b: (0, 0, 0)),
        ],
        out_specs=pl.BlockSpec((T_ALL, GROUP), lambda b: (b, 0)),
        out_shape=jax.ShapeDtypeStruct((nb * T_ALL, GROUP), F32),
        compiler_params=_cparams("parallel"),
        name="hyena_inv",
    )(p_l, p_c, p_n, g_l, g_c, u, gate, bias.reshape(1, 1, GROUP))


def hyena_mixer(slab, g_l, g_c, conv_w, f_w1, f_b1, f_w2, f_b2, f_w3, f_freq, bias):
    u = hyena_short_conv(slab, conv_w)
    spec_l = hyena_spectrum(SEQ, g_l, *hyena_filter_taps(SEQ, f_w1, f_b1, f_w2, f_b2, f_w3, f_freq))
    spec_c = hyena_spectrum(CTX_LEN, g_c, *hyena_filter_taps(CTX_LEN, f_w1, f_b1, f_w2, f_b2, f_w3, f_freq))
    p = hyena_forward_transform(u, 0, g_l, g_c, spec_l, spec_c, 0)
    z = hyena_inverse_transform(*p, g_l, g_c, u, 0, u, 1, bias[0])
    p = hyena_forward_transform(z, 0, g_l, g_c, spec_l, spec_c, 1)
    return hyena_inverse_transform(*p, g_l, g_c, z, 0, u, 2, bias[1])


def kernel(x, c, ctx, c_ctx, w_mod, b_mod, norm_w, ffn_w_gu, ffn_w_down, w_in, w_out,
           hy_conv, hy_f_w1, hy_f_b1, hy_f_w2, hy_f_b2, hy_f_w3, hy_f_freq, hy_bias,
           na_q_norm, na_k_norm, na_rpb, dn_conv, dn_a_log, dn_dt_bias, dn_norm,
           rw_mu, rw_w0, rw_w_up, rw_a0, rw_a_up, rw_g_up, rw_k_k, rw_k_a, rw_r_k, rw_ln_w, rw_ln_b):
    nb = x.shape[0]
    assert x.shape[1:] == (SEQ, D_MODEL) and ctx.shape[1:] == (CTX_LEN, D_MODEL) and nb + 1 <= 16
    s = jnp.concatenate([ctx, x], axis=1).reshape(nb * T_ALL, D_MODEL)
    cond = jnp.concatenate([c_ctx[None], c, jnp.zeros((15 - nb, D_MODEL), F32)], axis=0)
    mod = modulation_all(cond, w_mod, b_mod).reshape(DEPTH, 16, N_MOD, D_MODEL)
    g_l = dft_matrix(SEQ)
    g_c = dft_matrix(CTX_LEN)
    w_gu = ffn_w_gu.astype(BF16)
    w_down = ffn_w_down.astype(BF16)
    w_out_b = w_out.astype(BF16)
    dn_end = 6 * GROUP + 4 * GROUP + 4 * GROUP_HEADS
    w_in_p = jnp.concatenate(
        [w_in[:, :, :dn_end], jnp.zeros((DEPTH, D_MODEL, 6 * GROUP + DN_W - dn_end), F32), w_in[:, :, dn_end:]],
        axis=2).astype(BF16)
    for l in range(DEPTH):
        need_ctx = l < DEPTH - 1
        modc = mod[l, 0:1]
        modb = mod[l, 1:1 + nb]
        s = ffn_half_step(s, modc, modb, norm_w[l, 0], w_gu[l, 0], w_down[l, 0], 0)
        hy_s, na_s, dn_s, rw_s = input_projection(s, modc, modb, norm_w[l, 1], w_in_p[l])
        groups = (
            hyena_mixer(hy_s, g_l, g_c, hy_conv[l], hy_f_w1[l], hy_f_b1[l], hy_f_w2[l], hy_f_b2[l], hy_f_w3[l],
                        hy_f_freq[l], hy_bias[l]),
            na_mixer(na_s, na_q_norm[l], na_k_norm[l], na_bias_table(na_rpb[l]), need_ctx),
            deltanet_mixer(dn_s, dn_conv[l], dn_a_log[l], dn_dt_bias[l], dn_norm[l]),
            rwkv_mixer(rw_s, rw_mu[l], rw_w0[l], rw_w_up[l], rw_a0[l], rw_a_up[l], rw_g_up[l], rw_k_k[l],
                       rw_k_a[l], rw_r_k[l], rw_ln_w[l], rw_ln_b[l]),
        )
        s = output_projection(s, modc, modb, groups, w_out_b[l])
        s = ffn_half_step(s, modc, modb, norm_w[l, 2], w_gu[l, 1], w_down[l, 1], 2)
    return s.reshape(nb, T_ALL, D_MODEL)[:, CTX_LEN:]
```

```python
import functools
import math

import numpy as np
import jax
import jax.numpy as jnp
from jax import lax
from jax.experimental import pallas as pl
from jax.experimental.pallas import tpu as pltpu

D_MODEL = 1024
SEQ = 2048
DEPTH = 2
CTX_LEN = 256
T_ALL = CTX_LEN + SEQ
GRID_W = 64
GROUP = 256
HEAD_DIM = 64
GROUP_HEADS = 4
D_FF = 2816
N_MOD = 9
NORM_EPS = 1e-6

HY_ORDER = 2
HY_BANDS = 16
HY_TARGET = 1e-2
HY_SHORT_DECAY_PCT = 0.3
HY_LONG_DECAY_PCT = 1.5

NA_WIN_ROWS = 8
NA_WIN_COLS = 16

CHUNK = 64
RW_DECAY_RANK = 32
RW_AAA_RANK = 32
RW_GATE_RANK = 64
RW_LN_EPS = 64e-5

DN_W = 4 * GROUP + 128
RW_W = 3 * GROUP + 128
P_PAD = 3 * GROUP + 3 * GROUP + DN_W + RW_W

TM = 768
TF = 1408
VMEM_LIMIT = 56 * 1024 * 1024

F32 = jnp.float32
BF16 = jnp.bfloat16


def _cparams(*sem):
    return pltpu.CompilerParams(dimension_semantics=sem, vmem_limit_bytes=VMEM_LIMIT)


def _silu(x):
    return x * (1.0 / (1.0 + jnp.exp(-x)))


def _sigmoid(x):
    return 1.0 / (1.0 + jnp.exp(-x))


def _softplus(x):
    return jnp.maximum(x, 0.0) + jnp.log(1.0 + jnp.exp(-jnp.abs(x)))


def _dot(a, b):
    return jnp.dot(a, b, preferred_element_type=F32)


def _dot_nt(a, b):
    return lax.dot_general(a, b, (((1,), (1,)), ((), ())), preferred_element_type=F32)


def _dot_tn(a, b):
    return lax.dot_general(a, b, (((0,), (0,)), ((), ())), preferred_element_type=F32)


def _dot_hi(a, b):
    return jnp.dot(a, b, preferred_element_type=F32, precision=lax.Precision.HIGHEST)


def _mod_kernel(cond_ref, w_ref, b_ref, o_ref):
    a = _silu(cond_ref[...]).astype(BF16)
    o_ref[0] = _dot(a, w_ref[0].astype(BF16)) + b_ref[0]


def modulation_all(cond, w_mod, b_mod):
    r = cond.shape[0]
    tn = 1024
    return pl.pallas_call(
        _mod_kernel,
        grid=(DEPTH, N_MOD * D_MODEL // tn),
        in_specs=[
            pl.BlockSpec((r, D_MODEL), lambda l, j: (0, 0)),
            pl.BlockSpec((1, D_MODEL, tn), lambda l, j: (l, 0, j)),
            pl.BlockSpec((1, 1, tn), lambda l, j: (l, 0, j)),
        ],
        out_specs=pl.BlockSpec((1, r, tn), lambda l, j: (l, 0, j)),
        out_shape=jax.ShapeDtypeStruct((DEPTH, r, N_MOD * D_MODEL), F32),
        compiler_params=_cparams("parallel", "parallel"),
        name="modulation",
    )(cond, w_mod, b_mod.reshape(DEPTH, 1, N_MOD * D_MODEL))


def _row_mod(modc_ref, modb_ref, tile, idx):
    row = lax.broadcasted_iota(jnp.int32, (TM, 1), 0) + (tile % (T_ALL // TM)) * TM
    return jnp.where(row < CTX_LEN, modc_ref[0, idx:idx + 1, :], modb_ref[0, idx:idx + 1, :])


def _adaln(x, nw, shift, scale):
    y = x * lax.rsqrt(jnp.mean(x * x, axis=-1, keepdims=True) + NORM_EPS)
    return y * nw * (1.0 + scale) + shift


def _ffn_kernel(sub, x_ref, modc_ref, modb_ref, nw_ref, wg_ref, wu_ref, wd_ref, o_ref, h_ref, acc_ref):
    i = pl.program_id(0)
    j = pl.program_id(1)

    @pl.when(j == 0)
    def _():
        shift = _row_mod(modc_ref, modb_ref, i, 3 * sub)
        scale = _row_mod(modc_ref, modb_ref, i, 3 * sub + 1)
        h_ref[...] = _adaln(x_ref[...], nw_ref[...], shift, scale).astype(BF16)
        acc_ref[...] = jnp.zeros_like(acc_ref)

    h = h_ref[...]
    a = (_silu(_dot(h, wg_ref[...])) * _dot(h, wu_ref[...])).astype(BF16)
    acc_ref[...] += _dot(a, wd_ref[...])

    @pl.when(j == pl.num_programs(1) - 1)
    def _():
        gate = _row_mod(modc_ref, modb_ref, i, 3 * sub + 2)
        o_ref[...] = x_ref[...] + 0.5 * gate * acc_ref[...]


def ffn_half_step(x, modc, modb, nw, w_gu, w_down, sub):
    n = x.shape[0]
    tiles_per_b = T_ALL // TM
    nj = D_FF // TF
    return pl.pallas_call(
        functools.partial(_ffn_kernel, sub),
        grid=(n // TM, nj),
        in_specs=[
            pl.BlockSpec((TM, D_MODEL), lambda i, j: (i, 0)),
            pl.BlockSpec((1, N_MOD, D_MODEL), lambda i, j: (0, 0, 0)),
            pl.BlockSpec((1, N_MOD, D_MODEL), lambda i, j: (i // tiles_per_b, 0, 0)),
            pl.BlockSpec((1, D_MODEL), lambda i, j: (0, 0)),
            pl.BlockSpec((D_MODEL, TF), lambda i, j: (0, j)),
            pl.BlockSpec((D_MODEL, TF), lambda i, j: (0, nj + j)),
            pl.BlockSpec((TF, D_MODEL), lambda i, j: (j, 0)),
        ],
        out_specs=pl.BlockSpec((TM, D_MODEL), lambda i, j: (i, 0)),
        out_shape=jax.ShapeDtypeStruct((n, D_MODEL), F32),
        scratch_shapes=[pltpu.VMEM((TM, D_MODEL), BF16), pltpu.VMEM((TM, D_MODEL), F32)],
        compiler_params=_cparams("parallel", "arbitrary"),
        name=f"ffn{sub}",
    )(x, modc, modb, nw.reshape(1, D_MODEL), w_gu, w_gu, w_down)


def _inproj_kernel(x_ref, modc_ref, modb_ref, nw_ref, w_ref, hy_ref, na_ref, dn_ref, rw_ref):
    i = pl.program_id(0)
    shift = _row_mod(modc_ref, modb_ref, i, 3)
    scale = _row_mod(modc_ref, modb_ref, i, 4)
    h = _adaln(x_ref[...], nw_ref[...], shift, scale).astype(BF16)
    o0 = 3 * GROUP
    o1 = 6 * GROUP
    o2 = o1 + DN_W
    hy_ref[...] = _dot(h, w_ref[:, 0:o0])
    na_ref[...] = _dot(h, w_ref[:, o0:o1])
    dn_ref[...] = _dot(h, w_ref[:, o1:o2])
    rw_ref[...] = _dot(h, w_ref[:, o2:P_PAD])


def input_projection(x, modc, modb, nw, w_in_p):
    n = x.shape[0]
    tiles_per_b = T_ALL // TM
    widths = (3 * GROUP, 3 * GROUP, DN_W, RW_W)
    return pl.pallas_call(
        _inproj_kernel,
        grid=(n // TM,),
        in_specs=[
            pl.BlockSpec((TM, D_MODEL), lambda i: (i, 0)),
            pl.BlockSpec((1, N_MOD, D_MODEL), lambda i: (0, 0, 0)),
            pl.BlockSpec((1, N_MOD, D_MODEL), lambda i: (i // tiles_per_b, 0, 0)),
            pl.BlockSpec((1, D_MODEL), lambda i: (0, 0)),
            pl.BlockSpec((D_MODEL, P_PAD), lambda i: (0, 0)),
        ],
        out_specs=[pl.BlockSpec((TM, w), lambda i: (i, 0)) for w in widths],
        out_shape=[jax.ShapeDtypeStruct((n, w), F32) for w in widths],
        compiler_params=_cparams("parallel"),
        name="inproj",
    )(x, modc, modb, nw.reshape(1, D_MODEL), w_in_p)


def _outproj_kernel(x_ref, modc_ref, modb_ref, g0_ref, g1_ref, g2_ref, g3_ref, w_ref, o_ref):
    i = pl.program_id(0)
    y = _dot(g0_ref[...].astype(BF16), w_ref[0:GROUP, :])
    y += _dot(g1_ref[...].astype(BF16), w_ref[GROUP:2 * GROUP, :])
    y += _dot(g2_ref[...].astype(BF16), w_ref[2 * GROUP:3 * GROUP, :])
    y += _dot(g3_ref[...].astype(BF16), w_ref[3 * GROUP:4 * GROUP, :])
    gate = _row_mod(modc_ref, modb_ref, i, 5)
    o_ref[...] = x_ref[...] + gate * y


def output_projection(x, modc, modb, groups, w_out):
    n = x.shape[0]
    tiles_per_b = T_ALL // TM
    return pl.pallas_call(
        _outproj_kernel,
        grid=(n // TM,),
        in_specs=[
            pl.BlockSpec((TM, D_MODEL), lambda i: (i, 0)),
            pl.BlockSpec((1, N_MOD, D_MODEL), lambda i: (0, 0, 0)),
            pl.BlockSpec((1, N_MOD, D_MODEL), lambda i: (i // tiles_per_b, 0, 0)),
        ] + [pl.BlockSpec((TM, GROUP), lambda i: (i, 0))] * 4 + [
            pl.BlockSpec((D_MODEL, D_MODEL), lambda i: (0, 0)),
        ],
        out_specs=pl.BlockSpec((TM, D_MODEL), lambda i: (i, 0)),
        out_shape=jax.ShapeDtypeStruct((n, D_MODEL), F32),
        compiler_params=_cparams("parallel"),
        name="outproj",
    )(x, modc, modb, *groups, w_out)


def _head_mean_matrix(scale):
    r = lax.broadcasted_iota(jnp.int32, (GROUP, GROUP), 0) // HEAD_DIM
    c = lax.broadcasted_iota(jnp.int32, (GROUP, GROUP), 1) // HEAD_DIM
    return jnp.where(r == c, scale, 0.0).astype(BF16)


def _dot_split(a, m_bf16):
    hi = a.astype(BF16)
    lo = (a - hi.astype(F32)).astype(BF16)
    return _dot(hi, m_bf16) + _dot(lo, m_bf16)


def _lane_head(width=GROUP):
    return lax.broadcasted_iota(jnp.int32, (1, width), 1) // HEAD_DIM


NA_ROWS = SEQ // GRID_W
NA_LOCAL = NA_WIN_ROWS * GRID_W
NA_NEG = -1e30
NA_BLK = 256
NA_PAIR = 2


def na_bias_table(rpb):
    n_dr = 2 * NA_WIN_ROWS
    rows = jnp.pad(rpb, ((0, 0), (0, 1), (0, 128 - rpb.shape[2]))).reshape(GROUP_HEADS * n_dr, 128)
    toep = pl.pallas_call(
        _na_bias_kernel,
        out_shape=jax.ShapeDtypeStruct((GROUP_HEADS * n_dr, GRID_W * GRID_W), F32),
        name="na_bias",
    )(rows).reshape(GROUP_HEADS, n_dr, GRID_W, GRID_W)
    tab = jnp.stack([toep[:, NA_WIN_ROWS - 1 - p:2 * NA_WIN_ROWS - 1 - p] for p in range(NA_WIN_ROWS)], axis=0)
    tab = jnp.transpose(tab, (0, 1, 3, 2, 4))
    return tab.reshape(NA_WIN_ROWS, GROUP_HEADS, GRID_W, NA_LOCAL)


def _na_bias_kernel(rpb_ref, o_ref):
    n = GRID_W * GRID_W
    d = lax.broadcasted_iota(jnp.int32, (128, n), 0)
    cj = lax.broadcasted_iota(jnp.int32, (128, n), 1)
    onehot = jnp.where((cj % GRID_W) - (cj // GRID_W) + NA_WIN_COLS - 1 == d, 1.0, 0.0).astype(BF16)
    cj1 = lax.broadcasted_iota(jnp.int32, (1, n), 1)
    c = cj1 // GRID_W
    j = cj1 % GRID_W
    start = jnp.clip(c - NA_WIN_COLS // 2, 0, GRID_W - NA_WIN_COLS)
    in_win = jnp.logical_and(j >= start, j < start + NA_WIN_COLS)
    o_ref[...] = jnp.where(in_win, _dot_exact_rhs(rpb_ref[...], onehot), NA_NEG)


def _na_kernel(need_ctx, slab_ref, qw_ref, kw_ref, bias_ref, o_ref, q_s, k_s, v_s):
    hm = _head_mean_matrix(1.0 / HEAD_DIM)
    qw = qw_ref[...] * (HEAD_DIM ** -0.5)
    kw = kw_ref[...]

    def prep(i, c):
        r0 = pl.multiple_of(i * NA_BLK, NA_BLK)
        q = slab_ref[pl.ds(r0, NA_BLK), 0:GROUP]
        k = slab_ref[pl.ds(r0, NA_BLK), GROUP:2 * GROUP]
        q_s[pl.ds(r0, NA_BLK), :] = (q * lax.rsqrt(_dot_split(q * q, hm) + NORM_EPS) * qw).astype(BF16)
        k_s[pl.ds(r0, NA_BLK), :] = (k * lax.rsqrt(_dot_split(k * k, hm) + NORM_EPS) * kw).astype(BF16)
        v_s[pl.ds(r0, NA_BLK), :] = slab_ref[pl.ds(r0, NA_BLK), 2 * GROUP:3 * GROUP].astype(BF16)
        return c

    lax.fori_loop(0, T_ALL // NA_BLK, prep, 0)

    lane_h = _lane_head()
    kc = k_s[0:CTX_LEN, :]
    vc = v_s[0:CTX_LEN, :]

    if need_ctx:
        qc = q_s[0:CTX_LEN, :]
        out = jnp.zeros((CTX_LEN, GROUP), F32)
        for h in range(GROUP_HEADS):
            mask = lane_h == h
            s = _dot_nt(jnp.where(mask, qc, jnp.zeros_like(qc)), kc)
            e = jnp.exp(s - jnp.max(s, axis=-1, keepdims=True))
            p = e * (1.0 / jnp.sum(e, axis=-1, keepdims=True))
            out = jnp.where(mask, _dot(p.astype(BF16), vc), out)
        o_ref[0:CTX_LEN, :] = out
    else:
        o_ref[0:CTX_LEN, :] = jnp.zeros((CTX_LEN, GROUP), F32)

    def pair_body(i, c):
        rows = [i * NA_PAIR + t for t in range(NA_PAIR)]
        start = [jnp.clip(r - NA_WIN_ROWS // 2, 0, NA_ROWS - NA_WIN_ROWS) for r in rows]
        q0 = [pl.multiple_of(CTX_LEN + r * GRID_W, GRID_W) for r in rows]
        k0 = [pl.multiple_of(CTX_LEN + s * GRID_W, GRID_W) for s in start]
        q = [_expand_heads(q_s[pl.ds(a, GRID_W), :]) for a in q0]
        kb = [k_s[pl.ds(a, NA_LOCAL), :] for a in k0]
        vb = [v_s[pl.ds(a, NA_LOCAL), :] for a in k0]
        bias = [bias_ref[r - s].reshape(GROUP_HEADS * GRID_W, NA_LOCAL) for r, s in zip(rows, start)]
        s_loc = [_dot_nt(q[t], kb[t]) + bias[t] for t in range(NA_PAIR)]
        s_ctx = [_dot_nt(q[t], kc) for t in range(NA_PAIR)]
        m = [jnp.maximum(jnp.max(a, axis=-1, keepdims=True), jnp.max(b, axis=-1, keepdims=True))
             for a, b in zip(s_loc, s_ctx)]
        e_loc = [jnp.exp(a - mm) for a, mm in zip(s_loc, m)]
        e_ctx = [jnp.exp(b - mm) for b, mm in zip(s_ctx, m)]
        inv = [1.0 / (jnp.sum(a, axis=-1, keepdims=True) + jnp.sum(b, axis=-1, keepdims=True))
               for a, b in zip(e_loc, e_ctx)]
        o = [_dot((e_loc[t] * inv[t]).astype(BF16), vb[t]) + _dot((e_ctx[t] * inv[t]).astype(BF16), vc)
             for t in range(NA_PAIR)]
        for t in range(NA_PAIR):
            out = o[t][0:GRID_W]
            for h in range(1, GROUP_HEADS):
                out = jnp.where(lane_h == h, o[t][h * GRID_W:(h + 1) * GRID_W], out)
            o_ref[pl.ds(q0[t], GRID_W), :] = out
        return c

    lax.fori_loop(0, NA_ROWS // NA_PAIR, pair_body, 0)


def na_mixer(slab, q_norm, k_norm, bias_tab, need_ctx):
    n = slab.shape[0]
    tile4 = lambda w: jnp.tile(w, GROUP_HEADS).reshape(1, GROUP)
    return pl.pallas_call(
        functools.partial(_na_kernel, need_ctx),
        grid=(n // T_ALL,),
        in_specs=[
            pl.BlockSpec((T_ALL, 3 * GROUP), lambda b: (b, 0)),
            pl.BlockSpec((1, GROUP), lambda b: (0, 0)),
            pl.BlockSpec((1, GROUP), lambda b: (0, 0)),
            pl.BlockSpec((NA_WIN_ROWS, GROUP_HEADS, GRID_W, NA_LOCAL), lambda b: (0, 0, 0, 0)),
        ],
        out_specs=pl.BlockSpec((T_ALL, GROUP), lambda b: (b, 0)),
        out_shape=jax.ShapeDtypeStruct((n, GROUP), F32),
        scratch_shapes=[pltpu.VMEM((T_ALL, GROUP), BF16)] * 3,
        compiler_params=_cparams("parallel"),
        name="na_mixer",
    )(slab, tile4(q_norm), tile4(k_norm), bias_tab)


SEQ_BLK = 256
N_BLK = T_ALL // SEQ_BLK
N_CHUNK = T_ALL // CHUNK
CTX_CHUNKS = CTX_LEN // CHUNK


def _prev_cur_next(ref, i, c0, c1):
    r0 = pl.multiple_of(i * SEQ_BLK, SEQ_BLK)
    cur = ref[pl.ds(r0, SEQ_BLK), c0:c1]
    up0 = pl.multiple_of(jnp.maximum(r0 - 8, 0), 8)
    dn0 = pl.multiple_of(jnp.minimum(r0 + SEQ_BLK, T_ALL - 8), 8)
    up = ref[pl.ds(up0, 8), c0:c1][7:8, :]
    dn = ref[pl.ds(dn0, 8), c0:c1][0:1, :]
    up = jnp.where(i >= 2, up, 0.0)
    dn = jnp.where(jnp.logical_and(i >= 1, i <= N_BLK - 2), dn, 0.0)
    row = lax.broadcasted_iota(jnp.int32, (SEQ_BLK, 1), 0)
    prev = jnp.where(row == 0, up, pltpu.roll(cur, 1, 0))
    nxt = jnp.where(row == SEQ_BLK - 1, dn, pltpu.roll(cur, SEQ_BLK - 1, 0))
    return prev, cur, nxt


def _chunk_cumsum(x, reverse):
    pos = lax.broadcasted_iota(jnp.int32, (SEQ_BLK, 1), 0) % CHUNK
    s = 1
    while s < CHUNK:
        if reverse:
            x = x + jnp.where(pos < CHUNK - s, pltpu.roll(x, SEQ_BLK - s, 0), 0.0)
        else:
            x = x + jnp.where(pos >= s, pltpu.roll(x, s, 0), 0.0)
        s *= 2
    return x


def _split3(a):
    hi = a.astype(BF16)
    r1 = a - hi.astype(F32)
    mid = r1.astype(BF16)
    lo = (r1 - mid.astype(F32)).astype(BF16)
    return hi, mid, lo


def _dot_exact_rhs(a, m_bf16):
    hi, mid, lo = _split3(a)
    return _dot(hi, m_bf16) + _dot(mid, m_bf16) + _dot(lo, m_bf16)


def _expand_heads(x):
    lane_h = _lane_head()
    return jnp.concatenate([jnp.where(lane_h == h, x, 0.0) for h in range(GROUP_HEADS)], axis=0)


def _tile_heads(x):
    return jnp.concatenate([x] * GROUP_HEADS, axis=0)


def _fold_heads(x):
    return (x[0:CHUNK] + x[CHUNK:2 * CHUNK]) + (x[2 * CHUNK:3 * CHUNK] + x[3 * CHUNK:4 * CHUNK])


def _chunk_masks(reverse):
    n = GROUP_HEADS * CHUNK
    r = lax.broadcasted_iota(jnp.int32, (n, n), 0)
    c = lax.broadcasted_iota(jnp.int32, (n, n), 1)
    same = (r // CHUNK) == (c // CHUNK)
    ri = r % CHUNK
    ci = c % CHUNK
    if reverse:
        return jnp.logical_and(same, ri <= ci), jnp.logical_and(same, ri < ci)
    return jnp.logical_and(same, ri >= ci), jnp.logical_and(same, ri > ci)


def _chunk_of_step(n, reverse):
    if not reverse:
        return n
    return jnp.where(n < CTX_CHUNKS, CTX_CHUNKS - 1 - n, N_CHUNK + CTX_CHUNKS - 1 - n)


INV_BASE = 16


def _bdot(a, b):
    return _dot(a.astype(BF16), b.astype(BF16))


def _inverse_unit_triangular(mats):
    size = GROUP_HEADS * CHUNK
    r = lax.broadcasted_iota(jnp.int32, (size, size), 0)
    c = lax.broadcasted_iota(jnp.int32, (size, size), 1)
    inner = (r // INV_BASE) == (c // INV_BASE)
    eye = jnp.where(r == c, 1.0, 0.0)
    nd = [jnp.where(inner, n, 0.0) for n in mats]
    x = [eye - n for n in nd]
    p = [_bdot(n, n) for n in nd]
    k = 2
    while k < INV_BASE:
        x = [xi + _bdot(pi, xi) for xi, pi in zip(x, p)]
        k *= 2
        if k < INV_BASE:
            p = [_bdot(pi, pi) for pi in p]
    width = INV_BASE
    while width < CHUNK:
        outer = (r // (2 * width)) == (c // (2 * width))
        sel = jnp.logical_and(outer, jnp.logical_not(inner))
        t = [_bdot(jnp.where(sel, n, 0.0), xi) for n, xi in zip(mats, x)]
        x = [xi - _bdot(xi, ti) for xi, ti in zip(x, t)]
        inner = outer
        width *= 2
    return x


def _head_rows(gc, lane_onehot):
    hi, mid, lo = _split3(gc)
    t = _dot_nt(lane_onehot, hi) + _dot_nt(lane_onehot, mid) + _dot_nt(lane_onehot, lo)
    return jnp.concatenate([t[h:h + 1, :] for h in range(GROUP_HEADS)], axis=1)


def _head_cols(gc):
    return jnp.concatenate(
        [jnp.broadcast_to(gc[:, h * HEAD_DIM:h * HEAD_DIM + 1], (CHUNK, GROUP)) for h in range(GROUP_HEADS)], axis=0)


def _dn_kernel(slab_ref, conv_ref, alog_ref, dt_ref, nw_ref, o_ref, q_s, k_s, v_s, gc_s, beta_s, o_s):
    hsum = _head_mean_matrix(1.0)
    lane_h = _lane_head()
    col = lax.broadcasted_iota(jnp.int32, (128, GROUP), 0)
    lane = lax.broadcasted_iota(jnp.int32, (128, GROUP), 1) // HEAD_DIM
    neg_a = -jnp.exp(alog_ref[...])
    dtb = dt_ref[...]

    def prep(i, c):
        r0 = pl.multiple_of(i * SEQ_BLK, SEQ_BLK)
        for j, dst in enumerate((q_s, k_s, v_s)):
            prev, cur, nxt = _prev_cur_next(slab_ref, i, j * GROUP, (j + 1) * GROUP)
            w = conv_ref[:, j * GROUP:(j + 1) * GROUP]
            u = _silu(prev * w[0:1] + cur * w[1:2] + nxt * w[2:3])
            if j == 0:
                u = u * lax.rsqrt(_dot_split(u * u, hsum) + 1e-6) * (HEAD_DIM ** -0.5)
            elif j == 1:
                u = u * lax.rsqrt(_dot_split(u * u, hsum) + 1e-6)
            dst[pl.ds(r0, SEQ_BLK), :] = u
        ba = slab_ref[pl.ds(r0, SEQ_BLK), 4 * GROUP:4 * GROUP + 128]
        for d in range(2):
            e_b = jnp.where(col == 8 * d + lane, 1.0, 0.0).astype(BF16)
            e_a = jnp.where(col == 8 * d + 4 + lane, 1.0, 0.0).astype(BF16)
            beta_s[d, pl.ds(r0, SEQ_BLK), :] = _sigmoid(_dot_exact_rhs(ba, e_b))
            g = neg_a[d:d + 1] * _softplus(_dot_exact_rhs(ba, e_a) + dtb[d:d + 1])
            gc_s[d, pl.ds(r0, SEQ_BLK), :] = _chunk_cumsum(g, reverse=(d == 1))
        return c

    lax.fori_loop(0, N_BLK, prep, 0)

    onehot = jnp.where(
        lax.broadcasted_iota(jnp.int32, (8, GROUP), 1) == HEAD_DIM * lax.broadcasted_iota(jnp.int32, (8, GROUP), 0),
        1.0, 0.0).astype(BF16)

    masks = (_chunk_masks(False), _chunk_masks(True))

    def step(n, s):
        dirs = (0, 1)
        rows = [pl.ds(pl.multiple_of(_chunk_of_step(n, d == 1) * CHUNK, CHUNK), CHUNK) for d in dirs]
        gc = [gc_s[d, rows[d], :] for d in dirs]
        beta = [_tile_heads(beta_s[d, rows[d], :]) for d in dirs]
        ge = [_tile_heads(g) for g in gc]
        k_e = [_expand_heads(k_s[rows[d], :]) for d in dirs]
        q_e = [_expand_heads(q_s[rows[d], :]) for d in dirs]
        v_e = [_expand_heads(v_s[rows[d], :]) for d in dirs]
        kb_e = [k_e[d] * beta[d] for d in dirs]
        dec = []
        for d in dirs:
            incl = masks[d][0]
            delta = _head_cols(gc[d]) - _head_rows(gc[d], onehot)
            dec.append(jnp.where(incl, jnp.exp(jnp.where(incl, delta, 0.0)), 0.0))
        k_b = [k.astype(BF16) for k in k_e]
        m = [jnp.where(masks[d][1], _dot_nt(kb_e[d].astype(BF16), k_b[d]) * dec[d], 0.0) for d in dirs]
        attn = [(_dot_nt(q_e[d].astype(BF16), k_b[d]) * dec[d]).astype(BF16) for d in dirs]
        eg = [jnp.exp(g) for g in ge]
        rhs = [jnp.concatenate([v_e[d] * beta[d], kb_e[d] * eg[d]], axis=1) for d in dirs]
        x = _inverse_unit_triangular(m)
        sol = [_bdot(x[d], rhs[d]) for d in dirs]
        g_last = [gc[0][CHUNK - 1:CHUNK, :], gc[1][0:1, :]]
        s_b = [t.astype(BF16) for t in s]
        v_new = [(sol[d][:, 0:GROUP] - _dot(sol[d][:, GROUP:2 * GROUP].astype(BF16), s_b[d])).astype(BF16)
                 for d in dirs]
        o_e = [_dot((q_e[d] * eg[d]).astype(BF16), s_b[d]) + _dot(attn[d], v_new[d]) for d in dirs]
        k_dec = [(k_e[d] * jnp.exp(g_last[d] - ge[d])).astype(BF16) for d in dirs]
        for d in dirs:
            o_s[d, rows[d], :] = _fold_heads(o_e[d])
        return tuple(s[d] * jnp.exp(g_last[d]) + _dot_tn(k_dec[d], v_new[d]) for d in dirs)

    zero = jnp.zeros((GROUP, GROUP), F32)
    lax.fori_loop(0, N_CHUNK, step, (zero, zero))

    hmean = _head_mean_matrix(1.0 / HEAD_DIM)
    nw = nw_ref[...]

    def finish(i, c):
        r0 = pl.multiple_of(i * SEQ_BLK, SEQ_BLK)
        o = o_s[0, pl.ds(r0, SEQ_BLK), :] + o_s[1, pl.ds(r0, SEQ_BLK), :]
        z = slab_ref[pl.ds(r0, SEQ_BLK), 3 * GROUP:4 * GROUP]
        o_ref[pl.ds(r0, SEQ_BLK), :] = o * lax.rsqrt(_dot_split(o * o, hmean) + NORM_EPS) * nw * _silu(z)
        return c

    lax.fori_loop(0, N_BLK, finish, 0)


def deltanet_mixer(slab, conv_w, a_log, dt_bias, norm_w):
    n = slab.shape[0]
    lanes = lambda t: jnp.repeat(t, HEAD_DIM, axis=-1)
    seq = pltpu.VMEM((T_ALL, GROUP), F32)
    seq2 = pltpu.VMEM((2, T_ALL, GROUP), F32)
    return pl.pallas_call(
        _dn_kernel,
        grid=(n // T_ALL,),
        in_specs=[
            pl.BlockSpec((T_ALL, DN_W), lambda b: (b, 0)),
            pl.BlockSpec((3, 3 * GROUP), lambda b: (0, 0)),
            pl.BlockSpec((2, GROUP), lambda b: (0, 0)),
            pl.BlockSpec((2, GROUP), lambda b: (0, 0)),
            pl.BlockSpec((1, GROUP), lambda b: (0, 0)),
        ],
        out_specs=pl.BlockSpec((T_ALL, GROUP), lambda b: (b, 0)),
        out_shape=jax.ShapeDtypeStruct((n, GROUP), F32),
        scratch_shapes=[seq, seq, seq, seq2, seq2, seq2],
        compiler_params=_cparams("parallel"),
        name="deltanet",
    )(slab, conv_w, lanes(a_log), lanes(dt_bias), jnp.tile(norm_w, GROUP_HEADS).reshape(1, GROUP))


INTRA_CHUNKS = 4
INTER_BATCH = 4
GL_ROWS = 8


def _dn_prep_kernel(slab_ref, conv_ref, alog_ref, dt_ref, q_ref, k_ref, v_ref, gc_ref, beta_ref):
    hsum = _head_mean_matrix(1.0)
    col = lax.broadcasted_iota(jnp.int32, (128, GROUP), 0)
    lane = lax.broadcasted_iota(jnp.int32, (128, GROUP), 1) // HEAD_DIM
    neg_a = -jnp.exp(alog_ref[...])
    dtb = dt_ref[...]

    def prep(i, c):
        rows = pl.ds(pl.multiple_of(i * SEQ_BLK, SEQ_BLK), SEQ_BLK)
        for j, dst in enumerate((q_ref, k_ref, v_ref)):
            prev, cur, nxt = _prev_cur_next(slab_ref, i, j * GROUP, (j + 1) * GROUP)
            w = conv_ref[:, j * GROUP:(j + 1) * GROUP]
            u = _silu(prev * w[0:1] + cur * w[1:2] + nxt * w[2:3])
            if j == 0:
                u = u * lax.rsqrt(_dot_split(u * u, hsum) + 1e-6) * (HEAD_DIM ** -0.5)
            elif j == 1:
                u = u * lax.rsqrt(_dot_split(u * u, hsum) + 1e-6)
            dst[0] = u
        ba = slab_ref[rows, 4 * GROUP:4 * GROUP + 128]
        for d in range(2):
            e_b = jnp.where(col == 8 * d + lane, 1.0, 0.0).astype(BF16)
            e_a = jnp.where(col == 8 * d + 4 + lane, 1.0, 0.0).astype(BF16)
            beta_ref[d, 0] = _sigmoid(_dot_exact_rhs(ba, e_b))
            g = neg_a[d:d + 1] * _softplus(_dot_exact_rhs(ba, e_a) + dtb[d:d + 1])
            gc_ref[d, 0] = _chunk_cumsum(g, reverse=(d == 1))
        return c

    prep(pl.program_id(1), 0)


def _dn_intra_kernel(q_ref, k_ref, v_ref, gc_ref, beta_ref, u_ref, w_ref, attn_ref, qd_ref, kd_ref, gl_ref):
    onehot = jnp.where(
        lax.broadcasted_iota(jnp.int32, (8, GROUP), 1) == HEAD_DIM * lax.broadcasted_iota(jnp.int32, (8, GROUP), 0),
        1.0, 0.0).astype(BF16)
    masks = (_chunk_masks(False), _chunk_masks(True))
    chains = [(j, d) for j in range(INTRA_CHUNKS) for d in range(2)]
    rows = [pl.ds(j * CHUNK, CHUNK) for j, d in chains]
    gc = [gc_ref[d, 0, r, :] for (j, d), r in zip(chains, rows)]
    beta = [beta_ref[d, 0, r, :] for (j, d), r in zip(chains, rows)]
    q = [q_ref[0, r, :] for r in rows]
    k = [k_ref[0, r, :] for r in rows]
    v = [v_ref[0, r, :] for r in rows]
    eg = [jnp.exp(g) for g in gc]
    g_last = [g[0:1, :] if d == 1 else g[CHUNK - 1:CHUNK, :] for (j, d), g in zip(chains, gc)]
    kb = [a * b for a, b in zip(k, beta)]
    k_e = [_expand_heads(a).astype(BF16) for a in k]
    dec = []
    for (j, d), g in zip(chains, gc):
        incl = masks[d][0]
        delta = _head_cols(g) - _head_rows(g, onehot)
        dec.append(jnp.where(incl, jnp.exp(jnp.where(incl, delta, 0.0)), 0.0))
    m = [jnp.where(masks[d][1], _dot_nt(_expand_heads(a).astype(BF16), ke) * dc, 0.0)
         for (j, d), a, ke, dc in zip(chains, kb, k_e, dec)]
    attn = [_dot_nt(_expand_heads(a).astype(BF16), ke) * dc for a, ke, dc in zip(q, k_e, dec)]
    rhs = [jnp.concatenate([_expand_heads(a * b), _expand_heads(c * e)], axis=1)
           for a, b, c, e in zip(v, beta, kb, eg)]
    x = _inverse_unit_triangular(m)
    sol = [_fold_heads(_bdot(xi, r)) for xi, r in zip(x, rhs)]
    for i, ((j, d), r) in enumerate(zip(chains, rows)):
        u_ref[d, 0, r, :] = sol[i][:, 0:GROUP]
        w_ref[d, 0, r, :] = sol[i][:, GROUP:2 * GROUP].astype(BF16)
        attn_ref[d, 0, r, :] = _fold_heads(attn[i]).astype(BF16)
        qd_ref[d, 0, r, :] = (q[i] * eg[i]).astype(BF16)
        kd_ref[d, 0, r, :] = (k[i] * jnp.exp(g_last[i] - gc[i])).astype(BF16)
        gl_ref[d, 0, pl.ds(j * GL_ROWS, GL_ROWS), :] = jnp.broadcast_to(jnp.exp(g_last[i]), (GL_ROWS, GROUP))


def _same_head_mask():
    r = lax.broadcasted_iota(jnp.int32, (GROUP, GROUP), 0) // HEAD_DIM
    c = lax.broadcasted_iota(jnp.int32, (GROUP, GROUP), 1) // HEAD_DIM
    return r == c


def _dn_inter_kernel(uf, wf, af, qf, kf, gf, ub, wb, ab, qb, kb, gb, of_ref, ob_ref, s_ref):
    @pl.when(pl.program_id(1) == 0)
    def _():
        s_ref[...] = jnp.zeros_like(s_ref)

    same = _same_head_mask()
    ins = ((uf, wf, af, qf, kf, gf, of_ref), (ub, wb, ab, qb, kb, gb, ob_ref))
    chains = [(d, j) for j in range(INTER_BATCH) for d in range(2)]
    s = [s_ref[d * INTER_BATCH + j] for d, j in chains]
    s_b = [t.astype(BF16) for t in s]
    v_new = [(ins[d][0][0, j] - _dot(ins[d][1][0, j], sb)).astype(BF16) for (d, j), sb in zip(chains, s_b)]
    v_ne = [_expand_heads(t) for t in v_new]
    o = [_dot(ins[d][3][0, j], sb) + _dot(ins[d][2][0, j], ve) for (d, j), sb, ve in zip(chains, s_b, v_ne)]
    upd = [_dot_tn(ins[d][4][0, j], t) for (d, j), t in zip(chains, v_new)]
    for i, (d, j) in enumerate(chains):
        ins[d][6][j] = o[i]
        s_ref[d * INTER_BATCH + j] = s[i] * ins[d][5][0, j, 0:1, :] + jnp.where(same, upd[i], 0.0)


def _dn_finish_kernel(of_ref, ob_ref, z_ref, nw_ref, o_ref):
    hmean = _head_mean_matrix(1.0 / HEAD_DIM)
    nw = nw_ref[...]

    def finish(i, c):
        rows = pl.ds(pl.multiple_of(i * SEQ_BLK, SEQ_BLK), SEQ_BLK)
        o = of_ref[0, rows, :] + ob_ref[0, rows, :]
        o_ref[rows, :] = o * lax.rsqrt(_dot_split(o * o, hmean) + NORM_EPS) * nw * _silu(z_ref[rows, :])
        return c

    lax.fori_loop(0, N_BLK, finish, 0)


def _bwd_chunk(n):
    return _chunk_of_step(n, True)


def deltanet_mixer(slab, conv_w, a_log, dt_bias, norm_w):
    nb = slab.shape[0] // T_ALL
    assert nb % INTER_BATCH == 0 and N_CHUNK % INTRA_CHUNKS == 0
    lanes = lambda t: jnp.repeat(t, HEAD_DIM, axis=-1)
    seq = lambda dt: jax.ShapeDtypeStruct((nb, T_ALL, GROUP), dt)
    seq2 = lambda dt: jax.ShapeDtypeStruct((2, nb, T_ALL, GROUP), dt)
    b1 = pl.BlockSpec((1, T_ALL, GROUP), lambda b: (b, 0, 0))
    p1 = pl.BlockSpec((1, SEQ_BLK, GROUP), lambda b, i: (b, i, 0))
    p2 = pl.BlockSpec((2, 1, SEQ_BLK, GROUP), lambda b, i: (0, b, i, 0))
    small = lambda shape: pl.BlockSpec(shape, lambda b, i: (0, 0))
    q, k, v, gc, beta = pl.pallas_call(
        _dn_prep_kernel,
        grid=(nb, N_BLK),
        in_specs=[pl.BlockSpec((T_ALL, DN_W), lambda b, i: (b, 0)), small((3, 3 * GROUP)),
                  small((2, GROUP)), small((2, GROUP))],
        out_specs=[p1, p1, p1, p2, p2],
        out_shape=[seq(F32), seq(F32), seq(F32), seq2(F32), seq2(F32)],
        compiler_params=_cparams("parallel", "parallel"),
        name="deltanet_prep",
    )(slab, conv_w, lanes(a_log), lanes(dt_bias))

    rows = INTRA_CHUNKS * CHUNK
    c1 = pl.BlockSpec((1, rows, GROUP), lambda b, g: (b, g, 0))
    c2 = pl.BlockSpec((2, 1, rows, GROUP), lambda b, g: (0, b, g, 0))
    gl_shape = jax.ShapeDtypeStruct((2, nb, N_CHUNK * GL_ROWS, GROUP), F32)
    u, w, attn, qd, kd, gl = pl.pallas_call(
        _dn_intra_kernel,
        grid=(nb, N_CHUNK // INTRA_CHUNKS),
        in_specs=[c1, c1, c1, c2, c2],
        out_specs=[c2, c2, c2, c2, c2,
                   pl.BlockSpec((2, 1, INTRA_CHUNKS * GL_ROWS, GROUP), lambda b, g: (0, b, g, 0))],
        out_shape=[seq2(F32), seq2(BF16), seq2(BF16), seq2(BF16), seq2(BF16), gl_shape],
        compiler_params=_cparams("parallel", "parallel"),
        name="deltanet_intra",
    )(q, k, v, gc, beta)

    def per_dir(d, rows_per_chunk):
        pick = (lambda n: n) if d == 0 else _bwd_chunk
        return pl.BlockSpec((1, INTER_BATCH, rows_per_chunk, GROUP), lambda b, n: (d, b, pick(n), 0))

    def out_dir(d):
        pick = (lambda n: n) if d == 0 else _bwd_chunk
        return pl.BlockSpec((INTER_BATCH, CHUNK, GROUP), lambda b, n: (b, pick(n), 0))

    specs = [per_dir(d, r) for d in range(2) for r in (CHUNK,) * 5 + (GL_ROWS,)]
    o_f, o_b = pl.pallas_call(
        _dn_inter_kernel,
        grid=(nb // INTER_BATCH, N_CHUNK),
        in_specs=specs,
        out_specs=[out_dir(0), out_dir(1)],
        out_shape=[seq(F32), seq(F32)],
        scratch_shapes=[pltpu.VMEM((2 * INTER_BATCH, GROUP, GROUP), F32)],
        compiler_params=_cparams("parallel", "arbitrary"),
        name="deltanet_inter",
    )(u, w, attn, qd, kd, gl, u, w, attn, qd, kd, gl)

    return pl.pallas_call(
        _dn_finish_kernel,
        grid=(nb,),
        in_specs=[b1, b1, pl.BlockSpec((T_ALL, GROUP), lambda b: (b, 3)), _resident((1, GROUP))],
        out_specs=pl.BlockSpec((T_ALL, GROUP), lambda b: (b, 0)),
        out_shape=jax.ShapeDtypeStruct((nb * T_ALL, GROUP), F32),
        compiler_params=_cparams("parallel"),
        name="deltanet_finish",
    )(o_f, o_b, slab, jnp.tile(norm_w, GROUP_HEADS).reshape(1, GROUP))


RW_LR = RW_DECAY_RANK + RW_AAA_RANK + RW_GATE_RANK
RW_LR_OUT = 5 * GROUP


def _dot3(a, b_hi, b_lo):
    a_hi = a.astype(BF16)
    a_lo = (a - a_hi.astype(F32)).astype(BF16)
    return _dot(a_hi, b_hi) + (_dot(a_lo, b_hi) + _dot(a_hi, b_lo))


def _rw_kernel(slab_ref, mu_ref, pv_ref, wlr_ref, o_ref,
               at_s, rt_s, bh_s, kh_s, gl_s, v_s, bonus_s, g_s, y_s):
    hsum = _head_mean_matrix(1.0)
    hmean = _head_mean_matrix(1.0 / HEAD_DIM)
    wlr = wlr_ref[...]
    wlr_hi = wlr.astype(BF16)
    wlr_lo = (wlr - wlr_hi.astype(F32)).astype(BF16)
    pv = pv_ref[...]
    w0 = (pv[0:1], pv[1:2])
    a0 = (pv[2:3], pv[3:4])
    k_k, k_a, r_k = pv[4:5], pv[5:6], pv[6:7]
    lr_lane = lax.broadcasted_iota(jnp.int32, (1, RW_LR), 1)

    def shifted(i, c0, c1):
        prev, cur, nxt = _prev_cur_next(slab_ref, i, c0, c1)
        return cur + mu_ref[0:1, c0:c1] * (prev - cur) + mu_ref[1:2, c0:c1] * (nxt - cur)

    def prep(i, c):
        r0 = pl.multiple_of(i * SEQ_BLK, SEQ_BLK)
        rows = pl.ds(r0, SEQ_BLK)
        r = shifted(i, 0, GROUP)
        k = shifted(i, GROUP, 2 * GROUP)
        v = shifted(i, 2 * GROUP, 3 * GROUP)
        lr = shifted(i, 3 * GROUP, 3 * GROUP + RW_LR)
        t = jnp.where(lr_lane < RW_DECAY_RANK, jnp.tanh(lr),
                      jnp.where(lr_lane < RW_DECAY_RANK + RW_AAA_RANK, lr, _sigmoid(lr)))
        proj = _dot3(t, wlr_hi, wlr_lo)
        kq = k * k_k
        kk = kq * lax.rsqrt(_dot_split(kq * kq, hsum) + 1e-6)
        v_s[rows, :] = v.astype(BF16)
        g_s[rows, :] = proj[:, 4 * GROUP:5 * GROUP]
        ksum = jnp.zeros_like(k)
        for d in range(2):
            w_log = -_softplus(-(w0[d] + proj[:, d * GROUP:(d + 1) * GROUP])) - 0.5
            lw = -jnp.exp(w_log)
            a_gate = _sigmoid(a0[d] + proj[:, (2 + d) * GROUP:(3 + d) * GROUP])
            k_d = k * (1.0 + (a_gate - 1.0) * k_a)
            ksum = ksum + k_d
            cum = _chunk_cumsum(lw, reverse=(d == 1))
            inv = jnp.exp(-cum)
            at_s[d, rows, :] = (-kk * jnp.exp(cum - lw)).astype(BF16)
            rt_s[d, rows, :] = (r * jnp.exp(cum)).astype(BF16)
            bh_s[d, rows, :] = (kk * a_gate * inv).astype(BF16)
            kh_s[d, rows, :] = (k_d * inv).astype(BF16)
            for j in range(SEQ_BLK // CHUNK):
                last = j * CHUNK if d == 1 else (j + 1) * CHUNK - 1
                gl_s[d, pl.ds(i * (SEQ_BLK // CHUNK) + j, 1), :] = cum[last:last + 1, :]
        bonus_s[rows, :] = _dot_split(r * ksum * r_k, hsum) * v
        return c

    lax.fori_loop(0, N_BLK, prep, 0)

    masks = (_chunk_masks(False), _chunk_masks(True))

    def step(n, s):
        dirs = (0, 1)
        ch = [_chunk_of_step(n, d == 1) for d in dirs]
        rows = [pl.ds(pl.multiple_of(c * CHUNK, CHUNK), CHUNK) for c in ch]
        at_e = [_expand_heads(at_s[d, rows[d], :]) for d in dirs]
        rt_e = [_expand_heads(rt_s[d, rows[d], :]) for d in dirs]
        bh_e = [_expand_heads(bh_s[d, rows[d], :]) for d in dirs]
        kh_e = [_expand_heads(kh_s[d, rows[d], :]) for d in dirs]
        v_e = [_expand_heads(v_s[rows[d], :]) for d in dirs]
        gamma = [jnp.exp(gl_s[d, pl.ds(ch[d], 1), :]) for d in dirs]
        x = _inverse_unit_triangular([jnp.where(masks[d][1], -_dot_nt(at_e[d], bh_e[d]), 0.0) for d in dirs])
        a_ak = [jnp.where(masks[d][1], _dot_nt(at_e[d], kh_e[d]), 0.0).astype(BF16) for d in dirs]
        a_rb = [jnp.where(masks[d][0], _dot_nt(rt_e[d], bh_e[d]), 0.0).astype(BF16) for d in dirs]
        a_rk = [jnp.where(masks[d][0], _dot_nt(rt_e[d], kh_e[d]), 0.0).astype(BF16) for d in dirs]
        s_b = [t.astype(BF16) for t in s]
        rhs = [(_dot_nt(at_e[d], s_b[d]) + _dot(a_ak[d], v_e[d])).astype(BF16) for d in dirs]
        sa = [_dot(x[d].astype(BF16), rhs[d]).astype(BF16) for d in dirs]
        y_e = [_dot_nt(rt_e[d], s_b[d]) + _dot(a_rb[d], sa[d]) + _dot(a_rk[d], v_e[d]) for d in dirs]
        for d in dirs:
            y_s[d, rows[d], :] = _fold_heads(y_e[d])
        bg = [(bh_e[d].astype(F32) * gamma[d]).astype(BF16) for d in dirs]
        kg = [(kh_e[d].astype(F32) * gamma[d]).astype(BF16) for d in dirs]
        return tuple(s[d] * gamma[d] + _dot_tn(sa[d], bg[d]) + _dot_tn(v_e[d], kg[d]) for d in dirs)

    zero = jnp.zeros((GROUP, GROUP), F32)
    lax.fori_loop(0, N_CHUNK, step, (zero, zero))

    ln_w, ln_b = pv[7:8], pv[8:9]

    def finish(i, c):
        rows = pl.ds(pl.multiple_of(i * SEQ_BLK, SEQ_BLK), SEQ_BLK)
        y = y_s[0, rows, :] + y_s[1, rows, :]
        yc = y - _dot_split(y, hmean)
        yn = yc * lax.rsqrt(_dot_split(yc * yc, hmean) + RW_LN_EPS) * ln_w + ln_b
        o_ref[rows, :] = (yn + bonus_s[rows, :]) * g_s[rows, :]
        return c

    lax.fori_loop(0, N_BLK, finish, 0)


def rwkv_lowrank_weights(w_up, a_up, g_up):
    w = jnp.zeros((RW_LR, RW_LR_OUT), F32)
    o1 = RW_DECAY_RANK
    o2 = o1 + RW_AAA_RANK
    for d in range(2):
        w = w.at[0:o1, d * GROUP:(d + 1) * GROUP].set(w_up[d])
        w = w.at[o1:o2, (2 + d) * GROUP:(3 + d) * GROUP].set(a_up[d])
    return w.at[o2:RW_LR, 4 * GROUP:5 * GROUP].set(g_up)


def rwkv_mixer(slab, mu, w0, w_up, a0, a_up, g_up, k_k, k_a, r_k, ln_w, ln_b):
    n = slab.shape[0]
    pv = jnp.concatenate([w0, a0, k_k[None], k_a[None], r_k.reshape(1, GROUP), ln_w[None], ln_b[None],
                          jnp.zeros((7, GROUP), F32)], axis=0)
    seq_b = pltpu.VMEM((2, T_ALL, GROUP), BF16)
    return pl.pallas_call(
        _rw_kernel,
        grid=(n // T_ALL,),
        in_specs=[
            pl.BlockSpec((T_ALL, RW_W), lambda b: (b, 0)),
            pl.BlockSpec((2, RW_W), lambda b: (0, 0)),
            pl.BlockSpec((16, GROUP), lambda b: (0, 0)),
            pl.BlockSpec((RW_LR, RW_LR_OUT), lambda b: (0, 0)),
        ],
        out_specs=pl.BlockSpec((T_ALL, GROUP), lambda b: (b, 0)),
        out_shape=jax.ShapeDtypeStruct((n, GROUP), F32),
        scratch_shapes=[seq_b, seq_b, seq_b, seq_b, pltpu.VMEM((2, 40, GROUP), F32),
                        pltpu.VMEM((T_ALL, GROUP), BF16), pltpu.VMEM((T_ALL, GROUP), F32),
                        pltpu.VMEM((T_ALL, GROUP), F32), pltpu.VMEM((2, T_ALL, GROUP), F32)],
        compiler_params=_cparams("parallel"),
        name="rwkv7",
    )(slab, mu, pv, rwkv_lowrank_weights(w_up, a_up, g_up))


def _rw_prep_kernel(slab_ref, mu_ref, pv_ref, wlr_ref,
                    at_ref, rt_ref, bh_ref, kh_ref, bg_ref, kg_ref, gl_ref, v_ref, bonus_ref, g_ref):
    hsum = _head_mean_matrix(1.0)
    wlr = wlr_ref[...]
    wlr_hi = wlr.astype(BF16)
    wlr_lo = (wlr - wlr_hi.astype(F32)).astype(BF16)
    pv = pv_ref[...]
    w0 = (pv[0:1], pv[1:2])
    a0 = (pv[2:3], pv[3:4])
    k_k, k_a, r_k = pv[4:5], pv[5:6], pv[6:7]
    lr_lane = lax.broadcasted_iota(jnp.int32, (1, RW_LR), 1)
    per_blk = SEQ_BLK // CHUNK

    def shifted(i, c0, c1):
        prev, cur, nxt = _prev_cur_next(slab_ref, i, c0, c1)
        return cur + mu_ref[0:1, c0:c1] * (prev - cur) + mu_ref[1:2, c0:c1] * (nxt - cur)

    def prep(i, c):
        rows = pl.ds(pl.multiple_of(i * SEQ_BLK, SEQ_BLK), SEQ_BLK)
        r = shifted(i, 0, GROUP)
        k = shifted(i, GROUP, 2 * GROUP)
        v = shifted(i, 2 * GROUP, 3 * GROUP)
        lr = shifted(i, 3 * GROUP, 3 * GROUP + RW_LR)
        t = jnp.where(lr_lane < RW_DECAY_RANK, jnp.tanh(lr),
                      jnp.where(lr_lane < RW_DECAY_RANK + RW_AAA_RANK, lr, _sigmoid(lr)))
        proj = _dot3(t, wlr_hi, wlr_lo)
        kq = k * k_k
        kk = kq * lax.rsqrt(_dot_split(kq * kq, hsum) + 1e-6)
        v_ref[0] = v.astype(BF16)
        g_ref[...] = proj[:, 4 * GROUP:5 * GROUP]
        ksum = jnp.zeros_like(k)
        for d in range(2):
            w_log = -_softplus(-(w0[d] + proj[:, d * GROUP:(d + 1) * GROUP])) - 0.5
            lw = -jnp.exp(w_log)
            a_gate = _sigmoid(a0[d] + proj[:, (2 + d) * GROUP:(3 + d) * GROUP])
            k_d = k * (1.0 + (a_gate - 1.0) * k_a)
            ksum = ksum + k_d
            cum = _chunk_cumsum(lw, reverse=(d == 1))
            ends = [cum[j * CHUNK:j * CHUNK + 1, :] if d == 1 else cum[(j + 1) * CHUNK - 1:(j + 1) * CHUNK, :]
                    for j in range(per_blk)]
            to_end = jnp.exp(jnp.concatenate([jnp.broadcast_to(e, (CHUNK, GROUP)) for e in ends], axis=0) - cum)
            inv = jnp.exp(-cum)
            b = kk * a_gate
            at_ref[d, 0] = (-kk * jnp.exp(cum - lw)).astype(BF16)
            rt_ref[d, 0] = (r * jnp.exp(cum)).astype(BF16)
            bh_ref[d, 0] = (b * inv).astype(BF16)
            kh_ref[d, 0] = (k_d * inv).astype(BF16)
            bg_ref[d, 0] = (b * to_end).astype(BF16)
            kg_ref[d, 0] = (k_d * to_end).astype(BF16)
            gl_ref[d, 0] = jnp.concatenate([jnp.broadcast_to(jnp.exp(e), (GL_ROWS, GROUP)) for e in ends], axis=0)
        bonus_ref[...] = _dot_split(r * ksum * r_k, hsum) * v
        return c

    prep(pl.program_id(1), 0)


def _rw_intra_kernel(at_ref, rt_ref, bh_ref, kh_ref, v_ref, xc_ref, arb_ref, rhs0_ref, yk_ref):
    masks = (_chunk_masks(False), _chunk_masks(True))
    chains = [(j, d) for j in range(INTRA_CHUNKS) for d in range(2)]
    rows = [pl.ds(j * CHUNK, CHUNK) for j, d in chains]
    at_e = [_expand_heads(at_ref[d, 0, r, :]) for (j, d), r in zip(chains, rows)]
    rt_e = [_expand_heads(rt_ref[d, 0, r, :]) for (j, d), r in zip(chains, rows)]
    bh_e = [_expand_heads(bh_ref[d, 0, r, :]) for (j, d), r in zip(chains, rows)]
    kh_e = [_expand_heads(kh_ref[d, 0, r, :]) for (j, d), r in zip(chains, rows)]
    v_e = [_expand_heads(v_ref[0, r, :]) for r in rows]
    x = _inverse_unit_triangular(
        [jnp.where(masks[d][1], -_dot_nt(a, b), 0.0) for (j, d), a, b in zip(chains, at_e, bh_e)])
    a_ak = [_fold_heads(jnp.where(masks[d][1], _dot_nt(a, b), 0.0)).astype(BF16)
            for (j, d), a, b in zip(chains, at_e, kh_e)]
    a_rb = [_fold_heads(jnp.where(masks[d][0], _dot_nt(a, b), 0.0)).astype(BF16)
            for (j, d), a, b in zip(chains, rt_e, bh_e)]
    a_rk = [_fold_heads(jnp.where(masks[d][0], _dot_nt(a, b), 0.0)).astype(BF16)
            for (j, d), a, b in zip(chains, rt_e, kh_e)]
    rhs0 = [_dot(a, ve) for a, ve in zip(a_ak, v_e)]
    yk = [_dot(a, ve) for a, ve in zip(a_rk, v_e)]
    for i, ((j, d), r) in enumerate(zip(chains, rows)):
        xc_ref[d, 0, r, :] = _fold_heads(x[i]).astype(BF16)
        arb_ref[d, 0, r, :] = a_rb[i]
        rhs0_ref[d, 0, r, :] = rhs0[i]
        yk_ref[d, 0, r, :] = yk[i]


def _rw_inter_kernel(*refs):
    n_in = 10
    fwd, bwd = refs[0:n_in], refs[n_in:2 * n_in]
    yf_ref, yb_ref, s_ref = refs[2 * n_in:]

    @pl.when(pl.program_id(1) == 0)
    def _():
        s_ref[...] = jnp.zeros_like(s_ref)

    same = _same_head_mask()
    ins = (fwd, bwd)
    outs = (yf_ref, yb_ref)
    chains = [(d, j) for j in range(INTER_BATCH) for d in range(2)]

    def arg(idx):
        return [ins[d][idx][0, j] for d, j in chains]

    at, rt, bg, kg, xc, arb, rhs0, yk = (arg(i) for i in range(8))
    gamma = [ins[d][8][0, j, 0:1, :] for d, j in chains]
    v = [ins[d][9][j] for d, j in chains]
    s = [s_ref[d * INTER_BATCH + j] for d, j in chains]
    s_b = [t.astype(BF16) for t in s]
    rhs = [_expand_heads((_dot_nt(a, sb) + r0).astype(BF16)) for a, sb, r0 in zip(at, s_b, rhs0)]
    sa = [_dot(x, r).astype(BF16) for x, r in zip(xc, rhs)]
    sa_e = [_expand_heads(t) for t in sa]
    y = [_dot_nt(r, sb) + _dot(a, se) + y0 for r, sb, a, se, y0 in zip(rt, s_b, arb, sa_e, yk)]
    upd = [_dot_tn(a, b) + _dot_tn(c, e) for a, b, c, e in zip(sa, bg, v, kg)]
    for i, (d, j) in enumerate(chains):
        outs[d][j] = y[i]
        s_ref[d * INTER_BATCH + j] = s[i] * gamma[i] + jnp.where(same, upd[i], 0.0)


def _rw_finish_kernel(yf_ref, yb_ref, bonus_ref, g_ref, pv_ref, o_ref):
    hmean = _head_mean_matrix(1.0 / HEAD_DIM)
    ln_w, ln_b = pv_ref[7:8, :], pv_ref[8:9, :]

    def finish(i, c):
        rows = pl.ds(pl.multiple_of(i * SEQ_BLK, SEQ_BLK), SEQ_BLK)
        y = yf_ref[0, rows, :] + yb_ref[0, rows, :]
        yc = y - _dot_split(y, hmean)
        yn = yc * lax.rsqrt(_dot_split(yc * yc, hmean) + RW_LN_EPS) * ln_w + ln_b
        o_ref[rows, :] = (yn + bonus_ref[rows, :]) * g_ref[rows, :]
        return c

    lax.fori_loop(0, N_BLK, finish, 0)


def rwkv_mixer(slab, mu, w0, w_up, a0, a_up, g_up, k_k, k_a, r_k, ln_w, ln_b):
    n = slab.shape[0]
    nb = n // T_ALL
    assert nb % INTER_BATCH == 0 and N_CHUNK % INTRA_CHUNKS == 0
    pv = jnp.concatenate([w0, a0, k_k[None], k_a[None], r_k.reshape(1, GROUP), ln_w[None], ln_b[None],
                          jnp.zeros((7, GROUP), F32)], axis=0)
    seq = lambda dt: jax.ShapeDtypeStruct((nb, T_ALL, GROUP), dt)
    seq2 = lambda dt: jax.ShapeDtypeStruct((2, nb, T_ALL, GROUP), dt)
    flat = jax.ShapeDtypeStruct((n, GROUP), F32)
    gl_shape = jax.ShapeDtypeStruct((2, nb, N_CHUNK * GL_ROWS, GROUP), F32)
    b1 = pl.BlockSpec((1, T_ALL, GROUP), lambda b: (b, 0, 0))
    b2 = pl.BlockSpec((2, 1, T_ALL, GROUP), lambda b: (0, b, 0, 0))
    bflat = pl.BlockSpec((T_ALL, GROUP), lambda b: (b, 0))
    p1 = pl.BlockSpec((1, SEQ_BLK, GROUP), lambda b, i: (b, i, 0))
    p2 = pl.BlockSpec((2, 1, SEQ_BLK, GROUP), lambda b, i: (0, b, i, 0))
    pflat = pl.BlockSpec((SEQ_BLK, GROUP), lambda b, i: (b * N_BLK + i, 0))
    pgl = pl.BlockSpec((2, 1, SEQ_BLK // CHUNK * GL_ROWS, GROUP), lambda b, i: (0, b, i, 0))
    small = lambda shape: pl.BlockSpec(shape, lambda b, i: (0, 0))
    at, rt, bh, kh, bg, kg, gl, v, bonus, g = pl.pallas_call(
        _rw_prep_kernel,
        grid=(nb, N_BLK),
        in_specs=[pl.BlockSpec((T_ALL, RW_W), lambda b, i: (b, 0)), small((2, RW_W)), small((16, GROUP)),
                  small((RW_LR, RW_LR_OUT))],
        out_specs=[p2] * 6 + [pgl, p1, pflat, pflat],
        out_shape=[seq2(BF16)] * 6 + [gl_shape, seq(BF16), flat, flat],
        compiler_params=_cparams("parallel", "parallel"),
        name="rwkv7_prep",
    )(slab, mu, pv, rwkv_lowrank_weights(w_up, a_up, g_up))

    rows = INTRA_CHUNKS * CHUNK
    c1 = pl.BlockSpec((1, rows, GROUP), lambda b, g_: (b, g_, 0))
    c2 = pl.BlockSpec((2, 1, rows, GROUP), lambda b, g_: (0, b, g_, 0))
    xc, arb, rhs0, yk = pl.pallas_call(
        _rw_intra_kernel,
        grid=(nb, N_CHUNK // INTRA_CHUNKS),
        in_specs=[c2, c2, c2, c2, c1],
        out_specs=[c2, c2, c2, c2],
        out_shape=[seq2(BF16), seq2(BF16), seq2(F32), seq2(F32)],
        compiler_params=_cparams("parallel", "parallel"),
        name="rwkv7_intra",
    )(at, rt, bh, kh, v)

    def chunk_pick(d):
        return (lambda n_: n_) if d == 0 else _bwd_chunk

    def per_dir(d, rows_per_chunk):
        pick = chunk_pick(d)
        return pl.BlockSpec((1, INTER_BATCH, rows_per_chunk, GROUP), lambda b, n_: (d, b, pick(n_), 0))

    def shared(d):
        pick = chunk_pick(d)
        return pl.BlockSpec((INTER_BATCH, CHUNK, GROUP), lambda b, n_: (b, pick(n_), 0))

    specs = [s for d in range(2) for s in [per_dir(d, CHUNK)] * 8 + [per_dir(d, GL_ROWS), shared(d)]]
    per = (at, rt, bg, kg, xc, arb, rhs0, yk, gl, v)
    y_f, y_b = pl.pallas_call(
        _rw_inter_kernel,
        grid=(nb // INTER_BATCH, N_CHUNK),
        in_specs=specs,
        out_specs=[shared(0), shared(1)],
        out_shape=[seq(F32), seq(F32)],
        scratch_shapes=[pltpu.VMEM((2 * INTER_BATCH, GROUP, GROUP), F32)],
        compiler_params=_cparams("parallel", "arbitrary"),
        name="rwkv7_inter",
    )(*per, *per)

    return pl.pallas_call(
        _rw_finish_kernel,
        grid=(nb,),
        in_specs=[b1, b1, bflat, bflat, _resident((16, GROUP))],
        out_specs=bflat,
        out_shape=flat,
        compiler_params=_cparams("parallel"),
        name="rwkv7_finish",
    )(y_f, y_b, bonus, g, pv)


DFT_SPLIT = 64
DFT_BLK = 256


def _dft_tables(n):
    big = 2 * n
    t = np.arange(n, dtype=np.int64)[:, None]
    k1 = np.arange(n // DFT_SPLIT, dtype=np.int64)[None, :]
    k2 = np.arange(DFT_SPLIT, dtype=np.int64)[None, :]
    alpha = 2.0 * np.pi * ((DFT_SPLIT * t * k1) % big) / big
    beta = 2.0 * np.pi * ((t * k2) % big) / big

    def pad(a):
        out = np.zeros((n, 128), np.float32)
        out[:, :a.shape[1]] = a
        return out

    return np.stack([pad(np.cos(alpha)), pad(np.sin(alpha)), pad(np.cos(beta)), pad(np.sin(beta))])


def _dft_gen_kernel(n, tab_ref, g_ref):
    k = lax.broadcasted_iota(jnp.int32, (128, n), 1)
    row = lax.broadcasted_iota(jnp.int32, (128, n), 0)
    e_a = jnp.where(k // DFT_SPLIT == row, 1.0, 0.0).astype(BF16)
    e_b = jnp.where(jnp.logical_and(k % DFT_SPLIT == row, row < DFT_SPLIT), 1.0, 0.0).astype(BF16)
    ca = _dot_split(tab_ref[0], e_a)
    sa = _dot_split(tab_ref[1], e_a)
    cb = _dot_split(tab_ref[2], e_b)
    sb = _dot_split(tab_ref[3], e_b)
    g_ref[:, 0:n] = (ca * cb - sa * sb).astype(BF16)
    g_ref[:, n:2 * n] = (-(sa * cb + ca * sb)).astype(BF16)


def dft_matrix(n):
    blk = min(DFT_BLK, n)
    return pl.pallas_call(
        functools.partial(_dft_gen_kernel, n),
        grid=(n // blk,),
        in_specs=[pl.BlockSpec((4, blk, 128), lambda i: (0, i, 0))],
        out_specs=pl.BlockSpec((blk, 2 * n), lambda i: (i, 0)),
        out_shape=jax.ShapeDtypeStruct((n, 2 * n), BF16),
        compiler_params=_cparams("parallel"),
        name=f"dft_matrix_{n}",
    )(jnp.asarray(_dft_tables(n)))


HY_COLS_F = 2 * HY_ORDER * GROUP
HY_OC = HY_ORDER * GROUP


def _hyena_filter_kernel(n, z_ref, w1_ref, b1_ref, w2_ref, b2_ref, w3_ref, freq_ref, dl_ref, hs_ref, hd_ref):
    blk = min(SEQ_BLK, n)
    freq = freq_ref[...]
    dl = dl_ref[...]

    def fill(i, norm):
        r0 = pl.multiple_of(i * blk, blk)
        z = z_ref[pl.ds(r0, blk), :]
        hid = jnp.sin(freq * (_dot_hi(z, w1_ref[...]) + b1_ref[...]))
        hid = jnp.sin(freq * (_dot_hi(hid, w2_ref[...]) + b2_ref[...]))
        t = z[:, 0:1]
        h = _dot_hi(hid, w3_ref[...]) * jnp.exp(-t * dl)
        lag = lax.broadcasted_iota(jnp.int32, (blk, 1), 0) + r0
        hf = h[:, 0:HY_OC]
        hb = jnp.where(lag == 0, 0.0, h[:, HY_OC:2 * HY_OC])
        hs_ref[pl.ds(r0, blk), :] = hf + hb
        hd_ref[pl.ds(r0, blk), :] = hf - hb
        return norm + jnp.sum(jnp.abs(hf) + jnp.abs(hb), axis=0, keepdims=True)

    norm = lax.fori_loop(0, n // blk, fill, jnp.zeros((1, HY_OC), F32))
    inv = 1.0 / norm

    def scale(i, c):
        rows = pl.ds(pl.multiple_of(i * blk, blk), blk)
        hs_ref[rows, :] = hs_ref[rows, :] * inv
        hd_ref[rows, :] = hd_ref[rows, :] * inv
        return c

    lax.fori_loop(0, n // blk, scale, 0)


def hyena_filter_taps(n, f_w1, f_b1, f_w2, f_b2, f_w3, f_freq):
    t = jnp.linspace(0.0, 1.0, n, dtype=F32)[:, None]
    ang = 2.0 * math.pi * jnp.arange(n, dtype=F32)[:, None] / n
    bands = jnp.linspace(1e-4, HY_BANDS - 1, HY_BANDS, dtype=F32)[None]
    z = jnp.concatenate([t, jnp.cos(bands * ang), -jnp.sin(bands * ang)], axis=-1)
    emb = z.shape[1]
    z = jnp.pad(z, ((0, 0), (0, 128 - emb)))
    w1 = jnp.pad(f_w1, ((0, 128 - emb), (0, 0)))
    max_decay = math.log(HY_TARGET) / HY_SHORT_DECAY_PCT
    min_decay = math.log(HY_TARGET) / HY_LONG_DECAY_PCT
    deltas = jnp.abs(jnp.linspace(min_decay, max_decay, HY_OC, dtype=F32))
    dl = jnp.tile(deltas, 2).reshape(1, HY_COLS_F)
    hid = f_w2.shape[0]
    out = jax.ShapeDtypeStruct((n, HY_OC), F32)
    return pl.pallas_call(
        functools.partial(_hyena_filter_kernel, n),
        out_shape=[out, out],
        compiler_params=pltpu.CompilerParams(vmem_limit_bytes=VMEM_LIMIT),
        name=f"hyena_filter_{n}",
    )(z, w1, f_b1.reshape(1, hid), f_w2, f_b2.reshape(1, hid), f_w3, f_freq.reshape(1, hid), dl)


def _hyena_spectrum_kernel(n, g_ref, hs_ref, hd_ref, kr_ref, ki_ref, kn_ref):
    blk = min(2 * SEQ_BLK, n)
    big = 2.0 * n

    def split(ref):
        x = ref[...]
        hi = x.astype(BF16)
        return hi, (x - hi.astype(F32)).astype(BF16)

    s_hi, s_lo = split(hs_ref)
    d_hi, d_lo = split(hd_ref)

    def body(i, c):
        r0 = pl.multiple_of(i * blk, blk)
        rows = pl.ds(r0, blk)
        k = lax.broadcasted_iota(jnp.int32, (blk, 1), 0) + r0
        wgt = jnp.where(k == 0, 1.0 / big, 2.0 / big)
        gc = g_ref[rows, 0:n]
        gs = g_ref[rows, n:2 * n]
        kr_ref[rows, :] = (_dot(gc, s_hi) + _dot(gc, s_lo)) * wgt
        ki_ref[rows, :] = (_dot(gs, d_hi) + _dot(gs, d_lo)) * wgt
        return c

    lax.fori_loop(0, n // blk, body, 0)
    t = lax.broadcasted_iota(jnp.int32, (n, 1), 0)
    sign = jnp.where(t % 2 == 0, 1.0, -1.0)
    kn_ref[...] = jnp.broadcast_to(jnp.sum(sign * hs_ref[...], axis=0, keepdims=True) * (1.0 / big), (8, HY_OC))


def hyena_spectrum(n, g, hs, hd):
    out = jax.ShapeDtypeStruct((n, HY_OC), F32)
    return pl.pallas_call(
        functools.partial(_hyena_spectrum_kernel, n),
        out_shape=[out, out, jax.ShapeDtypeStruct((8, HY_OC), F32)],
        compiler_params=pltpu.CompilerParams(vmem_limit_bytes=VMEM_LIMIT),
        name=f"hyena_spectrum_{n}",
    )(g, hs, hd)


def _hyena_conv_kernel(slab_ref, w_ref, o_ref):
    def body(i, c):
        rows = pl.ds(pl.multiple_of(i * SEQ_BLK, SEQ_BLK), SEQ_BLK)
        for j in range(3):
            prev, cur, nxt = _prev_cur_next(slab_ref, i, j * GROUP, (j + 1) * GROUP)
            w = w_ref[:, j * GROUP:(j + 1) * GROUP]
            o_ref[rows, j * GROUP:(j + 1) * GROUP] = prev * w[0:1] + cur * w[1:2] + nxt * w[2:3]
        return c

    lax.fori_loop(0, N_BLK, body, 0)


def hyena_short_conv(slab, conv_w):
    n = slab.shape[0]
    return pl.pallas_call(
        _hyena_conv_kernel,
        grid=(n // T_ALL,),
        in_specs=[pl.BlockSpec((T_ALL, 3 * GROUP), lambda b: (b, 0)),
                  pl.BlockSpec((3, 3 * GROUP), lambda b: (0, 0))],
        out_specs=pl.BlockSpec((T_ALL, 3 * GROUP), lambda b: (b, 0)),
        out_shape=jax.ShapeDtypeStruct((n, 3 * GROUP), F32),
        compiler_params=_cparams("parallel"),
        name="hyena_short_conv",
    )(slab, conv_w)


HY_FBLK = 512


def _alt_sign(n):
    t = lax.broadcasted_iota(jnp.int32, (n, 1), 0)
    return jnp.where(t % 2 == 0, 1.0, -1.0)


def _hyena_fwd_kernel(x_ref, gl_ref, gc_ref, krl_ref, kil_ref, knl_ref, krc_ref, kic_ref, knc_ref,
                      pl_ref, pc_ref, pn_ref):
    def transform(x, g_ref, kr_ref, ki_ref, kn_ref, p_ref, n, blk):
        xb = x.astype(BF16)

        def body(i, c):
            rows = pl.ds(pl.multiple_of(i * blk, blk), blk)
            zr = _dot(g_ref[rows, 0:n], xb)
            zi = _dot(g_ref[rows, n:2 * n], xb)
            kr = kr_ref[rows, :]
            ki = ki_ref[rows, :]
            p_ref[0, 0, rows, :] = (zr * kr - zi * ki).astype(BF16)
            p_ref[0, 1, rows, :] = (zr * ki + zi * kr).astype(BF16)
            return c

        lax.fori_loop(0, n // blk, body, 0)
        return jnp.sum(_alt_sign(n) * x, axis=0, keepdims=True) * kn_ref[0:1, :]

    nyq_c = transform(x_ref[0:CTX_LEN, :], gc_ref, krc_ref, kic_ref, knc_ref, pc_ref, CTX_LEN, CTX_LEN)
    nyq_l = transform(x_ref[CTX_LEN:T_ALL, :], gl_ref, krl_ref, kil_ref, knl_ref, pl_ref, SEQ, HY_FBLK)
    pn_ref[0] = jnp.concatenate([nyq_l, nyq_c, jnp.zeros((6, GROUP), F32)], axis=0)


def _resident(shape):
    return pl.BlockSpec(shape, lambda b: (0,) * len(shape))


def hyena_forward_transform(x, col, g_l, g_c, spec_l, spec_c, order):
    n = x.shape[0]
    nb = n // T_ALL
    kcol = lambda shape: pl.BlockSpec(shape, lambda b: (0, order))
    return pl.pallas_call(
        _hyena_fwd_kernel,
        grid=(nb,),
        in_specs=[
            pl.BlockSpec((T_ALL, GROUP), lambda b: (b, col)),
            _resident((SEQ, 2 * SEQ)), _resident((CTX_LEN, 2 * CTX_LEN)),
            kcol((SEQ, GROUP)), kcol((SEQ, GROUP)), kcol((8, GROUP)),
            kcol((CTX_LEN, GROUP)), kcol((CTX_LEN, GROUP)), kcol((8, GROUP)),
        ],
        out_specs=[
            pl.BlockSpec((1, 2, SEQ, GROUP), lambda b: (b, 0, 0, 0)),
            pl.BlockSpec((1, 2, CTX_LEN, GROUP), lambda b: (b, 0, 0, 0)),
            pl.BlockSpec((1, 8, GROUP), lambda b: (b, 0, 0)),
        ],
        out_shape=[
            jax.ShapeDtypeStruct((nb, 2, SEQ, GROUP), BF16),
            jax.ShapeDtypeStruct((nb, 2, CTX_LEN, GROUP), BF16),
            jax.ShapeDtypeStruct((nb, 8, GROUP), F32),
        ],
        compiler_params=_cparams("parallel"),
        name=f"hyena_fwd_{order}",
    )(x, g_l, g_c, *spec_l, *spec_c)


def _hyena_inv_kernel(pl_ref, pc_ref, pn_ref, gl_ref, gc_ref, u_ref, gate_ref, bias_ref, o_ref):
    bias = bias_ref[0]

    def inverse(p_ref, nyq, g_ref, n, blk, off):
        pr = p_ref[0, 0]
        pi = p_ref[0, 1]

        def body(i, c):
            r0 = pl.multiple_of(i * blk, blk)
            rows = pl.ds(r0, blk)
            orow = pl.ds(pl.multiple_of(off + r0, blk), blk)
            t = lax.broadcasted_iota(jnp.int32, (blk, 1), 0)
            sign = jnp.where(t % 2 == 0, 1.0, -1.0)
            y = _dot(g_ref[rows, 0:n], pr) + _dot(g_ref[rows, n:2 * n], pi) + sign * nyq
            o_ref[orow, :] = gate_ref[orow, :] * (y + u_ref[orow, :] * bias)
            return c

        lax.fori_loop(0, n // blk, body, 0)

    inverse(pc_ref, pn_ref[0, 1:2, :], gc_ref, CTX_LEN, CTX_LEN, 0)
    inverse(pl_ref, pn_ref[0, 0:1, :], gl_ref, SEQ, SEQ_BLK, CTX_LEN)


def hyena_inverse_transform(p_l, p_c, p_n, g_l, g_c, u, ucol, gate, gcol, bias):
    nb = p_l.shape[0]
    return pl.pallas_call(
        _hyena_inv_kernel,
        grid=(nb,),
        in_specs=[
            pl.BlockSpec((1, 2, SEQ, GROUP), lambda b: (b, 0, 0, 0)),
            pl.BlockSpec((1, 2, CTX_LEN, GROUP), lambda b: (b, 0, 0, 0)),
            pl.BlockSpec((1, 8, GROUP), lambda b: (b, 0, 0)),
            _resident((SEQ, 2 * SEQ)), _resident((CTX_LEN, 2 * CTX_LEN)),
            pl.BlockSpec((T_ALL, GROUP), lambda b: (b, ucol)),
            pl.BlockSpec((T_ALL, GROUP), lambda b: (b, gcol)),
            pl.BlockSpec((1, 1, GROUP), lambda b: (0, 0, 0)),
        ],
        out_specs=pl.BlockSpec((T_ALL, GROUP), lambda b: (b, 0)),
        out_shape=jax.ShapeDtypeStruct((nb * T_ALL, GROUP), F32),
        compiler_params=_cparams("parallel"),
        name="hyena_inv",
    )(p_l, p_c, p_n, g_l, g_c, u, gate, bias.reshape(1, 1, GROUP))


def hyena_mixer(slab, g_l, g_c, conv_w, f_w1, f_b1, f_w2, f_b2, f_w3, f_freq, bias):
    u = hyena_short_conv(slab, conv_w)
    spec_l = hyena_spectrum(SEQ, g_l, *hyena_filter_taps(SEQ, f_w1, f_b1, f_w2, f_b2, f_w3, f_freq))
    spec_c = hyena_spectrum(CTX_LEN, g_c, *hyena_filter_taps(CTX_LEN, f_w1, f_b1, f_w2, f_b2, f_w3, f_freq))
    p = hyena_forward_transform(u, 0, g_l, g_c, spec_l, spec_c, 0)
    z = hyena_inverse_transform(*p, g_l, g_c, u, 0, u, 1, bias[0])
    p = hyena_forward_transform(z, 0, g_l, g_c, spec_l, spec_c, 1)
    return hyena_inverse_transform(*p, g_l, g_c, z, 0, u, 2, bias[1])


def kernel(x, c, ctx, c_ctx, w_mod, b_mod, norm_w, ffn_w_gu, ffn_w_down, w_in, w_out,
           hy_conv, hy_f_w1, hy_f_b1, hy_f_w2, hy_f_b2, hy_f_w3, hy_f_freq, hy_bias,
           na_q_norm, na_k_norm, na_rpb, dn_conv, dn_a_log, dn_dt_bias, dn_norm,
           rw_mu, rw_w0, rw_w_up, rw_a0, rw_a_up, rw_g_up, rw_k_k, rw_k_a, rw_r_k, rw_ln_w, rw_ln_b):
    nb = x.shape[0]
    assert x.shape[1:] == (SEQ, D_MODEL) and ctx.shape[1:] == (CTX_LEN, D_MODEL) and nb + 1 <= 16
    s = jnp.concatenate([ctx, x], axis=1).reshape(nb * T_ALL, D_MODEL)
    cond = jnp.concatenate([c_ctx[None], c, jnp.zeros((15 - nb, D_MODEL), F32)], axis=0)
    mod = modulation_all(cond, w_mod, b_mod).reshape(DEPTH, 16, N_MOD, D_MODEL)
    g_l = dft_matrix(SEQ)
    g_c = dft_matrix(CTX_LEN)
    w_gu = ffn_w_gu.astype(BF16)
    w_down = ffn_w_down.astype(BF16)
    w_out_b = w_out.astype(BF16)
    dn_end = 6 * GROUP + 4 * GROUP + 4 * GROUP_HEADS
    w_in_p = jnp.concatenate(
        [w_in[:, :, :dn_end], jnp.zeros((DEPTH, D_MODEL, 6 * GROUP + DN_W - dn_end), F32), w_in[:, :, dn_end:]],
        axis=2).astype(BF16)
    for l in range(DEPTH):
        need_ctx = l < DEPTH - 1
        modc = mod[l, 0:1]
        modb = mod[l, 1:1 + nb]
        s = ffn_half_step(s, modc, modb, norm_w[l, 0], w_gu[l, 0], w_down[l, 0], 0)
        hy_s, na_s, dn_s, rw_s = input_projection(s, modc, modb, norm_w[l, 1], w_in_p[l])
        groups = (
            hyena_mixer(hy_s, g_l, g_c, hy_conv[l], hy_f_w1[l], hy_f_b1[l], hy_f_w2[l], hy_f_b2[l], hy_f_w3[l],
                        hy_f_freq[l], hy_bias[l]),
            na_mixer(na_s, na_q_norm[l], na_k_norm[l], na_bias_table(na_rpb[l]), need_ctx),
            deltanet_mixer(dn_s, dn_conv[l], dn_a_log[l], dn_dt_bias[l], dn_norm[l]),
            rwkv_mixer(rw_s, rw_mu[l], rw_w0[l], rw_w_up[l], rw_a0[l], rw_a_up[l], rw_g_up[l], rw_k_k[l],
                       rw_k_a[l], rw_r_k[l], rw_ln_w[l], rw_ln_b[l]),
        )
        s = output_projection(s, modc, modb, groups, w_out_b[l])
        s = ffn_half_step(s, modc, modb, norm_w[l, 2], w_gu[l, 1], w_down[l, 1], 2)
    return s.reshape(nb, T_ALL, D_MODEL)[:, CTX_LEN:]
```

```python
import functools
import math

import numpy as np
import jax
import jax.numpy as jnp
from jax import lax
from jax.experimental import pallas as pl
from jax.experimental.pallas import tpu as pltpu

D_MODEL = 1024
SEQ = 2048
DEPTH = 2
CTX_LEN = 256
T_ALL = CTX_LEN + SEQ
GRID_W = 64
GROUP = 256
HEAD_DIM = 64
GROUP_HEADS = 4
D_FF = 2816
N_MOD = 9
NORM_EPS = 1e-6

HY_ORDER = 2
HY_BANDS = 16
HY_TARGET = 1e-2
HY_SHORT_DECAY_PCT = 0.3
HY_LONG_DECAY_PCT = 1.5

NA_WIN_ROWS = 8
NA_WIN_COLS = 16

CHUNK = 64
RW_DECAY_RANK = 32
RW_AAA_RANK = 32
RW_GATE_RANK = 64
RW_LN_EPS = 64e-5

DN_W = 4 * GROUP + 128
RW_W = 3 * GROUP + 128
P_PAD = 3 * GROUP + 3 * GROUP + DN_W + RW_W

TM = 768
TF = 1408
VMEM_LIMIT = 56 * 1024 * 1024

F32 = jnp.float32
BF16 = jnp.bfloat16


def _cparams(*sem):
    return pltpu.CompilerParams(dimension_semantics=sem, vmem_limit_bytes=VMEM_LIMIT)


def _silu(x):
    return x * (1.0 / (1.0 + jnp.exp(-x)))


def _sigmoid(x):
    return 1.0 / (1.0 + jnp.exp(-x))


def _softplus(x):
    return jnp.maximum(x, 0.0) + jnp.log(1.0 + jnp.exp(-jnp.abs(x)))


def _dot(a, b):
    return jnp.dot(a, b, preferred_element_type=F32)


def _dot_nt(a, b):
    return lax.dot_general(a, b, (((1,), (1,)), ((), ())), preferred_element_type=F32)


def _dot_tn(a, b):
    return lax.dot_general(a, b, (((0,), (0,)), ((), ())), preferred_element_type=F32)


def _dot_hi(a, b):
    return jnp.dot(a, b, preferred_element_type=F32, precision=lax.Precision.HIGHEST)


def _mod_kernel(cond_ref, w_ref, b_ref, o_ref):
    a = _silu(cond_ref[...]).astype(BF16)
    o_ref[0] = _dot(a, w_ref[0].astype(BF16)) + b_ref[0]


def modulation_all(cond, w_mod, b_mod):
    r = cond.shape[0]
    tn = 1024
    return pl.pallas_call(
        _mod_kernel,
        grid=(DEPTH, N_MOD * D_MODEL // tn),
        in_specs=[
            pl.BlockSpec((r, D_MODEL), lambda l, j: (0, 0)),
            pl.BlockSpec((1, D_MODEL, tn), lambda l, j: (l, 0, j)),
            pl.BlockSpec((1, 1, tn), lambda l, j: (l, 0, j)),
        ],
        out_specs=pl.BlockSpec((1, r, tn), lambda l, j: (l, 0, j)),
        out_shape=jax.ShapeDtypeStruct((DEPTH, r, N_MOD * D_MODEL), F32),
        compiler_params=_cparams("parallel", "parallel"),
        name="modulation",
    )(cond, w_mod, b_mod.reshape(DEPTH, 1, N_MOD * D_MODEL))


def _row_mod(modc_ref, modb_ref, tile, idx):
    row = lax.broadcasted_iota(jnp.int32, (TM, 1), 0) + (tile % (T_ALL // TM)) * TM
    return jnp.where(row < CTX_LEN, modc_ref[0, idx:idx + 1, :], modb_ref[0, idx:idx + 1, :])


def _adaln(x, nw, shift, scale):
    y = x * lax.rsqrt(jnp.mean(x * x, axis=-1, keepdims=True) + NORM_EPS)
    return y * nw * (1.0 + scale) + shift


def _ffn_kernel(sub, x_ref, modc_ref, modb_ref, nw_ref, wg_ref, wu_ref, wd_ref, o_ref, acc_ref):
    i = pl.program_id(0)
    j = pl.program_id(1)

    @pl.when(j == 0)
    def _():
        acc_ref[...] = jnp.zeros_like(acc_ref)

    shift = _row_mod(modc_ref, modb_ref, i, 3 * sub)
    scale = _row_mod(modc_ref, modb_ref, i, 3 * sub + 1)
    h = _adaln(x_ref[...], nw_ref[...], shift, scale).astype(BF16)
    a = (_silu(_dot(h, wg_ref[...])) * _dot(h, wu_ref[...])).astype(BF16)
    acc_ref[...] += _dot(a, wd_ref[...])

    @pl.when(j == pl.num_programs(1) - 1)
    def _():
        gate = _row_mod(modc_ref, modb_ref, i, 3 * sub + 2)
        o_ref[...] = x_ref[...] + 0.5 * gate * acc_ref[...]


def ffn_half_step(x, modc, modb, nw, w_gu, w_down, layer, which, sub):
    n = x.shape[0]
    tiles_per_b = T_ALL // TM
    nj = D_FF // TF
    return pl.pallas_call(
        functools.partial(_ffn_kernel, sub),
        grid=(n // TM, nj),
        in_specs=[
            pl.BlockSpec((TM, D_MODEL), lambda i, j: (i, 0)),
            pl.BlockSpec((1, N_MOD, D_MODEL), lambda i, j: (0, 0, 0)),
            pl.BlockSpec((1, N_MOD, D_MODEL), lambda i, j: (i // tiles_per_b, 0, 0)),
            pl.BlockSpec((1, D_MODEL), lambda i, j: (0, 0)),
            pl.BlockSpec((None, None, D_MODEL, TF), lambda i, j: (layer, which, 0, j)),
            pl.BlockSpec((None, None, D_MODEL, TF), lambda i, j: (layer, which, 0, nj + j)),
            pl.BlockSpec((None, None, TF, D_MODEL), lambda i, j: (layer, which, j, 0)),
        ],
        out_specs=pl.BlockSpec((TM, D_MODEL), lambda i, j: (i, 0)),
        out_shape=jax.ShapeDtypeStruct((n, D_MODEL), F32),
        scratch_shapes=[pltpu.VMEM((TM, D_MODEL), F32)],
        compiler_params=_cparams("parallel", "arbitrary"),
        name=f"ffn{sub}",
    )(x, modc, modb, nw.reshape(1, D_MODEL), w_gu, w_gu, w_down)


def _inproj_kernel(x_ref, modc_ref, modb_ref, nw_ref, w_ref, hy_ref, na_ref, dn_ref, rw_ref):
    i = pl.program_id(0)
    shift = _row_mod(modc_ref, modb_ref, i, 3)
    scale = _row_mod(modc_ref, modb_ref, i, 4)
    h = _adaln(x_ref[...], nw_ref[...], shift, scale).astype(BF16)
    o0 = 3 * GROUP
    o1 = 6 * GROUP
    o2 = o1 + DN_W
    hy_ref[...] = _dot(h, w_ref[:, 0:o0])
    na_ref[...] = _dot(h, w_ref[:, o0:o1])
    dn_ref[...] = _dot(h, w_ref[:, o1:o2])
    rw_ref[...] = _dot(h, w_ref[:, o2:P_PAD])


def input_projection(x, modc, modb, nw, w_in_p, layer):
    n = x.shape[0]
    tiles_per_b = T_ALL // TM
    widths = (3 * GROUP, 3 * GROUP, DN_W, RW_W)
    return pl.pallas_call(
        _inproj_kernel,
        grid=(n // TM,),
        in_specs=[
            pl.BlockSpec((TM, D_MODEL), lambda i: (i, 0)),
            pl.BlockSpec((1, N_MOD, D_MODEL), lambda i: (0, 0, 0)),
            pl.BlockSpec((1, N_MOD, D_MODEL), lambda i: (i // tiles_per_b, 0, 0)),
            pl.BlockSpec((1, D_MODEL), lambda i: (0, 0)),
            pl.BlockSpec((None, D_MODEL, P_PAD), lambda i: (layer, 0, 0)),
        ],
        out_specs=[pl.BlockSpec((TM, w), lambda i: (i, 0)) for w in widths],
        out_shape=[jax.ShapeDtypeStruct((n, w), F32) for w in widths],
        compiler_params=_cparams("parallel"),
        name="inproj",
    )(x, modc, modb, nw.reshape(1, D_MODEL), w_in_p)


def _outproj_kernel(x_ref, modc_ref, modb_ref, g0_ref, g1_ref, g2_ref, g3_ref, w_ref, o_ref):
    i = pl.program_id(0)
    y = _dot(g0_ref[...].astype(BF16), w_ref[0:GROUP, :])
    y += _dot(g1_ref[...].astype(BF16), w_ref[GROUP:2 * GROUP, :])
    y += _dot(g2_ref[...].astype(BF16), w_ref[2 * GROUP:3 * GROUP, :])
    y += _dot(g3_ref[...].astype(BF16), w_ref[3 * GROUP:4 * GROUP, :])
    gate = _row_mod(modc_ref, modb_ref, i, 5)
    o_ref[...] = x_ref[...] + gate * y


def output_projection(x, modc, modb, groups, w_out, layer):
    n = x.shape[0]
    tiles_per_b = T_ALL // TM
    return pl.pallas_call(
        _outproj_kernel,
        grid=(n // TM,),
        in_specs=[
            pl.BlockSpec((TM, D_MODEL), lambda i: (i, 0)),
            pl.BlockSpec((1, N_MOD, D_MODEL), lambda i: (0, 0, 0)),
            pl.BlockSpec((1, N_MOD, D_MODEL), lambda i: (i // tiles_per_b, 0, 0)),
        ] + [pl.BlockSpec((TM, GROUP), lambda i: (i, 0))] * 4 + [
            pl.BlockSpec((None, D_MODEL, D_MODEL), lambda i: (layer, 0, 0)),
        ],
        out_specs=pl.BlockSpec((TM, D_MODEL), lambda i: (i, 0)),
        out_shape=jax.ShapeDtypeStruct((n, D_MODEL), F32),
        compiler_params=_cparams("parallel"),
        name="outproj",
    )(x, modc, modb, *groups, w_out)


def _head_mean_matrix(scale):
    r = lax.broadcasted_iota(jnp.int32, (GROUP, GROUP), 0) // HEAD_DIM
    c = lax.broadcasted_iota(jnp.int32, (GROUP, GROUP), 1) // HEAD_DIM
    return jnp.where(r == c, scale, 0.0).astype(BF16)


def _dot_split(a, m_bf16):
    hi = a.astype(BF16)
    lo = (a - hi.astype(F32)).astype(BF16)
    return _dot(hi, m_bf16) + _dot(lo, m_bf16)


def _lane_head(width=GROUP):
    return lax.broadcasted_iota(jnp.int32, (1, width), 1) // HEAD_DIM


NA_ROWS = SEQ // GRID_W
NA_LOCAL = NA_WIN_ROWS * GRID_W
NA_NEG = -1e30
NA_BLK = 256
NA_PAIR = 2


def na_bias_table(rpb):
    n_dr = 2 * NA_WIN_ROWS
    rows = jnp.pad(rpb, ((0, 0), (0, 1), (0, 128 - rpb.shape[2]))).reshape(GROUP_HEADS * n_dr, 128)
    toep = pl.pallas_call(
        _na_bias_kernel,
        out_shape=jax.ShapeDtypeStruct((GROUP_HEADS * n_dr, GRID_W * GRID_W), F32),
        name="na_bias",
    )(rows).reshape(GROUP_HEADS, n_dr, GRID_W, GRID_W)
    tab = jnp.stack([toep[:, NA_WIN_ROWS - 1 - p:2 * NA_WIN_ROWS - 1 - p] for p in range(NA_WIN_ROWS)], axis=0)
    tab = jnp.transpose(tab, (0, 1, 3, 2, 4))
    return tab.reshape(NA_WIN_ROWS, GROUP_HEADS, GRID_W, NA_LOCAL)


def _na_bias_kernel(rpb_ref, o_ref):
    n = GRID_W * GRID_W
    d = lax.broadcasted_iota(jnp.int32, (128, n), 0)
    cj = lax.broadcasted_iota(jnp.int32, (128, n), 1)
    onehot = jnp.where((cj % GRID_W) - (cj // GRID_W) + NA_WIN_COLS - 1 == d, 1.0, 0.0).astype(BF16)
    cj1 = lax.broadcasted_iota(jnp.int32, (1, n), 1)
    c = cj1 // GRID_W
    j = cj1 % GRID_W
    start = jnp.clip(c - NA_WIN_COLS // 2, 0, GRID_W - NA_WIN_COLS)
    in_win = jnp.logical_and(j >= start, j < start + NA_WIN_COLS)
    o_ref[...] = jnp.where(in_win, _dot_exact_rhs(rpb_ref[...], onehot), NA_NEG)


def _na_kernel(need_ctx, slab_ref, qw_ref, kw_ref, bias_ref, o_ref, q_s, k_s, v_s):
    hm = _head_mean_matrix(1.0 / HEAD_DIM)
    qw = qw_ref[...] * (HEAD_DIM ** -0.5)
    kw = kw_ref[...]

    def prep(i, c):
        r0 = pl.multiple_of(i * NA_BLK, NA_BLK)
        q = slab_ref[pl.ds(r0, NA_BLK), 0:GROUP]
        k = slab_ref[pl.ds(r0, NA_BLK), GROUP:2 * GROUP]
        q_s[pl.ds(r0, NA_BLK), :] = (q * lax.rsqrt(_dot_split(q * q, hm) + NORM_EPS) * qw).astype(BF16)
        k_s[pl.ds(r0, NA_BLK), :] = (k * lax.rsqrt(_dot_split(k * k, hm) + NORM_EPS) * kw).astype(BF16)
        v_s[pl.ds(r0, NA_BLK), :] = slab_ref[pl.ds(r0, NA_BLK), 2 * GROUP:3 * GROUP].astype(BF16)
        return c

    lax.fori_loop(0, T_ALL // NA_BLK, prep, 0)

    lane_h = _lane_head()
    kc = k_s[0:CTX_LEN, :]
    vc = v_s[0:CTX_LEN, :]

    if need_ctx:
        qc = q_s[0:CTX_LEN, :]
        out = jnp.zeros((CTX_LEN, GROUP), F32)
        for h in range(GROUP_HEADS):
            mask = lane_h == h
            s = _dot_nt(jnp.where(mask, qc, jnp.zeros_like(qc)), kc)
            e = jnp.exp(s - jnp.max(s, axis=-1, keepdims=True))
            p = e * (1.0 / jnp.sum(e, axis=-1, keepdims=True))
            out = jnp.where(mask, _dot(p.astype(BF16), vc), out)
        o_ref[0:CTX_LEN, :] = out
    else:
        o_ref[0:CTX_LEN, :] = jnp.zeros((CTX_LEN, GROUP), F32)

    def pair_body(i, c):
        rows = [i * NA_PAIR + t for t in range(NA_PAIR)]
        start = [jnp.clip(r - NA_WIN_ROWS // 2, 0, NA_ROWS - NA_WIN_ROWS) for r in rows]
        q0 = [pl.multiple_of(CTX_LEN + r * GRID_W, GRID_W) for r in rows]
        k0 = [pl.multiple_of(CTX_LEN + s * GRID_W, GRID_W) for s in start]
        q = [_expand_heads(q_s[pl.ds(a, GRID_W), :]) for a in q0]
        kb = [k_s[pl.ds(a, NA_LOCAL), :] for a in k0]
        vb = [v_s[pl.ds(a, NA_LOCAL), :] for a in k0]
        bias = [bias_ref[r - s].reshape(GROUP_HEADS * GRID_W, NA_LOCAL) for r, s in zip(rows, start)]
        s_loc = [_dot_nt(q[t], kb[t]) + bias[t] for t in range(NA_PAIR)]
        s_ctx = [_dot_nt(q[t], kc) for t in range(NA_PAIR)]
        m = [jnp.maximum(jnp.max(a, axis=-1, keepdims=True), jnp.max(b, axis=-1, keepdims=True))
             for a, b in zip(s_loc, s_ctx)]
        e_loc = [jnp.exp(a - mm) for a, mm in zip(s_loc, m)]
        e_ctx = [jnp.exp(b - mm) for b, mm in zip(s_ctx, m)]
        inv = [1.0 / (jnp.sum(a, axis=-1, keepdims=True) + jnp.sum(b, axis=-1, keepdims=True))
               for a, b in zip(e_loc, e_ctx)]
        o = [_dot((e_loc[t] * inv[t]).astype(BF16), vb[t]) + _dot((e_ctx[t] * inv[t]).astype(BF16), vc)
             for t in range(NA_PAIR)]
        for t in range(NA_PAIR):
            out = o[t][0:GRID_W]
            for h in range(1, GROUP_HEADS):
                out = jnp.where(lane_h == h, o[t][h * GRID_W:(h + 1) * GRID_W], out)
            o_ref[pl.ds(q0[t], GRID_W), :] = out
        return c

    lax.fori_loop(0, NA_ROWS // NA_PAIR, pair_body, 0)


def na_mixer(slab, q_norm, k_norm, bias_tab, need_ctx):
    n = slab.shape[0]
    tile4 = lambda w: jnp.tile(w, GROUP_HEADS).reshape(1, GROUP)
    return pl.pallas_call(
        functools.partial(_na_kernel, need_ctx),
        grid=(n // T_ALL,),
        in_specs=[
            pl.BlockSpec((T_ALL, 3 * GROUP), lambda b: (b, 0)),
            pl.BlockSpec((1, GROUP), lambda b: (0, 0)),
            pl.BlockSpec((1, GROUP), lambda b: (0, 0)),
            pl.BlockSpec((NA_WIN_ROWS, GROUP_HEADS, GRID_W, NA_LOCAL), lambda b: (0, 0, 0, 0)),
        ],
        out_specs=pl.BlockSpec((T_ALL, GROUP), lambda b: (b, 0)),
        out_shape=jax.ShapeDtypeStruct((n, GROUP), F32),
        scratch_shapes=[pltpu.VMEM((T_ALL, GROUP), BF16)] * 3,
        compiler_params=_cparams("parallel"),
        name="na_mixer",
    )(slab, tile4(q_norm), tile4(k_norm), bias_tab)


SEQ_BLK = 256
N_BLK = T_ALL // SEQ_BLK
N_CHUNK = T_ALL // CHUNK
CTX_CHUNKS = CTX_LEN // CHUNK


def _prev_cur_next(ref, i, c0, c1):
    r0 = pl.multiple_of(i * SEQ_BLK, SEQ_BLK)
    cur = ref[pl.ds(r0, SEQ_BLK), c0:c1]
    up0 = pl.multiple_of(jnp.maximum(r0 - 8, 0), 8)
    dn0 = pl.multiple_of(jnp.minimum(r0 + SEQ_BLK, T_ALL - 8), 8)
    up = ref[pl.ds(up0, 8), c0:c1][7:8, :]
    dn = ref[pl.ds(dn0, 8), c0:c1][0:1, :]
    up = jnp.where(i >= 2, up, 0.0)
    dn = jnp.where(jnp.logical_and(i >= 1, i <= N_BLK - 2), dn, 0.0)
    row = lax.broadcasted_iota(jnp.int32, (SEQ_BLK, 1), 0)
    prev = jnp.where(row == 0, up, pltpu.roll(cur, 1, 0))
    nxt = jnp.where(row == SEQ_BLK - 1, dn, pltpu.roll(cur, SEQ_BLK - 1, 0))
    return prev, cur, nxt


def _chunk_cumsum(x, reverse):
    pos = lax.broadcasted_iota(jnp.int32, (SEQ_BLK, 1), 0) % CHUNK
    s = 1
    while s < CHUNK:
        if reverse:
            x = x + jnp.where(pos < CHUNK - s, pltpu.roll(x, SEQ_BLK - s, 0), 0.0)
        else:
            x = x + jnp.where(pos >= s, pltpu.roll(x, s, 0), 0.0)
        s *= 2
    return x


def _split3(a):
    hi = a.astype(BF16)
    r1 = a - hi.astype(F32)
    mid = r1.astype(BF16)
    lo = (r1 - mid.astype(F32)).astype(BF16)
    return hi, mid, lo


def _dot_exact_rhs(a, m_bf16):
    hi, mid, lo = _split3(a)
    return _dot(hi, m_bf16) + _dot(mid, m_bf16) + _dot(lo, m_bf16)


def _expand_heads(x):
    lane_h = _lane_head()
    return jnp.concatenate([jnp.where(lane_h == h, x, 0.0) for h in range(GROUP_HEADS)], axis=0)


def _chunk_of_step(n, reverse):
    if not reverse:
        return n
    return jnp.where(n < CTX_CHUNKS, CTX_CHUNKS - 1 - n, N_CHUNK + CTX_CHUNKS - 1 - n)


INV_BASE = 16


def _cat_dot(a, b):
    return _dot(a.astype(BF16), _expand_heads(b.astype(BF16)))


def _cat_index():
    i = lax.broadcasted_iota(jnp.int32, (CHUNK, GROUP_HEADS * CHUNK), 0)
    j = lax.broadcasted_iota(jnp.int32, (CHUNK, GROUP_HEADS * CHUNK), 1) % CHUNK
    return i, j


def _cat_masks(reverse):
    i, j = _cat_index()
    if reverse:
        return i <= j, i < j
    return i >= j, i > j


def _inverse_unit_triangular(mats):
    i, j = _cat_index()
    inner = (i // INV_BASE) == (j // INV_BASE)
    eye = jnp.where(i == j, 1.0, 0.0)
    nd = [jnp.where(inner, n, 0.0) for n in mats]
    x = [eye - n for n in nd]
    p = [_cat_dot(n, n) for n in nd]
    k = 2
    while k < INV_BASE:
        x = [xi + _cat_dot(pi, xi) for xi, pi in zip(x, p)]
        k *= 2
        if k < INV_BASE:
            p = [_cat_dot(pi, pi) for pi in p]
    width = INV_BASE
    while width < CHUNK:
        outer = (i // (2 * width)) == (j // (2 * width))
        sel = jnp.logical_and(outer, jnp.logical_not(inner))
        t = [_cat_dot(jnp.where(sel, n, 0.0), xi) for n, xi in zip(mats, x)]
        x = [xi - _cat_dot(xi, ti) for xi, ti in zip(x, t)]
        inner = outer
        width *= 2
    return x


def _head_rows(gc, lane_onehot):
    hi, mid, lo = _split3(gc)
    t = _dot_nt(lane_onehot, hi) + _dot_nt(lane_onehot, mid) + _dot_nt(lane_onehot, lo)
    return jnp.concatenate([t[h:h + 1, :] for h in range(GROUP_HEADS)], axis=1)


INTRA_CHUNKS = 4
INTER_BATCH = 4
GL_ROWS = 8


def _dn_prep_kernel(slab_ref, conv_ref, alog_ref, dt_ref, q_ref, k_ref, v_ref, gc_ref, beta_ref):
    hsum = _head_mean_matrix(1.0)
    col = lax.broadcasted_iota(jnp.int32, (128, GROUP), 0)
    lane = lax.broadcasted_iota(jnp.int32, (128, GROUP), 1) // HEAD_DIM
    neg_a = -jnp.exp(alog_ref[...])
    dtb = dt_ref[...]

    def prep(i, c):
        rows = pl.ds(pl.multiple_of(i * SEQ_BLK, SEQ_BLK), SEQ_BLK)
        for j, dst in enumerate((q_ref, k_ref, v_ref)):
            prev, cur, nxt = _prev_cur_next(slab_ref, i, j * GROUP, (j + 1) * GROUP)
            w = conv_ref[:, j * GROUP:(j + 1) * GROUP]
            u = _silu(prev * w[0:1] + cur * w[1:2] + nxt * w[2:3])
            if j == 0:
                u = u * lax.rsqrt(_dot_split(u * u, hsum) + 1e-6) * (HEAD_DIM ** -0.5)
            elif j == 1:
                u = u * lax.rsqrt(_dot_split(u * u, hsum) + 1e-6)
            dst[0] = u
        ba = slab_ref[rows, 4 * GROUP:4 * GROUP + 128]
        for d in range(2):
            e_b = jnp.where(col == 8 * d + lane, 1.0, 0.0).astype(BF16)
            e_a = jnp.where(col == 8 * d + 4 + lane, 1.0, 0.0).astype(BF16)
            beta_ref[d, 0] = _sigmoid(_dot_exact_rhs(ba, e_b))
            g = neg_a[d:d + 1] * _softplus(_dot_exact_rhs(ba, e_a) + dtb[d:d + 1])
            gc_ref[d, 0] = _chunk_cumsum(g, reverse=(d == 1))
        return c

    prep(pl.program_id(1), 0)


def _dn_intra_kernel(q_ref, k_ref, v_ref, gc_ref, beta_ref, u_ref, w_ref, attn_ref, qd_ref, kd_ref, gl_ref):
    onehot = jnp.where(
        lax.broadcasted_iota(jnp.int32, (8, GROUP), 1) == HEAD_DIM * lax.broadcasted_iota(jnp.int32, (8, GROUP), 0),
        1.0, 0.0).astype(BF16)
    masks = (_cat_masks(False), _cat_masks(True))
    chains = [(j, d) for j in range(INTRA_CHUNKS) for d in range(2)]
    rows = [pl.ds(j * CHUNK, CHUNK) for j, d in chains]
    gc = [gc_ref[d, 0, r, :] for (j, d), r in zip(chains, rows)]
    beta = [beta_ref[d, 0, r, :] for (j, d), r in zip(chains, rows)]
    q = [q_ref[0, r, :] for r in rows]
    k = [k_ref[0, r, :] for r in rows]
    v = [v_ref[0, r, :] for r in rows]
    eg = [jnp.exp(g) for g in gc]
    g_last = [g[0:1, :] if d == 1 else g[CHUNK - 1:CHUNK, :] for (j, d), g in zip(chains, gc)]
    kb = [a * b for a, b in zip(k, beta)]
    k_e = [_expand_heads(a.astype(BF16)) for a in k]
    dec = []
    for (j, d), g in zip(chains, gc):
        incl = masks[d][0]
        dec.append(jnp.where(incl, jnp.exp(jnp.where(incl, g - _head_rows(g, onehot), 0.0)), 0.0))
    m = [jnp.where(masks[d][1], _dot_nt(a.astype(BF16), ke) * dc, 0.0)
         for (j, d), a, ke, dc in zip(chains, kb, k_e, dec)]
    attn = [_dot_nt(a.astype(BF16), ke) * dc for a, ke, dc in zip(q, k_e, dec)]
    rhs = [jnp.concatenate([_expand_heads((a * b).astype(BF16)), _expand_heads((c * e).astype(BF16))], axis=1)
           for a, b, c, e in zip(v, beta, kb, eg)]
    x = _inverse_unit_triangular(m)
    sol = [_dot(xi.astype(BF16), r) for xi, r in zip(x, rhs)]
    for i, ((j, d), r) in enumerate(zip(chains, rows)):
        u_ref[d, 0, r, :] = sol[i][:, 0:GROUP]
        w_ref[d, 0, r, :] = sol[i][:, GROUP:2 * GROUP].astype(BF16)
        attn_ref[d, 0, r, :] = attn[i].astype(BF16)
        qd_ref[d, 0, r, :] = (q[i] * eg[i]).astype(BF16)
        kd_ref[d, 0, r, :] = (k[i] * jnp.exp(g_last[i] - gc[i])).astype(BF16)
        gl_ref[d, 0, pl.ds(j * GL_ROWS, GL_ROWS), :] = jnp.broadcast_to(jnp.exp(g_last[i]), (GL_ROWS, GROUP))


def _same_head_mask():
    r = lax.broadcasted_iota(jnp.int32, (GROUP, GROUP), 0) // HEAD_DIM
    c = lax.broadcasted_iota(jnp.int32, (GROUP, GROUP), 1) // HEAD_DIM
    return r == c


def _dn_inter_kernel(uf, wf, af, qf, kf, gf, ub, wb, ab, qb, kb, gb, of_ref, ob_ref, s_ref):
    @pl.when(pl.program_id(1) == 0)
    def _():
        s_ref[...] = jnp.zeros_like(s_ref)

    same = _same_head_mask()
    ins = ((uf, wf, af, qf, kf, gf, of_ref), (ub, wb, ab, qb, kb, gb, ob_ref))
    chains = [(d, j) for j in range(INTER_BATCH) for d in range(2)]
    s = [s_ref[d * INTER_BATCH + j] for d, j in chains]
    s_b = [t.astype(BF16) for t in s]
    v_new = [(ins[d][0][0, j] - _dot(ins[d][1][0, j], sb)).astype(BF16) for (d, j), sb in zip(chains, s_b)]
    v_ne = [_expand_heads(t) for t in v_new]
    o = [_dot(ins[d][3][0, j], sb) + _dot(ins[d][2][0, j], ve) for (d, j), sb, ve in zip(chains, s_b, v_ne)]
    upd = [_dot_tn(ins[d][4][0, j], t) for (d, j), t in zip(chains, v_new)]
    for i, (d, j) in enumerate(chains):
        ins[d][6][j] = o[i]
        s_ref[d * INTER_BATCH + j] = s[i] * ins[d][5][0, j, 0:1, :] + jnp.where(same, upd[i], 0.0)


def _dn_finish_kernel(of_ref, ob_ref, z_ref, nw_ref, o_ref):
    hmean = _head_mean_matrix(1.0 / HEAD_DIM)
    nw = nw_ref[...]

    def finish(i, c):
        rows = pl.ds(pl.multiple_of(i * SEQ_BLK, SEQ_BLK), SEQ_BLK)
        o = of_ref[0, rows, :] + ob_ref[0, rows, :]
        o_ref[rows, :] = o * lax.rsqrt(_dot_split(o * o, hmean) + NORM_EPS) * nw * _silu(z_ref[rows, :])
        return c

    lax.fori_loop(0, N_BLK, finish, 0)


def _bwd_chunk(n):
    return _chunk_of_step(n, True)


def deltanet_mixer(slab, conv_w, a_log, dt_bias, norm_w):
    nb = slab.shape[0] // T_ALL
    assert nb % INTER_BATCH == 0 and N_CHUNK % INTRA_CHUNKS == 0
    lanes = lambda t: jnp.repeat(t, HEAD_DIM, axis=-1)
    seq = lambda dt: jax.ShapeDtypeStruct((nb, T_ALL, GROUP), dt)
    seq2 = lambda dt: jax.ShapeDtypeStruct((2, nb, T_ALL, GROUP), dt)
    b1 = pl.BlockSpec((1, T_ALL, GROUP), lambda b: (b, 0, 0))
    p1 = pl.BlockSpec((1, SEQ_BLK, GROUP), lambda b, i: (b, i, 0))
    p2 = pl.BlockSpec((2, 1, SEQ_BLK, GROUP), lambda b, i: (0, b, i, 0))
    small = lambda shape: pl.BlockSpec(shape, lambda b, i: (0, 0))
    q, k, v, gc, beta = pl.pallas_call(
        _dn_prep_kernel,
        grid=(nb, N_BLK),
        in_specs=[pl.BlockSpec((T_ALL, DN_W), lambda b, i: (b, 0)), small((3, 3 * GROUP)),
                  small((2, GROUP)), small((2, GROUP))],
        out_specs=[p1, p1, p1, p2, p2],
        out_shape=[seq(F32), seq(F32), seq(F32), seq2(F32), seq2(F32)],
        compiler_params=_cparams("parallel", "parallel"),
        name="deltanet_prep",
    )(slab, conv_w, lanes(a_log), lanes(dt_bias))

    rows = INTRA_CHUNKS * CHUNK
    c1 = pl.BlockSpec((1, rows, GROUP), lambda b, g: (b, g, 0))
    c2 = pl.BlockSpec((2, 1, rows, GROUP), lambda b, g: (0, b, g, 0))
    gl_shape = jax.ShapeDtypeStruct((2, nb, N_CHUNK * GL_ROWS, GROUP), F32)
    u, w, attn, qd, kd, gl = pl.pallas_call(
        _dn_intra_kernel,
        grid=(nb, N_CHUNK // INTRA_CHUNKS),
        in_specs=[c1, c1, c1, c2, c2],
        out_specs=[c2, c2, c2, c2, c2,
                   pl.BlockSpec((2, 1, INTRA_CHUNKS * GL_ROWS, GROUP), lambda b, g: (0, b, g, 0))],
        out_shape=[seq2(F32), seq2(BF16), seq2(BF16), seq2(BF16), seq2(BF16), gl_shape],
        compiler_params=_cparams("parallel", "parallel"),
        name="deltanet_intra",
    )(q, k, v, gc, beta)

    def per_dir(d, rows_per_chunk):
        pick = (lambda n: n) if d == 0 else _bwd_chunk
        return pl.BlockSpec((1, INTER_BATCH, rows_per_chunk, GROUP), lambda b, n: (d, b, pick(n), 0))

    def out_dir(d):
        pick = (lambda n: n) if d == 0 else _bwd_chunk
        return pl.BlockSpec((INTER_BATCH, CHUNK, GROUP), lambda b, n: (b, pick(n), 0))

    specs = [per_dir(d, r) for d in range(2) for r in (CHUNK,) * 5 + (GL_ROWS,)]
    o_f, o_b = pl.pallas_call(
        _dn_inter_kernel,
        grid=(nb // INTER_BATCH, N_CHUNK),
        in_specs=specs,
        out_specs=[out_dir(0), out_dir(1)],
        out_shape=[seq(F32), seq(F32)],
        scratch_shapes=[pltpu.VMEM((2 * INTER_BATCH, GROUP, GROUP), F32)],
        compiler_params=_cparams("parallel", "arbitrary"),
        name="deltanet_inter",
    )(u, w, attn, qd, kd, gl, u, w, attn, qd, kd, gl)

    return pl.pallas_call(
        _dn_finish_kernel,
        grid=(nb,),
        in_specs=[b1, b1, pl.BlockSpec((T_ALL, GROUP), lambda b: (b, 3)), _resident((1, GROUP))],
        out_specs=pl.BlockSpec((T_ALL, GROUP), lambda b: (b, 0)),
        out_shape=jax.ShapeDtypeStruct((nb * T_ALL, GROUP), F32),
        compiler_params=_cparams("parallel"),
        name="deltanet_finish",
    )(o_f, o_b, slab, jnp.tile(norm_w, GROUP_HEADS).reshape(1, GROUP))


RW_LR = RW_DECAY_RANK + RW_AAA_RANK + RW_GATE_RANK
RW_LR_OUT = 5 * GROUP


def _dot3(a, b_hi, b_lo):
    a_hi = a.astype(BF16)
    a_lo = (a - a_hi.astype(F32)).astype(BF16)
    return _dot(a_hi, b_hi) + (_dot(a_lo, b_hi) + _dot(a_hi, b_lo))


def rwkv_lowrank_weights(w_up, a_up, g_up):
    w = jnp.zeros((RW_LR, RW_LR_OUT), F32)
    o1 = RW_DECAY_RANK
    o2 = o1 + RW_AAA_RANK
    for d in range(2):
        w = w.at[0:o1, d * GROUP:(d + 1) * GROUP].set(w_up[d])
        w = w.at[o1:o2, (2 + d) * GROUP:(3 + d) * GROUP].set(a_up[d])
    return w.at[o2:RW_LR, 4 * GROUP:5 * GROUP].set(g_up)


def _rw_prep_kernel(slab_ref, mu_ref, pv_ref, wlr_ref,
                    at_ref, rt_ref, bh_ref, kh_ref, bg_ref, kg_ref, gl_ref, v_ref, bonus_ref, g_ref):
    hsum = _head_mean_matrix(1.0)
    wlr = wlr_ref[...]
    wlr_hi = wlr.astype(BF16)
    wlr_lo = (wlr - wlr_hi.astype(F32)).astype(BF16)
    pv = pv_ref[...]
    w0 = (pv[0:1], pv[1:2])
    a0 = (pv[2:3], pv[3:4])
    k_k, k_a, r_k = pv[4:5], pv[5:6], pv[6:7]
    lr_lane = lax.broadcasted_iota(jnp.int32, (1, RW_LR), 1)
    per_blk = SEQ_BLK // CHUNK

    def shifted(i, c0, c1):
        prev, cur, nxt = _prev_cur_next(slab_ref, i, c0, c1)
        return cur + mu_ref[0:1, c0:c1] * (prev - cur) + mu_ref[1:2, c0:c1] * (nxt - cur)

    def prep(i, c):
        rows = pl.ds(pl.multiple_of(i * SEQ_BLK, SEQ_BLK), SEQ_BLK)
        r = shifted(i, 0, GROUP)
        k = shifted(i, GROUP, 2 * GROUP)
        v = shifted(i, 2 * GROUP, 3 * GROUP)
        lr = shifted(i, 3 * GROUP, 3 * GROUP + RW_LR)
        t = jnp.where(lr_lane < RW_DECAY_RANK, jnp.tanh(lr),
                      jnp.where(lr_lane < RW_DECAY_RANK + RW_AAA_RANK, lr, _sigmoid(lr)))
        proj = _dot3(t, wlr_hi, wlr_lo)
        kq = k * k_k
        kk = kq * lax.rsqrt(_dot_split(kq * kq, hsum) + 1e-6)
        v_ref[0] = v.astype(BF16)
        g_ref[...] = proj[:, 4 * GROUP:5 * GROUP]
        ksum = jnp.zeros_like(k)
        for d in range(2):
            w_log = -_softplus(-(w0[d] + proj[:, d * GROUP:(d + 1) * GROUP])) - 0.5
            lw = -jnp.exp(w_log)
            a_gate = _sigmoid(a0[d] + proj[:, (2 + d) * GROUP:(3 + d) * GROUP])
            k_d = k * (1.0 + (a_gate - 1.0) * k_a)
            ksum = ksum + k_d
            cum = _chunk_cumsum(lw, reverse=(d == 1))
            ends = [cum[j * CHUNK:j * CHUNK + 1, :] if d == 1 else cum[(j + 1) * CHUNK - 1:(j + 1) * CHUNK, :]
                    for j in range(per_blk)]
            to_end = jnp.exp(jnp.concatenate([jnp.broadcast_to(e, (CHUNK, GROUP)) for e in ends], axis=0) - cum)
            inv = jnp.exp(-cum)
            b = kk * a_gate
            at_ref[d, 0] = (-kk * jnp.exp(cum - lw)).astype(BF16)
            rt_ref[d, 0] = (r * jnp.exp(cum)).astype(BF16)
            bh_ref[d, 0] = (b * inv).astype(BF16)
            kh_ref[d, 0] = (k_d * inv).astype(BF16)
            bg_ref[d, 0] = (b * to_end).astype(BF16)
            kg_ref[d, 0] = (k_d * to_end).astype(BF16)
            gl_ref[d, 0] = jnp.concatenate([jnp.broadcast_to(jnp.exp(e), (GL_ROWS, GROUP)) for e in ends], axis=0)
        bonus_ref[...] = _dot_split(r * ksum * r_k, hsum) * v
        return c

    prep(pl.program_id(1), 0)


def _rw_intra_kernel(at_ref, rt_ref, bh_ref, kh_ref, v_ref, xc_ref, arb_ref, rhs0_ref, yk_ref):
    masks = (_cat_masks(False), _cat_masks(True))
    chains = [(j, d) for j in range(INTRA_CHUNKS) for d in range(2)]
    rows = [pl.ds(j * CHUNK, CHUNK) for j, d in chains]
    at = [at_ref[d, 0, r, :] for (j, d), r in zip(chains, rows)]
    rt = [rt_ref[d, 0, r, :] for (j, d), r in zip(chains, rows)]
    bh_e = [_expand_heads(bh_ref[d, 0, r, :]) for (j, d), r in zip(chains, rows)]
    kh_e = [_expand_heads(kh_ref[d, 0, r, :]) for (j, d), r in zip(chains, rows)]
    v_e = [_expand_heads(v_ref[0, r, :]) for r in rows]
    x = _inverse_unit_triangular(
        [jnp.where(masks[d][1], -_dot_nt(a, b), 0.0) for (j, d), a, b in zip(chains, at, bh_e)])
    a_ak = [jnp.where(masks[d][1], _dot_nt(a, b), 0.0).astype(BF16) for (j, d), a, b in zip(chains, at, kh_e)]
    a_rb = [jnp.where(masks[d][0], _dot_nt(a, b), 0.0).astype(BF16) for (j, d), a, b in zip(chains, rt, bh_e)]
    a_rk = [jnp.where(masks[d][0], _dot_nt(a, b), 0.0).astype(BF16) for (j, d), a, b in zip(chains, rt, kh_e)]
    rhs0 = [_dot(a, ve) for a, ve in zip(a_ak, v_e)]
    yk = [_dot(a, ve) for a, ve in zip(a_rk, v_e)]
    for i, ((j, d), r) in enumerate(zip(chains, rows)):
        xc_ref[d, 0, r, :] = x[i].astype(BF16)
        arb_ref[d, 0, r, :] = a_rb[i]
        rhs0_ref[d, 0, r, :] = rhs0[i]
        yk_ref[d, 0, r, :] = yk[i]


def _rw_inter_kernel(*refs):
    n_in = 10
    fwd, bwd = refs[0:n_in], refs[n_in:2 * n_in]
    yf_ref, yb_ref, s_ref = refs[2 * n_in:]

    @pl.when(pl.program_id(1) == 0)
    def _():
        s_ref[...] = jnp.zeros_like(s_ref)

    same = _same_head_mask()
    ins = (fwd, bwd)
    outs = (yf_ref, yb_ref)
    chains = [(d, j) for j in range(INTER_BATCH) for d in range(2)]

    def arg(idx):
        return [ins[d][idx][0, j] for d, j in chains]

    at, rt, bg, kg, xc, arb, rhs0, yk = (arg(i) for i in range(8))
    gamma = [ins[d][8][0, j, 0:1, :] for d, j in chains]
    v = [ins[d][9][j] for d, j in chains]
    s = [s_ref[d * INTER_BATCH + j] for d, j in chains]
    s_b = [t.astype(BF16) for t in s]
    rhs = [_expand_heads((_dot_nt(a, sb) + r0).astype(BF16)) for a, sb, r0 in zip(at, s_b, rhs0)]
    sa = [_dot(x, r).astype(BF16) for x, r in zip(xc, rhs)]
    sa_e = [_expand_heads(t) for t in sa]
    y = [_dot_nt(r, sb) + _dot(a, se) + y0 for r, sb, a, se, y0 in zip(rt, s_b, arb, sa_e, yk)]
    upd = [_dot_tn(a, b) + _dot_tn(c, e) for a, b, c, e in zip(sa, bg, v, kg)]
    for i, (d, j) in enumerate(chains):
        outs[d][j] = y[i]
        s_ref[d * INTER_BATCH + j] = s[i] * gamma[i] + jnp.where(same, upd[i], 0.0)


def _rw_finish_kernel(yf_ref, yb_ref, bonus_ref, g_ref, pv_ref, o_ref):
    hmean = _head_mean_matrix(1.0 / HEAD_DIM)
    ln_w, ln_b = pv_ref[7:8, :], pv_ref[8:9, :]

    def finish(i, c):
        rows = pl.ds(pl.multiple_of(i * SEQ_BLK, SEQ_BLK), SEQ_BLK)
        y = yf_ref[0, rows, :] + yb_ref[0, rows, :]
        yc = y - _dot_split(y, hmean)
        yn = yc * lax.rsqrt(_dot_split(yc * yc, hmean) + RW_LN_EPS) * ln_w + ln_b
        o_ref[rows, :] = (yn + bonus_ref[rows, :]) * g_ref[rows, :]
        return c

    lax.fori_loop(0, N_BLK, finish, 0)


def rwkv_mixer(slab, mu, w0, w_up, a0, a_up, g_up, k_k, k_a, r_k, ln_w, ln_b):
    n = slab.shape[0]
    nb = n // T_ALL
    assert nb % INTER_BATCH == 0 and N_CHUNK % INTRA_CHUNKS == 0
    pv = jnp.concatenate([w0, a0, k_k[None], k_a[None], r_k.reshape(1, GROUP), ln_w[None], ln_b[None],
                          jnp.zeros((7, GROUP), F32)], axis=0)
    seq = lambda dt: jax.ShapeDtypeStruct((nb, T_ALL, GROUP), dt)
    seq2 = lambda dt: jax.ShapeDtypeStruct((2, nb, T_ALL, GROUP), dt)
    flat = jax.ShapeDtypeStruct((n, GROUP), F32)
    gl_shape = jax.ShapeDtypeStruct((2, nb, N_CHUNK * GL_ROWS, GROUP), F32)
    b1 = pl.BlockSpec((1, T_ALL, GROUP), lambda b: (b, 0, 0))
    b2 = pl.BlockSpec((2, 1, T_ALL, GROUP), lambda b: (0, b, 0, 0))
    bflat = pl.BlockSpec((T_ALL, GROUP), lambda b: (b, 0))
    p1 = pl.BlockSpec((1, SEQ_BLK, GROUP), lambda b, i: (b, i, 0))
    p2 = pl.BlockSpec((2, 1, SEQ_BLK, GROUP), lambda b, i: (0, b, i, 0))
    pflat = pl.BlockSpec((SEQ_BLK, GROUP), lambda b, i: (b * N_BLK + i, 0))
    pgl = pl.BlockSpec((2, 1, SEQ_BLK // CHUNK * GL_ROWS, GROUP), lambda b, i: (0, b, i, 0))
    small = lambda shape: pl.BlockSpec(shape, lambda b, i: (0, 0))
    at, rt, bh, kh, bg, kg, gl, v, bonus, g = pl.pallas_call(
        _rw_prep_kernel,
        grid=(nb, N_BLK),
        in_specs=[pl.BlockSpec((T_ALL, RW_W), lambda b, i: (b, 0)), small((2, RW_W)), small((16, GROUP)),
                  small((RW_LR, RW_LR_OUT))],
        out_specs=[p2] * 6 + [pgl, p1, pflat, pflat],
        out_shape=[seq2(BF16)] * 6 + [gl_shape, seq(BF16), flat, flat],
        compiler_params=_cparams("parallel", "parallel"),
        name="rwkv7_prep",
    )(slab, mu, pv, rwkv_lowrank_weights(w_up, a_up, g_up))

    rows = INTRA_CHUNKS * CHUNK
    c1 = pl.BlockSpec((1, rows, GROUP), lambda b, g_: (b, g_, 0))
    c2 = pl.BlockSpec((2, 1, rows, GROUP), lambda b, g_: (0, b, g_, 0))
    xc, arb, rhs0, yk = pl.pallas_call(
        _rw_intra_kernel,
        grid=(nb, N_CHUNK // INTRA_CHUNKS),
        in_specs=[c2, c2, c2, c2, c1],
        out_specs=[c2, c2, c2, c2],
        out_shape=[seq2(BF16), seq2(BF16), seq2(F32), seq2(F32)],
        compiler_params=_cparams("parallel", "parallel"),
        name="rwkv7_intra",
    )(at, rt, bh, kh, v)

    def chunk_pick(d):
        return (lambda n_: n_) if d == 0 else _bwd_chunk

    def per_dir(d, rows_per_chunk):
        pick = chunk_pick(d)
        return pl.BlockSpec((1, INTER_BATCH, rows_per_chunk, GROUP), lambda b, n_: (d, b, pick(n_), 0))

    def shared(d):
        pick = chunk_pick(d)
        return pl.BlockSpec((INTER_BATCH, CHUNK, GROUP), lambda b, n_: (b, pick(n_), 0))

    specs = [s for d in range(2) for s in [per_dir(d, CHUNK)] * 8 + [per_dir(d, GL_ROWS), shared(d)]]
    per = (at, rt, bg, kg, xc, arb, rhs0, yk, gl, v)
    y_f, y_b = pl.pallas_call(
        _rw_inter_kernel,
        grid=(nb // INTER_BATCH, N_CHUNK),
        in_specs=specs,
        out_specs=[shared(0), shared(1)],
        out_shape=[seq(F32), seq(F32)],
        scratch_shapes=[pltpu.VMEM((2 * INTER_BATCH, GROUP, GROUP), F32)],
        compiler_params=_cparams("parallel", "arbitrary"),
        name="rwkv7_inter",
    )(*per, *per)

    return pl.pallas_call(
        _rw_finish_kernel,
        grid=(nb,),
        in_specs=[b1, b1, bflat, bflat, _resident((16, GROUP))],
        out_specs=bflat,
        out_shape=flat,
        compiler_params=_cparams("parallel"),
        name="rwkv7_finish",
    )(y_f, y_b, bonus, g, pv)


DFT_SPLIT = 64
DFT_BLK = 256


def _dft_tables(n):
    big = 2 * n
    t = np.arange(n, dtype=np.int64)[:, None]
    k1 = np.arange(n // DFT_SPLIT, dtype=np.int64)[None, :]
    k2 = np.arange(DFT_SPLIT, dtype=np.int64)[None, :]
    alpha = 2.0 * np.pi * ((DFT_SPLIT * t * k1) % big) / big
    beta = 2.0 * np.pi * ((t * k2) % big) / big

    def pad(a):
        out = np.zeros((n, 128), np.float32)
        out[:, :a.shape[1]] = a
        return out

    return np.stack([pad(np.cos(alpha)), pad(np.sin(alpha)), pad(np.cos(beta)), pad(np.sin(beta))])


def _dft_gen_kernel(n, tab_ref, g_ref):
    k = lax.broadcasted_iota(jnp.int32, (128, n), 1)
    row = lax.broadcasted_iota(jnp.int32, (128, n), 0)
    e_a = jnp.where(k // DFT_SPLIT == row, 1.0, 0.0).astype(BF16)
    e_b = jnp.where(jnp.logical_and(k % DFT_SPLIT == row, row < DFT_SPLIT), 1.0, 0.0).astype(BF16)
    ca = _dot_split(tab_ref[0], e_a)
    sa = _dot_split(tab_ref[1], e_a)
    cb = _dot_split(tab_ref[2], e_b)
    sb = _dot_split(tab_ref[3], e_b)
    g_ref[:, 0:n] = (ca * cb - sa * sb).astype(BF16)
    g_ref[:, n:2 * n] = (-(sa * cb + ca * sb)).astype(BF16)


def dft_matrix(n):
    blk = min(DFT_BLK, n)
    return pl.pallas_call(
        functools.partial(_dft_gen_kernel, n),
        grid=(n // blk,),
        in_specs=[pl.BlockSpec((4, blk, 128), lambda i: (0, i, 0))],
        out_specs=pl.BlockSpec((blk, 2 * n), lambda i: (i, 0)),
        out_shape=jax.ShapeDtypeStruct((n, 2 * n), BF16),
        compiler_params=_cparams("parallel"),
        name=f"dft_matrix_{n}",
    )(jnp.asarray(_dft_tables(n)))


HY_COLS_F = 2 * HY_ORDER * GROUP
HY_OC = HY_ORDER * GROUP


def _hyena_filter_kernel(n, z_ref, w1_ref, b1_ref, w2_ref, b2_ref, w3_ref, freq_ref, dl_ref, hs_ref, hd_ref):
    blk = min(SEQ_BLK, n)
    freq = freq_ref[...]
    dl = dl_ref[...]

    def fill(i, norm):
        r0 = pl.multiple_of(i * blk, blk)
        z = z_ref[pl.ds(r0, blk), :]
        hid = jnp.sin(freq * (_dot_hi(z, w1_ref[...]) + b1_ref[...]))
        hid = jnp.sin(freq * (_dot_hi(hid, w2_ref[...]) + b2_ref[...]))
        t = z[:, 0:1]
        h = _dot_hi(hid, w3_ref[...]) * jnp.exp(-t * dl)
        lag = lax.broadcasted_iota(jnp.int32, (blk, 1), 0) + r0
        hf = h[:, 0:HY_OC]
        hb = jnp.where(lag == 0, 0.0, h[:, HY_OC:2 * HY_OC])
        hs_ref[pl.ds(r0, blk), :] = hf + hb
        hd_ref[pl.ds(r0, blk), :] = hf - hb
        return norm + jnp.sum(jnp.abs(hf) + jnp.abs(hb), axis=0, keepdims=True)

    norm = lax.fori_loop(0, n // blk, fill, jnp.zeros((1, HY_OC), F32))
    inv = 1.0 / norm

    def scale(i, c):
        rows = pl.ds(pl.multiple_of(i * blk, blk), blk)
        hs_ref[rows, :] = hs_ref[rows, :] * inv
        hd_ref[rows, :] = hd_ref[rows, :] * inv
        return c

    lax.fori_loop(0, n // blk, scale, 0)


def hyena_filter_taps(n, f_w1, f_b1, f_w2, f_b2, f_w3, f_freq):
    t = jnp.linspace(0.0, 1.0, n, dtype=F32)[:, None]
    ang = 2.0 * math.pi * jnp.arange(n, dtype=F32)[:, None] / n
    bands = jnp.linspace(1e-4, HY_BANDS - 1, HY_BANDS, dtype=F32)[None]
    z = jnp.concatenate([t, jnp.cos(bands * ang), -jnp.sin(bands * ang)], axis=-1)
    emb = z.shape[1]
    z = jnp.pad(z, ((0, 0), (0, 128 - emb)))
    w1 = jnp.pad(f_w1, ((0, 128 - emb), (0, 0)))
    max_decay = math.log(HY_TARGET) / HY_SHORT_DECAY_PCT
    min_decay = math.log(HY_TARGET) / HY_LONG_DECAY_PCT
    deltas = jnp.abs(jnp.linspace(min_decay, max_decay, HY_OC, dtype=F32))
    dl = jnp.tile(deltas, 2).reshape(1, HY_COLS_F)
    hid = f_w2.shape[0]
    out = jax.ShapeDtypeStruct((n, HY_OC), F32)
    return pl.pallas_call(
        functools.partial(_hyena_filter_kernel, n),
        out_shape=[out, out],
        compiler_params=pltpu.CompilerParams(vmem_limit_bytes=VMEM_LIMIT),
        name=f"hyena_filter_{n}",
    )(z, w1, f_b1.reshape(1, hid), f_w2, f_b2.reshape(1, hid), f_w3, f_freq.reshape(1, hid), dl)


def _hyena_spectrum_kernel(n, g_ref, hs_ref, hd_ref, kr_ref, ki_ref, kn_ref):
    blk = min(2 * SEQ_BLK, n)
    big = 2.0 * n

    def split(ref):
        x = ref[...]
        hi = x.astype(BF16)
        return hi, (x - hi.astype(F32)).astype(BF16)

    s_hi, s_lo = split(hs_ref)
    d_hi, d_lo = split(hd_ref)

    def body(i, c):
        r0 = pl.multiple_of(i * blk, blk)
        rows = pl.ds(r0, blk)
        k = lax.broadcasted_iota(jnp.int32, (blk, 1), 0) + r0
        wgt = jnp.where(k == 0, 1.0 / big, 2.0 / big)
        gc = g_ref[rows, 0:n]
        gs = g_ref[rows, n:2 * n]
        kr_ref[rows, :] = (_dot(gc, s_hi) + _dot(gc, s_lo)) * wgt
        ki_ref[rows, :] = (_dot(gs, d_hi) + _dot(gs, d_lo)) * wgt
        return c

    lax.fori_loop(0, n // blk, body, 0)
    t = lax.broadcasted_iota(jnp.int32, (n, 1), 0)
    sign = jnp.where(t % 2 == 0, 1.0, -1.0)
    kn_ref[...] = jnp.broadcast_to(jnp.sum(sign * hs_ref[...], axis=0, keepdims=True) * (1.0 / big), (8, HY_OC))


def hyena_spectrum(n, g, hs, hd):
    out = jax.ShapeDtypeStruct((n, HY_OC), F32)
    return pl.pallas_call(
        functools.partial(_hyena_spectrum_kernel, n),
        out_shape=[out, out, jax.ShapeDtypeStruct((8, HY_OC), F32)],
        compiler_params=pltpu.CompilerParams(vmem_limit_bytes=VMEM_LIMIT),
        name=f"hyena_spectrum_{n}",
    )(g, hs, hd)


def _hyena_conv_kernel(slab_ref, w_ref, o_ref):
    def body(i, c):
        rows = pl.ds(pl.multiple_of(i * SEQ_BLK, SEQ_BLK), SEQ_BLK)
        for j in range(3):
            prev, cur, nxt = _prev_cur_next(slab_ref, i, j * GROUP, (j + 1) * GROUP)
            w = w_ref[:, j * GROUP:(j + 1) * GROUP]
            o_ref[rows, j * GROUP:(j + 1) * GROUP] = prev * w[0:1] + cur * w[1:2] + nxt * w[2:3]
        return c

    lax.fori_loop(0, N_BLK, body, 0)


def hyena_short_conv(slab, conv_w):
    n = slab.shape[0]
    return pl.pallas_call(
        _hyena_conv_kernel,
        grid=(n // T_ALL,),
        in_specs=[pl.BlockSpec((T_ALL, 3 * GROUP), lambda b: (b, 0)),
                  pl.BlockSpec((3, 3 * GROUP), lambda b: (0, 0))],
        out_specs=pl.BlockSpec((T_ALL, 3 * GROUP), lambda b: (b, 0)),
        out_shape=jax.ShapeDtypeStruct((n, 3 * GROUP), F32),
        compiler_params=_cparams("parallel"),
        name="hyena_short_conv",
    )(slab, conv_w)


HY_FBLK = 512


def _alt_sign(n):
    t = lax.broadcasted_iota(jnp.int32, (n, 1), 0)
    return jnp.where(t % 2 == 0, 1.0, -1.0)


def _hyena_fwd_kernel(x_ref, gl_ref, gc_ref, krl_ref, kil_ref, knl_ref, krc_ref, kic_ref, knc_ref,
                      pl_ref, pc_ref, pn_ref):
    def transform(x, g_ref, kr_ref, ki_ref, kn_ref, p_ref, n, blk):
        xb = x.astype(BF16)

        def body(i, c):
            rows = pl.ds(pl.multiple_of(i * blk, blk), blk)
            zr = _dot(g_ref[rows, 0:n], xb)
            zi = _dot(g_ref[rows, n:2 * n], xb)
            kr = kr_ref[rows, :]
            ki = ki_ref[rows, :]
            p_ref[0, 0, rows, :] = (zr * kr - zi * ki).astype(BF16)
            p_ref[0, 1, rows, :] = (zr * ki + zi * kr).astype(BF16)
            return c

        lax.fori_loop(0, n // blk, body, 0)
        return jnp.sum(_alt_sign(n) * x, axis=0, keepdims=True) * kn_ref[0:1, :]

    nyq_c = transform(x_ref[0:CTX_LEN, :], gc_ref, krc_ref, kic_ref, knc_ref, pc_ref, CTX_LEN, CTX_LEN)
    nyq_l = transform(x_ref[CTX_LEN:T_ALL, :], gl_ref, krl_ref, kil_ref, knl_ref, pl_ref, SEQ, HY_FBLK)
    pn_ref[0] = jnp.concatenate([nyq_l, nyq_c, jnp.zeros((6, GROUP), F32)], axis=0)


def _resident(shape):
    return pl.BlockSpec(shape, lambda b: (0,) * len(shape))


def hyena_forward_transform(x, col, g_l, g_c, spec_l, spec_c, order):
    n = x.shape[0]
    nb = n // T_ALL
    kcol = lambda shape: pl.BlockSpec(shape, lambda b: (0, order))
    return pl.pallas_call(
        _hyena_fwd_kernel,
        grid=(nb,),
        in_specs=[
            pl.BlockSpec((T_ALL, GROUP), lambda b: (b, col)),
            _resident((SEQ, 2 * SEQ)), _resident((CTX_LEN, 2 * CTX_LEN)),
            kcol((SEQ, GROUP)), kcol((SEQ, GROUP)), kcol((8, GROUP)),
            kcol((CTX_LEN, GROUP)), kcol((CTX_LEN, GROUP)), kcol((8, GROUP)),
        ],
        out_specs=[
            pl.BlockSpec((1, 2, SEQ, GROUP), lambda b: (b, 0, 0, 0)),
            pl.BlockSpec((1, 2, CTX_LEN, GROUP), lambda b: (b, 0, 0, 0)),
            pl.BlockSpec((1, 8, GROUP), lambda b: (b, 0, 0)),
        ],
        out_shape=[
            jax.ShapeDtypeStruct((nb, 2, SEQ, GROUP), BF16),
            jax.ShapeDtypeStruct((nb, 2, CTX_LEN, GROUP), BF16),
            jax.ShapeDtypeStruct((nb, 8, GROUP), F32),
        ],
        compiler_params=_cparams("parallel"),
        name=f"hyena_fwd_{order}",
    )(x, g_l, g_c, *spec_l, *spec_c)


def _hyena_inv_kernel(pl_ref, pc_ref, pn_ref, gl_ref, gc_ref, u_ref, gate_ref, bias_ref, o_ref):
    bias = bias_ref[0]

    def inverse(p_ref, nyq, g_ref, n, blk, off):
        pr = p_ref[0, 0]
        pi = p_ref[0, 1]

        def body(i, c):
            r0 = pl.multiple_of(i * blk, blk)
            rows = pl.ds(r0, blk)
            orow = pl.ds(pl.multiple_of(off + r0, blk), blk)
            t = lax.broadcasted_iota(jnp.int32, (blk, 1), 0)
            sign = jnp.where(t % 2 == 0, 1.0, -1.0)
            y = _dot(g_ref[rows, 0:n], pr) + _dot(g_ref[rows, n:2 * n], pi) + sign * nyq
            o_ref[orow, :] = gate_ref[orow, :] * (y + u_ref[orow, :] * bias)
            return c

        lax.fori_loop(0, n // blk, body, 0)

    inverse(pc_ref, pn_ref[0, 1:2, :], gc_ref, CTX_LEN, CTX_LEN, 0)
    inverse(pl_ref, pn_ref[0, 0:1, :], gl_ref, SEQ, SEQ_BLK, CTX_LEN)


def hyena_inverse_transform(p_l, p_c, p_n, g_l, g_c, u, ucol, gate, gcol, bias):
    nb = p_l.shape[0]
    return pl.pallas_call(
        _hyena_inv_kernel,
        grid=(nb,),
        in_specs=[
            pl.BlockSpec((1, 2, SEQ, GROUP), lambda b: (b, 0, 0, 0)),
            pl.BlockSpec((1, 2, CTX_LEN, GROUP), lambda b: (b, 0, 0, 0)),
            pl.BlockSpec((1, 8, GROUP), lambda b: (b, 0, 0)),
            _resident((SEQ, 2 * SEQ)), _resident((CTX_LEN, 2 * CTX_LEN)),
            pl.BlockSpec((T_ALL, GROUP), lambda b: (b, ucol)),
            pl.BlockSpec((T_ALL, GROUP), lambda b: (b, gcol)),
            pl.BlockSpec((1, 1, GROUP), lambda b: (0, 0, 0)),
        ],
        out_specs=pl.BlockSpec((T_ALL, GROUP), lambda b: (b, 0)),
        out_shape=jax.ShapeDtypeStruct((nb * T_ALL, GROUP), F32),
        compiler_params=_cparams("parallel"),
        name="hyena_inv",
    )(p_l, p_c, p_n, g_l, g_c, u, gate, bias.reshape(1, 1, GROUP))


def hyena_mixer(slab, g_l, g_c, conv_w, f_w1, f_b1, f_w2, f_b2, f_w3, f_freq, bias):
    u = hyena_short_conv(slab, conv_w)
    spec_l = hyena_spectrum(SEQ, g_l, *hyena_filter_taps(SEQ, f_w1, f_b1, f_w2, f_b2, f_w3, f_freq))
    spec_c = hyena_spectrum(CTX_LEN, g_c, *hyena_filter_taps(CTX_LEN, f_w1, f_b1, f_w2, f_b2, f_w3, f_freq))
    p = hyena_forward_transform(u, 0, g_l, g_c, spec_l, spec_c, 0)
    z = hyena_inverse_transform(*p, g_l, g_c, u, 0, u, 1, bias[0])
    p = hyena_forward_transform(z, 0, g_l, g_c, spec_l, spec_c, 1)
    return hyena_inverse_transform(*p, g_l, g_c, z, 0, u, 2, bias[1])


def kernel(x, c, ctx, c_ctx, w_mod, b_mod, norm_w, ffn_w_gu, ffn_w_down, w_in, w_out,
           hy_conv, hy_f_w1, hy_f_b1, hy_f_w2, hy_f_b2, hy_f_w3, hy_f_freq, hy_bias,
           na_q_norm, na_k_norm, na_rpb, dn_conv, dn_a_log, dn_dt_bias, dn_norm,
           rw_mu, rw_w0, rw_w_up, rw_a0, rw_a_up, rw_g_up, rw_k_k, rw_k_a, rw_r_k, rw_ln_w, rw_ln_b):
    nb = x.shape[0]
    assert x.shape[1:] == (SEQ, D_MODEL) and ctx.shape[1:] == (CTX_LEN, D_MODEL) and nb + 1 <= 16
    s = jnp.concatenate([ctx, x], axis=1).reshape(nb * T_ALL, D_MODEL)
    cond = jnp.concatenate([c_ctx[None], c, jnp.zeros((15 - nb, D_MODEL), F32)], axis=0)
    mod = modulation_all(cond, w_mod, b_mod).reshape(DEPTH, 16, N_MOD, D_MODEL)
    g_l = dft_matrix(SEQ)
    g_c = dft_matrix(CTX_LEN)
    w_gu = ffn_w_gu.astype(BF16)
    w_down = ffn_w_down.astype(BF16)
    w_out_b = w_out.astype(BF16)
    dn_end = 6 * GROUP + 4 * GROUP + 4 * GROUP_HEADS
    w_in_p = jnp.concatenate(
        [w_in[:, :, :dn_end], jnp.zeros((DEPTH, D_MODEL, 6 * GROUP + DN_W - dn_end), F32), w_in[:, :, dn_end:]],
        axis=2).astype(BF16)
    for l in range(DEPTH):
        need_ctx = l < DEPTH - 1
        modc = mod[l, 0:1]
        modb = mod[l, 1:1 + nb]
        s = ffn_half_step(s, modc, modb, norm_w[l, 0], w_gu, w_down, l, 0, 0)
        hy_s, na_s, dn_s, rw_s = input_projection(s, modc, modb, norm_w[l, 1], w_in_p, l)
        groups = (
            hyena_mixer(hy_s, g_l, g_c, hy_conv[l], hy_f_w1[l], hy_f_b1[l], hy_f_w2[l], hy_f_b2[l], hy_f_w3[l],
                        hy_f_freq[l], hy_bias[l]),
            na_mixer(na_s, na_q_norm[l], na_k_norm[l], na_bias_table(na_rpb[l]), need_ctx),
            deltanet_mixer(dn_s, dn_conv[l], dn_a_log[l], dn_dt_bias[l], dn_norm[l]),
            rwkv_mixer(rw_s, rw_mu[l], rw_w0[l], rw_w_up[l], rw_a0[l], rw_a_up[l], rw_g_up[l], rw_k_k[l],
                       rw_k_a[l], rw_r_k[l], rw_ln_w[l], rw_ln_b[l]),
        )
        s = output_projection(s, modc, modb, groups, w_out_b, l)
        s = ffn_half_step(s, modc, modb, norm_w[l, 2], w_gu, w_down, l, 1, 2)
    return s.reshape(nb, T_ALL, D_MODEL)[:, CTX_LEN:]
```

```python
import functools
import math

import numpy as np
import jax
import jax.numpy as jnp
from jax import lax
from jax.experimental import pallas as pl
from jax.experimental.pallas import tpu as pltpu

D_MODEL = 1024
SEQ = 2048
DEPTH = 2
CTX_LEN = 256
T_ALL = CTX_LEN + SEQ
GRID_W = 64
GROUP = 256
HEAD_DIM = 64
GROUP_HEADS = 4
D_FF = 2816
N_MOD = 9
NORM_EPS = 1e-6

HY_ORDER = 2
HY_BANDS = 16
HY_TARGET = 1e-2
HY_SHORT_DECAY_PCT = 0.3
HY_LONG_DECAY_PCT = 1.5

NA_WIN_ROWS = 8
NA_WIN_COLS = 16

CHUNK = 64
RW_DECAY_RANK = 32
RW_AAA_RANK = 32
RW_GATE_RANK = 64
RW_LN_EPS = 64e-5

DN_W = 4 * GROUP + 128
RW_W = 3 * GROUP + 128
P_PAD = 3 * GROUP + 3 * GROUP + DN_W + RW_W

TM = 768
TF = 512
VMEM_LIMIT = 56 * 1024 * 1024

F32 = jnp.float32
BF16 = jnp.bfloat16


def _cparams(*sem):
    return pltpu.CompilerParams(dimension_semantics=sem, vmem_limit_bytes=VMEM_LIMIT)


def _silu(x):
    return x * (1.0 / (1.0 + jnp.exp(-x)))


def _sigmoid(x):
    return 1.0 / (1.0 + jnp.exp(-x))


def _softplus(x):
    return jnp.maximum(x, 0.0) + jnp.log(1.0 + jnp.exp(-jnp.abs(x)))


def _dot(a, b):
    return jnp.dot(a, b, preferred_element_type=F32)


def _dot_nt(a, b):
    return lax.dot_general(a, b, (((1,), (1,)), ((), ())), preferred_element_type=F32)


def _dot_tn(a, b):
    return lax.dot_general(a, b, (((0,), (0,)), ((), ())), preferred_element_type=F32)


def _dot_hi(a, b):
    return jnp.dot(a, b, preferred_element_type=F32, precision=lax.Precision.HIGHEST)


def _mod_kernel(cond_ref, w_ref, b_ref, o_ref):
    a = _silu(cond_ref[...]).astype(BF16)
    o_ref[0] = _dot(a, w_ref[0].astype(BF16)) + b_ref[0]


def modulation_all(cond, w_mod, b_mod):
    r = cond.shape[0]
    tn = 1024
    return pl.pallas_call(
        _mod_kernel,
        grid=(DEPTH, N_MOD * D_MODEL // tn),
        in_specs=[
            pl.BlockSpec((r, D_MODEL), lambda l, j: (0, 0)),
            pl.BlockSpec((1, D_MODEL, tn), lambda l, j: (l, 0, j)),
            pl.BlockSpec((1, 1, tn), lambda l, j: (l, 0, j)),
        ],
        out_specs=pl.BlockSpec((1, r, tn), lambda l, j: (l, 0, j)),
        out_shape=jax.ShapeDtypeStruct((DEPTH, r, N_MOD * D_MODEL), F32),
        compiler_params=_cparams("parallel", "parallel"),
        name="modulation",
    )(cond, w_mod, b_mod.reshape(DEPTH, 1, N_MOD * D_MODEL))


def _row_mod(modc_ref, modb_ref, tile, idx):
    row = lax.broadcasted_iota(jnp.int32, (TM, 1), 0) + (tile % (T_ALL // TM)) * TM
    return jnp.where(row < CTX_LEN, modc_ref[0, idx:idx + 1, :], modb_ref[0, idx:idx + 1, :])


def _adaln(x, nw, shift, scale):
    y = x * lax.rsqrt(jnp.mean(x * x, axis=-1, keepdims=True) + NORM_EPS)
    return y * nw * (1.0 + scale) + shift


def _ffn_kernel(sub, x_ref, modc_ref, modb_ref, nw_ref, wgu_ref, wd_ref, o_ref):
    i = pl.program_id(0)
    x = x_ref[...]
    shift = _row_mod(modc_ref, modb_ref, i, 3 * sub)
    scale = _row_mod(modc_ref, modb_ref, i, 3 * sub + 1)
    h = _adaln(x, nw_ref[...], shift, scale).astype(BF16)
    acc = None
    for c0 in range(0, D_FF, TF):
        c1 = min(c0 + TF, D_FF)
        a = (_silu(_dot(h, wgu_ref[:, c0:c1])) * _dot(h, wgu_ref[:, D_FF + c0:D_FF + c1])).astype(BF16)
        part = _dot(a, wd_ref[c0:c1, :])
        acc = part if acc is None else acc + part
    gate = _row_mod(modc_ref, modb_ref, i, 3 * sub + 2)
    o_ref[...] = x + 0.5 * gate * acc


def ffn_half_step(x, modc, modb, nw, w_gu, w_down, layer, which, sub):
    n = x.shape[0]
    tiles_per_b = T_ALL // TM
    once = pl.Buffered(1)
    return pl.pallas_call(
        functools.partial(_ffn_kernel, sub),
        grid=(n // TM,),
        in_specs=[
            pl.BlockSpec((TM, D_MODEL), lambda i: (i, 0)),
            pl.BlockSpec((1, N_MOD, D_MODEL), lambda i: (0, 0, 0)),
            pl.BlockSpec((1, N_MOD, D_MODEL), lambda i: (i // tiles_per_b, 0, 0)),
            pl.BlockSpec((1, D_MODEL), lambda i: (0, 0)),
            pl.BlockSpec((None, None, D_MODEL, 2 * D_FF), lambda i: (layer, which, 0, 0), pipeline_mode=once),
            pl.BlockSpec((None, None, D_FF, D_MODEL), lambda i: (layer, which, 0, 0), pipeline_mode=once),
        ],
        out_specs=pl.BlockSpec((TM, D_MODEL), lambda i: (i, 0)),
        out_shape=jax.ShapeDtypeStruct((n, D_MODEL), F32),
        compiler_params=_cparams("parallel"),
        name=f"ffn{sub}",
    )(x, modc, modb, nw.reshape(1, D_MODEL), w_gu, w_down)


def _inproj_kernel(x_ref, modc_ref, modb_ref, nw_ref, w_ref, hy_ref, na_ref, dn_ref, rw_ref):
    i = pl.program_id(0)
    shift = _row_mod(modc_ref, modb_ref, i, 3)
    scale = _row_mod(modc_ref, modb_ref, i, 4)
    h = _adaln(x_ref[...], nw_ref[...], shift, scale).astype(BF16)
    o0 = 3 * GROUP
    o1 = 6 * GROUP
    o2 = o1 + DN_W
    hy_ref[...] = _dot(h, w_ref[:, 0:o0])
    na_ref[...] = _dot(h, w_ref[:, o0:o1])
    dn_ref[...] = _dot(h, w_ref[:, o1:o2])
    rw_ref[...] = _dot(h, w_ref[:, o2:P_PAD])


def input_projection(x, modc, modb, nw, w_in_p, layer):
    n = x.shape[0]
    tiles_per_b = T_ALL // TM
    widths = (3 * GROUP, 3 * GROUP, DN_W, RW_W)
    return pl.pallas_call(
        _inproj_kernel,
        grid=(n // TM,),
        in_specs=[
            pl.BlockSpec((TM, D_MODEL), lambda i: (i, 0)),
            pl.BlockSpec((1, N_MOD, D_MODEL), lambda i: (0, 0, 0)),
            pl.BlockSpec((1, N_MOD, D_MODEL), lambda i: (i // tiles_per_b, 0, 0)),
            pl.BlockSpec((1, D_MODEL), lambda i: (0, 0)),
            pl.BlockSpec((None, D_MODEL, P_PAD), lambda i: (layer, 0, 0)),
        ],
        out_specs=[pl.BlockSpec((TM, w), lambda i: (i, 0)) for w in widths],
        out_shape=[jax.ShapeDtypeStruct((n, w), F32) for w in widths],
        compiler_params=_cparams("parallel"),
        name="inproj",
    )(x, modc, modb, nw.reshape(1, D_MODEL), w_in_p)


def _outproj_kernel(x_ref, modc_ref, modb_ref, g0_ref, g1_ref, g2_ref, g3_ref, w_ref, o_ref):
    i = pl.program_id(0)
    y = _dot(g0_ref[...].astype(BF16), w_ref[0:GROUP, :])
    y += _dot(g1_ref[...].astype(BF16), w_ref[GROUP:2 * GROUP, :])
    y += _dot(g2_ref[...].astype(BF16), w_ref[2 * GROUP:3 * GROUP, :])
    y += _dot(g3_ref[...].astype(BF16), w_ref[3 * GROUP:4 * GROUP, :])
    gate = _row_mod(modc_ref, modb_ref, i, 5)
    o_ref[...] = x_ref[...] + gate * y


def output_projection(x, modc, modb, groups, w_out, layer):
    n = x.shape[0]
    tiles_per_b = T_ALL // TM
    return pl.pallas_call(
        _outproj_kernel,
        grid=(n // TM,),
        in_specs=[
            pl.BlockSpec((TM, D_MODEL), lambda i: (i, 0)),
            pl.BlockSpec((1, N_MOD, D_MODEL), lambda i: (0, 0, 0)),
            pl.BlockSpec((1, N_MOD, D_MODEL), lambda i: (i // tiles_per_b, 0, 0)),
        ] + [pl.BlockSpec((TM, GROUP), lambda i: (i, 0))] * 4 + [
            pl.BlockSpec((None, D_MODEL, D_MODEL), lambda i: (layer, 0, 0)),
        ],
        out_specs=pl.BlockSpec((TM, D_MODEL), lambda i: (i, 0)),
        out_shape=jax.ShapeDtypeStruct((n, D_MODEL), F32),
        compiler_params=_cparams("parallel"),
        name="outproj",
    )(x, modc, modb, *groups, w_out)


def _head_mean_matrix(scale):
    r = lax.broadcasted_iota(jnp.int32, (GROUP, GROUP), 0) // HEAD_DIM
    c = lax.broadcasted_iota(jnp.int32, (GROUP, GROUP), 1) // HEAD_DIM
    return jnp.where(r == c, scale, 0.0).astype(BF16)


def _dot_split(a, m_bf16):
    hi = a.astype(BF16)
    lo = (a - hi.astype(F32)).astype(BF16)
    return _dot(hi, m_bf16) + _dot(lo, m_bf16)


def _lane_head(width=GROUP):
    return lax.broadcasted_iota(jnp.int32, (1, width), 1) // HEAD_DIM


NA_ROWS = SEQ // GRID_W
NA_LOCAL = NA_WIN_ROWS * GRID_W
NA_NEG = -1e30
NA_BLK = 256
NA_PAIR = 4


def na_bias_table(rpb):
    n_dr = 2 * NA_WIN_ROWS
    rows = jnp.pad(rpb, ((0, 0), (0, 1), (0, 128 - rpb.shape[2]))).reshape(GROUP_HEADS * n_dr, 128)
    toep = pl.pallas_call(
        _na_bias_kernel,
        out_shape=jax.ShapeDtypeStruct((GROUP_HEADS * n_dr, GRID_W * GRID_W), F32),
        name="na_bias",
    )(rows).reshape(GROUP_HEADS, n_dr, GRID_W, GRID_W)
    tab = jnp.stack([toep[:, NA_WIN_ROWS - 1 - p:2 * NA_WIN_ROWS - 1 - p] for p in range(NA_WIN_ROWS)], axis=0)
    tab = jnp.transpose(tab, (0, 1, 3, 2, 4))
    return tab.reshape(NA_WIN_ROWS, GROUP_HEADS, GRID_W, NA_LOCAL)


def _na_bias_kernel(rpb_ref, o_ref):
    n = GRID_W * GRID_W
    d = lax.broadcasted_iota(jnp.int32, (128, n), 0)
    cj = lax.broadcasted_iota(jnp.int32, (128, n), 1)
    onehot = jnp.where((cj % GRID_W) - (cj // GRID_W) + NA_WIN_COLS - 1 == d, 1.0, 0.0).astype(BF16)
    cj1 = lax.broadcasted_iota(jnp.int32, (1, n), 1)
    c = cj1 // GRID_W
    j = cj1 % GRID_W
    start = jnp.clip(c - NA_WIN_COLS // 2, 0, GRID_W - NA_WIN_COLS)
    in_win = jnp.logical_and(j >= start, j < start + NA_WIN_COLS)
    o_ref[...] = jnp.where(in_win, _dot_exact_rhs(rpb_ref[...], onehot), NA_NEG)


def _na_kernel(need_ctx, slab_ref, qw_ref, kw_ref, bias_ref, o_ref, q_s, k_s, v_s):
    hm = _head_mean_matrix(1.0 / HEAD_DIM)
    qw = qw_ref[...] * (HEAD_DIM ** -0.5)
    kw = kw_ref[...]

    def prep(i, c):
        r0 = pl.multiple_of(i * NA_BLK, NA_BLK)
        q = slab_ref[pl.ds(r0, NA_BLK), 0:GROUP]
        k = slab_ref[pl.ds(r0, NA_BLK), GROUP:2 * GROUP]
        q_s[pl.ds(r0, NA_BLK), :] = (q * lax.rsqrt(_dot_split(q * q, hm) + NORM_EPS) * qw).astype(BF16)
        k_s[pl.ds(r0, NA_BLK), :] = (k * lax.rsqrt(_dot_split(k * k, hm) + NORM_EPS) * kw).astype(BF16)
        v_s[pl.ds(r0, NA_BLK), :] = slab_ref[pl.ds(r0, NA_BLK), 2 * GROUP:3 * GROUP].astype(BF16)
        return c

    lax.fori_loop(0, T_ALL // NA_BLK, prep, 0)

    lane_h = _lane_head()
    kc = k_s[0:CTX_LEN, :]
    vc = v_s[0:CTX_LEN, :]

    if need_ctx:
        qc = q_s[0:CTX_LEN, :]
        out = jnp.zeros((CTX_LEN, GROUP), F32)
        for h in range(GROUP_HEADS):
            mask = lane_h == h
            s = _dot_nt(jnp.where(mask, qc, jnp.zeros_like(qc)), kc)
            e = jnp.exp(s - jnp.max(s, axis=-1, keepdims=True))
            p = e * (1.0 / jnp.sum(e, axis=-1, keepdims=True))
            out = jnp.where(mask, _dot(p.astype(BF16), vc), out)
        o_ref[0:CTX_LEN, :] = out
    else:
        o_ref[0:CTX_LEN, :] = jnp.zeros((CTX_LEN, GROUP), F32)

    def pair_body(i, c):
        rows = [i * NA_PAIR + t for t in range(NA_PAIR)]
        start = [jnp.clip(r - NA_WIN_ROWS // 2, 0, NA_ROWS - NA_WIN_ROWS) for r in rows]
        q0 = [pl.multiple_of(CTX_LEN + r * GRID_W, GRID_W) for r in rows]
        k0 = [pl.multiple_of(CTX_LEN + s * GRID_W, GRID_W) for s in start]
        q = [_expand_heads(q_s[pl.ds(a, GRID_W), :]) for a in q0]
        kb = [k_s[pl.ds(a, NA_LOCAL), :] for a in k0]
        vb = [v_s[pl.ds(a, NA_LOCAL), :] for a in k0]
        bias = [bias_ref[r - s].reshape(GROUP_HEADS * GRID_W, NA_LOCAL) for r, s in zip(rows, start)]
        s_loc = [_dot_nt(q[t], kb[t]) + bias[t] for t in range(NA_PAIR)]
        s_ctx = [_dot_nt(q[t], kc) for t in range(NA_PAIR)]
        m = [jnp.maximum(jnp.max(a, axis=-1, keepdims=True), jnp.max(b, axis=-1, keepdims=True))
             for a, b in zip(s_loc, s_ctx)]
        e_loc = [jnp.exp(a - mm) for a, mm in zip(s_loc, m)]
        e_ctx = [jnp.exp(b - mm) for b, mm in zip(s_ctx, m)]
        inv = [1.0 / (jnp.sum(a, axis=-1, keepdims=True) + jnp.sum(b, axis=-1, keepdims=True))
               for a, b in zip(e_loc, e_ctx)]
        o = [_dot((e_loc[t] * inv[t]).astype(BF16), vb[t]) + _dot((e_ctx[t] * inv[t]).astype(BF16), vc)
             for t in range(NA_PAIR)]
        for t in range(NA_PAIR):
            out = o[t][0:GRID_W]
            for h in range(1, GROUP_HEADS):
                out = jnp.where(lane_h == h, o[t][h * GRID_W:(h + 1) * GRID_W], out)
            o_ref[pl.ds(q0[t], GRID_W), :] = out
        return c

    lax.fori_loop(0, NA_ROWS // NA_PAIR, pair_body, 0)


def na_mixer(slab, q_norm, k_norm, bias_tab, need_ctx):
    n = slab.shape[0]
    tile4 = lambda w: jnp.tile(w, GROUP_HEADS).reshape(1, GROUP)
    return pl.pallas_call(
        functools.partial(_na_kernel, need_ctx),
        grid=(n // T_ALL,),
        in_specs=[
            pl.BlockSpec((T_ALL, 3 * GROUP), lambda b: (b, 0)),
            pl.BlockSpec((1, GROUP), lambda b: (0, 0)),
            pl.BlockSpec((1, GROUP), lambda b: (0, 0)),
            pl.BlockSpec((NA_WIN_ROWS, GROUP_HEADS, GRID_W, NA_LOCAL), lambda b: (0, 0, 0, 0)),
        ],
        out_specs=pl.BlockSpec((T_ALL, GROUP), lambda b: (b, 0)),
        out_shape=jax.ShapeDtypeStruct((n, GROUP), F32),
        scratch_shapes=[pltpu.VMEM((T_ALL, GROUP), BF16)] * 3,
        compiler_params=_cparams("parallel"),
        name="na_mixer",
    )(slab, tile4(q_norm), tile4(k_norm), bias_tab)


SEQ_BLK = 256
N_BLK = T_ALL // SEQ_BLK
N_CHUNK = T_ALL // CHUNK
CTX_CHUNKS = CTX_LEN // CHUNK


def _prev_cur_next(ref, i, c0, c1):
    r0 = pl.multiple_of(i * SEQ_BLK, SEQ_BLK)
    cur = ref[pl.ds(r0, SEQ_BLK), c0:c1]
    up0 = pl.multiple_of(jnp.maximum(r0 - 8, 0), 8)
    dn0 = pl.multiple_of(jnp.minimum(r0 + SEQ_BLK, T_ALL - 8), 8)
    up = ref[pl.ds(up0, 8), c0:c1][7:8, :]
    dn = ref[pl.ds(dn0, 8), c0:c1][0:1, :]
    up = jnp.where(i >= 2, up, 0.0)
    dn = jnp.where(jnp.logical_and(i >= 1, i <= N_BLK - 2), dn, 0.0)
    row = lax.broadcasted_iota(jnp.int32, (SEQ_BLK, 1), 0)
    prev = jnp.where(row == 0, up, pltpu.roll(cur, 1, 0))
    nxt = jnp.where(row == SEQ_BLK - 1, dn, pltpu.roll(cur, SEQ_BLK - 1, 0))
    return prev, cur, nxt


def _chunk_cumsum(x, reverse):
    pos = lax.broadcasted_iota(jnp.int32, (SEQ_BLK, 1), 0) % CHUNK
    s = 1
    while s < CHUNK:
        if reverse:
            x = x + jnp.where(pos < CHUNK - s, pltpu.roll(x, SEQ_BLK - s, 0), 0.0)
        else:
            x = x + jnp.where(pos >= s, pltpu.roll(x, s, 0), 0.0)
        s *= 2
    return x


def _split3(a):
    hi = a.astype(BF16)
    r1 = a - hi.astype(F32)
    mid = r1.astype(BF16)
    lo = (r1 - mid.astype(F32)).astype(BF16)
    return hi, mid, lo


def _dot_exact_rhs(a, m_bf16):
    hi, mid, lo = _split3(a)
    return _dot(hi, m_bf16) + _dot(mid, m_bf16) + _dot(lo, m_bf16)


def _expand_heads(x):
    lane_h = _lane_head()
    return jnp.concatenate([jnp.where(lane_h == h, x, 0.0) for h in range(GROUP_HEADS)], axis=0)


def _chunk_of_step(n, reverse):
    if not reverse:
        return n
    return jnp.where(n < CTX_CHUNKS, CTX_CHUNKS - 1 - n, N_CHUNK + CTX_CHUNKS - 1 - n)


INV_BASE = 16


def _cat_dot(a, b):
    return _dot(a.astype(BF16), _expand_heads(b.astype(BF16)))


def _cat_index():
    i = lax.broadcasted_iota(jnp.int32, (CHUNK, GROUP_HEADS * CHUNK), 0)
    j = lax.broadcasted_iota(jnp.int32, (CHUNK, GROUP_HEADS * CHUNK), 1) % CHUNK
    return i, j


def _cat_masks(reverse):
    i, j = _cat_index()
    if reverse:
        return i <= j, i < j
    return i >= j, i > j


def _inverse_unit_triangular(mats):
    i, j = _cat_index()
    inner = (i // INV_BASE) == (j // INV_BASE)
    eye = jnp.where(i == j, 1.0, 0.0)
    nd = [jnp.where(inner, n, 0.0) for n in mats]
    x = [eye - n for n in nd]
    p = [_cat_dot(n, n) for n in nd]
    k = 2
    while k < INV_BASE:
        x = [xi + _cat_dot(pi, xi) for xi, pi in zip(x, p)]
        k *= 2
        if k < INV_BASE:
            p = [_cat_dot(pi, pi) for pi in p]
    width = INV_BASE
    while width < CHUNK:
        outer = (i // (2 * width)) == (j // (2 * width))
        sel = jnp.logical_and(outer, jnp.logical_not(inner))
        t = [_cat_dot(jnp.where(sel, n, 0.0), xi) for n, xi in zip(mats, x)]
        x = [xi - _cat_dot(xi, ti) for xi, ti in zip(x, t)]
        inner = outer
        width *= 2
    return x


def _head_rows(gc, lane_onehot):
    hi, mid, lo = _split3(gc)
    t = _dot_nt(lane_onehot, hi) + _dot_nt(lane_onehot, mid) + _dot_nt(lane_onehot, lo)
    return jnp.concatenate([t[h:h + 1, :] for h in range(GROUP_HEADS)], axis=1)


INTRA_CHUNKS = 4
INTER_BATCH = 4
GL_ROWS = 8


def _dn_prep_kernel(slab_ref, conv_ref, alog_ref, dt_ref, q_ref, k_ref, v_ref, gc_ref, beta_ref):
    hsum = _head_mean_matrix(1.0)
    col = lax.broadcasted_iota(jnp.int32, (128, GROUP), 0)
    lane = lax.broadcasted_iota(jnp.int32, (128, GROUP), 1) // HEAD_DIM
    neg_a = -jnp.exp(alog_ref[...])
    dtb = dt_ref[...]

    def prep(i, c):
        rows = pl.ds(pl.multiple_of(i * SEQ_BLK, SEQ_BLK), SEQ_BLK)
        for j, dst in enumerate((q_ref, k_ref, v_ref)):
            prev, cur, nxt = _prev_cur_next(slab_ref, i, j * GROUP, (j + 1) * GROUP)
            w = conv_ref[:, j * GROUP:(j + 1) * GROUP]
            u = _silu(prev * w[0:1] + cur * w[1:2] + nxt * w[2:3])
            if j == 0:
                u = u * lax.rsqrt(_dot_split(u * u, hsum) + 1e-6) * (HEAD_DIM ** -0.5)
            elif j == 1:
                u = u * lax.rsqrt(_dot_split(u * u, hsum) + 1e-6)
            dst[0] = u
        ba = slab_ref[rows, 4 * GROUP:4 * GROUP + 128]
        for d in range(2):
            e_b = jnp.where(col == 8 * d + lane, 1.0, 0.0).astype(BF16)
            e_a = jnp.where(col == 8 * d + 4 + lane, 1.0, 0.0).astype(BF16)
            beta_ref[d, 0] = _sigmoid(_dot_exact_rhs(ba, e_b))
            g = neg_a[d:d + 1] * _softplus(_dot_exact_rhs(ba, e_a) + dtb[d:d + 1])
            gc_ref[d, 0] = _chunk_cumsum(g, reverse=(d == 1))
        return c

    prep(pl.program_id(1), 0)


def _dn_intra_kernel(q_ref, k_ref, v_ref, gc_ref, beta_ref, u_ref, w_ref, attn_ref, qd_ref, kd_ref, gl_ref):
    onehot = jnp.where(
        lax.broadcasted_iota(jnp.int32, (8, GROUP), 1) == HEAD_DIM * lax.broadcasted_iota(jnp.int32, (8, GROUP), 0),
        1.0, 0.0).astype(BF16)
    masks = (_cat_masks(False), _cat_masks(True))
    chains = [(j, d) for j in range(INTRA_CHUNKS) for d in range(2)]
    rows = [pl.ds(j * CHUNK, CHUNK) for j, d in chains]
    gc = [gc_ref[d, 0, r, :] for (j, d), r in zip(chains, rows)]
    beta = [beta_ref[d, 0, r, :] for (j, d), r in zip(chains, rows)]
    q = [q_ref[0, r, :] for r in rows]
    k = [k_ref[0, r, :] for r in rows]
    v = [v_ref[0, r, :] for r in rows]
    eg = [jnp.exp(g) for g in gc]
    g_last = [g[0:1, :] if d == 1 else g[CHUNK - 1:CHUNK, :] for (j, d), g in zip(chains, gc)]
    kb = [a * b for a, b in zip(k, beta)]
    k_e = [_expand_heads(a.astype(BF16)) for a in k]
    dec = []
    for (j, d), g in zip(chains, gc):
        incl = masks[d][0]
        dec.append(jnp.where(incl, jnp.exp(jnp.where(incl, g - _head_rows(g, onehot), 0.0)), 0.0))
    m = [jnp.where(masks[d][1], _dot_nt(a.astype(BF16), ke) * dc, 0.0)
         for (j, d), a, ke, dc in zip(chains, kb, k_e, dec)]
    attn = [_dot_nt(a.astype(BF16), ke) * dc for a, ke, dc in zip(q, k_e, dec)]
    rhs = [jnp.concatenate([_expand_heads((a * b).astype(BF16)), _expand_heads((c * e).astype(BF16))], axis=1)
           for a, b, c, e in zip(v, beta, kb, eg)]
    x = _inverse_unit_triangular(m)
    sol = [_dot(xi.astype(BF16), r) for xi, r in zip(x, rhs)]
    for i, ((j, d), r) in enumerate(zip(chains, rows)):
        u_ref[d, 0, r, :] = sol[i][:, 0:GROUP]
        w_ref[d, 0, r, :] = sol[i][:, GROUP:2 * GROUP].astype(BF16)
        attn_ref[d, 0, r, :] = attn[i].astype(BF16)
        qd_ref[d, 0, r, :] = (q[i] * eg[i]).astype(BF16)
        kd_ref[d, 0, r, :] = (k[i] * jnp.exp(g_last[i] - gc[i])).astype(BF16)
        gl_ref[d, 0, pl.ds(j * GL_ROWS, GL_ROWS), :] = jnp.broadcast_to(jnp.exp(g_last[i]), (GL_ROWS, GROUP))


def _same_head_mask():
    r = lax.broadcasted_iota(jnp.int32, (GROUP, GROUP), 0) // HEAD_DIM
    c = lax.broadcasted_iota(jnp.int32, (GROUP, GROUP), 1) // HEAD_DIM
    return r == c


def _dn_inter_kernel(uf, wf, af, qf, kf, gf, ub, wb, ab, qb, kb, gb, of_ref, ob_ref, s_ref):
    @pl.when(pl.program_id(1) == 0)
    def _():
        s_ref[...] = jnp.zeros_like(s_ref)

    same = _same_head_mask()
    ins = ((uf, wf, af, qf, kf, gf, of_ref), (ub, wb, ab, qb, kb, gb, ob_ref))
    chains = [(d, j) for j in range(INTER_BATCH) for d in range(2)]
    s = [s_ref[d * INTER_BATCH + j] for d, j in chains]
    s_b = [t.astype(BF16) for t in s]
    v_new = [(ins[d][0][0, j] - _dot(ins[d][1][0, j], sb)).astype(BF16) for (d, j), sb in zip(chains, s_b)]
    v_ne = [_expand_heads(t) for t in v_new]
    o = [_dot(ins[d][3][0, j], sb) + _dot(ins[d][2][0, j], ve) for (d, j), sb, ve in zip(chains, s_b, v_ne)]
    upd = [_dot_tn(ins[d][4][0, j], t) for (d, j), t in zip(chains, v_new)]
    for i, (d, j) in enumerate(chains):
        ins[d][6][j] = o[i]
        s_ref[d * INTER_BATCH + j] = s[i] * ins[d][5][0, j, 0:1, :] + jnp.where(same, upd[i], 0.0)


def _dn_finish_kernel(of_ref, ob_ref, z_ref, nw_ref, o_ref):
    hmean = _head_mean_matrix(1.0 / HEAD_DIM)
    nw = nw_ref[...]

    def finish(i, c):
        rows = pl.ds(pl.multiple_of(i * SEQ_BLK, SEQ_BLK), SEQ_BLK)
        o = of_ref[0, rows, :] + ob_ref[0, rows, :]
        o_ref[rows, :] = o * lax.rsqrt(_dot_split(o * o, hmean) + NORM_EPS) * nw * _silu(z_ref[rows, :])
        return c

    lax.fori_loop(0, N_BLK, finish, 0)


def _bwd_chunk(n):
    return _chunk_of_step(n, True)


def deltanet_mixer(slab, conv_w, a_log, dt_bias, norm_w):
    nb = slab.shape[0] // T_ALL
    assert nb % INTER_BATCH == 0 and N_CHUNK % INTRA_CHUNKS == 0
    lanes = lambda t: jnp.repeat(t, HEAD_DIM, axis=-1)
    seq = lambda dt: jax.ShapeDtypeStruct((nb, T_ALL, GROUP), dt)
    seq2 = lambda dt: jax.ShapeDtypeStruct((2, nb, T_ALL, GROUP), dt)
    b1 = pl.BlockSpec((1, T_ALL, GROUP), lambda b: (b, 0, 0))
    p1 = pl.BlockSpec((1, SEQ_BLK, GROUP), lambda b, i: (b, i, 0))
    p2 = pl.BlockSpec((2, 1, SEQ_BLK, GROUP), lambda b, i: (0, b, i, 0))
    small = lambda shape: pl.BlockSpec(shape, lambda b, i: (0, 0))
    q, k, v, gc, beta = pl.pallas_call(
        _dn_prep_kernel,
        grid=(nb, N_BLK),
        in_specs=[pl.BlockSpec((T_ALL, DN_W), lambda b, i: (b, 0)), small((3, 3 * GROUP)),
                  small((2, GROUP)), small((2, GROUP))],
        out_specs=[p1, p1, p1, p2, p2],
        out_shape=[seq(F32), seq(F32), seq(F32), seq2(F32), seq2(F32)],
        compiler_params=_cparams("parallel", "parallel"),
        name="deltanet_prep",
    )(slab, conv_w, lanes(a_log), lanes(dt_bias))

    rows = INTRA_CHUNKS * CHUNK
    c1 = pl.BlockSpec((1, rows, GROUP), lambda b, g: (b, g, 0))
    c2 = pl.BlockSpec((2, 1, rows, GROUP), lambda b, g: (0, b, g, 0))
    gl_shape = jax.ShapeDtypeStruct((2, nb, N_CHUNK * GL_ROWS, GROUP), F32)
    u, w, attn, qd, kd, gl = pl.pallas_call(
        _dn_intra_kernel,
        grid=(nb, N_CHUNK // INTRA_CHUNKS),
        in_specs=[c1, c1, c1, c2, c2],
        out_specs=[c2, c2, c2, c2, c2,
                   pl.BlockSpec((2, 1, INTRA_CHUNKS * GL_ROWS, GROUP), lambda b, g: (0, b, g, 0))],
        out_shape=[seq2(F32), seq2(BF16), seq2(BF16), seq2(BF16), seq2(BF16), gl_shape],
        compiler_params=_cparams("parallel", "parallel"),
        name="deltanet_intra",
    )(q, k, v, gc, beta)

    def per_dir(d, rows_per_chunk):
        pick = (lambda n: n) if d == 0 else _bwd_chunk
        return pl.BlockSpec((1, INTER_BATCH, rows_per_chunk, GROUP), lambda b, n: (d, b, pick(n), 0))

    def out_dir(d):
        pick = (lambda n: n) if d == 0 else _bwd_chunk
        return pl.BlockSpec((INTER_BATCH, CHUNK, GROUP), lambda b, n: (b, pick(n), 0))

    specs = [per_dir(d, r) for d in range(2) for r in (CHUNK,) * 5 + (GL_ROWS,)]
    o_f, o_b = pl.pallas_call(
        _dn_inter_kernel,
        grid=(nb // INTER_BATCH, N_CHUNK),
        in_specs=specs,
        out_specs=[out_dir(0), out_dir(1)],
        out_shape=[seq(F32), seq(F32)],
        scratch_shapes=[pltpu.VMEM((2 * INTER_BATCH, GROUP, GROUP), F32)],
        compiler_params=_cparams("parallel", "arbitrary"),
        name="deltanet_inter",
    )(u, w, attn, qd, kd, gl, u, w, attn, qd, kd, gl)

    return pl.pallas_call(
        _dn_finish_kernel,
        grid=(nb,),
        in_specs=[b1, b1, pl.BlockSpec((T_ALL, GROUP), lambda b: (b, 3)), _resident((1, GROUP))],
        out_specs=pl.BlockSpec((T_ALL, GROUP), lambda b: (b, 0)),
        out_shape=jax.ShapeDtypeStruct((nb * T_ALL, GROUP), F32),
        compiler_params=_cparams("parallel"),
        name="deltanet_finish",
    )(o_f, o_b, slab, jnp.tile(norm_w, GROUP_HEADS).reshape(1, GROUP))


RW_LR = RW_DECAY_RANK + RW_AAA_RANK + RW_GATE_RANK
RW_LR_OUT = 5 * GROUP


def _dot3(a, b_hi, b_lo):
    a_hi = a.astype(BF16)
    a_lo = (a - a_hi.astype(F32)).astype(BF16)
    return _dot(a_hi, b_hi) + (_dot(a_lo, b_hi) + _dot(a_hi, b_lo))


def rwkv_lowrank_weights(w_up, a_up, g_up):
    w = jnp.zeros((RW_LR, RW_LR_OUT), F32)
    o1 = RW_DECAY_RANK
    o2 = o1 + RW_AAA_RANK
    for d in range(2):
        w = w.at[0:o1, d * GROUP:(d + 1) * GROUP].set(w_up[d])
        w = w.at[o1:o2, (2 + d) * GROUP:(3 + d) * GROUP].set(a_up[d])
    return w.at[o2:RW_LR, 4 * GROUP:5 * GROUP].set(g_up)


def _rw_prep_kernel(slab_ref, mu_ref, pv_ref, wlr_ref,
                    at_ref, rt_ref, bh_ref, kh_ref, bg_ref, kg_ref, gl_ref, v_ref, bonus_ref, g_ref):
    hsum = _head_mean_matrix(1.0)
    wlr = wlr_ref[...]
    wlr_hi = wlr.astype(BF16)
    wlr_lo = (wlr - wlr_hi.astype(F32)).astype(BF16)
    pv = pv_ref[...]
    w0 = (pv[0:1], pv[1:2])
    a0 = (pv[2:3], pv[3:4])
    k_k, k_a, r_k = pv[4:5], pv[5:6], pv[6:7]
    lr_lane = lax.broadcasted_iota(jnp.int32, (1, RW_LR), 1)
    per_blk = SEQ_BLK // CHUNK

    def shifted(i, c0, c1):
        prev, cur, nxt = _prev_cur_next(slab_ref, i, c0, c1)
        return cur + mu_ref[0:1, c0:c1] * (prev - cur) + mu_ref[1:2, c0:c1] * (nxt - cur)

    def prep(i, c):
        rows = pl.ds(pl.multiple_of(i * SEQ_BLK, SEQ_BLK), SEQ_BLK)
        r = shifted(i, 0, GROUP)
        k = shifted(i, GROUP, 2 * GROUP)
        v = shifted(i, 2 * GROUP, 3 * GROUP)
        lr = shifted(i, 3 * GROUP, 3 * GROUP + RW_LR)
        t = jnp.where(lr_lane < RW_DECAY_RANK, jnp.tanh(lr),
                      jnp.where(lr_lane < RW_DECAY_RANK + RW_AAA_RANK, lr, _sigmoid(lr)))
        proj = _dot3(t, wlr_hi, wlr_lo)
        kq = k * k_k
        kk = kq * lax.rsqrt(_dot_split(kq * kq, hsum) + 1e-6)
        v_ref[0] = v.astype(BF16)
        g_ref[...] = proj[:, 4 * GROUP:5 * GROUP]
        ksum = jnp.zeros_like(k)
        for d in range(2):
            w_log = -_softplus(-(w0[d] + proj[:, d * GROUP:(d + 1) * GROUP])) - 0.5
            lw = -jnp.exp(w_log)
            a_gate = _sigmoid(a0[d] + proj[:, (2 + d) * GROUP:(3 + d) * GROUP])
            k_d = k * (1.0 + (a_gate - 1.0) * k_a)
            ksum = ksum + k_d
            cum = _chunk_cumsum(lw, reverse=(d == 1))
            ends = [cum[j * CHUNK:j * CHUNK + 1, :] if d == 1 else cum[(j + 1) * CHUNK - 1:(j + 1) * CHUNK, :]
                    for j in range(per_blk)]
            to_end = jnp.exp(jnp.concatenate([jnp.broadcast_to(e, (CHUNK, GROUP)) for e in ends], axis=0) - cum)
            inv = jnp.exp(-cum)
            b = kk * a_gate
            at_ref[d, 0] = (-kk * jnp.exp(cum - lw)).astype(BF16)
            rt_ref[d, 0] = (r * jnp.exp(cum)).astype(BF16)
            bh_ref[d, 0] = (b * inv).astype(BF16)
            kh_ref[d, 0] = (k_d * inv).astype(BF16)
            bg_ref[d, 0] = (b * to_end).astype(BF16)
            kg_ref[d, 0] = (k_d * to_end).astype(BF16)
            gl_ref[d, 0] = jnp.concatenate([jnp.broadcast_to(jnp.exp(e), (GL_ROWS, GROUP)) for e in ends], axis=0)
        bonus_ref[...] = _dot_split(r * ksum * r_k, hsum) * v
        return c

    prep(pl.program_id(1), 0)


def _rw_intra_kernel(at_ref, rt_ref, bh_ref, kh_ref, v_ref, xc_ref, arb_ref, rhs0_ref, yk_ref):
    masks = (_cat_masks(False), _cat_masks(True))
    chains = [(j, d) for j in range(INTRA_CHUNKS) for d in range(2)]
    rows = [pl.ds(j * CHUNK, CHUNK) for j, d in chains]
    at = [at_ref[d, 0, r, :] for (j, d), r in zip(chains, rows)]
    rt = [rt_ref[d, 0, r, :] for (j, d), r in zip(chains, rows)]
    bh_e = [_expand_heads(bh_ref[d, 0, r, :]) for (j, d), r in zip(chains, rows)]
    kh_e = [_expand_heads(kh_ref[d, 0, r, :]) for (j, d), r in zip(chains, rows)]
    v_e = [_expand_heads(v_ref[0, r, :]) for r in rows]
    x = _inverse_unit_triangular(
        [jnp.where(masks[d][1], -_dot_nt(a, b), 0.0) for (j, d), a, b in zip(chains, at, bh_e)])
    a_ak = [jnp.where(masks[d][1], _dot_nt(a, b), 0.0).astype(BF16) for (j, d), a, b in zip(chains, at, kh_e)]
    a_rb = [jnp.where(masks[d][0], _dot_nt(a, b), 0.0).astype(BF16) for (j, d), a, b in zip(chains, rt, bh_e)]
    a_rk = [jnp.where(masks[d][0], _dot_nt(a, b), 0.0).astype(BF16) for (j, d), a, b in zip(chains, rt, kh_e)]
    rhs0 = [_dot(a, ve) for a, ve in zip(a_ak, v_e)]
    yk = [_dot(a, ve) for a, ve in zip(a_rk, v_e)]
    for i, ((j, d), r) in enumerate(zip(chains, rows)):
        xc_ref[d, 0, r, :] = x[i].astype(BF16)
        arb_ref[d, 0, r, :] = a_rb[i]
        rhs0_ref[d, 0, r, :] = rhs0[i]
        yk_ref[d, 0, r, :] = yk[i]


def _rw_inter_kernel(*refs):
    n_in = 10
    fwd, bwd = refs[0:n_in], refs[n_in:2 * n_in]
    yf_ref, yb_ref, s_ref = refs[2 * n_in:]

    @pl.when(pl.program_id(1) == 0)
    def _():
        s_ref[...] = jnp.zeros_like(s_ref)

    same = _same_head_mask()
    ins = (fwd, bwd)
    outs = (yf_ref, yb_ref)
    chains = [(d, j) for j in range(INTER_BATCH) for d in range(2)]

    def arg(idx):
        return [ins[d][idx][0, j] for d, j in chains]

    at, rt, bg, kg, xc, arb, rhs0, yk = (arg(i) for i in range(8))
    gamma = [ins[d][8][0, j, 0:1, :] for d, j in chains]
    v = [ins[d][9][j] for d, j in chains]
    s = [s_ref[d * INTER_BATCH + j] for d, j in chains]
    s_b = [t.astype(BF16) for t in s]
    rhs = [_expand_heads((_dot_nt(a, sb) + r0).astype(BF16)) for a, sb, r0 in zip(at, s_b, rhs0)]
    sa = [_dot(x, r).astype(BF16) for x, r in zip(xc, rhs)]
    sa_e = [_expand_heads(t) for t in sa]
    y = [_dot_nt(r, sb) + _dot(a, se) + y0 for r, sb, a, se, y0 in zip(rt, s_b, arb, sa_e, yk)]
    upd = [_dot_tn(a, b) + _dot_tn(c, e) for a, b, c, e in zip(sa, bg, v, kg)]
    for i, (d, j) in enumerate(chains):
        outs[d][j] = y[i]
        s_ref[d * INTER_BATCH + j] = s[i] * gamma[i] + jnp.where(same, upd[i], 0.0)


def _rw_finish_kernel(yf_ref, yb_ref, bonus_ref, g_ref, pv_ref, o_ref):
    hmean = _head_mean_matrix(1.0 / HEAD_DIM)
    ln_w, ln_b = pv_ref[7:8, :], pv_ref[8:9, :]

    def finish(i, c):
        rows = pl.ds(pl.multiple_of(i * SEQ_BLK, SEQ_BLK), SEQ_BLK)
        y = yf_ref[0, rows, :] + yb_ref[0, rows, :]
        yc = y - _dot_split(y, hmean)
        yn = yc * lax.rsqrt(_dot_split(yc * yc, hmean) + RW_LN_EPS) * ln_w + ln_b
        o_ref[rows, :] = (yn + bonus_ref[rows, :]) * g_ref[rows, :]
        return c

    lax.fori_loop(0, N_BLK, finish, 0)


def rwkv_mixer(slab, mu, w0, w_up, a0, a_up, g_up, k_k, k_a, r_k, ln_w, ln_b):
    n = slab.shape[0]
    nb = n // T_ALL
    assert nb % INTER_BATCH == 0 and N_CHUNK % INTRA_CHUNKS == 0
    pv = jnp.concatenate([w0, a0, k_k[None], k_a[None], r_k.reshape(1, GROUP), ln_w[None], ln_b[None],
                          jnp.zeros((7, GROUP), F32)], axis=0)
    seq = lambda dt: jax.ShapeDtypeStruct((nb, T_ALL, GROUP), dt)
    seq2 = lambda dt: jax.ShapeDtypeStruct((2, nb, T_ALL, GROUP), dt)
    flat = jax.ShapeDtypeStruct((n, GROUP), F32)
    gl_shape = jax.ShapeDtypeStruct((2, nb, N_CHUNK * GL_ROWS, GROUP), F32)
    b1 = pl.BlockSpec((1, T_ALL, GROUP), lambda b: (b, 0, 0))
    b2 = pl.BlockSpec((2, 1, T_ALL, GROUP), lambda b: (0, b, 0, 0))
    bflat = pl.BlockSpec((T_ALL, GROUP), lambda b: (b, 0))
    p1 = pl.BlockSpec((1, SEQ_BLK, GROUP), lambda b, i: (b, i, 0))
    p2 = pl.BlockSpec((2, 1, SEQ_BLK, GROUP), lambda b, i: (0, b, i, 0))
    pflat = pl.BlockSpec((SEQ_BLK, GROUP), lambda b, i: (b * N_BLK + i, 0))
    pgl = pl.BlockSpec((2, 1, SEQ_BLK // CHUNK * GL_ROWS, GROUP), lambda b, i: (0, b, i, 0))
    small = lambda shape: pl.BlockSpec(shape, lambda b, i: (0, 0))
    at, rt, bh, kh, bg, kg, gl, v, bonus, g = pl.pallas_call(
        _rw_prep_kernel,
        grid=(nb, N_BLK),
        in_specs=[pl.BlockSpec((T_ALL, RW_W), lambda b, i: (b, 0)), small((2, RW_W)), small((16, GROUP)),
                  small((RW_LR, RW_LR_OUT))],
        out_specs=[p2] * 6 + [pgl, p1, pflat, pflat],
        out_shape=[seq2(BF16)] * 6 + [gl_shape, seq(BF16), flat, flat],
        compiler_params=_cparams("parallel", "parallel"),
        name="rwkv7_prep",
    )(slab, mu, pv, rwkv_lowrank_weights(w_up, a_up, g_up))

    rows = INTRA_CHUNKS * CHUNK
    c1 = pl.BlockSpec((1, rows, GROUP), lambda b, g_: (b, g_, 0))
    c2 = pl.BlockSpec((2, 1, rows, GROUP), lambda b, g_: (0, b, g_, 0))
    xc, arb, rhs0, yk = pl.pallas_call(
        _rw_intra_kernel,
        grid=(nb, N_CHUNK // INTRA_CHUNKS),
        in_specs=[c2, c2, c2, c2, c1],
        out_specs=[c2, c2, c2, c2],
        out_shape=[seq2(BF16), seq2(BF16), seq2(F32), seq2(F32)],
        compiler_params=_cparams("parallel", "parallel"),
        name="rwkv7_intra",
    )(at, rt, bh, kh, v)

    def chunk_pick(d):
        return (lambda n_: n_) if d == 0 else _bwd_chunk

    def per_dir(d, rows_per_chunk):
        pick = chunk_pick(d)
        return pl.BlockSpec((1, INTER_BATCH, rows_per_chunk, GROUP), lambda b, n_: (d, b, pick(n_), 0))

    def shared(d):
        pick = chunk_pick(d)
        return pl.BlockSpec((INTER_BATCH, CHUNK, GROUP), lambda b, n_: (b, pick(n_), 0))

    specs = [s for d in range(2) for s in [per_dir(d, CHUNK)] * 8 + [per_dir(d, GL_ROWS), shared(d)]]
    per = (at, rt, bg, kg, xc, arb, rhs0, yk, gl, v)
    y_f, y_b = pl.pallas_call(
        _rw_inter_kernel,
        grid=(nb // INTER_BATCH, N_CHUNK),
        in_specs=specs,
        out_specs=[shared(0), shared(1)],
        out_shape=[seq(F32), seq(F32)],
        scratch_shapes=[pltpu.VMEM((2 * INTER_BATCH, GROUP, GROUP), F32)],
        compiler_params=_cparams("parallel", "arbitrary"),
        name="rwkv7_inter",
    )(*per, *per)

    return pl.pallas_call(
        _rw_finish_kernel,
        grid=(nb,),
        in_specs=[b1, b1, bflat, bflat, _resident((16, GROUP))],
        out_specs=bflat,
        out_shape=flat,
        compiler_params=_cparams("parallel"),
        name="rwkv7_finish",
    )(y_f, y_b, bonus, g, pv)


DFT_SPLIT = 64
DFT_BLK = 256


def _dft_tables(n):
    big = 2 * n
    t = np.arange(n, dtype=np.int64)[:, None]
    k1 = np.arange(n // DFT_SPLIT, dtype=np.int64)[None, :]
    k2 = np.arange(DFT_SPLIT, dtype=np.int64)[None, :]
    alpha = 2.0 * np.pi * ((DFT_SPLIT * t * k1) % big) / big
    beta = 2.0 * np.pi * ((t * k2) % big) / big

    def pad(a):
        out = np.zeros((n, 128), np.float32)
        out[:, :a.shape[1]] = a
        return out

    return np.stack([pad(np.cos(alpha)), pad(np.sin(alpha)), pad(np.cos(beta)), pad(np.sin(beta))])


def _dft_gen_kernel(n, tab_ref, g_ref):
    k = lax.broadcasted_iota(jnp.int32, (128, n), 1)
    row = lax.broadcasted_iota(jnp.int32, (128, n), 0)
    e_a = jnp.where(k // DFT_SPLIT == row, 1.0, 0.0).astype(BF16)
    e_b = jnp.where(jnp.logical_and(k % DFT_SPLIT == row, row < DFT_SPLIT), 1.0, 0.0).astype(BF16)
    ca = _dot_split(tab_ref[0], e_a)
    sa = _dot_split(tab_ref[1], e_a)
    cb = _dot_split(tab_ref[2], e_b)
    sb = _dot_split(tab_ref[3], e_b)
    g_ref[:, 0:n] = (ca * cb - sa * sb).astype(BF16)
    g_ref[:, n:2 * n] = (-(sa * cb + ca * sb)).astype(BF16)


def dft_matrix(n):
    blk = min(DFT_BLK, n)
    return pl.pallas_call(
        functools.partial(_dft_gen_kernel, n),
        grid=(n // blk,),
        in_specs=[pl.BlockSpec((4, blk, 128), lambda i: (0, i, 0))],
        out_specs=pl.BlockSpec((blk, 2 * n), lambda i: (i, 0)),
        out_shape=jax.ShapeDtypeStruct((n, 2 * n), BF16),
        compiler_params=_cparams("parallel"),
        name=f"dft_matrix_{n}",
    )(jnp.asarray(_dft_tables(n)))


HY_COLS_F = 2 * HY_ORDER * GROUP
HY_OC = HY_ORDER * GROUP


def _hyena_filter_kernel(n, z_ref, w1_ref, b1_ref, w2_ref, b2_ref, w3_ref, freq_ref, dl_ref, hs_ref, hd_ref):
    blk = min(SEQ_BLK, n)
    freq = freq_ref[...]
    dl = dl_ref[...]

    def fill(i, norm):
        r0 = pl.multiple_of(i * blk, blk)
        z = z_ref[pl.ds(r0, blk), :]
        hid = jnp.sin(freq * (_dot_hi(z, w1_ref[...]) + b1_ref[...]))
        hid = jnp.sin(freq * (_dot_hi(hid, w2_ref[...]) + b2_ref[...]))
        t = z[:, 0:1]
        h = _dot_hi(hid, w3_ref[...]) * jnp.exp(-t * dl)
        lag = lax.broadcasted_iota(jnp.int32, (blk, 1), 0) + r0
        hf = h[:, 0:HY_OC]
        hb = jnp.where(lag == 0, 0.0, h[:, HY_OC:2 * HY_OC])
        hs_ref[pl.ds(r0, blk), :] = hf + hb
        hd_ref[pl.ds(r0, blk), :] = hf - hb
        return norm + jnp.sum(jnp.abs(hf) + jnp.abs(hb), axis=0, keepdims=True)

    norm = lax.fori_loop(0, n // blk, fill, jnp.zeros((1, HY_OC), F32))
    inv = 1.0 / norm

    def scale(i, c):
        rows = pl.ds(pl.multiple_of(i * blk, blk), blk)
        hs_ref[rows, :] = hs_ref[rows, :] * inv
        hd_ref[rows, :] = hd_ref[rows, :] * inv
        return c

    lax.fori_loop(0, n // blk, scale, 0)


def hyena_filter_taps(n, f_w1, f_b1, f_w2, f_b2, f_w3, f_freq):
    t = jnp.linspace(0.0, 1.0, n, dtype=F32)[:, None]
    ang = 2.0 * math.pi * jnp.arange(n, dtype=F32)[:, None] / n
    bands = jnp.linspace(1e-4, HY_BANDS - 1, HY_BANDS, dtype=F32)[None]
    z = jnp.concatenate([t, jnp.cos(bands * ang), -jnp.sin(bands * ang)], axis=-1)
    emb = z.shape[1]
    z = jnp.pad(z, ((0, 0), (0, 128 - emb)))
    w1 = jnp.pad(f_w1, ((0, 128 - emb), (0, 0)))
    max_decay = math.log(HY_TARGET) / HY_SHORT_DECAY_PCT
    min_decay = math.log(HY_TARGET) / HY_LONG_DECAY_PCT
    deltas = jnp.abs(jnp.linspace(min_decay, max_decay, HY_OC, dtype=F32))
    dl = jnp.tile(deltas, 2).reshape(1, HY_COLS_F)
    hid = f_w2.shape[0]
    out = jax.ShapeDtypeStruct((n, HY_OC), F32)
    return pl.pallas_call(
        functools.partial(_hyena_filter_kernel, n),
        out_shape=[out, out],
        compiler_params=pltpu.CompilerParams(vmem_limit_bytes=VMEM_LIMIT),
        name=f"hyena_filter_{n}",
    )(z, w1, f_b1.reshape(1, hid), f_w2, f_b2.reshape(1, hid), f_w3, f_freq.reshape(1, hid), dl)


def _hyena_spectrum_kernel(n, g_ref, hs_ref, hd_ref, kr_ref, ki_ref, kn_ref):
    blk = min(2 * SEQ_BLK, n)
    big = 2.0 * n

    def split(ref):
        x = ref[...]
        hi = x.astype(BF16)
        return hi, (x - hi.astype(F32)).astype(BF16)

    s_hi, s_lo = split(hs_ref)
    d_hi, d_lo = split(hd_ref)

    def body(i, c):
        r0 = pl.multiple_of(i * blk, blk)
        rows = pl.ds(r0, blk)
        k = lax.broadcasted_iota(jnp.int32, (blk, 1), 0) + r0
        wgt = jnp.where(k == 0, 1.0 / big, 2.0 / big)
        gc = g_ref[rows, 0:n]
        gs = g_ref[rows, n:2 * n]
        kr_ref[rows, :] = (_dot(gc, s_hi) + _dot(gc, s_lo)) * wgt
        ki_ref[rows, :] = (_dot(gs, d_hi) + _dot(gs, d_lo)) * wgt
        return c

    lax.fori_loop(0, n // blk, body, 0)
    t = lax.broadcasted_iota(jnp.int32, (n, 1), 0)
    sign = jnp.where(t % 2 == 0, 1.0, -1.0)
    kn_ref[...] = jnp.broadcast_to(jnp.sum(sign * hs_ref[...], axis=0, keepdims=True) * (1.0 / big), (8, HY_OC))


def hyena_spectrum(n, g, hs, hd):
    out = jax.ShapeDtypeStruct((n, HY_OC), F32)
    return pl.pallas_call(
        functools.partial(_hyena_spectrum_kernel, n),
        out_shape=[out, out, jax.ShapeDtypeStruct((8, HY_OC), F32)],
        compiler_params=pltpu.CompilerParams(vmem_limit_bytes=VMEM_LIMIT),
        name=f"hyena_spectrum_{n}",
    )(g, hs, hd)


def _hyena_conv_kernel(slab_ref, w_ref, o_ref):
    def body(i, c):
        rows = pl.ds(pl.multiple_of(i * SEQ_BLK, SEQ_BLK), SEQ_BLK)
        for j in range(3):
            prev, cur, nxt = _prev_cur_next(slab_ref, i, j * GROUP, (j + 1) * GROUP)
            w = w_ref[:, j * GROUP:(j + 1) * GROUP]
            o_ref[rows, j * GROUP:(j + 1) * GROUP] = prev * w[0:1] + cur * w[1:2] + nxt * w[2:3]
        return c

    lax.fori_loop(0, N_BLK, body, 0)


def hyena_short_conv(slab, conv_w):
    n = slab.shape[0]
    return pl.pallas_call(
        _hyena_conv_kernel,
        grid=(n // T_ALL,),
        in_specs=[pl.BlockSpec((T_ALL, 3 * GROUP), lambda b: (b, 0)),
                  pl.BlockSpec((3, 3 * GROUP), lambda b: (0, 0))],
        out_specs=pl.BlockSpec((T_ALL, 3 * GROUP), lambda b: (b, 0)),
        out_shape=jax.ShapeDtypeStruct((n, 3 * GROUP), F32),
        compiler_params=_cparams("parallel"),
        name="hyena_short_conv",
    )(slab, conv_w)


HY_FBLK = 1024


def _alt_sign(n):
    t = lax.broadcasted_iota(jnp.int32, (n, 1), 0)
    return jnp.where(t % 2 == 0, 1.0, -1.0)


def _hyena_fwd_kernel(x_ref, gl_ref, gc_ref, krl_ref, kil_ref, knl_ref, krc_ref, kic_ref, knc_ref,
                      pl_ref, pc_ref, pn_ref):
    def transform(x, g_ref, kr_ref, ki_ref, kn_ref, p_ref, n, blk):
        xb = x.astype(BF16)

        def body(i, c):
            rows = pl.ds(pl.multiple_of(i * blk, blk), blk)
            zr = _dot(g_ref[rows, 0:n], xb)
            zi = _dot(g_ref[rows, n:2 * n], xb)
            kr = kr_ref[rows, :]
            ki = ki_ref[rows, :]
            p_ref[0, 0, rows, :] = (zr * kr - zi * ki).astype(BF16)
            p_ref[0, 1, rows, :] = (zr * ki + zi * kr).astype(BF16)
            return c

        lax.fori_loop(0, n // blk, body, 0)
        return jnp.sum(_alt_sign(n) * x, axis=0, keepdims=True) * kn_ref[0:1, :]

    nyq_c = transform(x_ref[0:CTX_LEN, :], gc_ref, krc_ref, kic_ref, knc_ref, pc_ref, CTX_LEN, CTX_LEN)
    nyq_l = transform(x_ref[CTX_LEN:T_ALL, :], gl_ref, krl_ref, kil_ref, knl_ref, pl_ref, SEQ, HY_FBLK)
    pn_ref[0] = jnp.concatenate([nyq_l, nyq_c, jnp.zeros((6, GROUP), F32)], axis=0)


def _resident(shape):
    return pl.BlockSpec(shape, lambda b: (0,) * len(shape))


def hyena_forward_transform(x, col, g_l, g_c, spec_l, spec_c, order):
    n = x.shape[0]
    nb = n // T_ALL
    kcol = lambda shape: pl.BlockSpec(shape, lambda b: (0, order))
    return pl.pallas_call(
        _hyena_fwd_kernel,
        grid=(nb,),
        in_specs=[
            pl.BlockSpec((T_ALL, GROUP), lambda b: (b, col)),
            _resident((SEQ, 2 * SEQ)), _resident((CTX_LEN, 2 * CTX_LEN)),
            kcol((SEQ, GROUP)), kcol((SEQ, GROUP)), kcol((8, GROUP)),
            kcol((CTX_LEN, GROUP)), kcol((CTX_LEN, GROUP)), kcol((8, GROUP)),
        ],
        out_specs=[
            pl.BlockSpec((1, 2, SEQ, GROUP), lambda b: (b, 0, 0, 0)),
            pl.BlockSpec((1, 2, CTX_LEN, GROUP), lambda b: (b, 0, 0, 0)),
            pl.BlockSpec((1, 8, GROUP), lambda b: (b, 0, 0)),
        ],
        out_shape=[
            jax.ShapeDtypeStruct((nb, 2, SEQ, GROUP), BF16),
            jax.ShapeDtypeStruct((nb, 2, CTX_LEN, GROUP), BF16),
            jax.ShapeDtypeStruct((nb, 8, GROUP), F32),
        ],
        compiler_params=_cparams("parallel"),
        name=f"hyena_fwd_{order}",
    )(x, g_l, g_c, *spec_l, *spec_c)


def _hyena_inv_kernel(pl_ref, pc_ref, pn_ref, gl_ref, gc_ref, u_ref, gate_ref, bias_ref, o_ref):
    bias = bias_ref[0]

    def inverse(p_ref, nyq, g_ref, n, blk, off):
        pr = p_ref[0, 0]
        pi = p_ref[0, 1]

        def body(i, c):
            r0 = pl.multiple_of(i * blk, blk)
            rows = pl.ds(r0, blk)
            orow = pl.ds(pl.multiple_of(off + r0, math.gcd(blk, CTX_LEN)), blk)
            t = lax.broadcasted_iota(jnp.int32, (blk, 1), 0)
            sign = jnp.where(t % 2 == 0, 1.0, -1.0)
            y = _dot(g_ref[rows, 0:n], pr) + _dot(g_ref[rows, n:2 * n], pi) + sign * nyq
            o_ref[orow, :] = gate_ref[orow, :] * (y + u_ref[orow, :] * bias)
            return c

        lax.fori_loop(0, n // blk, body, 0)

    inverse(pc_ref, pn_ref[0, 1:2, :], gc_ref, CTX_LEN, CTX_LEN, 0)
    inverse(pl_ref, pn_ref[0, 0:1, :], gl_ref, SEQ, HY_FBLK, CTX_LEN)


def hyena_inverse_transform(p_l, p_c, p_n, g_l, g_c, u, ucol, gate, gcol, bias):
    nb = p_l.shape[0]
    return pl.pallas_call(
        _hyena_inv_kernel,
        grid=(nb,),
        in_specs=[
            pl.BlockSpec((1, 2, SEQ, GROUP), lambda b: (b, 0, 0, 0)),
            pl.BlockSpec((1, 2, CTX_LEN, GROUP), lambda b: (b, 0, 0, 0)),
            pl.BlockSpec((1, 8, GROUP), lambda b: (b, 0, 0)),
            _resident((SEQ, 2 * SEQ)), _resident((CTX_LEN, 2 * CTX_LEN)),
            pl.BlockSpec((T_ALL, GROUP), lambda b: (b, ucol)),
            pl.BlockSpec((T_ALL, GROUP), lambda b: (b, gcol)),
            pl.BlockSpec((1, 1, GROUP), lambda b: (0, 0, 0)),
        ],
        out_specs=pl.BlockSpec((T_ALL, GROUP), lambda b: (b, 0)),
        out_shape=jax.ShapeDtypeStruct((nb * T_ALL, GROUP), F32),
        compiler_params=_cparams("parallel"),
        name="hyena_inv",
    )(p_l, p_c, p_n, g_l, g_c, u, gate, bias.reshape(1, 1, GROUP))


def hyena_mixer(slab, g_l, g_c, conv_w, f_w1, f_b1, f_w2, f_b2, f_w3, f_freq, bias):
    u = hyena_short_conv(slab, conv_w)
    spec_l = hyena_spectrum(SEQ, g_l, *hyena_filter_taps(SEQ, f_w1, f_b1, f_w2, f_b2, f_w3, f_freq))
    spec_c = hyena_spectrum(CTX_LEN, g_c, *hyena_filter_taps(CTX_LEN, f_w1, f_b1, f_w2, f_b2, f_w3, f_freq))
    p = hyena_forward_transform(u, 0, g_l, g_c, spec_l, spec_c, 0)
    z = hyena_inverse_transform(*p, g_l, g_c, u, 0, u, 1, bias[0])
    p = hyena_forward_transform(z, 0, g_l, g_c, spec_l, spec_c, 1)
    return hyena_inverse_transform(*p, g_l, g_c, z, 0, u, 2, bias[1])


def kernel(x, c, ctx, c_ctx, w_mod, b_mod, norm_w, ffn_w_gu, ffn_w_down, w_in, w_out,
           hy_conv, hy_f_w1, hy_f_b1, hy_f_w2, hy_f_b2, hy_f_w3, hy_f_freq, hy_bias,
           na_q_norm, na_k_norm, na_rpb, dn_conv, dn_a_log, dn_dt_bias, dn_norm,
           rw_mu, rw_w0, rw_w_up, rw_a0, rw_a_up, rw_g_up, rw_k_k, rw_k_a, rw_r_k, rw_ln_w, rw_ln_b):
    nb = x.shape[0]
    assert x.shape[1:] == (SEQ, D_MODEL) and ctx.shape[1:] == (CTX_LEN, D_MODEL) and nb + 1 <= 16
    s = jnp.concatenate([ctx, x], axis=1).reshape(nb * T_ALL, D_MODEL)
    cond = jnp.concatenate([c_ctx[None], c, jnp.zeros((15 - nb, D_MODEL), F32)], axis=0)
    mod = modulation_all(cond, w_mod, b_mod).reshape(DEPTH, 16, N_MOD, D_MODEL)
    g_l = dft_matrix(SEQ)
    g_c = dft_matrix(CTX_LEN)
    w_gu = ffn_w_gu.astype(BF16)
    w_down = ffn_w_down.astype(BF16)
    w_out_b = w_out.astype(BF16)
    dn_end = 6 * GROUP + 4 * GROUP + 4 * GROUP_HEADS
    w_in_p = jnp.concatenate(
        [w_in[:, :, :dn_end], jnp.zeros((DEPTH, D_MODEL, 6 * GROUP + DN_W - dn_end), F32), w_in[:, :, dn_end:]],
        axis=2).astype(BF16)
    for l in range(DEPTH):
        need_ctx = l < DEPTH - 1
        modc = mod[l, 0:1]
        modb = mod[l, 1:1 + nb]
        s = ffn_half_step(s, modc, modb, norm_w[l, 0], w_gu, w_down, l, 0, 0)
        hy_s, na_s, dn_s, rw_s = input_projection(s, modc, modb, norm_w[l, 1], w_in_p, l)
        groups = (
            hyena_mixer(hy_s, g_l, g_c, hy_conv[l], hy_f_w1[l], hy_f_b1[l], hy_f_w2[l], hy_f_b2[l], hy_f_w3[l],
                        hy_f_freq[l], hy_bias[l]),
            na_mixer(na_s, na_q_norm[l], na_k_norm[l], na_bias_table(na_rpb[l]), need_ctx),
            deltanet_mixer(dn_s, dn_conv[l], dn_a_log[l], dn_dt_bias[l], dn_norm[l]),
            rwkv_mixer(rw_s, rw_mu[l], rw_w0[l], rw_w_up[l], rw_a0[l], rw_a_up[l], rw_g_up[l], rw_k_k[l],
                       rw_k_a[l], rw_r_k[l], rw_ln_w[l], rw_ln_b[l]),
        )
        s = output_projection(s, modc, modb, groups, w_out_b, l)
        s = ffn_half_step(s, modc, modb, norm_w[l, 2], w_gu, w_down, l, 1, 2)
    return s.reshape(nb, T_ALL, D_MODEL)[:, CTX_LEN:]
```

```python
import functools
import math

import numpy as np
import jax
import jax.numpy as jnp
from jax import lax
from jax.experimental import pallas as pl
from jax.experimental.pallas import tpu as pltpu

D_MODEL = 1024
SEQ = 2048
DEPTH = 2
CTX_LEN = 256
T_ALL = CTX_LEN + SEQ
GRID_W = 64
GROUP = 256
HEAD_DIM = 64
GROUP_HEADS = 4
D_FF = 2816
N_MOD = 9
NORM_EPS = 1e-6

HY_ORDER = 2
HY_BANDS = 16
HY_TARGET = 1e-2
HY_SHORT_DECAY_PCT = 0.3
HY_LONG_DECAY_PCT = 1.5

NA_WIN_ROWS = 8
NA_WIN_COLS = 16

CHUNK = 64
RW_DECAY_RANK = 32
RW_AAA_RANK = 32
RW_GATE_RANK = 64
RW_LN_EPS = 64e-5

DN_W = 4 * GROUP + 128
RW_W = 3 * GROUP + 128
P_PAD = 3 * GROUP + 3 * GROUP + DN_W + RW_W

TM = 768
TF = 512
VMEM_LIMIT = 56 * 1024 * 1024

F32 = jnp.float32
BF16 = jnp.bfloat16


def _cparams(*sem):
    return pltpu.CompilerParams(dimension_semantics=sem, vmem_limit_bytes=VMEM_LIMIT)


def _silu(x):
    return x * (1.0 / (1.0 + jnp.exp(-x)))


def _sigmoid(x):
    return 1.0 / (1.0 + jnp.exp(-x))


def _softplus(x):
    return jnp.maximum(x, 0.0) + jnp.log(1.0 + jnp.exp(-jnp.abs(x)))


def _dot(a, b):
    return jnp.dot(a, b, preferred_element_type=F32)


def _dot_nt(a, b):
    return lax.dot_general(a, b, (((1,), (1,)), ((), ())), preferred_element_type=F32)


def _dot_tn(a, b):
    return lax.dot_general(a, b, (((0,), (0,)), ((), ())), preferred_element_type=F32)


def _dot_hi(a, b):
    return jnp.dot(a, b, preferred_element_type=F32, precision=lax.Precision.HIGHEST)


def _mod_kernel(cond_ref, w_ref, b_ref, o_ref):
    a = _silu(cond_ref[...]).astype(BF16)
    o_ref[0] = _dot(a, w_ref[0].astype(BF16)) + b_ref[0]


def modulation_all(cond, w_mod, b_mod):
    r = cond.shape[0]
    tn = 1024
    return pl.pallas_call(
        _mod_kernel,
        grid=(DEPTH, N_MOD * D_MODEL // tn),
        in_specs=[
            pl.BlockSpec((r, D_MODEL), lambda l, j: (0, 0)),
            pl.BlockSpec((1, D_MODEL, tn), lambda l, j: (l, 0, j)),
            pl.BlockSpec((1, 1, tn), lambda l, j: (l, 0, j)),
        ],
        out_specs=pl.BlockSpec((1, r, tn), lambda l, j: (l, 0, j)),
        out_shape=jax.ShapeDtypeStruct((DEPTH, r, N_MOD * D_MODEL), F32),
        compiler_params=_cparams("parallel", "parallel"),
        name="modulation",
    )(cond, w_mod, b_mod.reshape(DEPTH, 1, N_MOD * D_MODEL))


def _row_mod(modc_ref, modb_ref, tile, idx):
    row = lax.broadcasted_iota(jnp.int32, (TM, 1), 0) + (tile % (T_ALL // TM)) * TM
    return jnp.where(row < CTX_LEN, modc_ref[0, idx:idx + 1, :], modb_ref[0, idx:idx + 1, :])


def _adaln(x, nw, shift, scale):
    y = x * lax.rsqrt(jnp.mean(x * x, axis=-1, keepdims=True) + NORM_EPS)
    return y * nw * (1.0 + scale) + shift


def _ffn_kernel(sub, x_ref, modc_ref, modb_ref, nw_ref, wgu_ref, wd_ref, o_ref):
    i = pl.program_id(0)
    x = x_ref[...]
    shift = _row_mod(modc_ref, modb_ref, i, 3 * sub)
    scale = _row_mod(modc_ref, modb_ref, i, 3 * sub + 1)
    h = _adaln(x, nw_ref[...], shift, scale).astype(BF16)
    acc = None
    for c0 in range(0, D_FF, TF):
        c1 = min(c0 + TF, D_FF)
        a = (_silu(_dot(h, wgu_ref[:, c0:c1])) * _dot(h, wgu_ref[:, D_FF + c0:D_FF + c1])).astype(BF16)
        part = _dot(a, wd_ref[c0:c1, :])
        acc = part if acc is None else acc + part
    gate = _row_mod(modc_ref, modb_ref, i, 3 * sub + 2)
    o_ref[...] = x + 0.5 * gate * acc


def ffn_half_step(x, modc, modb, nw, w_gu, w_down, layer, which, sub):
    n = x.shape[0]
    tiles_per_b = T_ALL // TM
    once = pl.Buffered(1)
    return pl.pallas_call(
        functools.partial(_ffn_kernel, sub),
        grid=(n // TM,),
        in_specs=[
            pl.BlockSpec((TM, D_MODEL), lambda i: (i, 0)),
            pl.BlockSpec((1, N_MOD, D_MODEL), lambda i: (0, 0, 0)),
            pl.BlockSpec((1, N_MOD, D_MODEL), lambda i: (i // tiles_per_b, 0, 0)),
            pl.BlockSpec((1, D_MODEL), lambda i: (0, 0)),
            pl.BlockSpec((None, None, D_MODEL, 2 * D_FF), lambda i: (layer, which, 0, 0), pipeline_mode=once),
            pl.BlockSpec((None, None, D_FF, D_MODEL), lambda i: (layer, which, 0, 0), pipeline_mode=once),
        ],
        out_specs=pl.BlockSpec((TM, D_MODEL), lambda i: (i, 0)),
        out_shape=jax.ShapeDtypeStruct((n, D_MODEL), F32),
        compiler_params=_cparams("parallel"),
        name=f"ffn{sub}",
    )(x, modc, modb, nw.reshape(1, D_MODEL), w_gu, w_down)


def _inproj_kernel(x_ref, modc_ref, modb_ref, nw_ref, w_ref, hy_ref, na_ref, dn_ref, rw_ref):
    i = pl.program_id(0)
    shift = _row_mod(modc_ref, modb_ref, i, 3)
    scale = _row_mod(modc_ref, modb_ref, i, 4)
    h = _adaln(x_ref[...], nw_ref[...], shift, scale).astype(BF16)
    o0 = 3 * GROUP
    o1 = 6 * GROUP
    o2 = o1 + DN_W
    y = _dot(h, w_ref[...])
    hy_ref[...] = y[:, 0:o0]
    na_ref[...] = y[:, o0:o1]
    dn_ref[...] = y[:, o1:o2]
    rw_ref[...] = y[:, o2:P_PAD]


def input_projection(x, modc, modb, nw, w_in_p, layer):
    n = x.shape[0]
    tiles_per_b = T_ALL // TM
    widths = (3 * GROUP, 3 * GROUP, DN_W, RW_W)
    return pl.pallas_call(
        _inproj_kernel,
        grid=(n // TM,),
        in_specs=[
            pl.BlockSpec((TM, D_MODEL), lambda i: (i, 0)),
            pl.BlockSpec((1, N_MOD, D_MODEL), lambda i: (0, 0, 0)),
            pl.BlockSpec((1, N_MOD, D_MODEL), lambda i: (i // tiles_per_b, 0, 0)),
            pl.BlockSpec((1, D_MODEL), lambda i: (0, 0)),
            pl.BlockSpec((None, D_MODEL, P_PAD), lambda i: (layer, 0, 0)),
        ],
        out_specs=[pl.BlockSpec((TM, w), lambda i: (i, 0)) for w in widths],
        out_shape=[jax.ShapeDtypeStruct((n, w), F32) for w in widths],
        compiler_params=_cparams("parallel"),
        name="inproj",
    )(x, modc, modb, nw.reshape(1, D_MODEL), w_in_p)


def _outproj_kernel(x_ref, modc_ref, modb_ref, g0_ref, g1_ref, g2_ref, g3_ref, w_ref, o_ref):
    i = pl.program_id(0)
    y = _dot(g0_ref[...].astype(BF16), w_ref[0:GROUP, :])
    y += _dot(g1_ref[...].astype(BF16), w_ref[GROUP:2 * GROUP, :])
    y += _dot(g2_ref[...].astype(BF16), w_ref[2 * GROUP:3 * GROUP, :])
    y += _dot(g3_ref[...].astype(BF16), w_ref[3 * GROUP:4 * GROUP, :])
    gate = _row_mod(modc_ref, modb_ref, i, 5)
    o_ref[...] = x_ref[...] + gate * y


def output_projection(x, modc, modb, groups, w_out, layer):
    n = x.shape[0]
    tiles_per_b = T_ALL // TM
    return pl.pallas_call(
        _outproj_kernel,
        grid=(n // TM,),
        in_specs=[
            pl.BlockSpec((TM, D_MODEL), lambda i: (i, 0)),
            pl.BlockSpec((1, N_MOD, D_MODEL), lambda i: (0, 0, 0)),
            pl.BlockSpec((1, N_MOD, D_MODEL), lambda i: (i // tiles_per_b, 0, 0)),
        ] + [pl.BlockSpec((TM, GROUP), lambda i: (i, 0))] * 4 + [
            pl.BlockSpec((None, D_MODEL, D_MODEL), lambda i: (layer, 0, 0)),
        ],
        out_specs=pl.BlockSpec((TM, D_MODEL), lambda i: (i, 0)),
        out_shape=jax.ShapeDtypeStruct((n, D_MODEL), F32),
        compiler_params=_cparams("parallel"),
        name="outproj",
    )(x, modc, modb, *groups, w_out)


def _head_mean_matrix(scale):
    r = lax.broadcasted_iota(jnp.int32, (GROUP, GROUP), 0) // HEAD_DIM
    c = lax.broadcasted_iota(jnp.int32, (GROUP, GROUP), 1) // HEAD_DIM
    return jnp.where(r == c, scale, 0.0).astype(BF16)


def _dot_split(a, m_bf16):
    hi = a.astype(BF16)
    lo = (a - hi.astype(F32)).astype(BF16)
    return _dot(hi, m_bf16) + _dot(lo, m_bf16)


def _lane_head(width=GROUP):
    return lax.broadcasted_iota(jnp.int32, (1, width), 1) // HEAD_DIM


NA_ROWS = SEQ // GRID_W
NA_LOCAL = NA_WIN_ROWS * GRID_W
NA_NEG = -1e30
NA_BLK = 256
NA_PAIR = 4


def na_bias_table(rpb):
    n_dr = 2 * NA_WIN_ROWS
    rows = jnp.pad(rpb, ((0, 0), (0, 1), (0, 128 - rpb.shape[2]))).reshape(GROUP_HEADS * n_dr, 128)
    toep = pl.pallas_call(
        _na_bias_kernel,
        out_shape=jax.ShapeDtypeStruct((GROUP_HEADS * n_dr, GRID_W * GRID_W), F32),
        name="na_bias",
    )(rows).reshape(GROUP_HEADS, n_dr, GRID_W, GRID_W)
    tab = jnp.stack([toep[:, NA_WIN_ROWS - 1 - p:2 * NA_WIN_ROWS - 1 - p] for p in range(NA_WIN_ROWS)], axis=0)
    tab = jnp.transpose(tab, (0, 1, 3, 2, 4))
    return tab.reshape(NA_WIN_ROWS, GROUP_HEADS, GRID_W, NA_LOCAL)


def _na_bias_kernel(rpb_ref, o_ref):
    n = GRID_W * GRID_W
    d = lax.broadcasted_iota(jnp.int32, (128, n), 0)
    cj = lax.broadcasted_iota(jnp.int32, (128, n), 1)
    onehot = jnp.where((cj % GRID_W) - (cj // GRID_W) + NA_WIN_COLS - 1 == d, 1.0, 0.0).astype(BF16)
    cj1 = lax.broadcasted_iota(jnp.int32, (1, n), 1)
    c = cj1 // GRID_W
    j = cj1 % GRID_W
    start = jnp.clip(c - NA_WIN_COLS // 2, 0, GRID_W - NA_WIN_COLS)
    in_win = jnp.logical_and(j >= start, j < start + NA_WIN_COLS)
    o_ref[...] = jnp.where(in_win, _dot_exact_rhs(rpb_ref[...], onehot), NA_NEG)


def _na_kernel(need_ctx, slab_ref, qw_ref, kw_ref, bias_ref, o_ref, q_s, k_s, v_s):
    hm = _head_mean_matrix(1.0 / HEAD_DIM)
    qw = qw_ref[...] * (HEAD_DIM ** -0.5)
    kw = kw_ref[...]

    def prep(i, c):
        r0 = pl.multiple_of(i * NA_BLK, NA_BLK)
        q = slab_ref[pl.ds(r0, NA_BLK), 0:GROUP]
        k = slab_ref[pl.ds(r0, NA_BLK), GROUP:2 * GROUP]
        q_s[pl.ds(r0, NA_BLK), :] = (q * lax.rsqrt(_dot_split(q * q, hm) + NORM_EPS) * qw).astype(BF16)
        k_s[pl.ds(r0, NA_BLK), :] = (k * lax.rsqrt(_dot_split(k * k, hm) + NORM_EPS) * kw).astype(BF16)
        v_s[pl.ds(r0, NA_BLK), :] = slab_ref[pl.ds(r0, NA_BLK), 2 * GROUP:3 * GROUP].astype(BF16)
        return c

    lax.fori_loop(0, T_ALL // NA_BLK, prep, 0)

    lane_h = _lane_head()
    kc = k_s[0:CTX_LEN, :]
    vc = v_s[0:CTX_LEN, :]

    if need_ctx:
        qc = q_s[0:CTX_LEN, :]
        out = jnp.zeros((CTX_LEN, GROUP), F32)
        for h in range(GROUP_HEADS):
            mask = lane_h == h
            s = _dot_nt(jnp.where(mask, qc, jnp.zeros_like(qc)), kc)
            e = jnp.exp(s - jnp.max(s, axis=-1, keepdims=True))
            p = e * (1.0 / jnp.sum(e, axis=-1, keepdims=True))
            out = jnp.where(mask, _dot(p.astype(BF16), vc), out)
        o_ref[0:CTX_LEN, :] = out
    else:
        o_ref[0:CTX_LEN, :] = jnp.zeros((CTX_LEN, GROUP), F32)

    def pair_body(i, c):
        rows = [i * NA_PAIR + t for t in range(NA_PAIR)]
        start = [jnp.clip(r - NA_WIN_ROWS // 2, 0, NA_ROWS - NA_WIN_ROWS) for r in rows]
        q0 = [pl.multiple_of(CTX_LEN + r * GRID_W, GRID_W) for r in rows]
        k0 = [pl.multiple_of(CTX_LEN + s * GRID_W, GRID_W) for s in start]
        q = [_expand_heads(q_s[pl.ds(a, GRID_W), :]) for a in q0]
        kb = [k_s[pl.ds(a, NA_LOCAL), :] for a in k0]
        vb = [v_s[pl.ds(a, NA_LOCAL), :] for a in k0]
        bias = [bias_ref[r - s].reshape(GROUP_HEADS * GRID_W, NA_LOCAL) for r, s in zip(rows, start)]
        s_loc = [_dot_nt(q[t], kb[t]) + bias[t] for t in range(NA_PAIR)]
        s_ctx = [_dot_nt(q[t], kc) for t in range(NA_PAIR)]
        m = [jnp.maximum(jnp.max(a, axis=-1, keepdims=True), jnp.max(b, axis=-1, keepdims=True))
             for a, b in zip(s_loc, s_ctx)]
        e_loc = [jnp.exp(a - mm) for a, mm in zip(s_loc, m)]
        e_ctx = [jnp.exp(b - mm) for b, mm in zip(s_ctx, m)]
        inv = [1.0 / (jnp.sum(a, axis=-1, keepdims=True) + jnp.sum(b, axis=-1, keepdims=True))
               for a, b in zip(e_loc, e_ctx)]
        o = [_dot((e_loc[t] * inv[t]).astype(BF16), vb[t]) + _dot((e_ctx[t] * inv[t]).astype(BF16), vc)
             for t in range(NA_PAIR)]
        for t in range(NA_PAIR):
            out = o[t][0:GRID_W]
            for h in range(1, GROUP_HEADS):
                out = jnp.where(lane_h == h, o[t][h * GRID_W:(h + 1) * GRID_W], out)
            o_ref[pl.ds(q0[t], GRID_W), :] = out
        return c

    lax.fori_loop(0, NA_ROWS // NA_PAIR, pair_body, 0)


def na_mixer(slab, q_norm, k_norm, bias_tab, need_ctx):
    n = slab.shape[0]
    tile4 = lambda w: jnp.tile(w, GROUP_HEADS).reshape(1, GROUP)
    return pl.pallas_call(
        functools.partial(_na_kernel, need_ctx),
        grid=(n // T_ALL,),
        in_specs=[
            pl.BlockSpec((T_ALL, 3 * GROUP), lambda b: (b, 0)),
            pl.BlockSpec((1, GROUP), lambda b: (0, 0)),
            pl.BlockSpec((1, GROUP), lambda b: (0, 0)),
            pl.BlockSpec((NA_WIN_ROWS, GROUP_HEADS, GRID_W, NA_LOCAL), lambda b: (0, 0, 0, 0)),
        ],
        out_specs=pl.BlockSpec((T_ALL, GROUP), lambda b: (b, 0)),
        out_shape=jax.ShapeDtypeStruct((n, GROUP), F32),
        scratch_shapes=[pltpu.VMEM((T_ALL, GROUP), BF16)] * 3,
        compiler_params=_cparams("parallel"),
        name="na_mixer",
    )(slab, tile4(q_norm), tile4(k_norm), bias_tab)


SEQ_BLK = 256
N_BLK = T_ALL // SEQ_BLK
N_CHUNK = T_ALL // CHUNK
CTX_CHUNKS = CTX_LEN // CHUNK


def _prev_cur_next(ref, i, c0, c1):
    r0 = pl.multiple_of(i * SEQ_BLK, SEQ_BLK)
    cur = ref[pl.ds(r0, SEQ_BLK), c0:c1]
    up0 = pl.multiple_of(jnp.maximum(r0 - 8, 0), 8)
    dn0 = pl.multiple_of(jnp.minimum(r0 + SEQ_BLK, T_ALL - 8), 8)
    up = ref[pl.ds(up0, 8), c0:c1][7:8, :]
    dn = ref[pl.ds(dn0, 8), c0:c1][0:1, :]
    up = jnp.where(i >= 2, up, 0.0)
    dn = jnp.where(jnp.logical_and(i >= 1, i <= N_BLK - 2), dn, 0.0)
    row = lax.broadcasted_iota(jnp.int32, (SEQ_BLK, 1), 0)
    prev = jnp.where(row == 0, up, pltpu.roll(cur, 1, 0))
    nxt = jnp.where(row == SEQ_BLK - 1, dn, pltpu.roll(cur, SEQ_BLK - 1, 0))
    return prev, cur, nxt


def _chunk_cumsum(x, reverse):
    pos = lax.broadcasted_iota(jnp.int32, (SEQ_BLK, 1), 0) % CHUNK
    s = 1
    while s < CHUNK:
        if reverse:
            x = x + jnp.where(pos < CHUNK - s, pltpu.roll(x, SEQ_BLK - s, 0), 0.0)
        else:
            x = x + jnp.where(pos >= s, pltpu.roll(x, s, 0), 0.0)
        s *= 2
    return x


def _split3(a):
    hi = a.astype(BF16)
    r1 = a - hi.astype(F32)
    mid = r1.astype(BF16)
    lo = (r1 - mid.astype(F32)).astype(BF16)
    return hi, mid, lo


def _dot_exact_rhs(a, m_bf16):
    hi, mid, lo = _split3(a)
    return _dot(hi, m_bf16) + _dot(mid, m_bf16) + _dot(lo, m_bf16)


def _expand_heads(x):
    lane_h = _lane_head()
    return jnp.concatenate([jnp.where(lane_h == h, x, 0.0) for h in range(GROUP_HEADS)], axis=0)


def _chunk_of_step(n, reverse):
    if not reverse:
        return n
    return jnp.where(n < CTX_CHUNKS, CTX_CHUNKS - 1 - n, N_CHUNK + CTX_CHUNKS - 1 - n)


INV_BASE = 16


def _cat_dot(a, b):
    return _dot(a.astype(BF16), _expand_heads(b.astype(BF16)))


def _cat_index():
    i = lax.broadcasted_iota(jnp.int32, (CHUNK, GROUP_HEADS * CHUNK), 0)
    j = lax.broadcasted_iota(jnp.int32, (CHUNK, GROUP_HEADS * CHUNK), 1) % CHUNK
    return i, j


def _cat_masks(reverse):
    i, j = _cat_index()
    if reverse:
        return i <= j, i < j
    return i >= j, i > j


def _inverse_unit_triangular(mats):
    i, j = _cat_index()
    inner = (i // INV_BASE) == (j // INV_BASE)
    eye = jnp.where(i == j, 1.0, 0.0)
    nd = [jnp.where(inner, n, 0.0) for n in mats]
    x = [eye - n for n in nd]
    p = [_cat_dot(n, n) for n in nd]
    k = 2
    while k < INV_BASE:
        x = [xi + _cat_dot(pi, xi) for xi, pi in zip(x, p)]
        k *= 2
        if k < INV_BASE:
            p = [_cat_dot(pi, pi) for pi in p]
    width = INV_BASE
    while width < CHUNK:
        outer = (i // (2 * width)) == (j // (2 * width))
        sel = jnp.logical_and(outer, jnp.logical_not(inner))
        t = [_cat_dot(jnp.where(sel, n, 0.0), xi) for n, xi in zip(mats, x)]
        x = [xi - _cat_dot(xi, ti) for xi, ti in zip(x, t)]
        inner = outer
        width *= 2
    return x


def _head_rows(gc, lane_onehot):
    hi, mid, lo = _split3(gc)
    t = _dot_nt(lane_onehot, hi) + _dot_nt(lane_onehot, mid) + _dot_nt(lane_onehot, lo)
    return jnp.concatenate([t[h:h + 1, :] for h in range(GROUP_HEADS)], axis=1)


INTRA_CHUNKS = SEQ_BLK // CHUNK
INTER_BATCH = 4
INTER_CHUNKS = 2
GL_ROWS = 8


def _dn_prep_block(slab_ref, conv_ref, alog_ref, dt_ref, i):
    hsum = _head_mean_matrix(1.0)
    col = lax.broadcasted_iota(jnp.int32, (128, GROUP), 0)
    lane = lax.broadcasted_iota(jnp.int32, (128, GROUP), 1) // HEAD_DIM
    neg_a = -jnp.exp(alog_ref[...])
    dtb = dt_ref[...]
    rows = pl.ds(pl.multiple_of(i * SEQ_BLK, SEQ_BLK), SEQ_BLK)
    qkv = []
    for j in range(3):
        prev, cur, nxt = _prev_cur_next(slab_ref, i, j * GROUP, (j + 1) * GROUP)
        w = conv_ref[:, j * GROUP:(j + 1) * GROUP]
        u = _silu(prev * w[0:1] + cur * w[1:2] + nxt * w[2:3])
        if j == 0:
            u = u * lax.rsqrt(_dot_split(u * u, hsum) + 1e-6) * (HEAD_DIM ** -0.5)
        elif j == 1:
            u = u * lax.rsqrt(_dot_split(u * u, hsum) + 1e-6)
        qkv.append(u)
    ba = slab_ref[rows, 4 * GROUP:4 * GROUP + 128]
    gc, beta = [], []
    for d in range(2):
        e_b = jnp.where(col == 8 * d + lane, 1.0, 0.0).astype(BF16)
        e_a = jnp.where(col == 8 * d + 4 + lane, 1.0, 0.0).astype(BF16)
        beta.append(_sigmoid(_dot_exact_rhs(ba, e_b)))
        g = neg_a[d:d + 1] * _softplus(_dot_exact_rhs(ba, e_a) + dtb[d:d + 1])
        gc.append(_chunk_cumsum(g, reverse=(d == 1)))
    return qkv[0], qkv[1], qkv[2], gc, beta


def _dn_intra_kernel(slab_ref, conv_ref, alog_ref, dt_ref, u_ref, w_ref, attn_ref, qd_ref, kd_ref, gl_ref):
    q_blk, k_blk, v_blk, gc_blk, beta_blk = _dn_prep_block(slab_ref, conv_ref, alog_ref, dt_ref, pl.program_id(1))
    onehot = jnp.where(
        lax.broadcasted_iota(jnp.int32, (8, GROUP), 1) == HEAD_DIM * lax.broadcasted_iota(jnp.int32, (8, GROUP), 0),
        1.0, 0.0).astype(BF16)
    masks = (_cat_masks(False), _cat_masks(True))
    chains = [(j, d) for j in range(INTRA_CHUNKS) for d in range(2)]
    rows = [slice(j * CHUNK, (j + 1) * CHUNK) for j, d in chains]
    gc = [gc_blk[d][r] for (j, d), r in zip(chains, rows)]
    beta = [beta_blk[d][r] for (j, d), r in zip(chains, rows)]
    q = [q_blk[r] for r in rows]
    k = [k_blk[r] for r in rows]
    v = [v_blk[r] for r in rows]
    eg = [jnp.exp(g) for g in gc]
    g_last = [g[0:1, :] if d == 1 else g[CHUNK - 1:CHUNK, :] for (j, d), g in zip(chains, gc)]
    kb = [a * b for a, b in zip(k, beta)]
    k_e = [_expand_heads(a.astype(BF16)) for a in k]
    dec = []
    for (j, d), g in zip(chains, gc):
        incl = masks[d][0]
        dec.append(jnp.where(incl, jnp.exp(jnp.where(incl, g - _head_rows(g, onehot), 0.0)), 0.0))
    m = [jnp.where(masks[d][1], _dot_nt(a.astype(BF16), ke) * dc, 0.0)
         for (j, d), a, ke, dc in zip(chains, kb, k_e, dec)]
    attn = [_dot_nt(a.astype(BF16), ke) * dc for a, ke, dc in zip(q, k_e, dec)]
    rhs = [jnp.concatenate([_expand_heads((a * b).astype(BF16)), _expand_heads((c * e).astype(BF16))], axis=1)
           for a, b, c, e in zip(v, beta, kb, eg)]
    x = _inverse_unit_triangular(m)
    sol = [_dot(xi.astype(BF16), r) for xi, r in zip(x, rhs)]
    for i, ((j, d), r) in enumerate(zip(chains, rows)):
        u_ref[d, 0, r, :] = sol[i][:, 0:GROUP]
        w_ref[d, 0, r, :] = sol[i][:, GROUP:2 * GROUP].astype(BF16)
        attn_ref[d, 0, r, :] = attn[i].astype(BF16)
        qd_ref[d, 0, r, :] = (q[i] * eg[i]).astype(BF16)
        kd_ref[d, 0, r, :] = (k[i] * jnp.exp(g_last[i] - gc[i])).astype(BF16)
        gl_ref[d, 0, pl.ds(j * GL_ROWS, GL_ROWS), :] = jnp.broadcast_to(jnp.exp(g_last[i]), (GL_ROWS, GROUP))


def _same_head_mask():
    r = lax.broadcasted_iota(jnp.int32, (GROUP, GROUP), 0) // HEAD_DIM
    c = lax.broadcasted_iota(jnp.int32, (GROUP, GROUP), 1) // HEAD_DIM
    return r == c


def _dn_inter_kernel(uf, wf, af, qf, kf, gf, ub, wb, ab, qb, kb, gb, of_ref, ob_ref, s_ref):
    @pl.when(pl.program_id(1) == 0)
    def _():
        s_ref[...] = jnp.zeros_like(s_ref)

    same = _same_head_mask()
    ins = ((uf, wf, af, qf, kf, gf, of_ref), (ub, wb, ab, qb, kb, gb, ob_ref))
    chains = [(d, j) for j in range(INTER_BATCH) for d in range(2)]
    s = [s_ref[d * INTER_BATCH + j] for d, j in chains]
    for t in range(INTER_CHUNKS):
        rows = [_inter_rows(t, d == 1, CHUNK) for d, j in chains]
        grow = [_inter_rows(t, d == 1, GL_ROWS) for d, j in chains]
        s_b = [x.astype(BF16) for x in s]
        v_new = [(ins[d][0][0, j, r, :] - _dot(ins[d][1][0, j, r, :], sb)).astype(BF16)
                 for (d, j), r, sb in zip(chains, rows, s_b)]
        v_ne = [_expand_heads(x) for x in v_new]
        o = [_dot(ins[d][3][0, j, r, :], sb) + _dot(ins[d][2][0, j, r, :], ve)
             for (d, j), r, sb, ve in zip(chains, rows, s_b, v_ne)]
        upd = [_dot_tn(ins[d][4][0, j, r, :], x) for (d, j), r, x in zip(chains, rows, v_new)]
        for i, (d, j) in enumerate(chains):
            ins[d][6][j, rows[i], :] = o[i]
        s = [s[i] * ins[d][5][0, j, grow[i], :][0:1] + jnp.where(same, upd[i], 0.0) for i, (d, j) in enumerate(chains)]
    for i, (d, j) in enumerate(chains):
        s_ref[d * INTER_BATCH + j] = s[i]


def _dn_finish_kernel(of_ref, ob_ref, z_ref, nw_ref, o_ref):
    hmean = _head_mean_matrix(1.0 / HEAD_DIM)
    nw = nw_ref[...]

    def finish(i, c):
        rows = pl.ds(pl.multiple_of(i * SEQ_BLK, SEQ_BLK), SEQ_BLK)
        o = of_ref[0, rows, :] + ob_ref[0, rows, :]
        o_ref[rows, :] = o * lax.rsqrt(_dot_split(o * o, hmean) + NORM_EPS) * nw * _silu(z_ref[rows, :])
        return c

    lax.fori_loop(0, N_BLK, finish, 0)


def _inter_block(n, reverse):
    if not reverse:
        return n
    return _chunk_of_step(INTER_CHUNKS * n + INTER_CHUNKS - 1, True) // INTER_CHUNKS


def _inter_rows(t, reverse, rows_per_chunk):
    j = INTER_CHUNKS - 1 - t if reverse else t
    return slice(j * rows_per_chunk, (j + 1) * rows_per_chunk)


def deltanet_mixer(slab, conv_w, a_log, dt_bias, norm_w):
    nb = slab.shape[0] // T_ALL
    assert nb % INTER_BATCH == 0 and CTX_CHUNKS % INTER_CHUNKS == 0 and N_CHUNK % INTER_CHUNKS == 0
    lanes = lambda t: jnp.repeat(t, HEAD_DIM, axis=-1)
    seq = lambda dt: jax.ShapeDtypeStruct((nb, T_ALL, GROUP), dt)
    seq2 = lambda dt: jax.ShapeDtypeStruct((2, nb, T_ALL, GROUP), dt)
    b1 = pl.BlockSpec((1, T_ALL, GROUP), lambda b: (b, 0, 0))
    p2 = pl.BlockSpec((2, 1, SEQ_BLK, GROUP), lambda b, i: (0, b, i, 0))
    small = lambda shape: pl.BlockSpec(shape, lambda b, i: (0, 0))
    gl_shape = jax.ShapeDtypeStruct((2, nb, N_CHUNK * GL_ROWS, GROUP), F32)
    u, w, attn, qd, kd, gl = pl.pallas_call(
        _dn_intra_kernel,
        grid=(nb, N_BLK),
        in_specs=[pl.BlockSpec((T_ALL, DN_W), lambda b, i: (b, 0)), small((3, 3 * GROUP)),
                  small((2, GROUP)), small((2, GROUP))],
        out_specs=[p2, p2, p2, p2, p2,
                   pl.BlockSpec((2, 1, INTRA_CHUNKS * GL_ROWS, GROUP), lambda b, i: (0, b, i, 0))],
        out_shape=[seq2(F32), seq2(BF16), seq2(BF16), seq2(BF16), seq2(BF16), gl_shape],
        compiler_params=_cparams("parallel", "parallel"),
        name="deltanet_intra",
    )(slab, conv_w, lanes(a_log), lanes(dt_bias))

    def per_dir(d, rows_per_chunk):
        return pl.BlockSpec((1, INTER_BATCH, INTER_CHUNKS * rows_per_chunk, GROUP),
                            lambda b, n: (d, b, _inter_block(n, d == 1), 0))

    def out_dir(d):
        return pl.BlockSpec((INTER_BATCH, INTER_CHUNKS * CHUNK, GROUP), lambda b, n: (b, _inter_block(n, d == 1), 0))

    specs = [per_dir(d, r) for d in range(2) for r in (CHUNK,) * 5 + (GL_ROWS,)]
    o_f, o_b = pl.pallas_call(
        _dn_inter_kernel,
        grid=(nb // INTER_BATCH, N_CHUNK // INTER_CHUNKS),
        in_specs=specs,
        out_specs=[out_dir(0), out_dir(1)],
        out_shape=[seq(F32), seq(F32)],
        scratch_shapes=[pltpu.VMEM((2 * INTER_BATCH, GROUP, GROUP), F32)],
        compiler_params=_cparams("parallel", "arbitrary"),
        name="deltanet_inter",
    )(u, w, attn, qd, kd, gl, u, w, attn, qd, kd, gl)

    return pl.pallas_call(
        _dn_finish_kernel,
        grid=(nb,),
        in_specs=[b1, b1, pl.BlockSpec((T_ALL, GROUP), lambda b: (b, 3)), _resident((1, GROUP))],
        out_specs=pl.BlockSpec((T_ALL, GROUP), lambda b: (b, 0)),
        out_shape=jax.ShapeDtypeStruct((nb * T_ALL, GROUP), F32),
        compiler_params=_cparams("parallel"),
        name="deltanet_finish",
    )(o_f, o_b, slab, jnp.tile(norm_w, GROUP_HEADS).reshape(1, GROUP))


RW_LR = RW_DECAY_RANK + RW_AAA_RANK + RW_GATE_RANK
RW_LR_OUT = 5 * GROUP


def _dot3(a, b_hi, b_lo):
    a_hi = a.astype(BF16)
    a_lo = (a - a_hi.astype(F32)).astype(BF16)
    return _dot(a_hi, b_hi) + (_dot(a_lo, b_hi) + _dot(a_hi, b_lo))


def rwkv_lowrank_weights(w_up, a_up, g_up):
    w = jnp.zeros((RW_LR, RW_LR_OUT), F32)
    o1 = RW_DECAY_RANK
    o2 = o1 + RW_AAA_RANK
    for d in range(2):
        w = w.at[0:o1, d * GROUP:(d + 1) * GROUP].set(w_up[d])
        w = w.at[o1:o2, (2 + d) * GROUP:(3 + d) * GROUP].set(a_up[d])
    return w.at[o2:RW_LR, 4 * GROUP:5 * GROUP].set(g_up)


def _rw_intra_kernel(slab_ref, mu_ref, pv_ref, wlr_ref,
                     at_ref, rt_ref, bg_ref, kg_ref, gl_ref, v_ref, bonus_ref, g_ref,
                     xc_ref, arb_ref, rhs0_ref, yk_ref):
    i = pl.program_id(1)
    hsum = _head_mean_matrix(1.0)
    wlr = wlr_ref[...]
    wlr_hi = wlr.astype(BF16)
    wlr_lo = (wlr - wlr_hi.astype(F32)).astype(BF16)
    pv = pv_ref[...]
    w0 = (pv[0:1], pv[1:2])
    a0 = (pv[2:3], pv[3:4])
    k_k, k_a, r_k = pv[4:5], pv[5:6], pv[6:7]
    lr_lane = lax.broadcasted_iota(jnp.int32, (1, RW_LR), 1)

    def shifted(c0, c1):
        prev, cur, nxt = _prev_cur_next(slab_ref, i, c0, c1)
        return cur + mu_ref[0:1, c0:c1] * (prev - cur) + mu_ref[1:2, c0:c1] * (nxt - cur)

    r = shifted(0, GROUP)
    k = shifted(GROUP, 2 * GROUP)
    v = shifted(2 * GROUP, 3 * GROUP)
    lr = shifted(3 * GROUP, 3 * GROUP + RW_LR)
    t = jnp.where(lr_lane < RW_DECAY_RANK, jnp.tanh(lr),
                  jnp.where(lr_lane < RW_DECAY_RANK + RW_AAA_RANK, lr, _sigmoid(lr)))
    proj = _dot3(t, wlr_hi, wlr_lo)
    kq = k * k_k
    kk = kq * lax.rsqrt(_dot_split(kq * kq, hsum) + 1e-6)
    v_blk = v.astype(BF16)
    v_ref[0] = v_blk
    g_ref[...] = proj[:, 4 * GROUP:5 * GROUP]
    ksum = jnp.zeros_like(k)
    at_blk, rt_blk, bh_blk, kh_blk = [], [], [], []
    for d in range(2):
        w_log = -_softplus(-(w0[d] + proj[:, d * GROUP:(d + 1) * GROUP])) - 0.5
        lw = -jnp.exp(w_log)
        a_gate = _sigmoid(a0[d] + proj[:, (2 + d) * GROUP:(3 + d) * GROUP])
        k_d = k * (1.0 + (a_gate - 1.0) * k_a)
        ksum = ksum + k_d
        cum = _chunk_cumsum(lw, reverse=(d == 1))
        ends = [cum[j * CHUNK:j * CHUNK + 1, :] if d == 1 else cum[(j + 1) * CHUNK - 1:(j + 1) * CHUNK, :]
                for j in range(INTRA_CHUNKS)]
        to_end = jnp.exp(jnp.concatenate([jnp.broadcast_to(e, (CHUNK, GROUP)) for e in ends], axis=0) - cum)
        inv = jnp.exp(-cum)
        b = kk * a_gate
        at_blk.append((-kk * jnp.exp(cum - lw)).astype(BF16))
        rt_blk.append((r * jnp.exp(cum)).astype(BF16))
        bh_blk.append((b * inv).astype(BF16))
        kh_blk.append((k_d * inv).astype(BF16))
        at_ref[d, 0] = at_blk[d]
        rt_ref[d, 0] = rt_blk[d]
        bg_ref[d, 0] = (b * to_end).astype(BF16)
        kg_ref[d, 0] = (k_d * to_end).astype(BF16)
        gl_ref[d, 0] = jnp.concatenate([jnp.broadcast_to(jnp.exp(e), (GL_ROWS, GROUP)) for e in ends], axis=0)
    bonus_ref[...] = _dot_split(r * ksum * r_k, hsum) * v

    masks = (_cat_masks(False), _cat_masks(True))
    chains = [(j, d) for j in range(INTRA_CHUNKS) for d in range(2)]
    rows = [slice(j * CHUNK, (j + 1) * CHUNK) for j, d in chains]
    at = [at_blk[d][r_] for (j, d), r_ in zip(chains, rows)]
    rt = [rt_blk[d][r_] for (j, d), r_ in zip(chains, rows)]
    bh_e = [_expand_heads(bh_blk[d][r_]) for (j, d), r_ in zip(chains, rows)]
    kh_e = [_expand_heads(kh_blk[d][r_]) for (j, d), r_ in zip(chains, rows)]
    v_e = [_expand_heads(v_blk[r_]) for r_ in rows]
    x = _inverse_unit_triangular(
        [jnp.where(masks[d][1], -_dot_nt(a, b_), 0.0) for (j, d), a, b_ in zip(chains, at, bh_e)])
    a_ak = [jnp.where(masks[d][1], _dot_nt(a, b_), 0.0).astype(BF16) for (j, d), a, b_ in zip(chains, at, kh_e)]
    a_rb = [jnp.where(masks[d][0], _dot_nt(a, b_), 0.0).astype(BF16) for (j, d), a, b_ in zip(chains, rt, bh_e)]
    a_rk = [jnp.where(masks[d][0], _dot_nt(a, b_), 0.0).astype(BF16) for (j, d), a, b_ in zip(chains, rt, kh_e)]
    rhs0 = [_dot(a, ve) for a, ve in zip(a_ak, v_e)]
    yk = [_dot(a, ve) for a, ve in zip(a_rk, v_e)]
    for n_, ((j, d), r_) in enumerate(zip(chains, rows)):
        xc_ref[d, 0, r_, :] = x[n_].astype(BF16)
        arb_ref[d, 0, r_, :] = a_rb[n_]
        rhs0_ref[d, 0, r_, :] = rhs0[n_]
        yk_ref[d, 0, r_, :] = yk[n_]


def _rw_inter_kernel(*refs):
    n_in = 10
    fwd, bwd = refs[0:n_in], refs[n_in:2 * n_in]
    yf_ref, yb_ref, s_ref = refs[2 * n_in:]

    @pl.when(pl.program_id(1) == 0)
    def _():
        s_ref[...] = jnp.zeros_like(s_ref)

    same = _same_head_mask()
    ins = (fwd, bwd)
    outs = (yf_ref, yb_ref)
    chains = [(d, j) for j in range(INTER_BATCH) for d in range(2)]

    s = [s_ref[d * INTER_BATCH + j] for d, j in chains]
    for t in range(INTER_CHUNKS):
        rows = [_inter_rows(t, d == 1, CHUNK) for d, j in chains]

        def arg(idx):
            return [ins[d][idx][0, j, r, :] for (d, j), r in zip(chains, rows)]

        at, rt, bg, kg, xc, arb, rhs0, yk = (arg(i) for i in range(8))
        gamma = [ins[d][8][0, j, _inter_rows(t, d == 1, GL_ROWS), :][0:1] for d, j in chains]
        v = [ins[d][9][j, r, :] for (d, j), r in zip(chains, rows)]
        s_b = [x.astype(BF16) for x in s]
        rhs = [_expand_heads((_dot_nt(a, sb) + r0).astype(BF16)) for a, sb, r0 in zip(at, s_b, rhs0)]
        sa = [_dot(x, r).astype(BF16) for x, r in zip(xc, rhs)]
        sa_e = [_expand_heads(x) for x in sa]
        y = [_dot_nt(r, sb) + _dot(a, se) + y0 for r, sb, a, se, y0 in zip(rt, s_b, arb, sa_e, yk)]
        upd = [_dot_tn(a, b) + _dot_tn(c, e) for a, b, c, e in zip(sa, bg, v, kg)]
        for i, (d, j) in enumerate(chains):
            outs[d][j, rows[i], :] = y[i]
        s = [s[i] * gamma[i] + jnp.where(same, upd[i], 0.0) for i in range(len(chains))]
    for i, (d, j) in enumerate(chains):
        s_ref[d * INTER_BATCH + j] = s[i]


def _rw_finish_kernel(yf_ref, yb_ref, bonus_ref, g_ref, pv_ref, o_ref):
    hmean = _head_mean_matrix(1.0 / HEAD_DIM)
    ln_w, ln_b = pv_ref[7:8, :], pv_ref[8:9, :]

    def finish(i, c):
        rows = pl.ds(pl.multiple_of(i * SEQ_BLK, SEQ_BLK), SEQ_BLK)
        y = yf_ref[0, rows, :] + yb_ref[0, rows, :]
        yc = y - _dot_split(y, hmean)
        yn = yc * lax.rsqrt(_dot_split(yc * yc, hmean) + RW_LN_EPS) * ln_w + ln_b
        o_ref[rows, :] = (yn + bonus_ref[rows, :]) * g_ref[rows, :]
        return c

    lax.fori_loop(0, N_BLK, finish, 0)


def rwkv_mixer(slab, mu, w0, w_up, a0, a_up, g_up, k_k, k_a, r_k, ln_w, ln_b):
    n = slab.shape[0]
    nb = n // T_ALL
    assert nb % INTER_BATCH == 0 and CTX_CHUNKS % INTER_CHUNKS == 0 and N_CHUNK % INTER_CHUNKS == 0
    pv = jnp.concatenate([w0, a0, k_k[None], k_a[None], r_k.reshape(1, GROUP), ln_w[None], ln_b[None],
                          jnp.zeros((7, GROUP), F32)], axis=0)
    seq = lambda dt: jax.ShapeDtypeStruct((nb, T_ALL, GROUP), dt)
    seq2 = lambda dt: jax.ShapeDtypeStruct((2, nb, T_ALL, GROUP), dt)
    flat = jax.ShapeDtypeStruct((n, GROUP), F32)
    gl_shape = jax.ShapeDtypeStruct((2, nb, N_CHUNK * GL_ROWS, GROUP), F32)
    b1 = pl.BlockSpec((1, T_ALL, GROUP), lambda b: (b, 0, 0))
    bflat = pl.BlockSpec((T_ALL, GROUP), lambda b: (b, 0))
    p1 = pl.BlockSpec((1, SEQ_BLK, GROUP), lambda b, i: (b, i, 0))
    p2 = pl.BlockSpec((2, 1, SEQ_BLK, GROUP), lambda b, i: (0, b, i, 0))
    pflat = pl.BlockSpec((SEQ_BLK, GROUP), lambda b, i: (b * N_BLK + i, 0))
    pgl = pl.BlockSpec((2, 1, INTRA_CHUNKS * GL_ROWS, GROUP), lambda b, i: (0, b, i, 0))
    small = lambda shape: pl.BlockSpec(shape, lambda b, i: (0, 0))
    at, rt, bg, kg, gl, v, bonus, g, xc, arb, rhs0, yk = pl.pallas_call(
        _rw_intra_kernel,
        grid=(nb, N_BLK),
        in_specs=[pl.BlockSpec((T_ALL, RW_W), lambda b, i: (b, 0)), small((2, RW_W)), small((16, GROUP)),
                  small((RW_LR, RW_LR_OUT))],
        out_specs=[p2] * 4 + [pgl, p1, pflat, pflat] + [p2] * 4,
        out_shape=[seq2(BF16)] * 4 + [gl_shape, seq(BF16), flat, flat, seq2(BF16), seq2(BF16), seq2(F32), seq2(F32)],
        compiler_params=_cparams("parallel", "parallel"),
        name="rwkv7_intra",
    )(slab, mu, pv, rwkv_lowrank_weights(w_up, a_up, g_up))

    def per_dir(d, rows_per_chunk):
        return pl.BlockSpec((1, INTER_BATCH, INTER_CHUNKS * rows_per_chunk, GROUP),
                            lambda b, n_: (d, b, _inter_block(n_, d == 1), 0))

    def shared(d):
        return pl.BlockSpec((INTER_BATCH, INTER_CHUNKS * CHUNK, GROUP), lambda b, n_: (b, _inter_block(n_, d == 1), 0))

    specs = [s for d in range(2) for s in [per_dir(d, CHUNK)] * 8 + [per_dir(d, GL_ROWS), shared(d)]]
    per = (at, rt, bg, kg, xc, arb, rhs0, yk, gl, v)
    y_f, y_b = pl.pallas_call(
        _rw_inter_kernel,
        grid=(nb // INTER_BATCH, N_CHUNK // INTER_CHUNKS),
        in_specs=specs,
        out_specs=[shared(0), shared(1)],
        out_shape=[seq(F32), seq(F32)],
        scratch_shapes=[pltpu.VMEM((2 * INTER_BATCH, GROUP, GROUP), F32)],
        compiler_params=_cparams("parallel", "arbitrary"),
        name="rwkv7_inter",
    )(*per, *per)

    return pl.pallas_call(
        _rw_finish_kernel,
        grid=(nb,),
        in_specs=[b1, b1, bflat, bflat, _resident((16, GROUP))],
        out_specs=bflat,
        out_shape=flat,
        compiler_params=_cparams("parallel"),
        name="rwkv7_finish",
    )(y_f, y_b, bonus, g, pv)


DFT_SPLIT = 64
DFT_BLK = 256


def _dft_tables(n):
    big = 2 * n
    t = np.arange(n, dtype=np.int64)[:, None]
    k1 = np.arange(n // DFT_SPLIT, dtype=np.int64)[None, :]
    k2 = np.arange(DFT_SPLIT, dtype=np.int64)[None, :]
    alpha = 2.0 * np.pi * ((DFT_SPLIT * t * k1) % big) / big
    beta = 2.0 * np.pi * ((t * k2) % big) / big

    def pad(a):
        out = np.zeros((n, 128), np.float32)
        out[:, :a.shape[1]] = a
        return out

    return np.stack([pad(np.cos(alpha)), pad(np.sin(alpha)), pad(np.cos(beta)), pad(np.sin(beta))])


def _dft_gen_kernel(n, tab_ref, g_ref):
    k = lax.broadcasted_iota(jnp.int32, (128, n), 1)
    row = lax.broadcasted_iota(jnp.int32, (128, n), 0)
    e_a = jnp.where(k // DFT_SPLIT == row, 1.0, 0.0).astype(BF16)
    e_b = jnp.where(jnp.logical_and(k % DFT_SPLIT == row, row < DFT_SPLIT), 1.0, 0.0).astype(BF16)
    ca = _dot_split(tab_ref[0], e_a)
    sa = _dot_split(tab_ref[1], e_a)
    cb = _dot_split(tab_ref[2], e_b)
    sb = _dot_split(tab_ref[3], e_b)
    g_ref[:, 0:n] = (ca * cb - sa * sb).astype(BF16)
    g_ref[:, n:2 * n] = (-(sa * cb + ca * sb)).astype(BF16)


def dft_matrix(n):
    blk = min(DFT_BLK, n)
    return pl.pallas_call(
        functools.partial(_dft_gen_kernel, n),
        grid=(n // blk,),
        in_specs=[pl.BlockSpec((4, blk, 128), lambda i: (0, i, 0))],
        out_specs=pl.BlockSpec((blk, 2 * n), lambda i: (i, 0)),
        out_shape=jax.ShapeDtypeStruct((n, 2 * n), BF16),
        compiler_params=_cparams("parallel"),
        name=f"dft_matrix_{n}",
    )(jnp.asarray(_dft_tables(n)))


HY_COLS_F = 2 * HY_ORDER * GROUP
HY_OC = HY_ORDER * GROUP


def _hyena_filter_kernel(n, z_ref, w1_ref, b1_ref, w2_ref, b2_ref, w3_ref, freq_ref, dl_ref, hs_ref, hd_ref):
    blk = min(SEQ_BLK, n)
    freq = freq_ref[...]
    dl = dl_ref[...]

    def fill(i, norm):
        r0 = pl.multiple_of(i * blk, blk)
        z = z_ref[pl.ds(r0, blk), :]
        hid = jnp.sin(freq * (_dot_hi(z, w1_ref[...]) + b1_ref[...]))
        hid = jnp.sin(freq * (_dot_hi(hid, w2_ref[...]) + b2_ref[...]))
        t = z[:, 0:1]
        h = _dot_hi(hid, w3_ref[...]) * jnp.exp(-t * dl)
        lag = lax.broadcasted_iota(jnp.int32, (blk, 1), 0) + r0
        hf = h[:, 0:HY_OC]
        hb = jnp.where(lag == 0, 0.0, h[:, HY_OC:2 * HY_OC])
        hs_ref[pl.ds(r0, blk), :] = hf + hb
        hd_ref[pl.ds(r0, blk), :] = hf - hb
        return norm + jnp.sum(jnp.abs(hf) + jnp.abs(hb), axis=0, keepdims=True)

    norm = lax.fori_loop(0, n // blk, fill, jnp.zeros((1, HY_OC), F32))
    inv = 1.0 / norm

    def scale(i, c):
        rows = pl.ds(pl.multiple_of(i * blk, blk), blk)
        hs_ref[rows, :] = hs_ref[rows, :] * inv
        hd_ref[rows, :] = hd_ref[rows, :] * inv
        return c

    lax.fori_loop(0, n // blk, scale, 0)


def hyena_filter_taps(n, f_w1, f_b1, f_w2, f_b2, f_w3, f_freq):
    t = jnp.linspace(0.0, 1.0, n, dtype=F32)[:, None]
    ang = 2.0 * math.pi * jnp.arange(n, dtype=F32)[:, None] / n
    bands = jnp.linspace(1e-4, HY_BANDS - 1, HY_BANDS, dtype=F32)[None]
    z = jnp.concatenate([t, jnp.cos(bands * ang), -jnp.sin(bands * ang)], axis=-1)
    emb = z.shape[1]
    z = jnp.pad(z, ((0, 0), (0, 128 - emb)))
    w1 = jnp.pad(f_w1, ((0, 128 - emb), (0, 0)))
    max_decay = math.log(HY_TARGET) / HY_SHORT_DECAY_PCT
    min_decay = math.log(HY_TARGET) / HY_LONG_DECAY_PCT
    deltas = jnp.abs(jnp.linspace(min_decay, max_decay, HY_OC, dtype=F32))
    dl = jnp.tile(deltas, 2).reshape(1, HY_COLS_F)
    hid = f_w2.shape[0]
    out = jax.ShapeDtypeStruct((n, HY_OC), F32)
    return pl.pallas_call(
        functools.partial(_hyena_filter_kernel, n),
        out_shape=[out, out],
        compiler_params=pltpu.CompilerParams(vmem_limit_bytes=VMEM_LIMIT),
        name=f"hyena_filter_{n}",
    )(z, w1, f_b1.reshape(1, hid), f_w2, f_b2.reshape(1, hid), f_w3, f_freq.reshape(1, hid), dl)


def _hyena_spectrum_kernel(n, g_ref, hs_ref, hd_ref, kr_ref, ki_ref, kn_ref):
    blk = min(2 * SEQ_BLK, n)
    big = 2.0 * n

    def split(ref):
        x = ref[...]
        hi = x.astype(BF16)
        return hi, (x - hi.astype(F32)).astype(BF16)

    s_hi, s_lo = split(hs_ref)
    d_hi, d_lo = split(hd_ref)

    def body(i, c):
        r0 = pl.multiple_of(i * blk, blk)
        rows = pl.ds(r0, blk)
        k = lax.broadcasted_iota(jnp.int32, (blk, 1), 0) + r0
        wgt = jnp.where(k == 0, 1.0 / big, 2.0 / big)
        gc = g_ref[rows, 0:n]
        gs = g_ref[rows, n:2 * n]
        kr_ref[rows, :] = (_dot(gc, s_hi) + _dot(gc, s_lo)) * wgt
        ki_ref[rows, :] = (_dot(gs, d_hi) + _dot(gs, d_lo)) * wgt
        return c

    lax.fori_loop(0, n // blk, body, 0)
    t = lax.broadcasted_iota(jnp.int32, (n, 1), 0)
    sign = jnp.where(t % 2 == 0, 1.0, -1.0)
    kn_ref[...] = jnp.broadcast_to(jnp.sum(sign * hs_ref[...], axis=0, keepdims=True) * (1.0 / big), (8, HY_OC))


def hyena_spectrum(n, g, hs, hd):
    out = jax.ShapeDtypeStruct((n, HY_OC), F32)
    return pl.pallas_call(
        functools.partial(_hyena_spectrum_kernel, n),
        out_shape=[out, out, jax.ShapeDtypeStruct((8, HY_OC), F32)],
        compiler_params=pltpu.CompilerParams(vmem_limit_bytes=VMEM_LIMIT),
        name=f"hyena_spectrum_{n}",
    )(g, hs, hd)


def _hyena_conv_kernel(slab_ref, w_ref, o_ref):
    def body(i, c):
        rows = pl.ds(pl.multiple_of(i * SEQ_BLK, SEQ_BLK), SEQ_BLK)
        for j in range(3):
            prev, cur, nxt = _prev_cur_next(slab_ref, i, j * GROUP, (j + 1) * GROUP)
            w = w_ref[:, j * GROUP:(j + 1) * GROUP]
            o_ref[rows, j * GROUP:(j + 1) * GROUP] = prev * w[0:1] + cur * w[1:2] + nxt * w[2:3]
        return c

    lax.fori_loop(0, N_BLK, body, 0)


def hyena_short_conv(slab, conv_w):
    n = slab.shape[0]
    return pl.pallas_call(
        _hyena_conv_kernel,
        grid=(n // T_ALL,),
        in_specs=[pl.BlockSpec((T_ALL, 3 * GROUP), lambda b: (b, 0)),
                  pl.BlockSpec((3, 3 * GROUP), lambda b: (0, 0))],
        out_specs=pl.BlockSpec((T_ALL, 3 * GROUP), lambda b: (b, 0)),
        out_shape=jax.ShapeDtypeStruct((n, 3 * GROUP), F32),
        compiler_params=_cparams("parallel"),
        name="hyena_short_conv",
    )(slab, conv_w)


HY_FBLK = 1024


def _alt_sign(n):
    t = lax.broadcasted_iota(jnp.int32, (n, 1), 0)
    return jnp.where(t % 2 == 0, 1.0, -1.0)


def _hyena_fwd_kernel(x_ref, gl_ref, gc_ref, krl_ref, kil_ref, knl_ref, krc_ref, kic_ref, knc_ref,
                      pl_ref, pc_ref, pn_ref):
    def transform(x, g_ref, kr_ref, ki_ref, kn_ref, p_ref, n, blk):
        xb = x.astype(BF16)

        def body(i, c):
            rows = pl.ds(pl.multiple_of(i * blk, blk), blk)
            zr = _dot(g_ref[rows, 0:n], xb)
            zi = _dot(g_ref[rows, n:2 * n], xb)
            kr = kr_ref[rows, :]
            ki = ki_ref[rows, :]
            p_ref[0, 0, rows, :] = (zr * kr - zi * ki).astype(BF16)
            p_ref[0, 1, rows, :] = (zr * ki + zi * kr).astype(BF16)
            return c

        lax.fori_loop(0, n // blk, body, 0)
        return jnp.sum(_alt_sign(n) * x, axis=0, keepdims=True) * kn_ref[0:1, :]

    nyq_c = transform(x_ref[0:CTX_LEN, :], gc_ref, krc_ref, kic_ref, knc_ref, pc_ref, CTX_LEN, CTX_LEN)
    nyq_l = transform(x_ref[CTX_LEN:T_ALL, :], gl_ref, krl_ref, kil_ref, knl_ref, pl_ref, SEQ, HY_FBLK)
    pn_ref[0] = jnp.concatenate([nyq_l, nyq_c, jnp.zeros((6, GROUP), F32)], axis=0)


def _resident(shape):
    return pl.BlockSpec(shape, lambda b: (0,) * len(shape))


def hyena_forward_transform(x, col, g_l, g_c, spec_l, spec_c, order):
    n = x.shape[0]
    nb = n // T_ALL
    kcol = lambda shape: pl.BlockSpec(shape, lambda b: (0, order))
    return pl.pallas_call(
        _hyena_fwd_kernel,
        grid=(nb,),
        in_specs=[
            pl.BlockSpec((T_ALL, GROUP), lambda b: (b, col)),
            _resident((SEQ, 2 * SEQ)), _resident((CTX_LEN, 2 * CTX_LEN)),
            kcol((SEQ, GROUP)), kcol((SEQ, GROUP)), kcol((8, GROUP)),
            kcol((CTX_LEN, GROUP)), kcol((CTX_LEN, GROUP)), kcol((8, GROUP)),
        ],
        out_specs=[
            pl.BlockSpec((1, 2, SEQ, GROUP), lambda b: (b, 0, 0, 0)),
            pl.BlockSpec((1, 2, CTX_LEN, GROUP), lambda b: (b, 0, 0, 0)),
            pl.BlockSpec((1, 8, GROUP), lambda b: (b, 0, 0)),
        ],
        out_shape=[
            jax.ShapeDtypeStruct((nb, 2, SEQ, GROUP), BF16),
            jax.ShapeDtypeStruct((nb, 2, CTX_LEN, GROUP), BF16),
            jax.ShapeDtypeStruct((nb, 8, GROUP), F32),
        ],
        compiler_params=_cparams("parallel"),
        name=f"hyena_fwd_{order}",
    )(x, g_l, g_c, *spec_l, *spec_c)


def _hyena_inv_kernel(pl_ref, pc_ref, pn_ref, gl_ref, gc_ref, u_ref, gate_ref, bias_ref, o_ref):
    bias = bias_ref[0]

    def inverse(p_ref, nyq, g_ref, n, blk, off):
        pr = p_ref[0, 0]
        pi = p_ref[0, 1]

        def body(i, c):
            r0 = pl.multiple_of(i * blk, blk)
            rows = pl.ds(r0, blk)
            orow = pl.ds(pl.multiple_of(off + r0, math.gcd(blk, CTX_LEN)), blk)
            t = lax.broadcasted_iota(jnp.int32, (blk, 1), 0)
            sign = jnp.where(t % 2 == 0, 1.0, -1.0)
            y = _dot(g_ref[rows, 0:n], pr) + _dot(g_ref[rows, n:2 * n], pi) + sign * nyq
            o_ref[orow, :] = gate_ref[orow, :] * (y + u_ref[orow, :] * bias)
            return c

        lax.fori_loop(0, n // blk, body, 0)

    inverse(pc_ref, pn_ref[0, 1:2, :], gc_ref, CTX_LEN, CTX_LEN, 0)
    inverse(pl_ref, pn_ref[0, 0:1, :], gl_ref, SEQ, HY_FBLK, CTX_LEN)


def hyena_inverse_transform(p_l, p_c, p_n, g_l, g_c, u, ucol, gate, gcol, bias):
    nb = p_l.shape[0]
    return pl.pallas_call(
        _hyena_inv_kernel,
        grid=(nb,),
        in_specs=[
            pl.BlockSpec((1, 2, SEQ, GROUP), lambda b: (b, 0, 0, 0)),
            pl.BlockSpec((1, 2, CTX_LEN, GROUP), lambda b: (b, 0, 0, 0)),
            pl.BlockSpec((1, 8, GROUP), lambda b: (b, 0, 0)),
            _resident((SEQ, 2 * SEQ)), _resident((CTX_LEN, 2 * CTX_LEN)),
            pl.BlockSpec((T_ALL, GROUP), lambda b: (b, ucol)),
            pl.BlockSpec((T_ALL, GROUP), lambda b: (b, gcol)),
            pl.BlockSpec((1, 1, GROUP), lambda b: (0, 0, 0)),
        ],
        out_specs=pl.BlockSpec((T_ALL, GROUP), lambda b: (b, 0)),
        out_shape=jax.ShapeDtypeStruct((nb * T_ALL, GROUP), F32),
        compiler_params=_cparams("parallel"),
        name="hyena_inv",
    )(p_l, p_c, p_n, g_l, g_c, u, gate, bias.reshape(1, 1, GROUP))


def hyena_mixer(slab, g_l, g_c, conv_w, f_w1, f_b1, f_w2, f_b2, f_w3, f_freq, bias):
    u = hyena_short_conv(slab, conv_w)
    spec_l = hyena_spectrum(SEQ, g_l, *hyena_filter_taps(SEQ, f_w1, f_b1, f_w2, f_b2, f_w3, f_freq))
    spec_c = hyena_spectrum(CTX_LEN, g_c, *hyena_filter_taps(CTX_LEN, f_w1, f_b1, f_w2, f_b2, f_w3, f_freq))
    p = hyena_forward_transform(u, 0, g_l, g_c, spec_l, spec_c, 0)
    z = hyena_inverse_transform(*p, g_l, g_c, u, 0, u, 1, bias[0])
    p = hyena_forward_transform(z, 0, g_l, g_c, spec_l, spec_c, 1)
    return hyena_inverse_transform(*p, g_l, g_c, z, 0, u, 2, bias[1])


def kernel(x, c, ctx, c_ctx, w_mod, b_mod, norm_w, ffn_w_gu, ffn_w_down, w_in, w_out,
           hy_conv, hy_f_w1, hy_f_b1, hy_f_w2, hy_f_b2, hy_f_w3, hy_f_freq, hy_bias,
           na_q_norm, na_k_norm, na_rpb, dn_conv, dn_a_log, dn_dt_bias, dn_norm,
           rw_mu, rw_w0, rw_w_up, rw_a0, rw_a_up, rw_g_up, rw_k_k, rw_k_a, rw_r_k, rw_ln_w, rw_ln_b):
    nb = x.shape[0]
    assert x.shape[1:] == (SEQ, D_MODEL) and ctx.shape[1:] == (CTX_LEN, D_MODEL) and nb + 1 <= 16
    s = jnp.concatenate([ctx, x], axis=1).reshape(nb * T_ALL, D_MODEL)
    cond = jnp.concatenate([c_ctx[None], c, jnp.zeros((15 - nb, D_MODEL), F32)], axis=0)
    mod = modulation_all(cond, w_mod, b_mod).reshape(DEPTH, 16, N_MOD, D_MODEL)
    g_l = dft_matrix(SEQ)
    g_c = dft_matrix(CTX_LEN)
    w_gu = ffn_w_gu.astype(BF16)
    w_down = ffn_w_down.astype(BF16)
    w_out_b = w_out.astype(BF16)
    dn_end = 6 * GROUP + 4 * GROUP + 4 * GROUP_HEADS
    w_in_p = jnp.concatenate(
        [w_in[:, :, :dn_end], jnp.zeros((DEPTH, D_MODEL, 6 * GROUP + DN_W - dn_end), F32), w_in[:, :, dn_end:]],
        axis=2).astype(BF16)
    for l in range(DEPTH):
        need_ctx = l < DEPTH - 1
        modc = mod[l, 0:1]
        modb = mod[l, 1:1 + nb]
        s = ffn_half_step(s, modc, modb, norm_w[l, 0], w_gu, w_down, l, 0, 0)
        hy_s, na_s, dn_s, rw_s = input_projection(s, modc, modb, norm_w[l, 1], w_in_p, l)
        groups = (
            hyena_mixer(hy_s, g_l, g_c, hy_conv[l], hy_f_w1[l], hy_f_b1[l], hy_f_w2[l], hy_f_b2[l], hy_f_w3[l],
                        hy_f_freq[l], hy_bias[l]),
            na_mixer(na_s, na_q_norm[l], na_k_norm[l], na_bias_table(na_rpb[l]), need_ctx),
            deltanet_mixer(dn_s, dn_conv[l], dn_a_log[l], dn_dt_bias[l], dn_norm[l]),
            rwkv_mixer(rw_s, rw_mu[l], rw_w0[l], rw_w_up[l], rw_a0[l], rw_a_up[l], rw_g_up[l], rw_k_k[l],
                       rw_k_a[l], rw_r_k[l], rw_ln_w[l], rw_ln_b[l]),
        )
        s = output_projection(s, modc, modb, groups, w_out_b, l)
        s = ffn_half_step(s, modc, modb, norm_w[l, 2], w_gu, w_down, l, 1, 2)
    return s.reshape(nb, T_ALL, D_MODEL)[:, CTX_LEN:]
```

```python
import functools
import math

import numpy as np
import jax
import jax.numpy as jnp
from jax import lax
from jax.experimental import pallas as pl
from jax.experimental.pallas import tpu as pltpu

D_MODEL = 1024
SEQ = 2048
DEPTH = 2
CTX_LEN = 256
T_ALL = CTX_LEN + SEQ
GRID_W = 64
GROUP = 256
HEAD_DIM = 64
GROUP_HEADS = 4
D_FF = 2816
N_MOD = 9
NORM_EPS = 1e-6

HY_ORDER = 2
HY_BANDS = 16
HY_TARGET = 1e-2
HY_SHORT_DECAY_PCT = 0.3
HY_LONG_DECAY_PCT = 1.5

NA_WIN_ROWS = 8
NA_WIN_COLS = 16

CHUNK = 64
RW_DECAY_RANK = 32
RW_AAA_RANK = 32
RW_GATE_RANK = 64
RW_LN_EPS = 64e-5

DN_W = 4 * GROUP + 128
RW_W = 3 * GROUP + 128
P_PAD = 3 * GROUP + 3 * GROUP + DN_W + RW_W

TM = 768
TF = 512
VMEM_LIMIT = 56 * 1024 * 1024

F32 = jnp.float32
BF16 = jnp.bfloat16


def _cparams(*sem):
    return pltpu.CompilerParams(dimension_semantics=sem, vmem_limit_bytes=VMEM_LIMIT)


def _silu(x):
    return x * (1.0 / (1.0 + jnp.exp(-x)))


def _sigmoid(x):
    return 1.0 / (1.0 + jnp.exp(-x))


def _softplus(x):
    return jnp.maximum(x, 0.0) + jnp.log(1.0 + jnp.exp(-jnp.abs(x)))


def _dot(a, b):
    return jnp.dot(a, b, preferred_element_type=F32)


def _dot_nt(a, b):
    return lax.dot_general(a, b, (((1,), (1,)), ((), ())), preferred_element_type=F32)


def _dot_tn(a, b):
    return lax.dot_general(a, b, (((0,), (0,)), ((), ())), preferred_element_type=F32)


def _dot_hi(a, b):
    return jnp.dot(a, b, preferred_element_type=F32, precision=lax.Precision.HIGHEST)


def _mod_kernel(cond_ref, w_ref, b_ref, o_ref):
    a = _silu(cond_ref[...]).astype(BF16)
    o_ref[0] = _dot(a, w_ref[0].astype(BF16)) + b_ref[0]


def modulation_all(cond, w_mod, b_mod):
    r = cond.shape[0]
    tn = 1024
    return pl.pallas_call(
        _mod_kernel,
        grid=(DEPTH, N_MOD * D_MODEL // tn),
        in_specs=[
            pl.BlockSpec((r, D_MODEL), lambda l, j: (0, 0)),
            pl.BlockSpec((1, D_MODEL, tn), lambda l, j: (l, 0, j)),
            pl.BlockSpec((1, 1, tn), lambda l, j: (l, 0, j)),
        ],
        out_specs=pl.BlockSpec((1, r, tn), lambda l, j: (l, 0, j)),
        out_shape=jax.ShapeDtypeStruct((DEPTH, r, N_MOD * D_MODEL), F32),
        compiler_params=_cparams("parallel", "parallel"),
        name="modulation",
    )(cond, w_mod, b_mod.reshape(DEPTH, 1, N_MOD * D_MODEL))


TL = 512


def _row_mod(modc_ref, modb_ref, tile, idx, latent_only=False):
    if latent_only:
        return modb_ref[0, idx:idx + 1, :]
    row = lax.broadcasted_iota(jnp.int32, (TM, 1), 0) + (tile % (T_ALL // TM)) * TM
    return jnp.where(row < CTX_LEN, modc_ref[0, idx:idx + 1, :], modb_ref[0, idx:idx + 1, :])


def _adaln(x, nw, shift, scale):
    y = x * lax.rsqrt(jnp.mean(x * x, axis=-1, keepdims=True) + NORM_EPS)
    return y * nw * (1.0 + scale) + shift


def _ffn_kernel(sub, latent_only, x_ref, modc_ref, modb_ref, nw_ref, wgu_ref, wd_ref, o_ref):
    i = pl.program_id(0)
    x = x_ref[...]
    shift = _row_mod(modc_ref, modb_ref, i, 3 * sub, latent_only)
    scale = _row_mod(modc_ref, modb_ref, i, 3 * sub + 1, latent_only)
    h = _adaln(x, nw_ref[...], shift, scale).astype(BF16)
    acc = None
    for c0 in range(0, D_FF, TF):
        c1 = min(c0 + TF, D_FF)
        a = (_silu(_dot(h, wgu_ref[:, c0:c1])) * _dot(h, wgu_ref[:, D_FF + c0:D_FF + c1])).astype(BF16)
        part = _dot(a, wd_ref[c0:c1, :])
        acc = part if acc is None else acc + part
    gate = _row_mod(modc_ref, modb_ref, i, 3 * sub + 2, latent_only)
    o_ref[...] = x + 0.5 * gate * acc


def ffn_half_step(x, modc, modb, nw, w_gu, w_down, layer, which, sub, latent_only=False):
    n = x.shape[0]
    tm, tiles_per_b = (TL, SEQ // TL) if latent_only else (TM, T_ALL // TM)
    once = pl.Buffered(1)
    return pl.pallas_call(
        functools.partial(_ffn_kernel, sub, latent_only),
        grid=(n // tm,),
        in_specs=[
            pl.BlockSpec((tm, D_MODEL), lambda i: (i, 0)),
            pl.BlockSpec((1, N_MOD, D_MODEL), lambda i: (0, 0, 0)),
            pl.BlockSpec((1, N_MOD, D_MODEL), lambda i: (i // tiles_per_b, 0, 0)),
            pl.BlockSpec((1, D_MODEL), lambda i: (0, 0)),
            pl.BlockSpec((None, None, D_MODEL, 2 * D_FF), lambda i: (layer, which, 0, 0), pipeline_mode=once),
            pl.BlockSpec((None, None, D_FF, D_MODEL), lambda i: (layer, which, 0, 0), pipeline_mode=once),
        ],
        out_specs=pl.BlockSpec((tm, D_MODEL), lambda i: (i, 0)),
        out_shape=jax.ShapeDtypeStruct((n, D_MODEL), F32),
        compiler_params=_cparams("parallel"),
        name=f"ffn{sub}",
    )(x, modc, modb, nw.reshape(1, D_MODEL), w_gu, w_down)


def _inproj_kernel(x_ref, modc_ref, modb_ref, nw_ref, w_ref, hy_ref, na_ref, dn_ref, rw_ref):
    i = pl.program_id(0)
    shift = _row_mod(modc_ref, modb_ref, i, 3)
    scale = _row_mod(modc_ref, modb_ref, i, 4)
    h = _adaln(x_ref[...], nw_ref[...], shift, scale).astype(BF16)
    o0 = 3 * GROUP
    o1 = 6 * GROUP
    o2 = o1 + DN_W
    y = _dot(h, w_ref[...])
    hy_ref[...] = y[:, 0:o0]
    na_ref[...] = y[:, o0:o1]
    dn_ref[...] = y[:, o1:o2]
    rw_ref[...] = y[:, o2:P_PAD]


def input_projection(x, modc, modb, nw, w_in_p, layer):
    n = x.shape[0]
    tiles_per_b = T_ALL // TM
    widths = (3 * GROUP, 3 * GROUP, DN_W, RW_W)
    return pl.pallas_call(
        _inproj_kernel,
        grid=(n // TM,),
        in_specs=[
            pl.BlockSpec((TM, D_MODEL), lambda i: (i, 0)),
            pl.BlockSpec((1, N_MOD, D_MODEL), lambda i: (0, 0, 0)),
            pl.BlockSpec((1, N_MOD, D_MODEL), lambda i: (i // tiles_per_b, 0, 0)),
            pl.BlockSpec((1, D_MODEL), lambda i: (0, 0)),
            pl.BlockSpec((None, D_MODEL, P_PAD), lambda i: (layer, 0, 0)),
        ],
        out_specs=[pl.BlockSpec((TM, w), lambda i: (i, 0)) for w in widths],
        out_shape=[jax.ShapeDtypeStruct((n, w), F32) for w in widths],
        compiler_params=_cparams("parallel"),
        name="inproj",
    )(x, modc, modb, nw.reshape(1, D_MODEL), w_in_p)


def _outproj_kernel(latent_only, x_ref, modc_ref, modb_ref, g0_ref, g1_ref, g2_ref, g3_ref, w_ref, o_ref):
    i = pl.program_id(0)
    y = _dot(g0_ref[...].astype(BF16), w_ref[0:GROUP, :])
    y += _dot(g1_ref[...].astype(BF16), w_ref[GROUP:2 * GROUP, :])
    y += _dot(g2_ref[...].astype(BF16), w_ref[2 * GROUP:3 * GROUP, :])
    y += _dot(g3_ref[...].astype(BF16), w_ref[3 * GROUP:4 * GROUP, :])
    gate = _row_mod(modc_ref, modb_ref, i, 5, latent_only)
    o_ref[...] = x_ref[...] + gate * y


def output_projection(x, modc, modb, groups, w_out, layer, latent_only=False):
    n = x.shape[0]
    nb = n // T_ALL
    if latent_only:
        tm, tiles_per_b = TL, SEQ // TL
        rows = lambda i: pl.multiple_of((i // tiles_per_b) * T_ALL + CTX_LEN + (i % tiles_per_b) * TL, math.gcd(CTX_LEN, TL))
        win = lambda width: pl.BlockSpec((pl.Element(tm), pl.Element(width)), lambda i: (rows(i), 0))
    else:
        tm, tiles_per_b = TM, T_ALL // TM
        win = lambda width: pl.BlockSpec((tm, width), lambda i: (i, 0))
    n_out = nb * tiles_per_b * tm
    return pl.pallas_call(
        functools.partial(_outproj_kernel, latent_only),
        grid=(n_out // tm,),
        in_specs=[
            win(D_MODEL),
            pl.BlockSpec((1, N_MOD, D_MODEL), lambda i: (0, 0, 0)),
            pl.BlockSpec((1, N_MOD, D_MODEL), lambda i: (i // tiles_per_b, 0, 0)),
        ] + [win(GROUP)] * 4 + [
            pl.BlockSpec((None, D_MODEL, D_MODEL), lambda i: (layer, 0, 0)),
        ],
        out_specs=pl.BlockSpec((tm, D_MODEL), lambda i: (i, 0)),
        out_shape=jax.ShapeDtypeStruct((n_out, D_MODEL), F32),
        compiler_params=_cparams("parallel"),
        name="outproj",
    )(x, modc, modb, *groups, w_out)


def _head_mean_matrix(scale):
    r = lax.broadcasted_iota(jnp.int32, (GROUP, GROUP), 0) // HEAD_DIM
    c = lax.broadcasted_iota(jnp.int32, (GROUP, GROUP), 1) // HEAD_DIM
    return jnp.where(r == c, scale, 0.0).astype(BF16)


def _dot_split(a, m_bf16):
    hi = a.astype(BF16)
    lo = (a - hi.astype(F32)).astype(BF16)
    return _dot(hi, m_bf16) + _dot(lo, m_bf16)


def _lane_head(width=GROUP):
    return lax.broadcasted_iota(jnp.int32, (1, width), 1) // HEAD_DIM


NA_ROWS = SEQ // GRID_W
NA_LOCAL = NA_WIN_ROWS * GRID_W
NA_NEG = -1e30
NA_BLK = 256
NA_PAIR = 4


def na_bias_table(rpb):
    n_dr = 2 * NA_WIN_ROWS
    rows = jnp.pad(rpb, ((0, 0), (0, 1), (0, 128 - rpb.shape[2]))).reshape(GROUP_HEADS * n_dr, 128)
    toep = pl.pallas_call(
        _na_bias_kernel,
        out_shape=jax.ShapeDtypeStruct((GROUP_HEADS * n_dr, GRID_W * GRID_W), F32),
        name="na_bias",
    )(rows).reshape(GROUP_HEADS, n_dr, GRID_W, GRID_W)
    tab = jnp.stack([toep[:, NA_WIN_ROWS - 1 - p:2 * NA_WIN_ROWS - 1 - p] for p in range(NA_WIN_ROWS)], axis=0)
    tab = jnp.transpose(tab, (0, 1, 3, 2, 4))
    return tab.reshape(NA_WIN_ROWS, GROUP_HEADS, GRID_W, NA_LOCAL)


def _na_bias_kernel(rpb_ref, o_ref):
    n = GRID_W * GRID_W
    d = lax.broadcasted_iota(jnp.int32, (128, n), 0)
    cj = lax.broadcasted_iota(jnp.int32, (128, n), 1)
    onehot = jnp.where((cj % GRID_W) - (cj // GRID_W) + NA_WIN_COLS - 1 == d, 1.0, 0.0).astype(BF16)
    cj1 = lax.broadcasted_iota(jnp.int32, (1, n), 1)
    c = cj1 // GRID_W
    j = cj1 % GRID_W
    start = jnp.clip(c - NA_WIN_COLS // 2, 0, GRID_W - NA_WIN_COLS)
    in_win = jnp.logical_and(j >= start, j < start + NA_WIN_COLS)
    o_ref[...] = jnp.where(in_win, _dot_exact_rhs(rpb_ref[...], onehot), NA_NEG)


def _na_kernel(need_ctx, slab_ref, qw_ref, kw_ref, bias_ref, o_ref, q_s, k_s, v_s):
    hm = _head_mean_matrix(1.0 / HEAD_DIM)
    qw = qw_ref[...] * (HEAD_DIM ** -0.5)
    kw = kw_ref[...]

    def prep(i, c):
        r0 = pl.multiple_of(i * NA_BLK, NA_BLK)
        q = slab_ref[pl.ds(r0, NA_BLK), 0:GROUP]
        k = slab_ref[pl.ds(r0, NA_BLK), GROUP:2 * GROUP]
        q_s[pl.ds(r0, NA_BLK), :] = (q * lax.rsqrt(_dot_split(q * q, hm) + NORM_EPS) * qw).astype(BF16)
        k_s[pl.ds(r0, NA_BLK), :] = (k * lax.rsqrt(_dot_split(k * k, hm) + NORM_EPS) * kw).astype(BF16)
        v_s[pl.ds(r0, NA_BLK), :] = slab_ref[pl.ds(r0, NA_BLK), 2 * GROUP:3 * GROUP].astype(BF16)
        return c

    lax.fori_loop(0, T_ALL // NA_BLK, prep, 0)

    lane_h = _lane_head()
    kc = k_s[0:CTX_LEN, :]
    vc = v_s[0:CTX_LEN, :]

    if need_ctx:
        qc = q_s[0:CTX_LEN, :]
        out = jnp.zeros((CTX_LEN, GROUP), F32)
        for h in range(GROUP_HEADS):
            mask = lane_h == h
            s = _dot_nt(jnp.where(mask, qc, jnp.zeros_like(qc)), kc)
            e = jnp.exp(s - jnp.max(s, axis=-1, keepdims=True))
            p = e * (1.0 / jnp.sum(e, axis=-1, keepdims=True))
            out = jnp.where(mask, _dot(p.astype(BF16), vc), out)
        o_ref[0:CTX_LEN, :] = out
    else:
        o_ref[0:CTX_LEN, :] = jnp.zeros((CTX_LEN, GROUP), F32)

    def pair_body(i, c):
        rows = [i * NA_PAIR + t for t in range(NA_PAIR)]
        start = [jnp.clip(r - NA_WIN_ROWS // 2, 0, NA_ROWS - NA_WIN_ROWS) for r in rows]
        q0 = [pl.multiple_of(CTX_LEN + r * GRID_W, GRID_W) for r in rows]
        k0 = [pl.multiple_of(CTX_LEN + s * GRID_W, GRID_W) for s in start]
        q = [_expand_heads(q_s[pl.ds(a, GRID_W), :]) for a in q0]
        kb = [k_s[pl.ds(a, NA_LOCAL), :] for a in k0]
        vb = [v_s[pl.ds(a, NA_LOCAL), :] for a in k0]
        bias = [bias_ref[r - s].reshape(GROUP_HEADS * GRID_W, NA_LOCAL) for r, s in zip(rows, start)]
        s_loc = [_dot_nt(q[t], kb[t]) + bias[t] for t in range(NA_PAIR)]
        s_ctx = [_dot_nt(q[t], kc) for t in range(NA_PAIR)]
        m = [jnp.maximum(jnp.max(a, axis=-1, keepdims=True), jnp.max(b, axis=-1, keepdims=True))
             for a, b in zip(s_loc, s_ctx)]
        e_loc = [jnp.exp(a - mm) for a, mm in zip(s_loc, m)]
        e_ctx = [jnp.exp(b - mm) for b, mm in zip(s_ctx, m)]
        inv = [1.0 / (jnp.sum(a, axis=-1, keepdims=True) + jnp.sum(b, axis=-1, keepdims=True))
               for a, b in zip(e_loc, e_ctx)]
        o = [_dot((e_loc[t] * inv[t]).astype(BF16), vb[t]) + _dot((e_ctx[t] * inv[t]).astype(BF16), vc)
             for t in range(NA_PAIR)]
        for t in range(NA_PAIR):
            out = o[t][0:GRID_W]
            for h in range(1, GROUP_HEADS):
                out = jnp.where(lane_h == h, o[t][h * GRID_W:(h + 1) * GRID_W], out)
            o_ref[pl.ds(q0[t], GRID_W), :] = out
        return c

    lax.fori_loop(0, NA_ROWS // NA_PAIR, pair_body, 0)


def na_mixer(slab, q_norm, k_norm, bias_tab, need_ctx):
    n = slab.shape[0]
    tile4 = lambda w: jnp.tile(w, GROUP_HEADS).reshape(1, GROUP)
    return pl.pallas_call(
        functools.partial(_na_kernel, need_ctx),
        grid=(n // T_ALL,),
        in_specs=[
            pl.BlockSpec((T_ALL, 3 * GROUP), lambda b: (b, 0)),
            pl.BlockSpec((1, GROUP), lambda b: (0, 0)),
            pl.BlockSpec((1, GROUP), lambda b: (0, 0)),
            pl.BlockSpec((NA_WIN_ROWS, GROUP_HEADS, GRID_W, NA_LOCAL), lambda b: (0, 0, 0, 0)),
        ],
        out_specs=pl.BlockSpec((T_ALL, GROUP), lambda b: (b, 0)),
        out_shape=jax.ShapeDtypeStruct((n, GROUP), F32),
        scratch_shapes=[pltpu.VMEM((T_ALL, GROUP), BF16)] * 3,
        compiler_params=_cparams("parallel"),
        name="na_mixer",
    )(slab, tile4(q_norm), tile4(k_norm), bias_tab)


SEQ_BLK = 256
N_BLK = T_ALL // SEQ_BLK
N_CHUNK = T_ALL // CHUNK
CTX_CHUNKS = CTX_LEN // CHUNK


def _prev_cur_next(ref, i, c0, c1):
    r0 = pl.multiple_of(i * SEQ_BLK, SEQ_BLK)
    cur = ref[pl.ds(r0, SEQ_BLK), c0:c1]
    up0 = pl.multiple_of(jnp.maximum(r0 - 8, 0), 8)
    dn0 = pl.multiple_of(jnp.minimum(r0 + SEQ_BLK, T_ALL - 8), 8)
    up = ref[pl.ds(up0, 8), c0:c1][7:8, :]
    dn = ref[pl.ds(dn0, 8), c0:c1][0:1, :]
    up = jnp.where(i >= 2, up, 0.0)
    dn = jnp.where(jnp.logical_and(i >= 1, i <= N_BLK - 2), dn, 0.0)
    row = lax.broadcasted_iota(jnp.int32, (SEQ_BLK, 1), 0)
    prev = jnp.where(row == 0, up, pltpu.roll(cur, 1, 0))
    nxt = jnp.where(row == SEQ_BLK - 1, dn, pltpu.roll(cur, SEQ_BLK - 1, 0))
    return prev, cur, nxt


def _chunk_cumsum(x, reverse):
    pos = lax.broadcasted_iota(jnp.int32, (SEQ_BLK, 1), 0) % CHUNK
    s = 1
    while s < CHUNK:
        if reverse:
            x = x + jnp.where(pos < CHUNK - s, pltpu.roll(x, SEQ_BLK - s, 0), 0.0)
        else:
            x = x + jnp.where(pos >= s, pltpu.roll(x, s, 0), 0.0)
        s *= 2
    return x


def _split3(a):
    hi = a.astype(BF16)
    r1 = a - hi.astype(F32)
    mid = r1.astype(BF16)
    lo = (r1 - mid.astype(F32)).astype(BF16)
    return hi, mid, lo


def _dot_exact_rhs(a, m_bf16):
    hi, mid, lo = _split3(a)
    return _dot(hi, m_bf16) + _dot(mid, m_bf16) + _dot(lo, m_bf16)


def _expand_heads(x):
    lane_h = _lane_head()
    return jnp.concatenate([jnp.where(lane_h == h, x, 0.0) for h in range(GROUP_HEADS)], axis=0)


def _chunk_of_step(n, reverse):
    if not reverse:
        return n
    return jnp.where(n < CTX_CHUNKS, CTX_CHUNKS - 1 - n, N_CHUNK + CTX_CHUNKS - 1 - n)


INV_BASE = 16


def _cat_dot(a, b):
    return _dot(a.astype(BF16), _expand_heads(b.astype(BF16)))


def _cat_index():
    i = lax.broadcasted_iota(jnp.int32, (CHUNK, GROUP_HEADS * CHUNK), 0)
    j = lax.broadcasted_iota(jnp.int32, (CHUNK, GROUP_HEADS * CHUNK), 1) % CHUNK
    return i, j


def _cat_masks(reverse):
    i, j = _cat_index()
    if reverse:
        return i <= j, i < j
    return i >= j, i > j


def _inverse_unit_triangular(mats):
    i, j = _cat_index()
    inner = (i // INV_BASE) == (j // INV_BASE)
    eye = jnp.where(i == j, 1.0, 0.0)
    nd = [jnp.where(inner, n, 0.0) for n in mats]
    x = [eye - n for n in nd]
    p = [_cat_dot(n, n) for n in nd]
    k = 2
    while k < INV_BASE:
        x = [xi + _cat_dot(pi, xi) for xi, pi in zip(x, p)]
        k *= 2
        if k < INV_BASE:
            p = [_cat_dot(pi, pi) for pi in p]
    width = INV_BASE
    while width < CHUNK:
        outer = (i // (2 * width)) == (j // (2 * width))
        sel = jnp.logical_and(outer, jnp.logical_not(inner))
        t = [_cat_dot(jnp.where(sel, n, 0.0), xi) for n, xi in zip(mats, x)]
        x = [xi - _cat_dot(xi, ti) for xi, ti in zip(x, t)]
        inner = outer
        width *= 2
    return x


def _head_rows(gc, lane_onehot):
    hi, mid, lo = _split3(gc)
    t = _dot_nt(lane_onehot, hi) + _dot_nt(lane_onehot, mid) + _dot_nt(lane_onehot, lo)
    return jnp.concatenate([t[h:h + 1, :] for h in range(GROUP_HEADS)], axis=1)


INTRA_CHUNKS = SEQ_BLK // CHUNK
INTER_BATCH = 4
INTER_CHUNKS = 2
GL_ROWS = 8


def _dn_prep_block(slab_ref, conv_ref, alog_ref, dt_ref, i):
    hsum = _head_mean_matrix(1.0)
    col = lax.broadcasted_iota(jnp.int32, (128, GROUP), 0)
    lane = lax.broadcasted_iota(jnp.int32, (128, GROUP), 1) // HEAD_DIM
    neg_a = -jnp.exp(alog_ref[...])
    dtb = dt_ref[...]
    rows = pl.ds(pl.multiple_of(i * SEQ_BLK, SEQ_BLK), SEQ_BLK)
    qkv = []
    for j in range(3):
        prev, cur, nxt = _prev_cur_next(slab_ref, i, j * GROUP, (j + 1) * GROUP)
        w = conv_ref[:, j * GROUP:(j + 1) * GROUP]
        u = _silu(prev * w[0:1] + cur * w[1:2] + nxt * w[2:3])
        if j == 0:
            u = u * lax.rsqrt(_dot_split(u * u, hsum) + 1e-6) * (HEAD_DIM ** -0.5)
        elif j == 1:
            u = u * lax.rsqrt(_dot_split(u * u, hsum) + 1e-6)
        qkv.append(u)
    ba = slab_ref[rows, 4 * GROUP:4 * GROUP + 128]
    gc, beta = [], []
    for d in range(2):
        e_b = jnp.where(col == 8 * d + lane, 1.0, 0.0).astype(BF16)
        e_a = jnp.where(col == 8 * d + 4 + lane, 1.0, 0.0).astype(BF16)
        beta.append(_sigmoid(_dot_exact_rhs(ba, e_b)))
        g = neg_a[d:d + 1] * _softplus(_dot_exact_rhs(ba, e_a) + dtb[d:d + 1])
        gc.append(_chunk_cumsum(g, reverse=(d == 1)))
    return qkv[0], qkv[1], qkv[2], gc, beta


def _dn_intra_kernel(slab_ref, conv_ref, alog_ref, dt_ref, u_ref, w_ref, attn_ref, qd_ref, kd_ref, gl_ref):
    q_blk, k_blk, v_blk, gc_blk, beta_blk = _dn_prep_block(slab_ref, conv_ref, alog_ref, dt_ref, pl.program_id(1))
    onehot = jnp.where(
        lax.broadcasted_iota(jnp.int32, (8, GROUP), 1) == HEAD_DIM * lax.broadcasted_iota(jnp.int32, (8, GROUP), 0),
        1.0, 0.0).astype(BF16)
    masks = (_cat_masks(False), _cat_masks(True))
    chains = [(j, d) for j in range(INTRA_CHUNKS) for d in range(2)]
    rows = [slice(j * CHUNK, (j + 1) * CHUNK) for j, d in chains]
    gc = [gc_blk[d][r] for (j, d), r in zip(chains, rows)]
    beta = [beta_blk[d][r] for (j, d), r in zip(chains, rows)]
    q = [q_blk[r] for r in rows]
    k = [k_blk[r] for r in rows]
    v = [v_blk[r] for r in rows]
    eg = [jnp.exp(g) for g in gc]
    g_last = [g[0:1, :] if d == 1 else g[CHUNK - 1:CHUNK, :] for (j, d), g in zip(chains, gc)]
    kb = [a * b for a, b in zip(k, beta)]
    k_e = [_expand_heads(a.astype(BF16)) for a in k]
    dec = []
    for (j, d), g in zip(chains, gc):
        incl = masks[d][0]
        dec.append(jnp.where(incl, jnp.exp(jnp.where(incl, g - _head_rows(g, onehot), 0.0)), 0.0))
    m = [jnp.where(masks[d][1], _dot_nt(a.astype(BF16), ke) * dc, 0.0)
         for (j, d), a, ke, dc in zip(chains, kb, k_e, dec)]
    attn = [_dot_nt(a.astype(BF16), ke) * dc for a, ke, dc in zip(q, k_e, dec)]
    rhs = [jnp.concatenate([_expand_heads((a * b).astype(BF16)), _expand_heads((c * e).astype(BF16))], axis=1)
           for a, b, c, e in zip(v, beta, kb, eg)]
    x = _inverse_unit_triangular(m)
    sol = [_dot(xi.astype(BF16), r) for xi, r in zip(x, rhs)]
    for i, ((j, d), r) in enumerate(zip(chains, rows)):
        u_ref[d, 0, r, :] = sol[i][:, 0:GROUP]
        w_ref[d, 0, r, :] = sol[i][:, GROUP:2 * GROUP].astype(BF16)
        attn_ref[d, 0, r, :] = attn[i].astype(BF16)
        qd_ref[d, 0, r, :] = (q[i] * eg[i]).astype(BF16)
        kd_ref[d, 0, r, :] = (k[i] * jnp.exp(g_last[i] - gc[i])).astype(BF16)
        gl_ref[d, 0, pl.ds(j * GL_ROWS, GL_ROWS), :] = jnp.broadcast_to(jnp.exp(g_last[i]), (GL_ROWS, GROUP))


def _same_head_mask():
    r = lax.broadcasted_iota(jnp.int32, (GROUP, GROUP), 0) // HEAD_DIM
    c = lax.broadcasted_iota(jnp.int32, (GROUP, GROUP), 1) // HEAD_DIM
    return r == c


def _dn_inter_kernel(uf, wf, af, qf, kf, gf, ub, wb, ab, qb, kb, gb, of_ref, ob_ref, s_ref):
    @pl.when(pl.program_id(1) == 0)
    def _():
        s_ref[...] = jnp.zeros_like(s_ref)

    same = _same_head_mask()
    ins = ((uf, wf, af, qf, kf, gf, of_ref), (ub, wb, ab, qb, kb, gb, ob_ref))
    chains = [(d, j) for j in range(INTER_BATCH) for d in range(2)]
    s = [s_ref[d * INTER_BATCH + j] for d, j in chains]
    for t in range(INTER_CHUNKS):
        rows = [_inter_rows(t, d == 1, CHUNK) for d, j in chains]
        grow = [_inter_rows(t, d == 1, GL_ROWS) for d, j in chains]
        s_b = [x.astype(BF16) for x in s]
        v_new = [(ins[d][0][0, j, r, :] - _dot(ins[d][1][0, j, r, :], sb)).astype(BF16)
                 for (d, j), r, sb in zip(chains, rows, s_b)]
        v_ne = [_expand_heads(x) for x in v_new]
        o = [_dot(ins[d][3][0, j, r, :], sb) + _dot(ins[d][2][0, j, r, :], ve)
             for (d, j), r, sb, ve in zip(chains, rows, s_b, v_ne)]
        upd = [_dot_tn(ins[d][4][0, j, r, :], x) for (d, j), r, x in zip(chains, rows, v_new)]
        for i, (d, j) in enumerate(chains):
            ins[d][6][j, rows[i], :] = o[i]
        s = [s[i] * ins[d][5][0, j, grow[i], :][0:1] + jnp.where(same, upd[i], 0.0) for i, (d, j) in enumerate(chains)]
    for i, (d, j) in enumerate(chains):
        s_ref[d * INTER_BATCH + j] = s[i]


def _dn_finish_kernel(of_ref, ob_ref, z_ref, nw_ref, o_ref):
    hmean = _head_mean_matrix(1.0 / HEAD_DIM)
    nw = nw_ref[...]

    def finish(i, c):
        rows = pl.ds(pl.multiple_of(i * SEQ_BLK, SEQ_BLK), SEQ_BLK)
        o = of_ref[0, rows, :] + ob_ref[0, rows, :]
        o_ref[rows, :] = o * lax.rsqrt(_dot_split(o * o, hmean) + NORM_EPS) * nw * _silu(z_ref[rows, :])
        return c

    lax.fori_loop(0, N_BLK, finish, 0)


def _inter_block(n, reverse):
    if not reverse:
        return n
    return _chunk_of_step(INTER_CHUNKS * n + INTER_CHUNKS - 1, True) // INTER_CHUNKS


def _inter_rows(t, reverse, rows_per_chunk):
    j = INTER_CHUNKS - 1 - t if reverse else t
    return slice(j * rows_per_chunk, (j + 1) * rows_per_chunk)


def deltanet_mixer(slab, conv_w, a_log, dt_bias, norm_w):
    nb = slab.shape[0] // T_ALL
    assert nb % INTER_BATCH == 0 and CTX_CHUNKS % INTER_CHUNKS == 0 and N_CHUNK % INTER_CHUNKS == 0
    lanes = lambda t: jnp.repeat(t, HEAD_DIM, axis=-1)
    seq = lambda dt: jax.ShapeDtypeStruct((nb, T_ALL, GROUP), dt)
    seq2 = lambda dt: jax.ShapeDtypeStruct((2, nb, T_ALL, GROUP), dt)
    b1 = pl.BlockSpec((1, T_ALL, GROUP), lambda b: (b, 0, 0))
    p2 = pl.BlockSpec((2, 1, SEQ_BLK, GROUP), lambda b, i: (0, b, i, 0))
    small = lambda shape: pl.BlockSpec(shape, lambda b, i: (0, 0))
    gl_shape = jax.ShapeDtypeStruct((2, nb, N_CHUNK * GL_ROWS, GROUP), F32)
    u, w, attn, qd, kd, gl = pl.pallas_call(
        _dn_intra_kernel,
        grid=(nb, N_BLK),
        in_specs=[pl.BlockSpec((T_ALL, DN_W), lambda b, i: (b, 0)), small((3, 3 * GROUP)),
                  small((2, GROUP)), small((2, GROUP))],
        out_specs=[p2, p2, p2, p2, p2,
                   pl.BlockSpec((2, 1, INTRA_CHUNKS * GL_ROWS, GROUP), lambda b, i: (0, b, i, 0))],
        out_shape=[seq2(F32), seq2(BF16), seq2(BF16), seq2(BF16), seq2(BF16), gl_shape],
        compiler_params=_cparams("parallel", "parallel"),
        name="deltanet_intra",
    )(slab, conv_w, lanes(a_log), lanes(dt_bias))

    def per_dir(d, rows_per_chunk):
        return pl.BlockSpec((1, INTER_BATCH, INTER_CHUNKS * rows_per_chunk, GROUP),
                            lambda b, n: (d, b, _inter_block(n, d == 1), 0))

    def out_dir(d):
        return pl.BlockSpec((INTER_BATCH, INTER_CHUNKS * CHUNK, GROUP), lambda b, n: (b, _inter_block(n, d == 1), 0))

    specs = [per_dir(d, r) for d in range(2) for r in (CHUNK,) * 5 + (GL_ROWS,)]
    o_f, o_b = pl.pallas_call(
        _dn_inter_kernel,
        grid=(nb // INTER_BATCH, N_CHUNK // INTER_CHUNKS),
        in_specs=specs,
        out_specs=[out_dir(0), out_dir(1)],
        out_shape=[seq(F32), seq(F32)],
        scratch_shapes=[pltpu.VMEM((2 * INTER_BATCH, GROUP, GROUP), F32)],
        compiler_params=_cparams("parallel", "arbitrary"),
        name="deltanet_inter",
    )(u, w, attn, qd, kd, gl, u, w, attn, qd, kd, gl)

    return pl.pallas_call(
        _dn_finish_kernel,
        grid=(nb,),
        in_specs=[b1, b1, pl.BlockSpec((T_ALL, GROUP), lambda b: (b, 3)), _resident((1, GROUP))],
        out_specs=pl.BlockSpec((T_ALL, GROUP), lambda b: (b, 0)),
        out_shape=jax.ShapeDtypeStruct((nb * T_ALL, GROUP), F32),
        compiler_params=_cparams("parallel"),
        name="deltanet_finish",
    )(o_f, o_b, slab, jnp.tile(norm_w, GROUP_HEADS).reshape(1, GROUP))


RW_LR = RW_DECAY_RANK + RW_AAA_RANK + RW_GATE_RANK
RW_LR_OUT = 5 * GROUP


def _dot3(a, b_hi, b_lo):
    a_hi = a.astype(BF16)
    a_lo = (a - a_hi.astype(F32)).astype(BF16)
    return _dot(a_hi, b_hi) + (_dot(a_lo, b_hi) + _dot(a_hi, b_lo))


def rwkv_lowrank_weights(w_up, a_up, g_up):
    w = jnp.zeros((RW_LR, RW_LR_OUT), F32)
    o1 = RW_DECAY_RANK
    o2 = o1 + RW_AAA_RANK
    for d in range(2):
        w = w.at[0:o1, d * GROUP:(d + 1) * GROUP].set(w_up[d])
        w = w.at[o1:o2, (2 + d) * GROUP:(3 + d) * GROUP].set(a_up[d])
    return w.at[o2:RW_LR, 4 * GROUP:5 * GROUP].set(g_up)


def _rw_intra_kernel(slab_ref, mu_ref, pv_ref, wlr_ref,
                     at_ref, rt_ref, bg_ref, kg_ref, gl_ref, v_ref, bonus_ref, g_ref,
                     xc_ref, arb_ref, rhs0_ref, yk_ref):
    i = pl.program_id(1)
    hsum = _head_mean_matrix(1.0)
    wlr = wlr_ref[...]
    wlr_hi = wlr.astype(BF16)
    wlr_lo = (wlr - wlr_hi.astype(F32)).astype(BF16)
    pv = pv_ref[...]
    w0 = (pv[0:1], pv[1:2])
    a0 = (pv[2:3], pv[3:4])
    k_k, k_a, r_k = pv[4:5], pv[5:6], pv[6:7]
    lr_lane = lax.broadcasted_iota(jnp.int32, (1, RW_LR), 1)

    def shifted(c0, c1):
        prev, cur, nxt = _prev_cur_next(slab_ref, i, c0, c1)
        return cur + mu_ref[0:1, c0:c1] * (prev - cur) + mu_ref[1:2, c0:c1] * (nxt - cur)

    r = shifted(0, GROUP)
    k = shifted(GROUP, 2 * GROUP)
    v = shifted(2 * GROUP, 3 * GROUP)
    lr = shifted(3 * GROUP, 3 * GROUP + RW_LR)
    t = jnp.where(lr_lane < RW_DECAY_RANK, jnp.tanh(lr),
                  jnp.where(lr_lane < RW_DECAY_RANK + RW_AAA_RANK, lr, _sigmoid(lr)))
    proj = _dot3(t, wlr_hi, wlr_lo)
    kq = k * k_k
    kk = kq * lax.rsqrt(_dot_split(kq * kq, hsum) + 1e-6)
    v_blk = v.astype(BF16)
    v_ref[0] = v_blk
    g_ref[...] = proj[:, 4 * GROUP:5 * GROUP]
    ksum = jnp.zeros_like(k)
    at_blk, rt_blk, bh_blk, kh_blk = [], [], [], []
    for d in range(2):
        w_log = -_softplus(-(w0[d] + proj[:, d * GROUP:(d + 1) * GROUP])) - 0.5
        lw = -jnp.exp(w_log)
        a_gate = _sigmoid(a0[d] + proj[:, (2 + d) * GROUP:(3 + d) * GROUP])
        k_d = k * (1.0 + (a_gate - 1.0) * k_a)
        ksum = ksum + k_d
        cum = _chunk_cumsum(lw, reverse=(d == 1))
        ends = [cum[j * CHUNK:j * CHUNK + 1, :] if d == 1 else cum[(j + 1) * CHUNK - 1:(j + 1) * CHUNK, :]
                for j in range(INTRA_CHUNKS)]
        to_end = jnp.exp(jnp.concatenate([jnp.broadcast_to(e, (CHUNK, GROUP)) for e in ends], axis=0) - cum)
        inv = jnp.exp(-cum)
        b = kk * a_gate
        at_blk.append((-kk * jnp.exp(cum - lw)).astype(BF16))
        rt_blk.append((r * jnp.exp(cum)).astype(BF16))
        bh_blk.append((b * inv).astype(BF16))
        kh_blk.append((k_d * inv).astype(BF16))
        at_ref[d, 0] = at_blk[d]
        rt_ref[d, 0] = rt_blk[d]
        bg_ref[d, 0] = (b * to_end).astype(BF16)
        kg_ref[d, 0] = (k_d * to_end).astype(BF16)
        gl_ref[d, 0] = jnp.concatenate([jnp.broadcast_to(jnp.exp(e), (GL_ROWS, GROUP)) for e in ends], axis=0)
    bonus_ref[...] = _dot_split(r * ksum * r_k, hsum) * v

    masks = (_cat_masks(False), _cat_masks(True))
    chains = [(j, d) for j in range(INTRA_CHUNKS) for d in range(2)]
    rows = [slice(j * CHUNK, (j + 1) * CHUNK) for j, d in chains]
    at = [at_blk[d][r_] for (j, d), r_ in zip(chains, rows)]
    rt = [rt_blk[d][r_] for (j, d), r_ in zip(chains, rows)]
    bh_e = [_expand_heads(bh_blk[d][r_]) for (j, d), r_ in zip(chains, rows)]
    kh_e = [_expand_heads(kh_blk[d][r_]) for (j, d), r_ in zip(chains, rows)]
    v_e = [_expand_heads(v_blk[r_]) for r_ in rows]
    x = _inverse_unit_triangular(
        [jnp.where(masks[d][1], -_dot_nt(a, b_), 0.0) for (j, d), a, b_ in zip(chains, at, bh_e)])
    a_ak = [jnp.where(masks[d][1], _dot_nt(a, b_), 0.0).astype(BF16) for (j, d), a, b_ in zip(chains, at, kh_e)]
    a_rb = [jnp.where(masks[d][0], _dot_nt(a, b_), 0.0).astype(BF16) for (j, d), a, b_ in zip(chains, rt, bh_e)]
    a_rk = [jnp.where(masks[d][0], _dot_nt(a, b_), 0.0).astype(BF16) for (j, d), a, b_ in zip(chains, rt, kh_e)]
    rhs0 = [_dot(a, ve) for a, ve in zip(a_ak, v_e)]
    yk = [_dot(a, ve) for a, ve in zip(a_rk, v_e)]
    for n_, ((j, d), r_) in enumerate(zip(chains, rows)):
        xc_ref[d, 0, r_, :] = x[n_].astype(BF16)
        arb_ref[d, 0, r_, :] = a_rb[n_]
        rhs0_ref[d, 0, r_, :] = rhs0[n_]
        yk_ref[d, 0, r_, :] = yk[n_]


def _rw_inter_kernel(*refs):
    n_in = 10
    fwd, bwd = refs[0:n_in], refs[n_in:2 * n_in]
    yf_ref, yb_ref, s_ref = refs[2 * n_in:]

    @pl.when(pl.program_id(1) == 0)
    def _():
        s_ref[...] = jnp.zeros_like(s_ref)

    same = _same_head_mask()
    ins = (fwd, bwd)
    outs = (yf_ref, yb_ref)
    chains = [(d, j) for j in range(INTER_BATCH) for d in range(2)]

    s = [s_ref[d * INTER_BATCH + j] for d, j in chains]
    for t in range(INTER_CHUNKS):
        rows = [_inter_rows(t, d == 1, CHUNK) for d, j in chains]

        def arg(idx):
            return [ins[d][idx][0, j, r, :] for (d, j), r in zip(chains, rows)]

        at, rt, bg, kg, xc, arb, rhs0, yk = (arg(i) for i in range(8))
        gamma = [ins[d][8][0, j, _inter_rows(t, d == 1, GL_ROWS), :][0:1] for d, j in chains]
        v = [ins[d][9][j, r, :] for (d, j), r in zip(chains, rows)]
        s_b = [x.astype(BF16) for x in s]
        rhs = [_expand_heads((_dot_nt(a, sb) + r0).astype(BF16)) for a, sb, r0 in zip(at, s_b, rhs0)]
        sa = [_dot(x, r).astype(BF16) for x, r in zip(xc, rhs)]
        sa_e = [_expand_heads(x) for x in sa]
        y = [_dot_nt(r, sb) + _dot(a, se) + y0 for r, sb, a, se, y0 in zip(rt, s_b, arb, sa_e, yk)]
        upd = [_dot_tn(a, b) + _dot_tn(c, e) for a, b, c, e in zip(sa, bg, v, kg)]
        for i, (d, j) in enumerate(chains):
            outs[d][j, rows[i], :] = y[i]
        s = [s[i] * gamma[i] + jnp.where(same, upd[i], 0.0) for i in range(len(chains))]
    for i, (d, j) in enumerate(chains):
        s_ref[d * INTER_BATCH + j] = s[i]


def _rw_finish_kernel(yf_ref, yb_ref, bonus_ref, g_ref, pv_ref, o_ref):
    hmean = _head_mean_matrix(1.0 / HEAD_DIM)
    ln_w, ln_b = pv_ref[7:8, :], pv_ref[8:9, :]

    def finish(i, c):
        rows = pl.ds(pl.multiple_of(i * SEQ_BLK, SEQ_BLK), SEQ_BLK)
        y = yf_ref[0, rows, :] + yb_ref[0, rows, :]
        yc = y - _dot_split(y, hmean)
        yn = yc * lax.rsqrt(_dot_split(yc * yc, hmean) + RW_LN_EPS) * ln_w + ln_b
        o_ref[rows, :] = (yn + bonus_ref[rows, :]) * g_ref[rows, :]
        return c

    lax.fori_loop(0, N_BLK, finish, 0)


def rwkv_mixer(slab, mu, w0, w_up, a0, a_up, g_up, k_k, k_a, r_k, ln_w, ln_b):
    n = slab.shape[0]
    nb = n // T_ALL
    assert nb % INTER_BATCH == 0 and CTX_CHUNKS % INTER_CHUNKS == 0 and N_CHUNK % INTER_CHUNKS == 0
    pv = jnp.concatenate([w0, a0, k_k[None], k_a[None], r_k.reshape(1, GROUP), ln_w[None], ln_b[None],
                          jnp.zeros((7, GROUP), F32)], axis=0)
    seq = lambda dt: jax.ShapeDtypeStruct((nb, T_ALL, GROUP), dt)
    seq2 = lambda dt: jax.ShapeDtypeStruct((2, nb, T_ALL, GROUP), dt)
    flat = jax.ShapeDtypeStruct((n, GROUP), F32)
    gl_shape = jax.ShapeDtypeStruct((2, nb, N_CHUNK * GL_ROWS, GROUP), F32)
    b1 = pl.BlockSpec((1, T_ALL, GROUP), lambda b: (b, 0, 0))
    bflat = pl.BlockSpec((T_ALL, GROUP), lambda b: (b, 0))
    p1 = pl.BlockSpec((1, SEQ_BLK, GROUP), lambda b, i: (b, i, 0))
    p2 = pl.BlockSpec((2, 1, SEQ_BLK, GROUP), lambda b, i: (0, b, i, 0))
    pflat = pl.BlockSpec((SEQ_BLK, GROUP), lambda b, i: (b * N_BLK + i, 0))
    pgl = pl.BlockSpec((2, 1, INTRA_CHUNKS * GL_ROWS, GROUP), lambda b, i: (0, b, i, 0))
    small = lambda shape: pl.BlockSpec(shape, lambda b, i: (0, 0))
    at, rt, bg, kg, gl, v, bonus, g, xc, arb, rhs0, yk = pl.pallas_call(
        _rw_intra_kernel,
        grid=(nb, N_BLK),
        in_specs=[pl.BlockSpec((T_ALL, RW_W), lambda b, i: (b, 0)), small((2, RW_W)), small((16, GROUP)),
                  small((RW_LR, RW_LR_OUT))],
        out_specs=[p2] * 4 + [pgl, p1, pflat, pflat] + [p2] * 4,
        out_shape=[seq2(BF16)] * 4 + [gl_shape, seq(BF16), flat, flat, seq2(BF16), seq2(BF16), seq2(F32), seq2(F32)],
        compiler_params=_cparams("parallel", "parallel"),
        name="rwkv7_intra",
    )(slab, mu, pv, rwkv_lowrank_weights(w_up, a_up, g_up))

    def per_dir(d, rows_per_chunk):
        return pl.BlockSpec((1, INTER_BATCH, INTER_CHUNKS * rows_per_chunk, GROUP),
                            lambda b, n_: (d, b, _inter_block(n_, d == 1), 0))

    def shared(d):
        return pl.BlockSpec((INTER_BATCH, INTER_CHUNKS * CHUNK, GROUP), lambda b, n_: (b, _inter_block(n_, d == 1), 0))

    specs = [s for d in range(2) for s in [per_dir(d, CHUNK)] * 8 + [per_dir(d, GL_ROWS), shared(d)]]
    per = (at, rt, bg, kg, xc, arb, rhs0, yk, gl, v)
    y_f, y_b = pl.pallas_call(
        _rw_inter_kernel,
        grid=(nb // INTER_BATCH, N_CHUNK // INTER_CHUNKS),
        in_specs=specs,
        out_specs=[shared(0), shared(1)],
        out_shape=[seq(F32), seq(F32)],
        scratch_shapes=[pltpu.VMEM((2 * INTER_BATCH, GROUP, GROUP), F32)],
        compiler_params=_cparams("parallel", "arbitrary"),
        name="rwkv7_inter",
    )(*per, *per)

    return pl.pallas_call(
        _rw_finish_kernel,
        grid=(nb,),
        in_specs=[b1, b1, bflat, bflat, _resident((16, GROUP))],
        out_specs=bflat,
        out_shape=flat,
        compiler_params=_cparams("parallel"),
        name="rwkv7_finish",
    )(y_f, y_b, bonus, g, pv)


DFT_SPLIT = 64
DFT_BLK = 256


def _dft_tables(n):
    big = 2 * n
    t = np.arange(n, dtype=np.int64)[:, None]
    k1 = np.arange(n // DFT_SPLIT, dtype=np.int64)[None, :]
    k2 = np.arange(DFT_SPLIT, dtype=np.int64)[None, :]
    alpha = 2.0 * np.pi * ((DFT_SPLIT * t * k1) % big) / big
    beta = 2.0 * np.pi * ((t * k2) % big) / big

    def pad(a):
        out = np.zeros((n, 128), np.float32)
        out[:, :a.shape[1]] = a
        return out

    return np.stack([pad(np.cos(alpha)), pad(np.sin(alpha)), pad(np.cos(beta)), pad(np.sin(beta))])


def _dft_gen_kernel(n, tab_ref, g_ref):
    k = lax.broadcasted_iota(jnp.int32, (128, n), 1)
    row = lax.broadcasted_iota(jnp.int32, (128, n), 0)
    e_a = jnp.where(k // DFT_SPLIT == row, 1.0, 0.0).astype(BF16)
    e_b = jnp.where(jnp.logical_and(k % DFT_SPLIT == row, row < DFT_SPLIT), 1.0, 0.0).astype(BF16)
    ca = _dot_split(tab_ref[0], e_a)
    sa = _dot_split(tab_ref[1], e_a)
    cb = _dot_split(tab_ref[2], e_b)
    sb = _dot_split(tab_ref[3], e_b)
    g_ref[:, 0:n] = (ca * cb - sa * sb).astype(BF16)
    g_ref[:, n:2 * n] = (-(sa * cb + ca * sb)).astype(BF16)


def dft_matrix(n):
    blk = min(DFT_BLK, n)
    return pl.pallas_call(
        functools.partial(_dft_gen_kernel, n),
        grid=(n // blk,),
        in_specs=[pl.BlockSpec((4, blk, 128), lambda i: (0, i, 0))],
        out_specs=pl.BlockSpec((blk, 2 * n), lambda i: (i, 0)),
        out_shape=jax.ShapeDtypeStruct((n, 2 * n), BF16),
        compiler_params=_cparams("parallel"),
        name=f"dft_matrix_{n}",
    )(jnp.asarray(_dft_tables(n)))


HY_COLS_F = 2 * HY_ORDER * GROUP
HY_OC = HY_ORDER * GROUP


def _hyena_filter_kernel(n, z_ref, w1_ref, b1_ref, w2_ref, b2_ref, w3_ref, freq_ref, dl_ref, hs_ref, hd_ref):
    blk = min(SEQ_BLK, n)
    freq = freq_ref[...]
    dl = dl_ref[...]

    def fill(i, norm):
        r0 = pl.multiple_of(i * blk, blk)
        z = z_ref[pl.ds(r0, blk), :]
        hid = jnp.sin(freq * (_dot_hi(z, w1_ref[...]) + b1_ref[...]))
        hid = jnp.sin(freq * (_dot_hi(hid, w2_ref[...]) + b2_ref[...]))
        t = z[:, 0:1]
        h = _dot_hi(hid, w3_ref[...]) * jnp.exp(-t * dl)
        lag = lax.broadcasted_iota(jnp.int32, (blk, 1), 0) + r0
        hf = h[:, 0:HY_OC]
        hb = jnp.where(lag == 0, 0.0, h[:, HY_OC:2 * HY_OC])
        hs_ref[pl.ds(r0, blk), :] = hf + hb
        hd_ref[pl.ds(r0, blk), :] = hf - hb
        return norm + jnp.sum(jnp.abs(hf) + jnp.abs(hb), axis=0, keepdims=True)

    norm = lax.fori_loop(0, n // blk, fill, jnp.zeros((1, HY_OC), F32))
    inv = 1.0 / norm

    def scale(i, c):
        rows = pl.ds(pl.multiple_of(i * blk, blk), blk)
        hs_ref[rows, :] = hs_ref[rows, :] * inv
        hd_ref[rows, :] = hd_ref[rows, :] * inv
        return c

    lax.fori_loop(0, n // blk, scale, 0)


def hyena_filter_taps(n, f_w1, f_b1, f_w2, f_b2, f_w3, f_freq):
    f32 = np.float32
    t = np.linspace(0.0, 1.0, n, dtype=f32)[:, None]
    ang = (f32(2.0 * math.pi) * np.arange(n, dtype=f32)[:, None] / f32(n)).astype(f32)
    bands = np.linspace(1e-4, HY_BANDS - 1, HY_BANDS, dtype=f32)[None]
    arg = (bands * ang).astype(f32)
    z = np.concatenate([t, np.cos(arg).astype(f32), -np.sin(arg).astype(f32)], axis=-1)
    emb = z.shape[1]
    z = jnp.asarray(np.pad(z, ((0, 0), (0, 128 - emb))))
    w1 = jnp.pad(f_w1, ((0, 128 - emb), (0, 0)))
    max_decay = math.log(HY_TARGET) / HY_SHORT_DECAY_PCT
    min_decay = math.log(HY_TARGET) / HY_LONG_DECAY_PCT
    deltas = np.abs(np.linspace(min_decay, max_decay, HY_OC, dtype=f32))
    dl = jnp.asarray(np.tile(deltas, 2).reshape(1, HY_COLS_F))
    hid = f_w2.shape[0]
    out = jax.ShapeDtypeStruct((n, HY_OC), F32)
    return pl.pallas_call(
        functools.partial(_hyena_filter_kernel, n),
        out_shape=[out, out],
        compiler_params=pltpu.CompilerParams(vmem_limit_bytes=VMEM_LIMIT),
        name=f"hyena_filter_{n}",
    )(z, w1, f_b1.reshape(1, hid), f_w2, f_b2.reshape(1, hid), f_w3, f_freq.reshape(1, hid), dl)


def _hyena_spectrum_kernel(n, g_ref, hs_ref, hd_ref, kr_ref, ki_ref, kn_ref):
    blk = min(2 * SEQ_BLK, n)
    big = 2.0 * n

    def split(ref):
        x = ref[...]
        hi = x.astype(BF16)
        return hi, (x - hi.astype(F32)).astype(BF16)

    s_hi, s_lo = split(hs_ref)
    d_hi, d_lo = split(hd_ref)

    def body(i, c):
        r0 = pl.multiple_of(i * blk, blk)
        rows = pl.ds(r0, blk)
        k = lax.broadcasted_iota(jnp.int32, (blk, 1), 0) + r0
        wgt = jnp.where(k == 0, 1.0 / big, 2.0 / big)
        gc = g_ref[rows, 0:n]
        gs = g_ref[rows, n:2 * n]
        kr_ref[rows, :] = (_dot(gc, s_hi) + _dot(gc, s_lo)) * wgt
        ki_ref[rows, :] = (_dot(gs, d_hi) + _dot(gs, d_lo)) * wgt
        return c

    lax.fori_loop(0, n // blk, body, 0)
    t = lax.broadcasted_iota(jnp.int32, (n, 1), 0)
    sign = jnp.where(t % 2 == 0, 1.0, -1.0)
    kn_ref[...] = jnp.broadcast_to(jnp.sum(sign * hs_ref[...], axis=0, keepdims=True) * (1.0 / big), (8, HY_OC))


def hyena_spectrum(n, g, hs, hd):
    out = jax.ShapeDtypeStruct((n, HY_OC), F32)
    return pl.pallas_call(
        functools.partial(_hyena_spectrum_kernel, n),
        out_shape=[out, out, jax.ShapeDtypeStruct((8, HY_OC), F32)],
        compiler_params=pltpu.CompilerParams(vmem_limit_bytes=VMEM_LIMIT),
        name=f"hyena_spectrum_{n}",
    )(g, hs, hd)


def _hyena_conv_kernel(slab_ref, w_ref, o_ref):
    def body(i, c):
        rows = pl.ds(pl.multiple_of(i * SEQ_BLK, SEQ_BLK), SEQ_BLK)
        for j in range(3):
            prev, cur, nxt = _prev_cur_next(slab_ref, i, j * GROUP, (j + 1) * GROUP)
            w = w_ref[:, j * GROUP:(j + 1) * GROUP]
            o_ref[rows, j * GROUP:(j + 1) * GROUP] = prev * w[0:1] + cur * w[1:2] + nxt * w[2:3]
        return c

    lax.fori_loop(0, N_BLK, body, 0)


def hyena_short_conv(slab, conv_w):
    n = slab.shape[0]
    return pl.pallas_call(
        _hyena_conv_kernel,
        grid=(n // T_ALL,),
        in_specs=[pl.BlockSpec((T_ALL, 3 * GROUP), lambda b: (b, 0)),
                  pl.BlockSpec((3, 3 * GROUP), lambda b: (0, 0))],
        out_specs=pl.BlockSpec((T_ALL, 3 * GROUP), lambda b: (b, 0)),
        out_shape=jax.ShapeDtypeStruct((n, 3 * GROUP), F32),
        compiler_params=_cparams("parallel"),
        name="hyena_short_conv",
    )(slab, conv_w)


HY_FBLK = 1024


def _alt_sign(n):
    t = lax.broadcasted_iota(jnp.int32, (n, 1), 0)
    return jnp.where(t % 2 == 0, 1.0, -1.0)


def _hyena_fwd_kernel(x_ref, gl_ref, gc_ref, krl_ref, kil_ref, knl_ref, krc_ref, kic_ref, knc_ref,
                      pl_ref, pc_ref, pn_ref):
    def transform(x, g_ref, kr_ref, ki_ref, kn_ref, p_ref, n, blk):
        xb = x.astype(BF16)

        def body(i, c):
            rows = pl.ds(pl.multiple_of(i * blk, blk), blk)
            zr = _dot(g_ref[rows, 0:n], xb)
            zi = _dot(g_ref[rows, n:2 * n], xb)
            kr = kr_ref[rows, :]
            ki = ki_ref[rows, :]
            p_ref[0, 0, rows, :] = (zr * kr - zi * ki).astype(BF16)
            p_ref[0, 1, rows, :] = (zr * ki + zi * kr).astype(BF16)
            return c

        lax.fori_loop(0, n // blk, body, 0)
        return jnp.sum(_alt_sign(n) * x, axis=0, keepdims=True) * kn_ref[0:1, :]

    nyq_c = transform(x_ref[0:CTX_LEN, :], gc_ref, krc_ref, kic_ref, knc_ref, pc_ref, CTX_LEN, CTX_LEN)
    nyq_l = transform(x_ref[CTX_LEN:T_ALL, :], gl_ref, krl_ref, kil_ref, knl_ref, pl_ref, SEQ, HY_FBLK)
    pn_ref[0] = jnp.concatenate([nyq_l, nyq_c, jnp.zeros((6, GROUP), F32)], axis=0)


def _resident(shape):
    return pl.BlockSpec(shape, lambda b: (0,) * len(shape))


def hyena_forward_transform(x, col, g_l, g_c, spec_l, spec_c, order):
    n = x.shape[0]
    nb = n // T_ALL
    kcol = lambda shape: pl.BlockSpec(shape, lambda b: (0, order))
    return pl.pallas_call(
        _hyena_fwd_kernel,
        grid=(nb,),
        in_specs=[
            pl.BlockSpec((T_ALL, GROUP), lambda b: (b, col)),
            _resident((SEQ, 2 * SEQ)), _resident((CTX_LEN, 2 * CTX_LEN)),
            kcol((SEQ, GROUP)), kcol((SEQ, GROUP)), kcol((8, GROUP)),
            kcol((CTX_LEN, GROUP)), kcol((CTX_LEN, GROUP)), kcol((8, GROUP)),
        ],
        out_specs=[
            pl.BlockSpec((1, 2, SEQ, GROUP), lambda b: (b, 0, 0, 0)),
            pl.BlockSpec((1, 2, CTX_LEN, GROUP), lambda b: (b, 0, 0, 0)),
            pl.BlockSpec((1, 8, GROUP), lambda b: (b, 0, 0)),
        ],
        out_shape=[
            jax.ShapeDtypeStruct((nb, 2, SEQ, GROUP), BF16),
            jax.ShapeDtypeStruct((nb, 2, CTX_LEN, GROUP), BF16),
            jax.ShapeDtypeStruct((nb, 8, GROUP), F32),
        ],
        compiler_params=_cparams("parallel"),
        name=f"hyena_fwd_{order}",
    )(x, g_l, g_c, *spec_l, *spec_c)


def _hyena_inv_kernel(pl_ref, pc_ref, pn_ref, gl_ref, gc_ref, u_ref, gate_ref, bias_ref, o_ref):
    bias = bias_ref[0]

    def inverse(p_ref, nyq, g_ref, n, blk, off):
        pr = p_ref[0, 0]
        pi = p_ref[0, 1]

        def body(i, c):
            r0 = pl.multiple_of(i * blk, blk)
            rows = pl.ds(r0, blk)
            orow = pl.ds(pl.multiple_of(off + r0, math.gcd(blk, CTX_LEN)), blk)
            t = lax.broadcasted_iota(jnp.int32, (blk, 1), 0)
            sign = jnp.where(t % 2 == 0, 1.0, -1.0)
            y = _dot(g_ref[rows, 0:n], pr) + _dot(g_ref[rows, n:2 * n], pi) + sign * nyq
            o_ref[orow, :] = gate_ref[orow, :] * (y + u_ref[orow, :] * bias)
            return c

        lax.fori_loop(0, n // blk, body, 0)

    inverse(pc_ref, pn_ref[0, 1:2, :], gc_ref, CTX_LEN, CTX_LEN, 0)
    inverse(pl_ref, pn_ref[0, 0:1, :], gl_ref, SEQ, HY_FBLK, CTX_LEN)


def hyena_inverse_transform(p_l, p_c, p_n, g_l, g_c, u, ucol, gate, gcol, bias):
    nb = p_l.shape[0]
    return pl.pallas_call(
        _hyena_inv_kernel,
        grid=(nb,),
        in_specs=[
            pl.BlockSpec((1, 2, SEQ, GROUP), lambda b: (b, 0, 0, 0)),
            pl.BlockSpec((1, 2, CTX_LEN, GROUP), lambda b: (b, 0, 0, 0)),
            pl.BlockSpec((1, 8, GROUP), lambda b: (b, 0, 0)),
            _resident((SEQ, 2 * SEQ)), _resident((CTX_LEN, 2 * CTX_LEN)),
            pl.BlockSpec((T_ALL, GROUP), lambda b: (b, ucol)),
            pl.BlockSpec((T_ALL, GROUP), lambda b: (b, gcol)),
            pl.BlockSpec((1, 1, GROUP), lambda b: (0, 0, 0)),
        ],
        out_specs=pl.BlockSpec((T_ALL, GROUP), lambda b: (b, 0)),
        out_shape=jax.ShapeDtypeStruct((nb * T_ALL, GROUP), F32),
        compiler_params=_cparams("parallel"),
        name="hyena_inv",
    )(p_l, p_c, p_n, g_l, g_c, u, gate, bias.reshape(1, 1, GROUP))


def hyena_mixer(slab, g_l, g_c, conv_w, f_w1, f_b1, f_w2, f_b2, f_w3, f_freq, bias):
    u = hyena_short_conv(slab, conv_w)
    spec_l = hyena_spectrum(SEQ, g_l, *hyena_filter_taps(SEQ, f_w1, f_b1, f_w2, f_b2, f_w3, f_freq))
    spec_c = hyena_spectrum(CTX_LEN, g_c, *hyena_filter_taps(CTX_LEN, f_w1, f_b1, f_w2, f_b2, f_w3, f_freq))
    p = hyena_forward_transform(u, 0, g_l, g_c, spec_l, spec_c, 0)
    z = hyena_inverse_transform(*p, g_l, g_c, u, 0, u, 1, bias[0])
    p = hyena_forward_transform(z, 0, g_l, g_c, spec_l, spec_c, 1)
    return hyena_inverse_transform(*p, g_l, g_c, z, 0, u, 2, bias[1])


def kernel(x, c, ctx, c_ctx, w_mod, b_mod, norm_w, ffn_w_gu, ffn_w_down, w_in, w_out,
           hy_conv, hy_f_w1, hy_f_b1, hy_f_w2, hy_f_b2, hy_f_w3, hy_f_freq, hy_bias,
           na_q_norm, na_k_norm, na_rpb, dn_conv, dn_a_log, dn_dt_bias, dn_norm,
           rw_mu, rw_w0, rw_w_up, rw_a0, rw_a_up, rw_g_up, rw_k_k, rw_k_a, rw_r_k, rw_ln_w, rw_ln_b):
    nb = x.shape[0]
    assert x.shape[1:] == (SEQ, D_MODEL) and ctx.shape[1:] == (CTX_LEN, D_MODEL) and nb + 1 <= 16
    s = jnp.concatenate([ctx, x], axis=1).reshape(nb * T_ALL, D_MODEL)
    cond = jnp.concatenate([c_ctx[None], c, jnp.zeros((15 - nb, D_MODEL), F32)], axis=0)
    mod = modulation_all(cond, w_mod, b_mod).reshape(DEPTH, 16, N_MOD, D_MODEL)
    g_l = dft_matrix(SEQ)
    g_c = dft_matrix(CTX_LEN)
    w_gu = ffn_w_gu.astype(BF16)
    w_down = ffn_w_down.astype(BF16)
    w_out_b = w_out.astype(BF16)
    dn_end = 6 * GROUP + 4 * GROUP + 4 * GROUP_HEADS
    w_in_p = jnp.concatenate(
        [w_in[:, :, :dn_end], jnp.zeros((DEPTH, D_MODEL, 6 * GROUP + DN_W - dn_end), F32), w_in[:, :, dn_end:]],
        axis=2).astype(BF16)
    for l in range(DEPTH):
        need_ctx = l < DEPTH - 1
        modc = mod[l, 0:1]
        modb = mod[l, 1:1 + nb]
        s = ffn_half_step(s, modc, modb, norm_w[l, 0], w_gu, w_down, l, 0, 0)
        hy_s, na_s, dn_s, rw_s = input_projection(s, modc, modb, norm_w[l, 1], w_in_p, l)
        groups = (
            hyena_mixer(hy_s, g_l, g_c, hy_conv[l], hy_f_w1[l], hy_f_b1[l], hy_f_w2[l], hy_f_b2[l], hy_f_w3[l],
                        hy_f_freq[l], hy_bias[l]),
            na_mixer(na_s, na_q_norm[l], na_k_norm[l], na_bias_table(na_rpb[l]), need_ctx),
            deltanet_mixer(dn_s, dn_conv[l], dn_a_log[l], dn_dt_bias[l], dn_norm[l]),
            rwkv_mixer(rw_s, rw_mu[l], rw_w0[l], rw_w_up[l], rw_a0[l], rw_a_up[l], rw_g_up[l], rw_k_k[l],
                       rw_k_a[l], rw_r_k[l], rw_ln_w[l], rw_ln_b[l]),
        )
        s = output_projection(s, modc, modb, groups, w_out_b, l, latent_only=not need_ctx)
        s = ffn_half_step(s, modc, modb, norm_w[l, 2], w_gu, w_down, l, 1, 2, latent_only=not need_ctx)
    return s.reshape(nb, SEQ, D_MODEL)
```

```python
import functools
import math

import numpy as np
import jax
import jax.numpy as jnp
from jax import lax
from jax.experimental import pallas as pl
from jax.experimental.pallas import tpu as pltpu

D_MODEL = 1024
SEQ = 2048
DEPTH = 2
CTX_LEN = 256
T_ALL = CTX_LEN + SEQ
GRID_W = 64
GROUP = 256
HEAD_DIM = 64
GROUP_HEADS = 4
D_FF = 2816
N_MOD = 9
NORM_EPS = 1e-6

HY_ORDER = 2
HY_BANDS = 16
HY_TARGET = 1e-2
HY_SHORT_DECAY_PCT = 0.3
HY_LONG_DECAY_PCT = 1.5

NA_WIN_ROWS = 8
NA_WIN_COLS = 16

CHUNK = 64
RW_DECAY_RANK = 32
RW_AAA_RANK = 32
RW_GATE_RANK = 64
RW_LN_EPS = 64e-5

DN_W = 4 * GROUP + 128
RW_W = 3 * GROUP + 128
P_PAD = 3 * GROUP + 3 * GROUP + DN_W + RW_W

TM = 768
TF = 512
VMEM_LIMIT = 56 * 1024 * 1024

F32 = jnp.float32
BF16 = jnp.bfloat16


def _cparams(*sem):
    return pltpu.CompilerParams(dimension_semantics=sem, vmem_limit_bytes=VMEM_LIMIT)


def _silu(x):
    return x * (1.0 / (1.0 + jnp.exp(-x)))


def _sigmoid(x):
    return 1.0 / (1.0 + jnp.exp(-x))


def _softplus(x):
    return jnp.maximum(x, 0.0) + jnp.log(1.0 + jnp.exp(-jnp.abs(x)))


def _dot(a, b):
    return jnp.dot(a, b, preferred_element_type=F32)


def _dot_nt(a, b):
    return lax.dot_general(a, b, (((1,), (1,)), ((), ())), preferred_element_type=F32)


def _dot_tn(a, b):
    return lax.dot_general(a, b, (((0,), (0,)), ((), ())), preferred_element_type=F32)


def _dot_hi(a, b):
    return jnp.dot(a, b, preferred_element_type=F32, precision=lax.Precision.HIGHEST)


def _mod_kernel(cond_ref, w_ref, b_ref, o_ref):
    a = _silu(cond_ref[...]).astype(BF16)
    o_ref[0] = _dot(a, w_ref[0].astype(BF16)) + b_ref[0]


def modulation_all(cond, w_mod, b_mod):
    r = cond.shape[0]
    tn = 1024
    return pl.pallas_call(
        _mod_kernel,
        grid=(DEPTH, N_MOD * D_MODEL // tn),
        in_specs=[
            pl.BlockSpec((r, D_MODEL), lambda l, j: (0, 0)),
            pl.BlockSpec((1, D_MODEL, tn), lambda l, j: (l, 0, j)),
            pl.BlockSpec((1, 1, tn), lambda l, j: (l, 0, j)),
        ],
        out_specs=pl.BlockSpec((1, r, tn), lambda l, j: (l, 0, j)),
        out_shape=jax.ShapeDtypeStruct((DEPTH, r, N_MOD * D_MODEL), F32),
        compiler_params=_cparams("parallel", "parallel"),
        name="modulation",
    )(cond, w_mod, b_mod.reshape(DEPTH, 1, N_MOD * D_MODEL))


TL = 512


def _row_mod(modc_ref, modb_ref, tile, idx, latent_only=False):
    if latent_only:
        return modb_ref[0, idx:idx + 1, :]
    row = lax.broadcasted_iota(jnp.int32, (TM, 1), 0) + (tile % (T_ALL // TM)) * TM
    return jnp.where(row < CTX_LEN, modc_ref[0, idx:idx + 1, :], modb_ref[0, idx:idx + 1, :])


def _adaln(x, nw, shift, scale):
    y = x * lax.rsqrt(jnp.mean(x * x, axis=-1, keepdims=True) + NORM_EPS)
    return y * nw * (1.0 + scale) + shift


def _ffn_body(sub, latent_only, i, x, modc_ref, modb_ref, nw_ref, wgu_ref, wd_ref):
    shift = _row_mod(modc_ref, modb_ref, i, 3 * sub, latent_only)
    scale = _row_mod(modc_ref, modb_ref, i, 3 * sub + 1, latent_only)
    h = _adaln(x, nw_ref[...], shift, scale).astype(BF16)
    acc = None
    for c0 in range(0, D_FF, TF):
        c1 = min(c0 + TF, D_FF)
        a = (_silu(_dot(h, wgu_ref[:, c0:c1])) * _dot(h, wgu_ref[:, D_FF + c0:D_FF + c1])).astype(BF16)
        part = _dot(a, wd_ref[c0:c1, :])
        acc = part if acc is None else acc + part
    gate = _row_mod(modc_ref, modb_ref, i, 3 * sub + 2, latent_only)
    return x + 0.5 * gate * acc


def _ffn_kernel(sub, latent_only, x_ref, modc_ref, modb_ref, nw_ref, wgu_ref, wd_ref, o_ref):
    o_ref[...] = _ffn_body(sub, latent_only, pl.program_id(0), x_ref[...], modc_ref, modb_ref, nw_ref, wgu_ref, wd_ref)


def ffn_half_step(x, modc, modb, nw, w_gu, w_down, layer, which, sub, latent_only=False):
    n = x.shape[0]
    tm, tiles_per_b = (TL, SEQ // TL) if latent_only else (TM, T_ALL // TM)
    once = pl.Buffered(1)
    return pl.pallas_call(
        functools.partial(_ffn_kernel, sub, latent_only),
        grid=(n // tm,),
        in_specs=[
            pl.BlockSpec((tm, D_MODEL), lambda i: (i, 0)),
            pl.BlockSpec((1, N_MOD, D_MODEL), lambda i: (0, 0, 0)),
            pl.BlockSpec((1, N_MOD, D_MODEL), lambda i: (i // tiles_per_b, 0, 0)),
            pl.BlockSpec((1, D_MODEL), lambda i: (0, 0)),
            pl.BlockSpec((None, None, D_MODEL, 2 * D_FF), lambda i: (layer, which, 0, 0), pipeline_mode=once),
            pl.BlockSpec((None, None, D_FF, D_MODEL), lambda i: (layer, which, 0, 0), pipeline_mode=once),
        ],
        out_specs=pl.BlockSpec((tm, D_MODEL), lambda i: (i, 0)),
        out_shape=jax.ShapeDtypeStruct((n, D_MODEL), F32),
        compiler_params=_cparams("parallel"),
        name=f"ffn{sub}",
    )(x, modc, modb, nw.reshape(1, D_MODEL), w_gu, w_down)


def _inproj_kernel(x_ref, modc_ref, modb_ref, nw_ref, w_ref, hy_ref, na_ref, dn_ref, rw_ref):
    i = pl.program_id(0)
    shift = _row_mod(modc_ref, modb_ref, i, 3)
    scale = _row_mod(modc_ref, modb_ref, i, 4)
    h = _adaln(x_ref[...], nw_ref[...], shift, scale).astype(BF16)
    o0 = 3 * GROUP
    o1 = 6 * GROUP
    o2 = o1 + DN_W
    y = _dot(h, w_ref[...])
    hy_ref[...] = y[:, 0:o0]
    na_ref[...] = y[:, o0:o1]
    dn_ref[...] = y[:, o1:o2]
    rw_ref[...] = y[:, o2:P_PAD]


def input_projection(x, modc, modb, nw, w_in_p, layer):
    n = x.shape[0]
    tiles_per_b = T_ALL // TM
    widths = (3 * GROUP, 3 * GROUP, DN_W, RW_W)
    return pl.pallas_call(
        _inproj_kernel,
        grid=(n // TM,),
        in_specs=[
            pl.BlockSpec((TM, D_MODEL), lambda i: (i, 0)),
            pl.BlockSpec((1, N_MOD, D_MODEL), lambda i: (0, 0, 0)),
            pl.BlockSpec((1, N_MOD, D_MODEL), lambda i: (i // tiles_per_b, 0, 0)),
            pl.BlockSpec((1, D_MODEL), lambda i: (0, 0)),
            pl.BlockSpec((None, D_MODEL, P_PAD), lambda i: (layer, 0, 0)),
        ],
        out_specs=[pl.BlockSpec((TM, w), lambda i: (i, 0)) for w in widths],
        out_shape=[jax.ShapeDtypeStruct((n, w), F32) for w in widths],
        compiler_params=_cparams("parallel"),
        name="inproj",
    )(x, modc, modb, nw.reshape(1, D_MODEL), w_in_p)


def _mix_ffn_kernel(latent_only, x_ref, modc_ref, modb_ref, nw_ref, hy_ref, na_ref,
                    of_ref, ob_ref, z_ref, dnw_ref, yf_ref, yb_ref, bonus_ref, g_ref, pv_ref,
                    wo_ref, wgu_ref, wd_ref, o_ref):
    i = pl.program_id(0)
    hmean = _head_mean_matrix(1.0 / HEAD_DIM)
    o = of_ref[...] + ob_ref[...]
    g_dn = o * lax.rsqrt(_dot_split(o * o, hmean) + NORM_EPS) * dnw_ref[...] * _silu(z_ref[...])
    y = yf_ref[...] + yb_ref[...]
    yc = y - _dot_split(y, hmean)
    yn = yc * lax.rsqrt(_dot_split(yc * yc, hmean) + RW_LN_EPS) * pv_ref[7:8, :] + pv_ref[8:9, :]
    g_rw = (yn + bonus_ref[...]) * g_ref[...]
    mix = _dot(hy_ref[...].astype(BF16), wo_ref[0:GROUP, :])
    mix += _dot(na_ref[...].astype(BF16), wo_ref[GROUP:2 * GROUP, :])
    mix += _dot(g_dn.astype(BF16), wo_ref[2 * GROUP:3 * GROUP, :])
    mix += _dot(g_rw.astype(BF16), wo_ref[3 * GROUP:4 * GROUP, :])
    x = x_ref[...] + _row_mod(modc_ref, modb_ref, i, 5, latent_only) * mix
    o_ref[...] = _ffn_body(2, latent_only, i, x, modc_ref, modb_ref, nw_ref, wgu_ref, wd_ref)


def mixer_output_ffn(x, modc, modb, nw, hy, na, dn, rw, w_out, w_gu, w_down, layer, latent_only):
    n = x.shape[0]
    nb = n // T_ALL
    o_f, o_b, dn_slab, dn_norm = dn
    y_f, y_b, bonus, gate, pv = rw
    flat = lambda t: t.reshape(n, GROUP)
    if latent_only:
        tm, tiles_per_b = TL, SEQ // TL
        rows = lambda i: pl.multiple_of((i // tiles_per_b) * T_ALL + CTX_LEN + (i % tiles_per_b) * TL,
                                        math.gcd(CTX_LEN, TL))
        win = lambda width, col=0: pl.BlockSpec((pl.Element(tm), pl.Element(width)), lambda i: (rows(i), col * width))
    else:
        tm, tiles_per_b = TM, T_ALL // TM
        win = lambda width, col=0: pl.BlockSpec((tm, width), lambda i: (i, col))
    n_out = nb * tiles_per_b * tm
    once = pl.Buffered(1)
    small = lambda shape: pl.BlockSpec(shape, lambda i: (0,) * len(shape))
    return pl.pallas_call(
        functools.partial(_mix_ffn_kernel, latent_only),
        grid=(n_out // tm,),
        in_specs=[
            win(D_MODEL), small((1, N_MOD, D_MODEL)),
            pl.BlockSpec((1, N_MOD, D_MODEL), lambda i: (i // tiles_per_b, 0, 0)), small((1, D_MODEL)),
            win(GROUP), win(GROUP),
            win(GROUP), win(GROUP), win(GROUP, 3), small((1, GROUP)),
            win(GROUP), win(GROUP), win(GROUP), win(GROUP), small((16, GROUP)),
            pl.BlockSpec((None, D_MODEL, D_MODEL), lambda i: (layer, 0, 0), pipeline_mode=once),
            pl.BlockSpec((None, None, D_MODEL, 2 * D_FF), lambda i: (layer, 1, 0, 0), pipeline_mode=once),
            pl.BlockSpec((None, None, D_FF, D_MODEL), lambda i: (layer, 1, 0, 0), pipeline_mode=once),
        ],
        out_specs=pl.BlockSpec((tm, D_MODEL), lambda i: (i, 0)),
        out_shape=jax.ShapeDtypeStruct((n_out, D_MODEL), F32),
        compiler_params=_cparams("parallel"),
        name="mix_ffn",
    )(x, modc, modb, nw.reshape(1, D_MODEL), hy, na, flat(o_f), flat(o_b), dn_slab,
      jnp.tile(dn_norm, GROUP_HEADS).reshape(1, GROUP), flat(y_f), flat(y_b), bonus, gate, pv,
      w_out, w_gu, w_down)


def _head_mean_matrix(scale):
    r = lax.broadcasted_iota(jnp.int32, (GROUP, GROUP), 0) // HEAD_DIM
    c = lax.broadcasted_iota(jnp.int32, (GROUP, GROUP), 1) // HEAD_DIM
    return jnp.where(r == c, scale, 0.0).astype(BF16)


def _dot_split(a, m_bf16):
    hi = a.astype(BF16)
    lo = (a - hi.astype(F32)).astype(BF16)
    return _dot(hi, m_bf16) + _dot(lo, m_bf16)


def _lane_head(width=GROUP):
    return lax.broadcasted_iota(jnp.int32, (1, width), 1) // HEAD_DIM


NA_ROWS = SEQ // GRID_W
NA_LOCAL = NA_WIN_ROWS * GRID_W
NA_NEG = -1e30
NA_BLK = 256
NA_PAIR = 4


def na_bias_table(rpb):
    n_dr = 2 * NA_WIN_ROWS
    rows = jnp.pad(rpb, ((0, 0), (0, 1), (0, 128 - rpb.shape[2]))).reshape(GROUP_HEADS * n_dr, 128)
    toep = pl.pallas_call(
        _na_bias_kernel,
        out_shape=jax.ShapeDtypeStruct((GROUP_HEADS * n_dr, GRID_W * GRID_W), F32),
        name="na_bias",
    )(rows).reshape(GROUP_HEADS, n_dr, GRID_W, GRID_W)
    tab = jnp.stack([toep[:, NA_WIN_ROWS - 1 - p:2 * NA_WIN_ROWS - 1 - p] for p in range(NA_WIN_ROWS)], axis=0)
    tab = jnp.transpose(tab, (0, 1, 3, 2, 4))
    return tab.reshape(NA_WIN_ROWS, GROUP_HEADS, GRID_W, NA_LOCAL)


def _na_bias_kernel(rpb_ref, o_ref):
    n = GRID_W * GRID_W
    d = lax.broadcasted_iota(jnp.int32, (128, n), 0)
    cj = lax.broadcasted_iota(jnp.int32, (128, n), 1)
    onehot = jnp.where((cj % GRID_W) - (cj // GRID_W) + NA_WIN_COLS - 1 == d, 1.0, 0.0).astype(BF16)
    cj1 = lax.broadcasted_iota(jnp.int32, (1, n), 1)
    c = cj1 // GRID_W
    j = cj1 % GRID_W
    start = jnp.clip(c - NA_WIN_COLS // 2, 0, GRID_W - NA_WIN_COLS)
    in_win = jnp.logical_and(j >= start, j < start + NA_WIN_COLS)
    o_ref[...] = jnp.where(in_win, _dot_exact_rhs(rpb_ref[...], onehot), NA_NEG)


def _na_kernel(need_ctx, slab_ref, qw_ref, kw_ref, bias_ref, o_ref, q_s, k_s, v_s):
    hm = _head_mean_matrix(1.0 / HEAD_DIM)
    qw = qw_ref[...] * (HEAD_DIM ** -0.5)
    kw = kw_ref[...]

    def prep(i, c):
        r0 = pl.multiple_of(i * NA_BLK, NA_BLK)
        q = slab_ref[pl.ds(r0, NA_BLK), 0:GROUP]
        k = slab_ref[pl.ds(r0, NA_BLK), GROUP:2 * GROUP]
        q_s[pl.ds(r0, NA_BLK), :] = (q * lax.rsqrt(_dot_split(q * q, hm) + NORM_EPS) * qw).astype(BF16)
        k_s[pl.ds(r0, NA_BLK), :] = (k * lax.rsqrt(_dot_split(k * k, hm) + NORM_EPS) * kw).astype(BF16)
        v_s[pl.ds(r0, NA_BLK), :] = slab_ref[pl.ds(r0, NA_BLK), 2 * GROUP:3 * GROUP].astype(BF16)
        return c

    lax.fori_loop(0, T_ALL // NA_BLK, prep, 0)

    lane_h = _lane_head()
    kc = k_s[0:CTX_LEN, :]
    vc = v_s[0:CTX_LEN, :]

    if need_ctx:
        qc = q_s[0:CTX_LEN, :]
        out = jnp.zeros((CTX_LEN, GROUP), F32)
        for h in range(GROUP_HEADS):
            mask = lane_h == h
            s = _dot_nt(jnp.where(mask, qc, jnp.zeros_like(qc)), kc)
            e = jnp.exp(s - jnp.max(s, axis=-1, keepdims=True))
            p = e * (1.0 / jnp.sum(e, axis=-1, keepdims=True))
            out = jnp.where(mask, _dot(p.astype(BF16), vc), out)
        o_ref[0:CTX_LEN, :] = out
    else:
        o_ref[0:CTX_LEN, :] = jnp.zeros((CTX_LEN, GROUP), F32)

    def pair_body(i, c):
        rows = [i * NA_PAIR + t for t in range(NA_PAIR)]
        start = [jnp.clip(r - NA_WIN_ROWS // 2, 0, NA_ROWS - NA_WIN_ROWS) for r in rows]
        q0 = [pl.multiple_of(CTX_LEN + r * GRID_W, GRID_W) for r in rows]
        k0 = [pl.multiple_of(CTX_LEN + s * GRID_W, GRID_W) for s in start]
        q = [_expand_heads(q_s[pl.ds(a, GRID_W), :]) for a in q0]
        kb = [k_s[pl.ds(a, NA_LOCAL), :] for a in k0]
        vb = [v_s[pl.ds(a, NA_LOCAL), :] for a in k0]
        bias = [bias_ref[r - s].reshape(GROUP_HEADS * GRID_W, NA_LOCAL) for r, s in zip(rows, start)]
        s_loc = [_dot_nt(q[t], kb[t]) + bias[t] for t in range(NA_PAIR)]
        s_ctx = [_dot_nt(q[t], kc) for t in range(NA_PAIR)]
        m = [jnp.maximum(jnp.max(a, axis=-1, keepdims=True), jnp.max(b, axis=-1, keepdims=True))
             for a, b in zip(s_loc, s_ctx)]
        e_loc = [jnp.exp(a - mm) for a, mm in zip(s_loc, m)]
        e_ctx = [jnp.exp(b - mm) for b, mm in zip(s_ctx, m)]
        inv = [1.0 / (jnp.sum(a, axis=-1, keepdims=True) + jnp.sum(b, axis=-1, keepdims=True))
               for a, b in zip(e_loc, e_ctx)]
        o = [_dot((e_loc[t] * inv[t]).astype(BF16), vb[t]) + _dot((e_ctx[t] * inv[t]).astype(BF16), vc)
             for t in range(NA_PAIR)]
        for t in range(NA_PAIR):
            out = o[t][0:GRID_W]
            for h in range(1, GROUP_HEADS):
                out = jnp.where(lane_h == h, o[t][h * GRID_W:(h + 1) * GRID_W], out)
            o_ref[pl.ds(q0[t], GRID_W), :] = out
        return c

    lax.fori_loop(0, NA_ROWS // NA_PAIR, pair_body, 0)


def na_mixer(slab, q_norm, k_norm, bias_tab, need_ctx):
    n = slab.shape[0]
    tile4 = lambda w: jnp.tile(w, GROUP_HEADS).reshape(1, GROUP)
    return pl.pallas_call(
        functools.partial(_na_kernel, need_ctx),
        grid=(n // T_ALL,),
        in_specs=[
            pl.BlockSpec((T_ALL, 3 * GROUP), lambda b: (b, 0)),
            pl.BlockSpec((1, GROUP), lambda b: (0, 0)),
            pl.BlockSpec((1, GROUP), lambda b: (0, 0)),
            pl.BlockSpec((NA_WIN_ROWS, GROUP_HEADS, GRID_W, NA_LOCAL), lambda b: (0, 0, 0, 0)),
        ],
        out_specs=pl.BlockSpec((T_ALL, GROUP), lambda b: (b, 0)),
        out_shape=jax.ShapeDtypeStruct((n, GROUP), F32),
        scratch_shapes=[pltpu.VMEM((T_ALL, GROUP), BF16)] * 3,
        compiler_params=_cparams("parallel"),
        name="na_mixer",
    )(slab, tile4(q_norm), tile4(k_norm), bias_tab)


SEQ_BLK = 256
N_BLK = T_ALL // SEQ_BLK
N_CHUNK = T_ALL // CHUNK
CTX_CHUNKS = CTX_LEN // CHUNK


def _prev_cur_next(ref, i, c0, c1):
    r0 = pl.multiple_of(i * SEQ_BLK, SEQ_BLK)
    cur = ref[pl.ds(r0, SEQ_BLK), c0:c1]
    up0 = pl.multiple_of(jnp.maximum(r0 - 8, 0), 8)
    dn0 = pl.multiple_of(jnp.minimum(r0 + SEQ_BLK, T_ALL - 8), 8)
    up = ref[pl.ds(up0, 8), c0:c1][7:8, :]
    dn = ref[pl.ds(dn0, 8), c0:c1][0:1, :]
    up = jnp.where(i >= 2, up, 0.0)
    dn = jnp.where(jnp.logical_and(i >= 1, i <= N_BLK - 2), dn, 0.0)
    row = lax.broadcasted_iota(jnp.int32, (SEQ_BLK, 1), 0)
    prev = jnp.where(row == 0, up, pltpu.roll(cur, 1, 0))
    nxt = jnp.where(row == SEQ_BLK - 1, dn, pltpu.roll(cur, SEQ_BLK - 1, 0))
    return prev, cur, nxt


def _chunk_cumsum(x, reverse):
    pos = lax.broadcasted_iota(jnp.int32, (SEQ_BLK, 1), 0) % CHUNK
    s = 1
    while s < CHUNK:
        if reverse:
            x = x + jnp.where(pos < CHUNK - s, pltpu.roll(x, SEQ_BLK - s, 0), 0.0)
        else:
            x = x + jnp.where(pos >= s, pltpu.roll(x, s, 0), 0.0)
        s *= 2
    return x


def _split3(a):
    hi = a.astype(BF16)
    r1 = a - hi.astype(F32)
    mid = r1.astype(BF16)
    lo = (r1 - mid.astype(F32)).astype(BF16)
    return hi, mid, lo


def _dot_exact_rhs(a, m_bf16):
    hi, mid, lo = _split3(a)
    return _dot(hi, m_bf16) + _dot(mid, m_bf16) + _dot(lo, m_bf16)


def _expand_heads(x):
    lane_h = _lane_head()
    return jnp.concatenate([jnp.where(lane_h == h, x, 0.0) for h in range(GROUP_HEADS)], axis=0)


def _chunk_of_step(n, reverse):
    if not reverse:
        return n
    return jnp.where(n < CTX_CHUNKS, CTX_CHUNKS - 1 - n, N_CHUNK + CTX_CHUNKS - 1 - n)


INV_BASE = 16


def _cat_dot(a, b):
    return _dot(a.astype(BF16), _expand_heads(b.astype(BF16)))


def _cat_index():
    i = lax.broadcasted_iota(jnp.int32, (CHUNK, GROUP_HEADS * CHUNK), 0)
    j = lax.broadcasted_iota(jnp.int32, (CHUNK, GROUP_HEADS * CHUNK), 1) % CHUNK
    return i, j


def _cat_masks(reverse):
    i, j = _cat_index()
    if reverse:
        return i <= j, i < j
    return i >= j, i > j


def _inverse_unit_triangular(mats):
    i, j = _cat_index()
    inner = (i // INV_BASE) == (j // INV_BASE)
    eye = jnp.where(i == j, 1.0, 0.0)
    nd = [jnp.where(inner, n, 0.0) for n in mats]
    x = [eye - n for n in nd]
    p = [_cat_dot(n, n) for n in nd]
    k = 2
    while k < INV_BASE:
        x = [xi + _cat_dot(pi, xi) for xi, pi in zip(x, p)]
        k *= 2
        if k < INV_BASE:
            p = [_cat_dot(pi, pi) for pi in p]
    width = INV_BASE
    while width < CHUNK:
        outer = (i // (2 * width)) == (j // (2 * width))
        sel = jnp.logical_and(outer, jnp.logical_not(inner))
        t = [_cat_dot(jnp.where(sel, n, 0.0), xi) for n, xi in zip(mats, x)]
        x = [xi - _cat_dot(xi, ti) for xi, ti in zip(x, t)]
        inner = outer
        width *= 2
    return x


def _head_rows(gc, lane_onehot):
    hi, mid, lo = _split3(gc)
    t = _dot_nt(lane_onehot, hi) + _dot_nt(lane_onehot, mid) + _dot_nt(lane_onehot, lo)
    return jnp.concatenate([t[h:h + 1, :] for h in range(GROUP_HEADS)], axis=1)


INTRA_CHUNKS = SEQ_BLK // CHUNK
INTER_BATCH = 4
INTER_CHUNKS = 2
GL_ROWS = 8


def _dn_prep_block(slab_ref, conv_ref, alog_ref, dt_ref, i):
    hsum = _head_mean_matrix(1.0)
    col = lax.broadcasted_iota(jnp.int32, (128, GROUP), 0)
    lane = lax.broadcasted_iota(jnp.int32, (128, GROUP), 1) // HEAD_DIM
    neg_a = -jnp.exp(alog_ref[...])
    dtb = dt_ref[...]
    rows = pl.ds(pl.multiple_of(i * SEQ_BLK, SEQ_BLK), SEQ_BLK)
    qkv = []
    for j in range(3):
        prev, cur, nxt = _prev_cur_next(slab_ref, i, j * GROUP, (j + 1) * GROUP)
        w = conv_ref[:, j * GROUP:(j + 1) * GROUP]
        u = _silu(prev * w[0:1] + cur * w[1:2] + nxt * w[2:3])
        if j == 0:
            u = u * lax.rsqrt(_dot_split(u * u, hsum) + 1e-6) * (HEAD_DIM ** -0.5)
        elif j == 1:
            u = u * lax.rsqrt(_dot_split(u * u, hsum) + 1e-6)
        qkv.append(u)
    ba = slab_ref[rows, 4 * GROUP:4 * GROUP + 128]
    gc, beta = [], []
    for d in range(2):
        e_b = jnp.where(col == 8 * d + lane, 1.0, 0.0).astype(BF16)
        e_a = jnp.where(col == 8 * d + 4 + lane, 1.0, 0.0).astype(BF16)
        beta.append(_sigmoid(_dot_exact_rhs(ba, e_b)))
        g = neg_a[d:d + 1] * _softplus(_dot_exact_rhs(ba, e_a) + dtb[d:d + 1])
        gc.append(_chunk_cumsum(g, reverse=(d == 1)))
    return qkv[0], qkv[1], qkv[2], gc, beta


def _dn_intra_kernel(slab_ref, conv_ref, alog_ref, dt_ref, u_ref, w_ref, attn_ref, qd_ref, kd_ref, gl_ref):
    q_blk, k_blk, v_blk, gc_blk, beta_blk = _dn_prep_block(slab_ref, conv_ref, alog_ref, dt_ref, pl.program_id(1))
    onehot = jnp.where(
        lax.broadcasted_iota(jnp.int32, (8, GROUP), 1) == HEAD_DIM * lax.broadcasted_iota(jnp.int32, (8, GROUP), 0),
        1.0, 0.0).astype(BF16)
    masks = (_cat_masks(False), _cat_masks(True))
    chains = [(j, d) for j in range(INTRA_CHUNKS) for d in range(2)]
    rows = [slice(j * CHUNK, (j + 1) * CHUNK) for j, d in chains]
    gc = [gc_blk[d][r] for (j, d), r in zip(chains, rows)]
    beta = [beta_blk[d][r] for (j, d), r in zip(chains, rows)]
    q = [q_blk[r] for r in rows]
    k = [k_blk[r] for r in rows]
    v = [v_blk[r] for r in rows]
    eg = [jnp.exp(g) for g in gc]
    g_last = [g[0:1, :] if d == 1 else g[CHUNK - 1:CHUNK, :] for (j, d), g in zip(chains, gc)]
    kb = [a * b for a, b in zip(k, beta)]
    k_e = [_expand_heads(a.astype(BF16)) for a in k]
    dec = []
    for (j, d), g in zip(chains, gc):
        incl = masks[d][0]
        dec.append(jnp.where(incl, jnp.exp(jnp.where(incl, g - _head_rows(g, onehot), 0.0)), 0.0))
    m = [jnp.where(masks[d][1], _dot_nt(a.astype(BF16), ke) * dc, 0.0)
         for (j, d), a, ke, dc in zip(chains, kb, k_e, dec)]
    attn = [_dot_nt(a.astype(BF16), ke) * dc for a, ke, dc in zip(q, k_e, dec)]
    rhs = [jnp.concatenate([_expand_heads((a * b).astype(BF16)), _expand_heads((c * e).astype(BF16))], axis=1)
           for a, b, c, e in zip(v, beta, kb, eg)]
    x = _inverse_unit_triangular(m)
    sol = [_dot(xi.astype(BF16), r) for xi, r in zip(x, rhs)]
    for i, ((j, d), r) in enumerate(zip(chains, rows)):
        u_ref[d, 0, r, :] = sol[i][:, 0:GROUP]
        w_ref[d, 0, r, :] = sol[i][:, GROUP:2 * GROUP].astype(BF16)
        attn_ref[d, 0, r, :] = attn[i].astype(BF16)
        qd_ref[d, 0, r, :] = (q[i] * eg[i]).astype(BF16)
        kd_ref[d, 0, r, :] = (k[i] * jnp.exp(g_last[i] - gc[i])).astype(BF16)
        gl_ref[d, 0, pl.ds(j * GL_ROWS, GL_ROWS), :] = jnp.broadcast_to(jnp.exp(g_last[i]), (GL_ROWS, GROUP))


def _same_head_mask():
    r = lax.broadcasted_iota(jnp.int32, (GROUP, GROUP), 0) // HEAD_DIM
    c = lax.broadcasted_iota(jnp.int32, (GROUP, GROUP), 1) // HEAD_DIM
    return r == c


def _dn_inter_kernel(uf, wf, af, qf, kf, gf, ub, wb, ab, qb, kb, gb, of_ref, ob_ref, s_ref):
    @pl.when(pl.program_id(1) == 0)
    def _():
        s_ref[...] = jnp.zeros_like(s_ref)

    same = _same_head_mask()
    ins = ((uf, wf, af, qf, kf, gf, of_ref), (ub, wb, ab, qb, kb, gb, ob_ref))
    chains = [(d, j) for j in range(INTER_BATCH) for d in range(2)]
    s = [s_ref[d * INTER_BATCH + j] for d, j in chains]
    for t in range(INTER_CHUNKS):
        rows = [_inter_rows(t, d == 1, CHUNK) for d, j in chains]
        grow = [_inter_rows(t, d == 1, GL_ROWS) for d, j in chains]
        s_b = [x.astype(BF16) for x in s]
        v_new = [(ins[d][0][0, j, r, :] - _dot(ins[d][1][0, j, r, :], sb)).astype(BF16)
                 for (d, j), r, sb in zip(chains, rows, s_b)]
        v_ne = [_expand_heads(x) for x in v_new]
        o = [_dot(ins[d][3][0, j, r, :], sb) + _dot(ins[d][2][0, j, r, :], ve)
             for (d, j), r, sb, ve in zip(chains, rows, s_b, v_ne)]
        upd = [_dot_tn(ins[d][4][0, j, r, :], x) for (d, j), r, x in zip(chains, rows, v_new)]
        for i, (d, j) in enumerate(chains):
            ins[d][6][j, rows[i], :] = o[i]
        s = [s[i] * ins[d][5][0, j, grow[i], :][0:1] + jnp.where(same, upd[i], 0.0) for i, (d, j) in enumerate(chains)]
    for i, (d, j) in enumerate(chains):
        s_ref[d * INTER_BATCH + j] = s[i]


def _inter_block(n, reverse):
    if not reverse:
        return n
    return _chunk_of_step(INTER_CHUNKS * n + INTER_CHUNKS - 1, True) // INTER_CHUNKS


def _inter_rows(t, reverse, rows_per_chunk):
    j = INTER_CHUNKS - 1 - t if reverse else t
    return slice(j * rows_per_chunk, (j + 1) * rows_per_chunk)


def deltanet_mixer(slab, conv_w, a_log, dt_bias):
    nb = slab.shape[0] // T_ALL
    assert nb % INTER_BATCH == 0 and CTX_CHUNKS % INTER_CHUNKS == 0 and N_CHUNK % INTER_CHUNKS == 0
    lanes = lambda t: jnp.repeat(t, HEAD_DIM, axis=-1)
    seq = lambda dt: jax.ShapeDtypeStruct((nb, T_ALL, GROUP), dt)
    seq2 = lambda dt: jax.ShapeDtypeStruct((2, nb, T_ALL, GROUP), dt)
    p2 = pl.BlockSpec((2, 1, SEQ_BLK, GROUP), lambda b, i: (0, b, i, 0))
    small = lambda shape: pl.BlockSpec(shape, lambda b, i: (0, 0))
    gl_shape = jax.ShapeDtypeStruct((2, nb, N_CHUNK * GL_ROWS, GROUP), F32)
    u, w, attn, qd, kd, gl = pl.pallas_call(
        _dn_intra_kernel,
        grid=(nb, N_BLK),
        in_specs=[pl.BlockSpec((T_ALL, DN_W), lambda b, i: (b, 0)), small((3, 3 * GROUP)),
                  small((2, GROUP)), small((2, GROUP))],
        out_specs=[p2, p2, p2, p2, p2,
                   pl.BlockSpec((2, 1, INTRA_CHUNKS * GL_ROWS, GROUP), lambda b, i: (0, b, i, 0))],
        out_shape=[seq2(F32), seq2(BF16), seq2(BF16), seq2(BF16), seq2(BF16), gl_shape],
        compiler_params=_cparams("parallel", "parallel"),
        name="deltanet_intra",
    )(slab, conv_w, lanes(a_log), lanes(dt_bias))

    def per_dir(d, rows_per_chunk):
        return pl.BlockSpec((1, INTER_BATCH, INTER_CHUNKS * rows_per_chunk, GROUP),
                            lambda b, n: (d, b, _inter_block(n, d == 1), 0))

    def out_dir(d):
        return pl.BlockSpec((INTER_BATCH, INTER_CHUNKS * CHUNK, GROUP), lambda b, n: (b, _inter_block(n, d == 1), 0))

    specs = [per_dir(d, r) for d in range(2) for r in (CHUNK,) * 5 + (GL_ROWS,)]
    o_f, o_b = pl.pallas_call(
        _dn_inter_kernel,
        grid=(nb // INTER_BATCH, N_CHUNK // INTER_CHUNKS),
        in_specs=specs,
        out_specs=[out_dir(0), out_dir(1)],
        out_shape=[seq(F32), seq(F32)],
        scratch_shapes=[pltpu.VMEM((2 * INTER_BATCH, GROUP, GROUP), F32)],
        compiler_params=_cparams("parallel", "arbitrary"),
        name="deltanet_inter",
    )(u, w, attn, qd, kd, gl, u, w, attn, qd, kd, gl)

    return o_f, o_b


RW_LR = RW_DECAY_RANK + RW_AAA_RANK + RW_GATE_RANK
RW_LR_OUT = 5 * GROUP


def _dot3(a, b_hi, b_lo):
    a_hi = a.astype(BF16)
    a_lo = (a - a_hi.astype(F32)).astype(BF16)
    return _dot(a_hi, b_hi) + (_dot(a_lo, b_hi) + _dot(a_hi, b_lo))


def rwkv_lowrank_weights(w_up, a_up, g_up):
    w = jnp.zeros((RW_LR, RW_LR_OUT), F32)
    o1 = RW_DECAY_RANK
    o2 = o1 + RW_AAA_RANK
    for d in range(2):
        w = w.at[0:o1, d * GROUP:(d + 1) * GROUP].set(w_up[d])
        w = w.at[o1:o2, (2 + d) * GROUP:(3 + d) * GROUP].set(a_up[d])
    return w.at[o2:RW_LR, 4 * GROUP:5 * GROUP].set(g_up)


def _rw_intra_kernel(slab_ref, mu_ref, pv_ref, wlr_ref,
                     at_ref, rt_ref, bg_ref, kg_ref, gl_ref, v_ref, bonus_ref, g_ref,
                     xc_ref, arb_ref, rhs0_ref, yk_ref):
    i = pl.program_id(1)
    hsum = _head_mean_matrix(1.0)
    wlr = wlr_ref[...]
    wlr_hi = wlr.astype(BF16)
    wlr_lo = (wlr - wlr_hi.astype(F32)).astype(BF16)
    pv = pv_ref[...]
    w0 = (pv[0:1], pv[1:2])
    a0 = (pv[2:3], pv[3:4])
    k_k, k_a, r_k = pv[4:5], pv[5:6], pv[6:7]
    lr_lane = lax.broadcasted_iota(jnp.int32, (1, RW_LR), 1)

    def shifted(c0, c1):
        prev, cur, nxt = _prev_cur_next(slab_ref, i, c0, c1)
        return cur + mu_ref[0:1, c0:c1] * (prev - cur) + mu_ref[1:2, c0:c1] * (nxt - cur)

    r = shifted(0, GROUP)
    k = shifted(GROUP, 2 * GROUP)
    v = shifted(2 * GROUP, 3 * GROUP)
    lr = shifted(3 * GROUP, 3 * GROUP + RW_LR)
    t = jnp.where(lr_lane < RW_DECAY_RANK, jnp.tanh(lr),
                  jnp.where(lr_lane < RW_DECAY_RANK + RW_AAA_RANK, lr, _sigmoid(lr)))
    proj = _dot3(t, wlr_hi, wlr_lo)
    kq = k * k_k
    kk = kq * lax.rsqrt(_dot_split(kq * kq, hsum) + 1e-6)
    v_blk = v.astype(BF16)
    v_ref[0] = v_blk
    g_ref[...] = proj[:, 4 * GROUP:5 * GROUP]
    ksum = jnp.zeros_like(k)
    at_blk, rt_blk, bh_blk, kh_blk = [], [], [], []
    for d in range(2):
        w_log = -_softplus(-(w0[d] + proj[:, d * GROUP:(d + 1) * GROUP])) - 0.5
        lw = -jnp.exp(w_log)
        a_gate = _sigmoid(a0[d] + proj[:, (2 + d) * GROUP:(3 + d) * GROUP])
        k_d = k * (1.0 + (a_gate - 1.0) * k_a)
        ksum = ksum + k_d
        cum = _chunk_cumsum(lw, reverse=(d == 1))
        ends = [cum[j * CHUNK:j * CHUNK + 1, :] if d == 1 else cum[(j + 1) * CHUNK - 1:(j + 1) * CHUNK, :]
                for j in range(INTRA_CHUNKS)]
        to_end = jnp.exp(jnp.concatenate([jnp.broadcast_to(e, (CHUNK, GROUP)) for e in ends], axis=0) - cum)
        inv = jnp.exp(-cum)
        b = kk * a_gate
        at_blk.append((-kk * jnp.exp(cum - lw)).astype(BF16))
        rt_blk.append((r * jnp.exp(cum)).astype(BF16))
        bh_blk.append((b * inv).astype(BF16))
        kh_blk.append((k_d * inv).astype(BF16))
        at_ref[d, 0] = at_blk[d]
        rt_ref[d, 0] = rt_blk[d]
        bg_ref[d, 0] = (b * to_end).astype(BF16)
        kg_ref[d, 0] = (k_d * to_end).astype(BF16)
        gl_ref[d, 0] = jnp.concatenate([jnp.broadcast_to(jnp.exp(e), (GL_ROWS, GROUP)) for e in ends], axis=0)
    bonus_ref[...] = _dot_split(r * ksum * r_k, hsum) * v

    masks = (_cat_masks(False), _cat_masks(True))
    chains = [(j, d) for j in range(INTRA_CHUNKS) for d in range(2)]
    rows = [slice(j * CHUNK, (j + 1) * CHUNK) for j, d in chains]
    at = [at_blk[d][r_] for (j, d), r_ in zip(chains, rows)]
    rt = [rt_blk[d][r_] for (j, d), r_ in zip(chains, rows)]
    bh_e = [_expand_heads(bh_blk[d][r_]) for (j, d), r_ in zip(chains, rows)]
    kh_e = [_expand_heads(kh_blk[d][r_]) for (j, d), r_ in zip(chains, rows)]
    v_e = [_expand_heads(v_blk[r_]) for r_ in rows]
    x = _inverse_unit_triangular(
        [jnp.where(masks[d][1], -_dot_nt(a, b_), 0.0) for (j, d), a, b_ in zip(chains, at, bh_e)])
    a_ak = [jnp.where(masks[d][1], _dot_nt(a, b_), 0.0).astype(BF16) for (j, d), a, b_ in zip(chains, at, kh_e)]
    a_rb = [jnp.where(masks[d][0], _dot_nt(a, b_), 0.0).astype(BF16) for (j, d), a, b_ in zip(chains, rt, bh_e)]
    a_rk = [jnp.where(masks[d][0], _dot_nt(a, b_), 0.0).astype(BF16) for (j, d), a, b_ in zip(chains, rt, kh_e)]
    rhs0 = [_dot(a, ve) for a, ve in zip(a_ak, v_e)]
    yk = [_dot(a, ve) for a, ve in zip(a_rk, v_e)]
    for n_, ((j, d), r_) in enumerate(zip(chains, rows)):
        xc_ref[d, 0, r_, :] = x[n_].astype(BF16)
        arb_ref[d, 0, r_, :] = a_rb[n_]
        rhs0_ref[d, 0, r_, :] = rhs0[n_]
        yk_ref[d, 0, r_, :] = yk[n_]


def _rw_inter_kernel(*refs):
    n_in = 10
    fwd, bwd = refs[0:n_in], refs[n_in:2 * n_in]
    yf_ref, yb_ref, s_ref = refs[2 * n_in:]

    @pl.when(pl.program_id(1) == 0)
    def _():
        s_ref[...] = jnp.zeros_like(s_ref)

    same = _same_head_mask()
    ins = (fwd, bwd)
    outs = (yf_ref, yb_ref)
    chains = [(d, j) for j in range(INTER_BATCH) for d in range(2)]

    s = [s_ref[d * INTER_BATCH + j] for d, j in chains]
    for t in range(INTER_CHUNKS):
        rows = [_inter_rows(t, d == 1, CHUNK) for d, j in chains]

        def arg(idx):
            return [ins[d][idx][0, j, r, :] for (d, j), r in zip(chains, rows)]

        at, rt, bg, kg, xc, arb, rhs0, yk = (arg(i) for i in range(8))
        gamma = [ins[d][8][0, j, _inter_rows(t, d == 1, GL_ROWS), :][0:1] for d, j in chains]
        v = [ins[d][9][j, r, :] for (d, j), r in zip(chains, rows)]
        s_b = [x.astype(BF16) for x in s]
        rhs = [_expand_heads((_dot_nt(a, sb) + r0).astype(BF16)) for a, sb, r0 in zip(at, s_b, rhs0)]
        sa = [_dot(x, r).astype(BF16) for x, r in zip(xc, rhs)]
        sa_e = [_expand_heads(x) for x in sa]
        y = [_dot_nt(r, sb) + _dot(a, se) + y0 for r, sb, a, se, y0 in zip(rt, s_b, arb, sa_e, yk)]
        upd = [_dot_tn(a, b) + _dot_tn(c, e) for a, b, c, e in zip(sa, bg, v, kg)]
        for i, (d, j) in enumerate(chains):
            outs[d][j, rows[i], :] = y[i]
        s = [s[i] * gamma[i] + jnp.where(same, upd[i], 0.0) for i in range(len(chains))]
    for i, (d, j) in enumerate(chains):
        s_ref[d * INTER_BATCH + j] = s[i]


def rwkv_mixer(slab, mu, w0, w_up, a0, a_up, g_up, k_k, k_a, r_k, ln_w, ln_b):
    n = slab.shape[0]
    nb = n // T_ALL
    assert nb % INTER_BATCH == 0 and CTX_CHUNKS % INTER_CHUNKS == 0 and N_CHUNK % INTER_CHUNKS == 0
    pv = jnp.concatenate([w0, a0, k_k[None], k_a[None], r_k.reshape(1, GROUP), ln_w[None], ln_b[None],
                          jnp.zeros((7, GROUP), F32)], axis=0)
    seq = lambda dt: jax.ShapeDtypeStruct((nb, T_ALL, GROUP), dt)
    seq2 = lambda dt: jax.ShapeDtypeStruct((2, nb, T_ALL, GROUP), dt)
    flat = jax.ShapeDtypeStruct((n, GROUP), F32)
    gl_shape = jax.ShapeDtypeStruct((2, nb, N_CHUNK * GL_ROWS, GROUP), F32)
    p1 = pl.BlockSpec((1, SEQ_BLK, GROUP), lambda b, i: (b, i, 0))
    p2 = pl.BlockSpec((2, 1, SEQ_BLK, GROUP), lambda b, i: (0, b, i, 0))
    pflat = pl.BlockSpec((SEQ_BLK, GROUP), lambda b, i: (b * N_BLK + i, 0))
    pgl = pl.BlockSpec((2, 1, INTRA_CHUNKS * GL_ROWS, GROUP), lambda b, i: (0, b, i, 0))
    small = lambda shape: pl.BlockSpec(shape, lambda b, i: (0, 0))
    at, rt, bg, kg, gl, v, bonus, g, xc, arb, rhs0, yk = pl.pallas_call(
        _rw_intra_kernel,
        grid=(nb, N_BLK),
        in_specs=[pl.BlockSpec((T_ALL, RW_W), lambda b, i: (b, 0)), small((2, RW_W)), small((16, GROUP)),
                  small((RW_LR, RW_LR_OUT))],
        out_specs=[p2] * 4 + [pgl, p1, pflat, pflat] + [p2] * 4,
        out_shape=[seq2(BF16)] * 4 + [gl_shape, seq(BF16), flat, flat, seq2(BF16), seq2(BF16), seq2(F32), seq2(F32)],
        compiler_params=_cparams("parallel", "parallel"),
        name="rwkv7_intra",
    )(slab, mu, pv, rwkv_lowrank_weights(w_up, a_up, g_up))

    def per_dir(d, rows_per_chunk):
        return pl.BlockSpec((1, INTER_BATCH, INTER_CHUNKS * rows_per_chunk, GROUP),
                            lambda b, n_: (d, b, _inter_block(n_, d == 1), 0))

    def shared(d):
        return pl.BlockSpec((INTER_BATCH, INTER_CHUNKS * CHUNK, GROUP), lambda b, n_: (b, _inter_block(n_, d == 1), 0))

    specs = [s for d in range(2) for s in [per_dir(d, CHUNK)] * 8 + [per_dir(d, GL_ROWS), shared(d)]]
    per = (at, rt, bg, kg, xc, arb, rhs0, yk, gl, v)
    y_f, y_b = pl.pallas_call(
        _rw_inter_kernel,
        grid=(nb // INTER_BATCH, N_CHUNK // INTER_CHUNKS),
        in_specs=specs,
        out_specs=[shared(0), shared(1)],
        out_shape=[seq(F32), seq(F32)],
        scratch_shapes=[pltpu.VMEM((2 * INTER_BATCH, GROUP, GROUP), F32)],
        compiler_params=_cparams("parallel", "arbitrary"),
        name="rwkv7_inter",
    )(*per, *per)

    return y_f, y_b, bonus, g, pv


DFT_SPLIT = 64
DFT_BLK = 256


def _dft_tables(n):
    big = 2 * n
    t = np.arange(n, dtype=np.int64)[:, None]
    k1 = np.arange(n // DFT_SPLIT, dtype=np.int64)[None, :]
    k2 = np.arange(DFT_SPLIT, dtype=np.int64)[None, :]
    alpha = 2.0 * np.pi * ((DFT_SPLIT * t * k1) % big) / big
    beta = 2.0 * np.pi * ((t * k2) % big) / big

    def pad(a):
        out = np.zeros((n, 128), np.float32)
        out[:, :a.shape[1]] = a
        return out

    return np.stack([pad(np.cos(alpha)), pad(np.sin(alpha)), pad(np.cos(beta)), pad(np.sin(beta))])


def _dft_gen_kernel(n, tab_ref, g_ref):
    k = lax.broadcasted_iota(jnp.int32, (128, n), 1)
    row = lax.broadcasted_iota(jnp.int32, (128, n), 0)
    e_a = jnp.where(k // DFT_SPLIT == row, 1.0, 0.0).astype(BF16)
    e_b = jnp.where(jnp.logical_and(k % DFT_SPLIT == row, row < DFT_SPLIT), 1.0, 0.0).astype(BF16)
    ca = _dot_split(tab_ref[0], e_a)
    sa = _dot_split(tab_ref[1], e_a)
    cb = _dot_split(tab_ref[2], e_b)
    sb = _dot_split(tab_ref[3], e_b)
    g_ref[:, 0:n] = (ca * cb - sa * sb).astype(BF16)
    g_ref[:, n:2 * n] = (-(sa * cb + ca * sb)).astype(BF16)


def dft_matrix(n):
    blk = min(DFT_BLK, n)
    return pl.pallas_call(
        functools.partial(_dft_gen_kernel, n),
        grid=(n // blk,),
        in_specs=[pl.BlockSpec((4, blk, 128), lambda i: (0, i, 0))],
        out_specs=pl.BlockSpec((blk, 2 * n), lambda i: (i, 0)),
        out_shape=jax.ShapeDtypeStruct((n, 2 * n), BF16),
        compiler_params=_cparams("parallel"),
        name=f"dft_matrix_{n}",
    )(jnp.asarray(_dft_tables(n)))


HY_COLS_F = 2 * HY_ORDER * GROUP
HY_OC = HY_ORDER * GROUP


def _hyena_filter_kernel(n, z_ref, w1_ref, b1_ref, w2_ref, b2_ref, w3_ref, freq_ref, dl_ref, hs_ref, hd_ref):
    blk = min(SEQ_BLK, n)
    freq = freq_ref[...]
    dl = dl_ref[...]

    def fill(i, norm):
        r0 = pl.multiple_of(i * blk, blk)
        z = z_ref[pl.ds(r0, blk), :]
        hid = jnp.sin(freq * (_dot_hi(z, w1_ref[...]) + b1_ref[...]))
        hid = jnp.sin(freq * (_dot_hi(hid, w2_ref[...]) + b2_ref[...]))
        t = z[:, 0:1]
        h = _dot_hi(hid, w3_ref[...]) * jnp.exp(-t * dl)
        lag = lax.broadcasted_iota(jnp.int32, (blk, 1), 0) + r0
        hf = h[:, 0:HY_OC]
        hb = jnp.where(lag == 0, 0.0, h[:, HY_OC:2 * HY_OC])
        hs_ref[pl.ds(r0, blk), :] = hf + hb
        hd_ref[pl.ds(r0, blk), :] = hf - hb
        return norm + jnp.sum(jnp.abs(hf) + jnp.abs(hb), axis=0, keepdims=True)

    norm = lax.fori_loop(0, n // blk, fill, jnp.zeros((1, HY_OC), F32))
    inv = 1.0 / norm

    def scale(i, c):
        rows = pl.ds(pl.multiple_of(i * blk, blk), blk)
        hs_ref[rows, :] = hs_ref[rows, :] * inv
        hd_ref[rows, :] = hd_ref[rows, :] * inv
        return c

    lax.fori_loop(0, n // blk, scale, 0)


def hyena_filter_taps(n, f_w1, f_b1, f_w2, f_b2, f_w3, f_freq):
    f32 = np.float32
    t = np.linspace(0.0, 1.0, n, dtype=f32)[:, None]
    ang = (f32(2.0 * math.pi) * np.arange(n, dtype=f32)[:, None] / f32(n)).astype(f32)
    bands = np.linspace(1e-4, HY_BANDS - 1, HY_BANDS, dtype=f32)[None]
    arg = (bands * ang).astype(f32)
    z = np.concatenate([t, np.cos(arg).astype(f32), -np.sin(arg).astype(f32)], axis=-1)
    emb = z.shape[1]
    z = jnp.asarray(np.pad(z, ((0, 0), (0, 128 - emb))))
    w1 = jnp.pad(f_w1, ((0, 128 - emb), (0, 0)))
    max_decay = math.log(HY_TARGET) / HY_SHORT_DECAY_PCT
    min_decay = math.log(HY_TARGET) / HY_LONG_DECAY_PCT
    deltas = np.abs(np.linspace(min_decay, max_decay, HY_OC, dtype=f32))
    dl = jnp.asarray(np.tile(deltas, 2).reshape(1, HY_COLS_F))
    hid = f_w2.shape[0]
    out = jax.ShapeDtypeStruct((n, HY_OC), F32)
    return pl.pallas_call(
        functools.partial(_hyena_filter_kernel, n),
        out_shape=[out, out],
        compiler_params=pltpu.CompilerParams(vmem_limit_bytes=VMEM_LIMIT),
        name=f"hyena_filter_{n}",
    )(z, w1, f_b1.reshape(1, hid), f_w2, f_b2.reshape(1, hid), f_w3, f_freq.reshape(1, hid), dl)


def _hyena_spectrum_kernel(n, g_ref, hs_ref, hd_ref, kr_ref, ki_ref, kn_ref):
    blk = min(2 * SEQ_BLK, n)
    big = 2.0 * n

    def split(ref):
        x = ref[...]
        hi = x.astype(BF16)
        return hi, (x - hi.astype(F32)).astype(BF16)

    s_hi, s_lo = split(hs_ref)
    d_hi, d_lo = split(hd_ref)

    def body(i, c):
        r0 = pl.multiple_of(i * blk, blk)
        rows = pl.ds(r0, blk)
        k = lax.broadcasted_iota(jnp.int32, (blk, 1), 0) + r0
        wgt = jnp.where(k == 0, 1.0 / big, 2.0 / big)
        gc = g_ref[rows, 0:n]
        gs = g_ref[rows, n:2 * n]
        kr_ref[rows, :] = (_dot(gc, s_hi) + _dot(gc, s_lo)) * wgt
        ki_ref[rows, :] = (_dot(gs, d_hi) + _dot(gs, d_lo)) * wgt
        return c

    lax.fori_loop(0, n // blk, body, 0)
    t = lax.broadcasted_iota(jnp.int32, (n, 1), 0)
    sign = jnp.where(t % 2 == 0, 1.0, -1.0)
    kn_ref[...] = jnp.broadcast_to(jnp.sum(sign * hs_ref[...], axis=0, keepdims=True) * (1.0 / big), (8, HY_OC))


def hyena_spectrum(n, g, hs, hd):
    out = jax.ShapeDtypeStruct((n, HY_OC), F32)
    return pl.pallas_call(
        functools.partial(_hyena_spectrum_kernel, n),
        out_shape=[out, out, jax.ShapeDtypeStruct((8, HY_OC), F32)],
        compiler_params=pltpu.CompilerParams(vmem_limit_bytes=VMEM_LIMIT),
        name=f"hyena_spectrum_{n}",
    )(g, hs, hd)


def _hyena_conv_kernel(slab_ref, w_ref, o_ref):
    def body(i, c):
        rows = pl.ds(pl.multiple_of(i * SEQ_BLK, SEQ_BLK), SEQ_BLK)
        for j in range(3):
            prev, cur, nxt = _prev_cur_next(slab_ref, i, j * GROUP, (j + 1) * GROUP)
            w = w_ref[:, j * GROUP:(j + 1) * GROUP]
            o_ref[rows, j * GROUP:(j + 1) * GROUP] = prev * w[0:1] + cur * w[1:2] + nxt * w[2:3]
        return c

    lax.fori_loop(0, N_BLK, body, 0)


def hyena_short_conv(slab, conv_w):
    n = slab.shape[0]
    return pl.pallas_call(
        _hyena_conv_kernel,
        grid=(n // T_ALL,),
        in_specs=[pl.BlockSpec((T_ALL, 3 * GROUP), lambda b: (b, 0)),
                  pl.BlockSpec((3, 3 * GROUP), lambda b: (0, 0))],
        out_specs=pl.BlockSpec((T_ALL, 3 * GROUP), lambda b: (b, 0)),
        out_shape=jax.ShapeDtypeStruct((n, 3 * GROUP), F32),
        compiler_params=_cparams("parallel"),
        name="hyena_short_conv",
    )(slab, conv_w)


HY_FBLK = 1024


def _alt_sign(n):
    t = lax.broadcasted_iota(jnp.int32, (n, 1), 0)
    return jnp.where(t % 2 == 0, 1.0, -1.0)


def _hyena_fwd_kernel(x_ref, gl_ref, gc_ref, krl_ref, kil_ref, knl_ref, krc_ref, kic_ref, knc_ref,
                      pl_ref, pc_ref, pn_ref):
    def transform(x, g_ref, kr_ref, ki_ref, kn_ref, p_ref, n, blk):
        xb = x.astype(BF16)

        def body(i, c):
            rows = pl.ds(pl.multiple_of(i * blk, blk), blk)
            zr = _dot(g_ref[rows, 0:n], xb)
            zi = _dot(g_ref[rows, n:2 * n], xb)
            kr = kr_ref[rows, :]
            ki = ki_ref[rows, :]
            p_ref[0, 0, rows, :] = (zr * kr - zi * ki).astype(BF16)
            p_ref[0, 1, rows, :] = (zr * ki + zi * kr).astype(BF16)
            return c

        lax.fori_loop(0, n // blk, body, 0)
        return jnp.sum(_alt_sign(n) * x, axis=0, keepdims=True) * kn_ref[0:1, :]

    nyq_c = transform(x_ref[0:CTX_LEN, :], gc_ref, krc_ref, kic_ref, knc_ref, pc_ref, CTX_LEN, CTX_LEN)
    nyq_l = transform(x_ref[CTX_LEN:T_ALL, :], gl_ref, krl_ref, kil_ref, knl_ref, pl_ref, SEQ, HY_FBLK)
    pn_ref[0] = jnp.concatenate([nyq_l, nyq_c, jnp.zeros((6, GROUP), F32)], axis=0)


def _resident(shape):
    return pl.BlockSpec(shape, lambda b: (0,) * len(shape))


def hyena_forward_transform(x, col, g_l, g_c, spec_l, spec_c, order):
    n = x.shape[0]
    nb = n // T_ALL
    kcol = lambda shape: pl.BlockSpec(shape, lambda b: (0, order))
    return pl.pallas_call(
        _hyena_fwd_kernel,
        grid=(nb,),
        in_specs=[
            pl.BlockSpec((T_ALL, GROUP), lambda b: (b, col)),
            _resident((SEQ, 2 * SEQ)), _resident((CTX_LEN, 2 * CTX_LEN)),
            kcol((SEQ, GROUP)), kcol((SEQ, GROUP)), kcol((8, GROUP)),
            kcol((CTX_LEN, GROUP)), kcol((CTX_LEN, GROUP)), kcol((8, GROUP)),
        ],
        out_specs=[
            pl.BlockSpec((1, 2, SEQ, GROUP), lambda b: (b, 0, 0, 0)),
            pl.BlockSpec((1, 2, CTX_LEN, GROUP), lambda b: (b, 0, 0, 0)),
            pl.BlockSpec((1, 8, GROUP), lambda b: (b, 0, 0)),
        ],
        out_shape=[
            jax.ShapeDtypeStruct((nb, 2, SEQ, GROUP), BF16),
            jax.ShapeDtypeStruct((nb, 2, CTX_LEN, GROUP), BF16),
            jax.ShapeDtypeStruct((nb, 8, GROUP), F32),
        ],
        compiler_params=_cparams("parallel"),
        name=f"hyena_fwd_{order}",
    )(x, g_l, g_c, *spec_l, *spec_c)


def _hyena_inv_kernel(pl_ref, pc_ref, pn_ref, gl_ref, gc_ref, u_ref, gate_ref, bias_ref, o_ref):
    bias = bias_ref[0]

    def inverse(p_ref, nyq, g_ref, n, blk, off):
        pr = p_ref[0, 0]
        pi = p_ref[0, 1]

        def body(i, c):
            r0 = pl.multiple_of(i * blk, blk)
            rows = pl.ds(r0, blk)
            orow = pl.ds(pl.multiple_of(off + r0, math.gcd(blk, CTX_LEN)), blk)
            t = lax.broadcasted_iota(jnp.int32, (blk, 1), 0)
            sign = jnp.where(t % 2 == 0, 1.0, -1.0)
            y = _dot(g_ref[rows, 0:n], pr) + _dot(g_ref[rows, n:2 * n], pi) + sign * nyq
            o_ref[orow, :] = gate_ref[orow, :] * (y + u_ref[orow, :] * bias)
            return c

        lax.fori_loop(0, n // blk, body, 0)

    inverse(pc_ref, pn_ref[0, 1:2, :], gc_ref, CTX_LEN, CTX_LEN, 0)
    inverse(pl_ref, pn_ref[0, 0:1, :], gl_ref, SEQ, HY_FBLK, CTX_LEN)


def hyena_inverse_transform(p_l, p_c, p_n, g_l, g_c, u, ucol, gate, gcol, bias):
    nb = p_l.shape[0]
    return pl.pallas_call(
        _hyena_inv_kernel,
        grid=(nb,),
        in_specs=[
            pl.BlockSpec((1, 2, SEQ, GROUP), lambda b: (b, 0, 0, 0)),
            pl.BlockSpec((1, 2, CTX_LEN, GROUP), lambda b: (b, 0, 0, 0)),
            pl.BlockSpec((1, 8, GROUP), lambda b: (b, 0, 0)),
            _resident((SEQ, 2 * SEQ)), _resident((CTX_LEN, 2 * CTX_LEN)),
            pl.BlockSpec((T_ALL, GROUP), lambda b: (b, ucol)),
            pl.BlockSpec((T_ALL, GROUP), lambda b: (b, gcol)),
            pl.BlockSpec((1, 1, GROUP), lambda b: (0, 0, 0)),
        ],
        out_specs=pl.BlockSpec((T_ALL, GROUP), lambda b: (b, 0)),
        out_shape=jax.ShapeDtypeStruct((nb * T_ALL, GROUP), F32),
        compiler_params=_cparams("parallel"),
        name="hyena_inv",
    )(p_l, p_c, p_n, g_l, g_c, u, gate, bias.reshape(1, 1, GROUP))


def hyena_mixer(slab, g_l, g_c, conv_w, f_w1, f_b1, f_w2, f_b2, f_w3, f_freq, bias):
    u = hyena_short_conv(slab, conv_w)
    spec_l = hyena_spectrum(SEQ, g_l, *hyena_filter_taps(SEQ, f_w1, f_b1, f_w2, f_b2, f_w3, f_freq))
    spec_c = hyena_spectrum(CTX_LEN, g_c, *hyena_filter_taps(CTX_LEN, f_w1, f_b1, f_w2, f_b2, f_w3, f_freq))
    p = hyena_forward_transform(u, 0, g_l, g_c, spec_l, spec_c, 0)
    z = hyena_inverse_transform(*p, g_l, g_c, u, 0, u, 1, bias[0])
    p = hyena_forward_transform(z, 0, g_l, g_c, spec_l, spec_c, 1)
    return hyena_inverse_transform(*p, g_l, g_c, z, 0, u, 2, bias[1])


def kernel(x, c, ctx, c_ctx, w_mod, b_mod, norm_w, ffn_w_gu, ffn_w_down, w_in, w_out,
           hy_conv, hy_f_w1, hy_f_b1, hy_f_w2, hy_f_b2, hy_f_w3, hy_f_freq, hy_bias,
           na_q_norm, na_k_norm, na_rpb, dn_conv, dn_a_log, dn_dt_bias, dn_norm,
           rw_mu, rw_w0, rw_w_up, rw_a0, rw_a_up, rw_g_up, rw_k_k, rw_k_a, rw_r_k, rw_ln_w, rw_ln_b):
    nb = x.shape[0]
    assert x.shape[1:] == (SEQ, D_MODEL) and ctx.shape[1:] == (CTX_LEN, D_MODEL) and nb + 1 <= 16
    s = jnp.concatenate([ctx, x], axis=1).reshape(nb * T_ALL, D_MODEL)
    cond = jnp.concatenate([c_ctx[None], c, jnp.zeros((15 - nb, D_MODEL), F32)], axis=0)
    mod = modulation_all(cond, w_mod, b_mod).reshape(DEPTH, 16, N_MOD, D_MODEL)
    g_l = dft_matrix(SEQ)
    g_c = dft_matrix(CTX_LEN)
    w_gu = ffn_w_gu.astype(BF16)
    w_down = ffn_w_down.astype(BF16)
    w_out_b = w_out.astype(BF16)
    dn_end = 6 * GROUP + 4 * GROUP + 4 * GROUP_HEADS
    w_in_p = jnp.concatenate(
        [w_in[:, :, :dn_end], jnp.zeros((DEPTH, D_MODEL, 6 * GROUP + DN_W - dn_end), F32), w_in[:, :, dn_end:]],
        axis=2).astype(BF16)
    for l in range(DEPTH):
        need_ctx = l < DEPTH - 1
        modc = mod[l, 0:1]
        modb = mod[l, 1:1 + nb]
        s = ffn_half_step(s, modc, modb, norm_w[l, 0], w_gu, w_down, l, 0, 0)
        hy_s, na_s, dn_s, rw_s = input_projection(s, modc, modb, norm_w[l, 1], w_in_p, l)
        hy_g = hyena_mixer(hy_s, g_l, g_c, hy_conv[l], hy_f_w1[l], hy_f_b1[l], hy_f_w2[l], hy_f_b2[l], hy_f_w3[l],
                           hy_f_freq[l], hy_bias[l])
        na_g = na_mixer(na_s, na_q_norm[l], na_k_norm[l], na_bias_table(na_rpb[l]), need_ctx)
        dn_parts = deltanet_mixer(dn_s, dn_conv[l], dn_a_log[l], dn_dt_bias[l]) + (dn_s, dn_norm[l])
        rw_parts = rwkv_mixer(rw_s, rw_mu[l], rw_w0[l], rw_w_up[l], rw_a0[l], rw_a_up[l], rw_g_up[l], rw_k_k[l],
                              rw_k_a[l], rw_r_k[l], rw_ln_w[l], rw_ln_b[l])
        s = mixer_output_ffn(s, modc, modb, norm_w[l, 2], hy_g, na_g, dn_parts, rw_parts, w_out_b, w_gu, w_down, l,
                             latent_only=not need_ctx)
    return s.reshape(nb, SEQ, D_MODEL)
```

```python
import functools
import math

import numpy as np
import jax
import jax.numpy as jnp
from jax import lax
from jax.experimental import pallas as pl
from jax.experimental.pallas import tpu as pltpu

D_MODEL = 1024
SEQ = 2048
DEPTH = 2
CTX_LEN = 256
T_ALL = CTX_LEN + SEQ
GRID_W = 64
GROUP = 256
HEAD_DIM = 64
GROUP_HEADS = 4
D_FF = 2816
N_MOD = 9
NORM_EPS = 1e-6

HY_ORDER = 2
HY_BANDS = 16
HY_TARGET = 1e-2
HY_SHORT_DECAY_PCT = 0.3
HY_LONG_DECAY_PCT = 1.5

NA_WIN_ROWS = 8
NA_WIN_COLS = 16

CHUNK = 64
RW_DECAY_RANK = 32
RW_AAA_RANK = 32
RW_GATE_RANK = 64
RW_LN_EPS = 64e-5

DN_W = 4 * GROUP + 128
RW_W = 3 * GROUP + 128
P_PAD = 3 * GROUP + 3 * GROUP + DN_W + RW_W

TM = 768
TF = 512
VMEM_LIMIT = 56 * 1024 * 1024

F32 = jnp.float32
BF16 = jnp.bfloat16


def _cparams(*sem):
    return pltpu.CompilerParams(dimension_semantics=sem, vmem_limit_bytes=VMEM_LIMIT)


def _silu(x):
    return x * (1.0 / (1.0 + jnp.exp(-x)))


def _sigmoid(x):
    return 1.0 / (1.0 + jnp.exp(-x))


def _softplus(x):
    return jnp.maximum(x, 0.0) + jnp.log(1.0 + jnp.exp(-jnp.abs(x)))


def _dot(a, b):
    return jnp.dot(a, b, preferred_element_type=F32)


def _dot_nt(a, b):
    return lax.dot_general(a, b, (((1,), (1,)), ((), ())), preferred_element_type=F32)


def _dot_tn(a, b):
    return lax.dot_general(a, b, (((0,), (0,)), ((), ())), preferred_element_type=F32)


def _dot_hi(a, b):
    return jnp.dot(a, b, preferred_element_type=F32, precision=lax.Precision.HIGHEST)


def _mod_kernel(cond_ref, w_ref, b_ref, o_ref):
    a = _silu(cond_ref[...]).astype(BF16)
    o_ref[0] = _dot(a, w_ref[0].astype(BF16)) + b_ref[0]


def modulation_all(cond, w_mod, b_mod):
    r = cond.shape[0]
    tn = 1024
    return pl.pallas_call(
        _mod_kernel,
        grid=(DEPTH, N_MOD * D_MODEL // tn),
        in_specs=[
            pl.BlockSpec((r, D_MODEL), lambda l, j: (0, 0)),
            pl.BlockSpec((1, D_MODEL, tn), lambda l, j: (l, 0, j)),
            pl.BlockSpec((1, 1, tn), lambda l, j: (l, 0, j)),
        ],
        out_specs=pl.BlockSpec((1, r, tn), lambda l, j: (l, 0, j)),
        out_shape=jax.ShapeDtypeStruct((DEPTH, r, N_MOD * D_MODEL), F32),
        compiler_params=_cparams("parallel", "parallel"),
        name="modulation",
    )(cond, w_mod, b_mod.reshape(DEPTH, 1, N_MOD * D_MODEL))


TL = 512


def _row_mod(modc_ref, modb_ref, tile, idx, latent_only=False):
    if latent_only:
        return modb_ref[0, idx:idx + 1, :]
    row = lax.broadcasted_iota(jnp.int32, (TM, 1), 0) + (tile % (T_ALL // TM)) * TM
    return jnp.where(row < CTX_LEN, modc_ref[0, idx:idx + 1, :], modb_ref[0, idx:idx + 1, :])


def _adaln(x, nw, shift, scale):
    y = x * lax.rsqrt(jnp.mean(x * x, axis=-1, keepdims=True) + NORM_EPS)
    return y * nw * (1.0 + scale) + shift


def _ffn_body(sub, latent_only, i, x, modc_ref, modb_ref, nw_ref, wgu_ref, wd_ref):
    shift = _row_mod(modc_ref, modb_ref, i, 3 * sub, latent_only)
    scale = _row_mod(modc_ref, modb_ref, i, 3 * sub + 1, latent_only)
    h = _adaln(x, nw_ref[...], shift, scale).astype(BF16)
    acc = None
    for c0 in range(0, D_FF, TF):
        c1 = min(c0 + TF, D_FF)
        a = (_silu(_dot(h, wgu_ref[:, c0:c1])) * _dot(h, wgu_ref[:, D_FF + c0:D_FF + c1])).astype(BF16)
        part = _dot(a, wd_ref[c0:c1, :])
        acc = part if acc is None else acc + part
    gate = _row_mod(modc_ref, modb_ref, i, 3 * sub + 2, latent_only)
    return x + 0.5 * gate * acc


def _ffn_kernel(sub, latent_only, x_ref, modc_ref, modb_ref, nw_ref, wgu_ref, wd_ref, o_ref):
    o_ref[...] = _ffn_body(sub, latent_only, pl.program_id(0), x_ref[...], modc_ref, modb_ref, nw_ref, wgu_ref, wd_ref)


def _ffn_parts_kernel(sub, ctx_ref, xa_ref, xb_ref, modc_ref, modb_ref, nw_ref, wgu_ref, wd_ref, o_ref):
    i = pl.program_id(0)
    first = jnp.concatenate([ctx_ref[...], xa_ref[...]], axis=0)
    x = jnp.where(i % (T_ALL // TM) == 0, first, xb_ref[...])
    o_ref[...] = _ffn_body(sub, False, i, x, modc_ref, modb_ref, nw_ref, wgu_ref, wd_ref)


def ffn_half_step(x, modc, modb, nw, w_gu, w_down, layer, which, sub, parts=None):
    tiles_per_b = T_ALL // TM
    once = pl.Buffered(1)
    common = [
        pl.BlockSpec((1, N_MOD, D_MODEL), lambda i: (0, 0, 0)),
        pl.BlockSpec((1, N_MOD, D_MODEL), lambda i: (i // tiles_per_b, 0, 0)),
        pl.BlockSpec((1, D_MODEL), lambda i: (0, 0)),
        pl.BlockSpec((None, None, D_MODEL, 2 * D_FF), lambda i: (layer, which, 0, 0), pipeline_mode=once),
        pl.BlockSpec((None, None, D_FF, D_MODEL), lambda i: (layer, which, 0, 0), pipeline_mode=once),
    ]
    if parts is None:
        n = x.shape[0]
        body = functools.partial(_ffn_kernel, sub, False)
        data_specs = [pl.BlockSpec((TM, D_MODEL), lambda i: (i, 0))]
        data = (x,)
    else:
        lat, ctx = parts
        nb = lat.shape[0]
        n = nb * T_ALL
        head = TM - CTX_LEN

        def window(rows, start):
            return pl.BlockSpec((pl.Element(rows), pl.Element(D_MODEL)),
                                lambda i: (pl.multiple_of(start(i // tiles_per_b, i % tiles_per_b), CTX_LEN), 0))

        body = functools.partial(_ffn_parts_kernel, sub)
        data_specs = [
            pl.BlockSpec((CTX_LEN, D_MODEL), lambda i: (i // tiles_per_b, 0)),
            window(head, lambda b, t: b * SEQ),
            window(TM, lambda b, t: b * SEQ + head + (jnp.maximum(t, 1) - 1) * TM),
        ]
        lat2 = lat.reshape(nb * SEQ, D_MODEL)
        data = (ctx.reshape(nb * CTX_LEN, D_MODEL), lat2, lat2)
    return pl.pallas_call(
        body,
        grid=(n // TM,),
        in_specs=data_specs + common,
        out_specs=pl.BlockSpec((TM, D_MODEL), lambda i: (i, 0)),
        out_shape=jax.ShapeDtypeStruct((n, D_MODEL), F32),
        compiler_params=_cparams("parallel"),
        name=f"ffn{sub}",
    )(*data, modc, modb, nw.reshape(1, D_MODEL), w_gu, w_down)


def _inproj_kernel(x_ref, modc_ref, modb_ref, nw_ref, w_ref, hy_ref, na_ref, dn_ref, rw_ref):
    i = pl.program_id(0)
    shift = _row_mod(modc_ref, modb_ref, i, 3)
    scale = _row_mod(modc_ref, modb_ref, i, 4)
    h = _adaln(x_ref[...], nw_ref[...], shift, scale).astype(BF16)
    o0 = 3 * GROUP
    o1 = 6 * GROUP
    o2 = o1 + DN_W
    y = _dot(h, w_ref[...])
    hy_ref[...] = y[:, 0:o0]
    na_ref[...] = y[:, o0:o1]
    dn_ref[...] = y[:, o1:o2]
    rw_ref[...] = y[:, o2:P_PAD]


def input_projection(x, modc, modb, nw, w_in_p, layer):
    n = x.shape[0]
    tiles_per_b = T_ALL // TM
    widths = (3 * GROUP, 3 * GROUP, DN_W, RW_W)
    return pl.pallas_call(
        _inproj_kernel,
        grid=(n // TM,),
        in_specs=[
            pl.BlockSpec((TM, D_MODEL), lambda i: (i, 0)),
            pl.BlockSpec((1, N_MOD, D_MODEL), lambda i: (0, 0, 0)),
            pl.BlockSpec((1, N_MOD, D_MODEL), lambda i: (i // tiles_per_b, 0, 0)),
            pl.BlockSpec((1, D_MODEL), lambda i: (0, 0)),
            pl.BlockSpec((None, D_MODEL, P_PAD), lambda i: (layer, 0, 0)),
        ],
        out_specs=[pl.BlockSpec((TM, w), lambda i: (i, 0)) for w in widths],
        out_shape=[jax.ShapeDtypeStruct((n, w), F32) for w in widths],
        compiler_params=_cparams("parallel"),
        name="inproj",
    )(x, modc, modb, nw.reshape(1, D_MODEL), w_in_p)


def _mix_ffn_kernel(latent_only, x_ref, modc_ref, modb_ref, nw_ref, hy_ref, na_ref,
                    of_ref, ob_ref, z_ref, dnw_ref, yf_ref, yb_ref, bonus_ref, g_ref, pv_ref,
                    wo_ref, wgu_ref, wd_ref, o_ref):
    i = pl.program_id(0)
    hmean = _head_mean_matrix(1.0 / HEAD_DIM)
    o = of_ref[...] + ob_ref[...]
    g_dn = o * lax.rsqrt(_dot_split(o * o, hmean) + NORM_EPS) * dnw_ref[...] * _silu(z_ref[...])
    y = yf_ref[...] + yb_ref[...]
    yc = y - _dot_split(y, hmean)
    yn = yc * lax.rsqrt(_dot_split(yc * yc, hmean) + RW_LN_EPS) * pv_ref[7:8, :] + pv_ref[8:9, :]
    g_rw = (yn + bonus_ref[...]) * g_ref[...]
    mix = _dot(hy_ref[...].astype(BF16), wo_ref[0:GROUP, :])
    mix += _dot(na_ref[...].astype(BF16), wo_ref[GROUP:2 * GROUP, :])
    mix += _dot(g_dn.astype(BF16), wo_ref[2 * GROUP:3 * GROUP, :])
    mix += _dot(g_rw.astype(BF16), wo_ref[3 * GROUP:4 * GROUP, :])
    x = x_ref[...] + _row_mod(modc_ref, modb_ref, i, 5, latent_only) * mix
    o_ref[...] = _ffn_body(2, latent_only, i, x, modc_ref, modb_ref, nw_ref, wgu_ref, wd_ref)


def mixer_output_ffn(x, modc, modb, nw, hy, na, dn, rw, w_out, w_gu, w_down, layer, latent_only):
    n = x.shape[0]
    nb = n // T_ALL
    o_f, o_b, dn_slab, dn_norm = dn
    y_f, y_b, bonus, gate, pv = rw
    flat = lambda t: t.reshape(n, GROUP)
    if latent_only:
        tm, tiles_per_b = TL, SEQ // TL
        rows = lambda i: pl.multiple_of((i // tiles_per_b) * T_ALL + CTX_LEN + (i % tiles_per_b) * TL,
                                        math.gcd(CTX_LEN, TL))
        win = lambda width, col=0: pl.BlockSpec((pl.Element(tm), pl.Element(width)), lambda i: (rows(i), col * width))
    else:
        tm, tiles_per_b = TM, T_ALL // TM
        win = lambda width, col=0: pl.BlockSpec((tm, width), lambda i: (i, col))
    n_out = nb * tiles_per_b * tm
    once = pl.Buffered(1)
    small = lambda shape: pl.BlockSpec(shape, lambda i: (0,) * len(shape))
    return pl.pallas_call(
        functools.partial(_mix_ffn_kernel, latent_only),
        grid=(n_out // tm,),
        in_specs=[
            win(D_MODEL), small((1, N_MOD, D_MODEL)),
            pl.BlockSpec((1, N_MOD, D_MODEL), lambda i: (i // tiles_per_b, 0, 0)), small((1, D_MODEL)),
            win(GROUP), win(GROUP),
            win(GROUP), win(GROUP), win(GROUP, 3), small((1, GROUP)),
            win(GROUP), win(GROUP), win(GROUP), win(GROUP), small((16, GROUP)),
            pl.BlockSpec((None, D_MODEL, D_MODEL), lambda i: (layer, 0, 0), pipeline_mode=once),
            pl.BlockSpec((None, None, D_MODEL, 2 * D_FF), lambda i: (layer, 1, 0, 0), pipeline_mode=once),
            pl.BlockSpec((None, None, D_FF, D_MODEL), lambda i: (layer, 1, 0, 0), pipeline_mode=once),
        ],
        out_specs=pl.BlockSpec((tm, D_MODEL), lambda i: (i, 0)),
        out_shape=jax.ShapeDtypeStruct((n_out, D_MODEL), F32),
        compiler_params=_cparams("parallel"),
        name="mix_ffn",
    )(x, modc, modb, nw.reshape(1, D_MODEL), hy, na, flat(o_f), flat(o_b), dn_slab,
      jnp.tile(dn_norm, GROUP_HEADS).reshape(1, GROUP), flat(y_f), flat(y_b), bonus, gate, pv,
      w_out, w_gu, w_down)


def _head_mean_matrix(scale):
    r = lax.broadcasted_iota(jnp.int32, (GROUP, GROUP), 0) // HEAD_DIM
    c = lax.broadcasted_iota(jnp.int32, (GROUP, GROUP), 1) // HEAD_DIM
    return jnp.where(r == c, scale, 0.0).astype(BF16)


def _dot_split(a, m_bf16):
    hi = a.astype(BF16)
    lo = (a - hi.astype(F32)).astype(BF16)
    return _dot(hi, m_bf16) + _dot(lo, m_bf16)


def _lane_head(width=GROUP):
    return lax.broadcasted_iota(jnp.int32, (1, width), 1) // HEAD_DIM


NA_ROWS = SEQ // GRID_W
NA_LOCAL = NA_WIN_ROWS * GRID_W
NA_NEG = -1e30
NA_BLK = 256
NA_PAIR = 4


def na_bias_table(rpb):
    n_dr = 2 * NA_WIN_ROWS
    rows = jnp.pad(rpb, ((0, 0), (0, 1), (0, 128 - rpb.shape[2]))).reshape(GROUP_HEADS * n_dr, 128)
    toep = pl.pallas_call(
        _na_bias_kernel,
        out_shape=jax.ShapeDtypeStruct((GROUP_HEADS * n_dr, GRID_W * GRID_W), F32),
        name="na_bias",
    )(rows).reshape(GROUP_HEADS, n_dr, GRID_W, GRID_W)
    tab = jnp.stack([toep[:, NA_WIN_ROWS - 1 - p:2 * NA_WIN_ROWS - 1 - p] for p in range(NA_WIN_ROWS)], axis=0)
    tab = jnp.transpose(tab, (0, 1, 3, 2, 4))
    return tab.reshape(NA_WIN_ROWS, GROUP_HEADS, GRID_W, NA_LOCAL)


def _na_bias_kernel(rpb_ref, o_ref):
    n = GRID_W * GRID_W
    d = lax.broadcasted_iota(jnp.int32, (128, n), 0)
    cj = lax.broadcasted_iota(jnp.int32, (128, n), 1)
    onehot = jnp.where((cj % GRID_W) - (cj // GRID_W) + NA_WIN_COLS - 1 == d, 1.0, 0.0).astype(BF16)
    cj1 = lax.broadcasted_iota(jnp.int32, (1, n), 1)
    c = cj1 // GRID_W
    j = cj1 % GRID_W
    start = jnp.clip(c - NA_WIN_COLS // 2, 0, GRID_W - NA_WIN_COLS)
    in_win = jnp.logical_and(j >= start, j < start + NA_WIN_COLS)
    o_ref[...] = jnp.where(in_win, _dot_exact_rhs(rpb_ref[...], onehot), NA_NEG)


def _na_kernel(need_ctx, slab_ref, qw_ref, kw_ref, bias_ref, o_ref, q_s, k_s, v_s):
    hm = _head_mean_matrix(1.0 / HEAD_DIM)
    qw = qw_ref[...] * (HEAD_DIM ** -0.5)
    kw = kw_ref[...]

    def prep(i, c):
        r0 = pl.multiple_of(i * NA_BLK, NA_BLK)
        q = slab_ref[pl.ds(r0, NA_BLK), 0:GROUP]
        k = slab_ref[pl.ds(r0, NA_BLK), GROUP:2 * GROUP]
        q_s[pl.ds(r0, NA_BLK), :] = (q * lax.rsqrt(_dot_split(q * q, hm) + NORM_EPS) * qw).astype(BF16)
        k_s[pl.ds(r0, NA_BLK), :] = (k * lax.rsqrt(_dot_split(k * k, hm) + NORM_EPS) * kw).astype(BF16)
        v_s[pl.ds(r0, NA_BLK), :] = slab_ref[pl.ds(r0, NA_BLK), 2 * GROUP:3 * GROUP].astype(BF16)
        return c

    lax.fori_loop(0, T_ALL // NA_BLK, prep, 0)

    lane_h = _lane_head()
    kc = k_s[0:CTX_LEN, :]
    vc = v_s[0:CTX_LEN, :]

    if need_ctx:
        qc = q_s[0:CTX_LEN, :]
        out = jnp.zeros((CTX_LEN, GROUP), F32)
        for h in range(GROUP_HEADS):
            mask = lane_h == h
            s = _dot_nt(jnp.where(mask, qc, jnp.zeros_like(qc)), kc)
            e = jnp.exp(s - jnp.max(s, axis=-1, keepdims=True))
            p = e * (1.0 / jnp.sum(e, axis=-1, keepdims=True))
            out = jnp.where(mask, _dot(p.astype(BF16), vc), out)
        o_ref[0:CTX_LEN, :] = out
    else:
        o_ref[0:CTX_LEN, :] = jnp.zeros((CTX_LEN, GROUP), F32)

    def pair_body(i, c):
        rows = [i * NA_PAIR + t for t in range(NA_PAIR)]
        start = [jnp.clip(r - NA_WIN_ROWS // 2, 0, NA_ROWS - NA_WIN_ROWS) for r in rows]
        q0 = [pl.multiple_of(CTX_LEN + r * GRID_W, GRID_W) for r in rows]
        k0 = [pl.multiple_of(CTX_LEN + s * GRID_W, GRID_W) for s in start]
        q = [_expand_heads(q_s[pl.ds(a, GRID_W), :]) for a in q0]
        kb = [k_s[pl.ds(a, NA_LOCAL), :] for a in k0]
        vb = [v_s[pl.ds(a, NA_LOCAL), :] for a in k0]
        bias = [bias_ref[r - s].reshape(GROUP_HEADS * GRID_W, NA_LOCAL) for r, s in zip(rows, start)]
        s_loc = [_dot_nt(q[t], kb[t]) + bias[t] for t in range(NA_PAIR)]
        s_ctx = [_dot_nt(q[t], kc) for t in range(NA_PAIR)]
        m = [jnp.maximum(jnp.max(a, axis=-1, keepdims=True), jnp.max(b, axis=-1, keepdims=True))
             for a, b in zip(s_loc, s_ctx)]
        e_loc = [jnp.exp(a - mm) for a, mm in zip(s_loc, m)]
        e_ctx = [jnp.exp(b - mm) for b, mm in zip(s_ctx, m)]
        inv = [1.0 / (jnp.sum(a, axis=-1, keepdims=True) + jnp.sum(b, axis=-1, keepdims=True))
               for a, b in zip(e_loc, e_ctx)]
        o = [_dot((e_loc[t] * inv[t]).astype(BF16), vb[t]) + _dot((e_ctx[t] * inv[t]).astype(BF16), vc)
             for t in range(NA_PAIR)]
        for t in range(NA_PAIR):
            out = o[t][0:GRID_W]
            for h in range(1, GROUP_HEADS):
                out = jnp.where(lane_h == h, o[t][h * GRID_W:(h + 1) * GRID_W], out)
            o_ref[pl.ds(q0[t], GRID_W), :] = out
        return c

    lax.fori_loop(0, NA_ROWS // NA_PAIR, pair_body, 0)


def na_mixer(slab, q_norm, k_norm, bias_tab, need_ctx):
    n = slab.shape[0]
    tile4 = lambda w: jnp.tile(w, GROUP_HEADS).reshape(1, GROUP)
    return pl.pallas_call(
        functools.partial(_na_kernel, need_ctx),
        grid=(n // T_ALL,),
        in_specs=[
            pl.BlockSpec((T_ALL, 3 * GROUP), lambda b: (b, 0)),
            pl.BlockSpec((1, GROUP), lambda b: (0, 0)),
            pl.BlockSpec((1, GROUP), lambda b: (0, 0)),
            pl.BlockSpec((NA_WIN_ROWS, GROUP_HEADS, GRID_W, NA_LOCAL), lambda b: (0, 0, 0, 0)),
        ],
        out_specs=pl.BlockSpec((T_ALL, GROUP), lambda b: (b, 0)),
        out_shape=jax.ShapeDtypeStruct((n, GROUP), F32),
        scratch_shapes=[pltpu.VMEM((T_ALL, GROUP), BF16)] * 3,
        compiler_params=_cparams("parallel"),
        name="na_mixer",
    )(slab, tile4(q_norm), tile4(k_norm), bias_tab)


SEQ_BLK = 256
N_BLK = T_ALL // SEQ_BLK
N_CHUNK = T_ALL // CHUNK
CTX_CHUNKS = CTX_LEN // CHUNK


def _prev_cur_next(ref, i, c0, c1):
    r0 = pl.multiple_of(i * SEQ_BLK, SEQ_BLK)
    cur = ref[pl.ds(r0, SEQ_BLK), c0:c1]
    up0 = pl.multiple_of(jnp.maximum(r0 - 8, 0), 8)
    dn0 = pl.multiple_of(jnp.minimum(r0 + SEQ_BLK, T_ALL - 8), 8)
    up = ref[pl.ds(up0, 8), c0:c1][7:8, :]
    dn = ref[pl.ds(dn0, 8), c0:c1][0:1, :]
    up = jnp.where(i >= 2, up, 0.0)
    dn = jnp.where(jnp.logical_and(i >= 1, i <= N_BLK - 2), dn, 0.0)
    row = lax.broadcasted_iota(jnp.int32, (SEQ_BLK, 1), 0)
    prev = jnp.where(row == 0, up, pltpu.roll(cur, 1, 0))
    nxt = jnp.where(row == SEQ_BLK - 1, dn, pltpu.roll(cur, SEQ_BLK - 1, 0))
    return prev, cur, nxt


def _chunk_cumsum(x, reverse):
    pos = lax.broadcasted_iota(jnp.int32, (SEQ_BLK, 1), 0) % CHUNK
    s = 1
    while s < CHUNK:
        if reverse:
            x = x + jnp.where(pos < CHUNK - s, pltpu.roll(x, SEQ_BLK - s, 0), 0.0)
        else:
            x = x + jnp.where(pos >= s, pltpu.roll(x, s, 0), 0.0)
        s *= 2
    return x


def _split3(a):
    hi = a.astype(BF16)
    r1 = a - hi.astype(F32)
    mid = r1.astype(BF16)
    lo = (r1 - mid.astype(F32)).astype(BF16)
    return hi, mid, lo


def _dot_exact_rhs(a, m_bf16):
    hi, mid, lo = _split3(a)
    return _dot(hi, m_bf16) + _dot(mid, m_bf16) + _dot(lo, m_bf16)


def _expand_heads(x):
    lane_h = _lane_head()
    return jnp.concatenate([jnp.where(lane_h == h, x, 0.0) for h in range(GROUP_HEADS)], axis=0)


def _chunk_of_step(n, reverse):
    if not reverse:
        return n
    return jnp.where(n < CTX_CHUNKS, CTX_CHUNKS - 1 - n, N_CHUNK + CTX_CHUNKS - 1 - n)


INV_BASE = 16


def _cat_dot(a, b):
    return _dot(a.astype(BF16), _expand_heads(b.astype(BF16)))


def _cat_index():
    i = lax.broadcasted_iota(jnp.int32, (CHUNK, GROUP_HEADS * CHUNK), 0)
    j = lax.broadcasted_iota(jnp.int32, (CHUNK, GROUP_HEADS * CHUNK), 1) % CHUNK
    return i, j


def _cat_masks(reverse):
    i, j = _cat_index()
    if reverse:
        return i <= j, i < j
    return i >= j, i > j


def _inverse_unit_triangular(mats):
    i, j = _cat_index()
    inner = (i // INV_BASE) == (j // INV_BASE)
    eye = jnp.where(i == j, 1.0, 0.0)
    nd = [jnp.where(inner, n, 0.0) for n in mats]
    x = [eye - n for n in nd]
    p = [_cat_dot(n, n) for n in nd]
    k = 2
    while k < INV_BASE:
        x = [xi + _cat_dot(pi, xi) for xi, pi in zip(x, p)]
        k *= 2
        if k < INV_BASE:
            p = [_cat_dot(pi, pi) for pi in p]
    width = INV_BASE
    while width < CHUNK:
        outer = (i // (2 * width)) == (j // (2 * width))
        sel = jnp.logical_and(outer, jnp.logical_not(inner))
        t = [_cat_dot(jnp.where(sel, n, 0.0), xi) for n, xi in zip(mats, x)]
        x = [xi - _cat_dot(xi, ti) for xi, ti in zip(x, t)]
        inner = outer
        width *= 2
    return x


def _head_rows(gc, lane_onehot):
    hi, mid, lo = _split3(gc)
    t = _dot_nt(lane_onehot, hi) + _dot_nt(lane_onehot, mid) + _dot_nt(lane_onehot, lo)
    return jnp.concatenate([t[h:h + 1, :] for h in range(GROUP_HEADS)], axis=1)


INTRA_CHUNKS = SEQ_BLK // CHUNK
INTER_BATCH = 8
INTER_CHUNKS = 2
GL_ROWS = 8


def _dn_prep_block(slab_ref, conv_ref, alog_ref, dt_ref, i):
    hsum = _head_mean_matrix(1.0)
    col = lax.broadcasted_iota(jnp.int32, (128, GROUP), 0)
    lane = lax.broadcasted_iota(jnp.int32, (128, GROUP), 1) // HEAD_DIM
    neg_a = -jnp.exp(alog_ref[...])
    dtb = dt_ref[...]
    rows = pl.ds(pl.multiple_of(i * SEQ_BLK, SEQ_BLK), SEQ_BLK)
    qkv = []
    for j in range(3):
        prev, cur, nxt = _prev_cur_next(slab_ref, i, j * GROUP, (j + 1) * GROUP)
        w = conv_ref[:, j * GROUP:(j + 1) * GROUP]
        u = _silu(prev * w[0:1] + cur * w[1:2] + nxt * w[2:3])
        if j == 0:
            u = u * lax.rsqrt(_dot_split(u * u, hsum) + 1e-6) * (HEAD_DIM ** -0.5)
        elif j == 1:
            u = u * lax.rsqrt(_dot_split(u * u, hsum) + 1e-6)
        qkv.append(u)
    ba = slab_ref[rows, 4 * GROUP:4 * GROUP + 128]
    gc, beta = [], []
    for d in range(2):
        e_b = jnp.where(col == 8 * d + lane, 1.0, 0.0).astype(BF16)
        e_a = jnp.where(col == 8 * d + 4 + lane, 1.0, 0.0).astype(BF16)
        beta.append(_sigmoid(_dot_exact_rhs(ba, e_b)))
        g = neg_a[d:d + 1] * _softplus(_dot_exact_rhs(ba, e_a) + dtb[d:d + 1])
        gc.append(_chunk_cumsum(g, reverse=(d == 1)))
    return qkv[0], qkv[1], qkv[2], gc, beta


def _dn_intra_kernel(slab_ref, conv_ref, alog_ref, dt_ref, u_ref, w_ref, attn_ref, qd_ref, kd_ref, gl_ref):
    q_blk, k_blk, v_blk, gc_blk, beta_blk = _dn_prep_block(slab_ref, conv_ref, alog_ref, dt_ref, pl.program_id(1))
    onehot = jnp.where(
        lax.broadcasted_iota(jnp.int32, (8, GROUP), 1) == HEAD_DIM * lax.broadcasted_iota(jnp.int32, (8, GROUP), 0),
        1.0, 0.0).astype(BF16)
    masks = (_cat_masks(False), _cat_masks(True))
    chains = [(j, d) for j in range(INTRA_CHUNKS) for d in range(2)]
    rows = [slice(j * CHUNK, (j + 1) * CHUNK) for j, d in chains]
    gc = [gc_blk[d][r] for (j, d), r in zip(chains, rows)]
    beta = [beta_blk[d][r] for (j, d), r in zip(chains, rows)]
    q = [q_blk[r] for r in rows]
    k = [k_blk[r] for r in rows]
    v = [v_blk[r] for r in rows]
    eg = [jnp.exp(g) for g in gc]
    g_last = [g[0:1, :] if d == 1 else g[CHUNK - 1:CHUNK, :] for (j, d), g in zip(chains, gc)]
    kb = [a * b for a, b in zip(k, beta)]
    k_e = [_expand_heads(a.astype(BF16)) for a in k]
    dec = []
    for (j, d), g in zip(chains, gc):
        incl = masks[d][0]
        dec.append(jnp.where(incl, jnp.exp(jnp.where(incl, g - _head_rows(g, onehot), 0.0)), 0.0))
    m = [jnp.where(masks[d][1], _dot_nt(a.astype(BF16), ke) * dc, 0.0)
         for (j, d), a, ke, dc in zip(chains, kb, k_e, dec)]
    attn = [_dot_nt(a.astype(BF16), ke) * dc for a, ke, dc in zip(q, k_e, dec)]
    rhs = [jnp.concatenate([_expand_heads((a * b).astype(BF16)), _expand_heads((c * e).astype(BF16))], axis=1)
           for a, b, c, e in zip(v, beta, kb, eg)]
    x = _inverse_unit_triangular(m)
    sol = [_dot(xi.astype(BF16), r) for xi, r in zip(x, rhs)]
    for i, ((j, d), r) in enumerate(zip(chains, rows)):
        u_ref[d, 0, r, :] = sol[i][:, 0:GROUP]
        w_ref[d, 0, r, :] = sol[i][:, GROUP:2 * GROUP].astype(BF16)
        attn_ref[d, 0, r, :] = attn[i].astype(BF16)
        qd_ref[d, 0, r, :] = (q[i] * eg[i]).astype(BF16)
        kd_ref[d, 0, r, :] = (k[i] * jnp.exp(g_last[i] - gc[i])).astype(BF16)
        gl_ref[d, 0, pl.ds(j * GL_ROWS, GL_ROWS), :] = jnp.broadcast_to(jnp.exp(g_last[i]), (GL_ROWS, GROUP))


def _same_head_mask():
    r = lax.broadcasted_iota(jnp.int32, (GROUP, GROUP), 0) // HEAD_DIM
    c = lax.broadcasted_iota(jnp.int32, (GROUP, GROUP), 1) // HEAD_DIM
    return r == c


def _dn_inter_kernel(uf, wf, af, qf, kf, gf, ub, wb, ab, qb, kb, gb, of_ref, ob_ref, s_ref):
    @pl.when(pl.program_id(1) == 0)
    def _():
        s_ref[...] = jnp.zeros_like(s_ref)

    same = _same_head_mask()
    ins = ((uf, wf, af, qf, kf, gf, of_ref), (ub, wb, ab, qb, kb, gb, ob_ref))
    chains = [(d, j) for j in range(INTER_BATCH) for d in range(2)]
    s = [s_ref[d * INTER_BATCH + j] for d, j in chains]
    for t in range(INTER_CHUNKS):
        rows = [_inter_rows(t, d == 1, CHUNK) for d, j in chains]
        grow = [_inter_rows(t, d == 1, GL_ROWS) for d, j in chains]
        s_b = [x.astype(BF16) for x in s]
        v_new = [(ins[d][0][0, j, r, :] - _dot(ins[d][1][0, j, r, :], sb)).astype(BF16)
                 for (d, j), r, sb in zip(chains, rows, s_b)]
        v_ne = [_expand_heads(x) for x in v_new]
        o = [_dot(ins[d][3][0, j, r, :], sb) + _dot(ins[d][2][0, j, r, :], ve)
             for (d, j), r, sb, ve in zip(chains, rows, s_b, v_ne)]
        upd = [_dot_tn(ins[d][4][0, j, r, :], x) for (d, j), r, x in zip(chains, rows, v_new)]
        for i, (d, j) in enumerate(chains):
            ins[d][6][j, rows[i], :] = o[i]
        s = [s[i] * ins[d][5][0, j, grow[i], :][0:1] + jnp.where(same, upd[i], 0.0) for i, (d, j) in enumerate(chains)]
    for i, (d, j) in enumerate(chains):
        s_ref[d * INTER_BATCH + j] = s[i]


def _inter_block(n, reverse):
    if not reverse:
        return n
    return _chunk_of_step(INTER_CHUNKS * n + INTER_CHUNKS - 1, True) // INTER_CHUNKS


def _inter_rows(t, reverse, rows_per_chunk):
    j = INTER_CHUNKS - 1 - t if reverse else t
    return slice(j * rows_per_chunk, (j + 1) * rows_per_chunk)


def deltanet_mixer(slab, conv_w, a_log, dt_bias):
    nb = slab.shape[0] // T_ALL
    assert nb % INTER_BATCH == 0 and CTX_CHUNKS % INTER_CHUNKS == 0 and N_CHUNK % INTER_CHUNKS == 0
    lanes = lambda t: jnp.repeat(t, HEAD_DIM, axis=-1)
    seq = lambda dt: jax.ShapeDtypeStruct((nb, T_ALL, GROUP), dt)
    seq2 = lambda dt: jax.ShapeDtypeStruct((2, nb, T_ALL, GROUP), dt)
    p2 = pl.BlockSpec((2, 1, SEQ_BLK, GROUP), lambda b, i: (0, b, i, 0))
    small = lambda shape: pl.BlockSpec(shape, lambda b, i: (0, 0))
    gl_shape = jax.ShapeDtypeStruct((2, nb, N_CHUNK * GL_ROWS, GROUP), F32)
    u, w, attn, qd, kd, gl = pl.pallas_call(
        _dn_intra_kernel,
        grid=(nb, N_BLK),
        in_specs=[pl.BlockSpec((T_ALL, DN_W), lambda b, i: (b, 0)), small((3, 3 * GROUP)),
                  small((2, GROUP)), small((2, GROUP))],
        out_specs=[p2, p2, p2, p2, p2,
                   pl.BlockSpec((2, 1, INTRA_CHUNKS * GL_ROWS, GROUP), lambda b, i: (0, b, i, 0))],
        out_shape=[seq2(F32), seq2(BF16), seq2(BF16), seq2(BF16), seq2(BF16), gl_shape],
        compiler_params=_cparams("parallel", "parallel"),
        name="deltanet_intra",
    )(slab, conv_w, lanes(a_log), lanes(dt_bias))

    def per_dir(d, rows_per_chunk):
        return pl.BlockSpec((1, INTER_BATCH, INTER_CHUNKS * rows_per_chunk, GROUP),
                            lambda b, n: (d, b, _inter_block(n, d == 1), 0))

    def out_dir(d):
        return pl.BlockSpec((INTER_BATCH, INTER_CHUNKS * CHUNK, GROUP), lambda b, n: (b, _inter_block(n, d == 1), 0))

    specs = [per_dir(d, r) for d in range(2) for r in (CHUNK,) * 5 + (GL_ROWS,)]
    o_f, o_b = pl.pallas_call(
        _dn_inter_kernel,
        grid=(nb // INTER_BATCH, N_CHUNK // INTER_CHUNKS),
        in_specs=specs,
        out_specs=[out_dir(0), out_dir(1)],
        out_shape=[seq(F32), seq(F32)],
        scratch_shapes=[pltpu.VMEM((2 * INTER_BATCH, GROUP, GROUP), F32)],
        compiler_params=_cparams("parallel", "arbitrary"),
        name="deltanet_inter",
    )(u, w, attn, qd, kd, gl, u, w, attn, qd, kd, gl)

    return o_f, o_b


RW_LR = RW_DECAY_RANK + RW_AAA_RANK + RW_GATE_RANK
RW_LR_OUT = 5 * GROUP


def _dot3(a, b_hi, b_lo):
    a_hi = a.astype(BF16)
    a_lo = (a - a_hi.astype(F32)).astype(BF16)
    return _dot(a_hi, b_hi) + (_dot(a_lo, b_hi) + _dot(a_hi, b_lo))


def rwkv_lowrank_weights(w_up, a_up, g_up):
    w = jnp.zeros((RW_LR, RW_LR_OUT), F32)
    o1 = RW_DECAY_RANK
    o2 = o1 + RW_AAA_RANK
    for d in range(2):
        w = w.at[0:o1, d * GROUP:(d + 1) * GROUP].set(w_up[d])
        w = w.at[o1:o2, (2 + d) * GROUP:(3 + d) * GROUP].set(a_up[d])
    return w.at[o2:RW_LR, 4 * GROUP:5 * GROUP].set(g_up)


def _rw_intra_kernel(slab_ref, mu_ref, pv_ref, wlr_ref,
                     at_ref, rt_ref, bg_ref, kg_ref, gl_ref, v_ref, bonus_ref, g_ref,
                     xc_ref, arb_ref, rhs0_ref, yk_ref):
    i = pl.program_id(1)
    hsum = _head_mean_matrix(1.0)
    wlr = wlr_ref[...]
    wlr_hi = wlr.astype(BF16)
    wlr_lo = (wlr - wlr_hi.astype(F32)).astype(BF16)
    pv = pv_ref[...]
    w0 = (pv[0:1], pv[1:2])
    a0 = (pv[2:3], pv[3:4])
    k_k, k_a, r_k = pv[4:5], pv[5:6], pv[6:7]
    lr_lane = lax.broadcasted_iota(jnp.int32, (1, RW_LR), 1)

    def shifted(c0, c1):
        prev, cur, nxt = _prev_cur_next(slab_ref, i, c0, c1)
        return cur + mu_ref[0:1, c0:c1] * (prev - cur) + mu_ref[1:2, c0:c1] * (nxt - cur)

    r = shifted(0, GROUP)
    k = shifted(GROUP, 2 * GROUP)
    v = shifted(2 * GROUP, 3 * GROUP)
    lr = shifted(3 * GROUP, 3 * GROUP + RW_LR)
    t = jnp.where(lr_lane < RW_DECAY_RANK, jnp.tanh(lr),
                  jnp.where(lr_lane < RW_DECAY_RANK + RW_AAA_RANK, lr, _sigmoid(lr)))
    proj = _dot3(t, wlr_hi, wlr_lo)
    kq = k * k_k
    kk = kq * lax.rsqrt(_dot_split(kq * kq, hsum) + 1e-6)
    v_blk = v.astype(BF16)
    v_ref[0] = v_blk
    g_ref[...] = proj[:, 4 * GROUP:5 * GROUP]
    ksum = jnp.zeros_like(k)
    at_blk, rt_blk, bh_blk, kh_blk = [], [], [], []
    for d in range(2):
        w_log = -_softplus(-(w0[d] + proj[:, d * GROUP:(d + 1) * GROUP])) - 0.5
        lw = -jnp.exp(w_log)
        a_gate = _sigmoid(a0[d] + proj[:, (2 + d) * GROUP:(3 + d) * GROUP])
        k_d = k * (1.0 + (a_gate - 1.0) * k_a)
        ksum = ksum + k_d
        cum = _chunk_cumsum(lw, reverse=(d == 1))
        ends = [cum[j * CHUNK:j * CHUNK + 1, :] if d == 1 else cum[(j + 1) * CHUNK - 1:(j + 1) * CHUNK, :]
                for j in range(INTRA_CHUNKS)]
        to_end = jnp.exp(jnp.concatenate([jnp.broadcast_to(e, (CHUNK, GROUP)) for e in ends], axis=0) - cum)
        inv = jnp.exp(-cum)
        b = kk * a_gate
        at_blk.append((-kk * jnp.exp(cum - lw)).astype(BF16))
        rt_blk.append((r * jnp.exp(cum)).astype(BF16))
        bh_blk.append((b * inv).astype(BF16))
        kh_blk.append((k_d * inv).astype(BF16))
        at_ref[d, 0] = at_blk[d]
        rt_ref[d, 0] = rt_blk[d]
        bg_ref[d, 0] = (b * to_end).astype(BF16)
        kg_ref[d, 0] = (k_d * to_end).astype(BF16)
        gl_ref[d, 0] = jnp.concatenate([jnp.broadcast_to(jnp.exp(e), (GL_ROWS, GROUP)) for e in ends], axis=0)
    bonus_ref[...] = _dot_split(r * ksum * r_k, hsum) * v

    masks = (_cat_masks(False), _cat_masks(True))
    chains = [(j, d) for j in range(INTRA_CHUNKS) for d in range(2)]
    rows = [slice(j * CHUNK, (j + 1) * CHUNK) for j, d in chains]
    at = [at_blk[d][r_] for (j, d), r_ in zip(chains, rows)]
    rt = [rt_blk[d][r_] for (j, d), r_ in zip(chains, rows)]
    bh_e = [_expand_heads(bh_blk[d][r_]) for (j, d), r_ in zip(chains, rows)]
    kh_e = [_expand_heads(kh_blk[d][r_]) for (j, d), r_ in zip(chains, rows)]
    v_e = [_expand_heads(v_blk[r_]) for r_ in rows]
    x = _inverse_unit_triangular(
        [jnp.where(masks[d][1], -_dot_nt(a, b_), 0.0) for (j, d), a, b_ in zip(chains, at, bh_e)])
    a_ak = [jnp.where(masks[d][1], _dot_nt(a, b_), 0.0).astype(BF16) for (j, d), a, b_ in zip(chains, at, kh_e)]
    a_rb = [jnp.where(masks[d][0], _dot_nt(a, b_), 0.0).astype(BF16) for (j, d), a, b_ in zip(chains, rt, bh_e)]
    a_rk = [jnp.where(masks[d][0], _dot_nt(a, b_), 0.0).astype(BF16) for (j, d), a, b_ in zip(chains, rt, kh_e)]
    rhs0 = [_dot(a, ve) for a, ve in zip(a_ak, v_e)]
    yk = [_dot(a, ve) for a, ve in zip(a_rk, v_e)]
    for n_, ((j, d), r_) in enumerate(zip(chains, rows)):
        xc_ref[d, 0, r_, :] = x[n_].astype(BF16)
        arb_ref[d, 0, r_, :] = a_rb[n_]
        rhs0_ref[d, 0, r_, :] = rhs0[n_]
        yk_ref[d, 0, r_, :] = yk[n_]


def _rw_inter_kernel(*refs):
    n_in = 10
    fwd, bwd = refs[0:n_in], refs[n_in:2 * n_in]
    yf_ref, yb_ref, s_ref = refs[2 * n_in:]

    @pl.when(pl.program_id(1) == 0)
    def _():
        s_ref[...] = jnp.zeros_like(s_ref)

    same = _same_head_mask()
    ins = (fwd, bwd)
    outs = (yf_ref, yb_ref)
    chains = [(d, j) for j in range(INTER_BATCH) for d in range(2)]

    s = [s_ref[d * INTER_BATCH + j] for d, j in chains]
    for t in range(INTER_CHUNKS):
        rows = [_inter_rows(t, d == 1, CHUNK) for d, j in chains]

        def arg(idx):
            return [ins[d][idx][0, j, r, :] for (d, j), r in zip(chains, rows)]

        at, rt, bg, kg, xc, arb, rhs0, yk = (arg(i) for i in range(8))
        gamma = [ins[d][8][0, j, _inter_rows(t, d == 1, GL_ROWS), :][0:1] for d, j in chains]
        v = [ins[d][9][j, r, :] for (d, j), r in zip(chains, rows)]
        s_b = [x.astype(BF16) for x in s]
        rhs = [_expand_heads((_dot_nt(a, sb) + r0).astype(BF16)) for a, sb, r0 in zip(at, s_b, rhs0)]
        sa = [_dot(x, r).astype(BF16) for x, r in zip(xc, rhs)]
        sa_e = [_expand_heads(x) for x in sa]
        y = [_dot_nt(r, sb) + _dot(a, se) + y0 for r, sb, a, se, y0 in zip(rt, s_b, arb, sa_e, yk)]
        upd = [_dot_tn(a, b) + _dot_tn(c, e) for a, b, c, e in zip(sa, bg, v, kg)]
        for i, (d, j) in enumerate(chains):
            outs[d][j, rows[i], :] = y[i]
        s = [s[i] * gamma[i] + jnp.where(same, upd[i], 0.0) for i in range(len(chains))]
    for i, (d, j) in enumerate(chains):
        s_ref[d * INTER_BATCH + j] = s[i]


def rwkv_mixer(slab, mu, w0, w_up, a0, a_up, g_up, k_k, k_a, r_k, ln_w, ln_b):
    n = slab.shape[0]
    nb = n // T_ALL
    assert nb % INTER_BATCH == 0 and CTX_CHUNKS % INTER_CHUNKS == 0 and N_CHUNK % INTER_CHUNKS == 0
    pv = jnp.concatenate([w0, a0, k_k[None], k_a[None], r_k.reshape(1, GROUP), ln_w[None], ln_b[None],
                          jnp.zeros((7, GROUP), F32)], axis=0)
    seq = lambda dt: jax.ShapeDtypeStruct((nb, T_ALL, GROUP), dt)
    seq2 = lambda dt: jax.ShapeDtypeStruct((2, nb, T_ALL, GROUP), dt)
    flat = jax.ShapeDtypeStruct((n, GROUP), F32)
    gl_shape = jax.ShapeDtypeStruct((2, nb, N_CHUNK * GL_ROWS, GROUP), F32)
    p1 = pl.BlockSpec((1, SEQ_BLK, GROUP), lambda b, i: (b, i, 0))
    p2 = pl.BlockSpec((2, 1, SEQ_BLK, GROUP), lambda b, i: (0, b, i, 0))
    pflat = pl.BlockSpec((SEQ_BLK, GROUP), lambda b, i: (b * N_BLK + i, 0))
    pgl = pl.BlockSpec((2, 1, INTRA_CHUNKS * GL_ROWS, GROUP), lambda b, i: (0, b, i, 0))
    small = lambda shape: pl.BlockSpec(shape, lambda b, i: (0, 0))
    at, rt, bg, kg, gl, v, bonus, g, xc, arb, rhs0, yk = pl.pallas_call(
        _rw_intra_kernel,
        grid=(nb, N_BLK),
        in_specs=[pl.BlockSpec((T_ALL, RW_W), lambda b, i: (b, 0)), small((2, RW_W)), small((16, GROUP)),
                  small((RW_LR, RW_LR_OUT))],
        out_specs=[p2] * 4 + [pgl, p1, pflat, pflat] + [p2] * 4,
        out_shape=[seq2(BF16)] * 4 + [gl_shape, seq(BF16), flat, flat, seq2(BF16), seq2(BF16), seq2(F32), seq2(F32)],
        compiler_params=_cparams("parallel", "parallel"),
        name="rwkv7_intra",
    )(slab, mu, pv, rwkv_lowrank_weights(w_up, a_up, g_up))

    def per_dir(d, rows_per_chunk):
        return pl.BlockSpec((1, INTER_BATCH, INTER_CHUNKS * rows_per_chunk, GROUP),
                            lambda b, n_: (d, b, _inter_block(n_, d == 1), 0))

    def shared(d):
        return pl.BlockSpec((INTER_BATCH, INTER_CHUNKS * CHUNK, GROUP), lambda b, n_: (b, _inter_block(n_, d == 1), 0))

    specs = [s for d in range(2) for s in [per_dir(d, CHUNK)] * 8 + [per_dir(d, GL_ROWS), shared(d)]]
    per = (at, rt, bg, kg, xc, arb, rhs0, yk, gl, v)
    y_f, y_b = pl.pallas_call(
        _rw_inter_kernel,
        grid=(nb // INTER_BATCH, N_CHUNK // INTER_CHUNKS),
        in_specs=specs,
        out_specs=[shared(0), shared(1)],
        out_shape=[seq(F32), seq(F32)],
        scratch_shapes=[pltpu.VMEM((2 * INTER_BATCH, GROUP, GROUP), F32)],
        compiler_params=_cparams("parallel", "arbitrary"),
        name="rwkv7_inter",
    )(*per, *per)

    return y_f, y_b, bonus, g, pv


DFT_SPLIT = 64
DFT_BLK = 256


def _dft_tables(n):
    big = 2 * n
    t = np.arange(n, dtype=np.int64)[:, None]
    k1 = np.arange(n // DFT_SPLIT, dtype=np.int64)[None, :]
    k2 = np.arange(DFT_SPLIT, dtype=np.int64)[None, :]
    alpha = 2.0 * np.pi * ((DFT_SPLIT * t * k1) % big) / big
    beta = 2.0 * np.pi * ((t * k2) % big) / big

    def pad(a):
        out = np.zeros((n, 128), np.float32)
        out[:, :a.shape[1]] = a
        return out

    return np.stack([pad(np.cos(alpha)), pad(np.sin(alpha)), pad(np.cos(beta)), pad(np.sin(beta))])


def _dft_gen_kernel(n, tab_ref, g_ref):
    k = lax.broadcasted_iota(jnp.int32, (128, n), 1)
    row = lax.broadcasted_iota(jnp.int32, (128, n), 0)
    e_a = jnp.where(k // DFT_SPLIT == row, 1.0, 0.0).astype(BF16)
    e_b = jnp.where(jnp.logical_and(k % DFT_SPLIT == row, row < DFT_SPLIT), 1.0, 0.0).astype(BF16)
    ca = _dot_split(tab_ref[0], e_a)
    sa = _dot_split(tab_ref[1], e_a)
    cb = _dot_split(tab_ref[2], e_b)
    sb = _dot_split(tab_ref[3], e_b)
    g_ref[:, 0:n] = (ca * cb - sa * sb).astype(BF16)
    g_ref[:, n:2 * n] = (-(sa * cb + ca * sb)).astype(BF16)


def dft_matrix(n):
    blk = min(DFT_BLK, n)
    return pl.pallas_call(
        functools.partial(_dft_gen_kernel, n),
        grid=(n // blk,),
        in_specs=[pl.BlockSpec((4, blk, 128), lambda i: (0, i, 0))],
        out_specs=pl.BlockSpec((blk, 2 * n), lambda i: (i, 0)),
        out_shape=jax.ShapeDtypeStruct((n, 2 * n), BF16),
        compiler_params=_cparams("parallel"),
        name=f"dft_matrix_{n}",
    )(jnp.asarray(_dft_tables(n)))


HY_COLS_F = 2 * HY_ORDER * GROUP
HY_OC = HY_ORDER * GROUP


def _hyena_filter_kernel(n, z_ref, w1_ref, b1_ref, w2_ref, b2_ref, w3_ref, freq_ref, dl_ref, hs_ref, hd_ref):
    blk = min(SEQ_BLK, n)
    freq = freq_ref[...]
    dl = dl_ref[...]

    def fill(i, norm):
        r0 = pl.multiple_of(i * blk, blk)
        z = z_ref[pl.ds(r0, blk), :]
        hid = jnp.sin(freq * (_dot_hi(z, w1_ref[...]) + b1_ref[...]))
        hid = jnp.sin(freq * (_dot_hi(hid, w2_ref[...]) + b2_ref[...]))
        t = z[:, 0:1]
        h = _dot_hi(hid, w3_ref[...]) * jnp.exp(-t * dl)
        lag = lax.broadcasted_iota(jnp.int32, (blk, 1), 0) + r0
        hf = h[:, 0:HY_OC]
        hb = jnp.where(lag == 0, 0.0, h[:, HY_OC:2 * HY_OC])
        hs_ref[pl.ds(r0, blk), :] = hf + hb
        hd_ref[pl.ds(r0, blk), :] = hf - hb
        return norm + jnp.sum(jnp.abs(hf) + jnp.abs(hb), axis=0, keepdims=True)

    norm = lax.fori_loop(0, n // blk, fill, jnp.zeros((1, HY_OC), F32))
    inv = 1.0 / norm

    def scale(i, c):
        rows = pl.ds(pl.multiple_of(i * blk, blk), blk)
        hs_ref[rows, :] = hs_ref[rows, :] * inv
        hd_ref[rows, :] = hd_ref[rows, :] * inv
        return c

    lax.fori_loop(0, n // blk, scale, 0)


def hyena_filter_taps(n, f_w1, f_b1, f_w2, f_b2, f_w3, f_freq):
    f32 = np.float32
    t = np.linspace(0.0, 1.0, n, dtype=f32)[:, None]
    ang = (f32(2.0 * math.pi) * np.arange(n, dtype=f32)[:, None] / f32(n)).astype(f32)
    bands = np.linspace(1e-4, HY_BANDS - 1, HY_BANDS, dtype=f32)[None]
    arg = (bands * ang).astype(f32)
    z = np.concatenate([t, np.cos(arg).astype(f32), -np.sin(arg).astype(f32)], axis=-1)
    emb = z.shape[1]
    z = jnp.asarray(np.pad(z, ((0, 0), (0, 128 - emb))))
    w1 = jnp.pad(f_w1, ((0, 128 - emb), (0, 0)))
    max_decay = math.log(HY_TARGET) / HY_SHORT_DECAY_PCT
    min_decay = math.log(HY_TARGET) / HY_LONG_DECAY_PCT
    deltas = np.abs(np.linspace(min_decay, max_decay, HY_OC, dtype=f32))
    dl = jnp.asarray(np.tile(deltas, 2).reshape(1, HY_COLS_F))
    hid = f_w2.shape[0]
    out = jax.ShapeDtypeStruct((n, HY_OC), F32)
    return pl.pallas_call(
        functools.partial(_hyena_filter_kernel, n),
        out_shape=[out, out],
        compiler_params=pltpu.CompilerParams(vmem_limit_bytes=VMEM_LIMIT),
        name=f"hyena_filter_{n}",
    )(z, w1, f_b1.reshape(1, hid), f_w2, f_b2.reshape(1, hid), f_w3, f_freq.reshape(1, hid), dl)


def _hyena_spectrum_kernel(n, g_ref, hs_ref, hd_ref, kr_ref, ki_ref, kn_ref):
    blk = min(2 * SEQ_BLK, n)
    big = 2.0 * n

    def split(ref):
        x = ref[...]
        hi = x.astype(BF16)
        return hi, (x - hi.astype(F32)).astype(BF16)

    s_hi, s_lo = split(hs_ref)
    d_hi, d_lo = split(hd_ref)

    def body(i, c):
        r0 = pl.multiple_of(i * blk, blk)
        rows = pl.ds(r0, blk)
        k = lax.broadcasted_iota(jnp.int32, (blk, 1), 0) + r0
        wgt = jnp.where(k == 0, 1.0 / big, 2.0 / big)
        gc = g_ref[rows, 0:n]
        gs = g_ref[rows, n:2 * n]
        kr_ref[rows, :] = (_dot(gc, s_hi) + _dot(gc, s_lo)) * wgt
        ki_ref[rows, :] = (_dot(gs, d_hi) + _dot(gs, d_lo)) * wgt
        return c

    lax.fori_loop(0, n // blk, body, 0)
    t = lax.broadcasted_iota(jnp.int32, (n, 1), 0)
    sign = jnp.where(t % 2 == 0, 1.0, -1.0)
    kn_ref[...] = jnp.broadcast_to(jnp.sum(sign * hs_ref[...], axis=0, keepdims=True) * (1.0 / big), (8, HY_OC))


def hyena_spectrum(n, g, hs, hd):
    out = jax.ShapeDtypeStruct((n, HY_OC), F32)
    return pl.pallas_call(
        functools.partial(_hyena_spectrum_kernel, n),
        out_shape=[out, out, jax.ShapeDtypeStruct((8, HY_OC), F32)],
        compiler_params=pltpu.CompilerParams(vmem_limit_bytes=VMEM_LIMIT),
        name=f"hyena_spectrum_{n}",
    )(g, hs, hd)


def _hyena_conv_kernel(slab_ref, w_ref, o_ref):
    def body(i, c):
        rows = pl.ds(pl.multiple_of(i * SEQ_BLK, SEQ_BLK), SEQ_BLK)
        for j in range(3):
            prev, cur, nxt = _prev_cur_next(slab_ref, i, j * GROUP, (j + 1) * GROUP)
            w = w_ref[:, j * GROUP:(j + 1) * GROUP]
            o_ref[rows, j * GROUP:(j + 1) * GROUP] = prev * w[0:1] + cur * w[1:2] + nxt * w[2:3]
        return c

    lax.fori_loop(0, N_BLK, body, 0)


def hyena_short_conv(slab, conv_w):
    n = slab.shape[0]
    return pl.pallas_call(
        _hyena_conv_kernel,
        grid=(n // T_ALL,),
        in_specs=[pl.BlockSpec((T_ALL, 3 * GROUP), lambda b: (b, 0)),
                  pl.BlockSpec((3, 3 * GROUP), lambda b: (0, 0))],
        out_specs=pl.BlockSpec((T_ALL, 3 * GROUP), lambda b: (b, 0)),
        out_shape=jax.ShapeDtypeStruct((n, 3 * GROUP), F32),
        compiler_params=_cparams("parallel"),
        name="hyena_short_conv",
    )(slab, conv_w)


HY_FBLK = 1024


def _alt_sign(n):
    t = lax.broadcasted_iota(jnp.int32, (n, 1), 0)
    return jnp.where(t % 2 == 0, 1.0, -1.0)


def _hyena_fwd_kernel(x_ref, gl_ref, gc_ref, krl_ref, kil_ref, knl_ref, krc_ref, kic_ref, knc_ref,
                      pl_ref, pc_ref, pn_ref):
    def transform(x, g_ref, kr_ref, ki_ref, kn_ref, p_ref, n, blk):
        xb = x.astype(BF16)

        def body(i, c):
            rows = pl.ds(pl.multiple_of(i * blk, blk), blk)
            zr = _dot(g_ref[rows, 0:n], xb)
            zi = _dot(g_ref[rows, n:2 * n], xb)
            kr = kr_ref[rows, :]
            ki = ki_ref[rows, :]
            p_ref[0, 0, rows, :] = (zr * kr - zi * ki).astype(BF16)
            p_ref[0, 1, rows, :] = (zr * ki + zi * kr).astype(BF16)
            return c

        lax.fori_loop(0, n // blk, body, 0)
        return jnp.sum(_alt_sign(n) * x, axis=0, keepdims=True) * kn_ref[0:1, :]

    nyq_c = transform(x_ref[0:CTX_LEN, :], gc_ref, krc_ref, kic_ref, knc_ref, pc_ref, CTX_LEN, CTX_LEN)
    nyq_l = transform(x_ref[CTX_LEN:T_ALL, :], gl_ref, krl_ref, kil_ref, knl_ref, pl_ref, SEQ, HY_FBLK)
    pn_ref[0] = jnp.concatenate([nyq_l, nyq_c, jnp.zeros((6, GROUP), F32)], axis=0)


def _resident(shape):
    return pl.BlockSpec(shape, lambda b: (0,) * len(shape))


def hyena_forward_transform(x, col, g_l, g_c, spec_l, spec_c, order):
    n = x.shape[0]
    nb = n // T_ALL
    kcol = lambda shape: pl.BlockSpec(shape, lambda b: (0, order))
    return pl.pallas_call(
        _hyena_fwd_kernel,
        grid=(nb,),
        in_specs=[
            pl.BlockSpec((T_ALL, GROUP), lambda b: (b, col)),
            _resident((SEQ, 2 * SEQ)), _resident((CTX_LEN, 2 * CTX_LEN)),
            kcol((SEQ, GROUP)), kcol((SEQ, GROUP)), kcol((8, GROUP)),
            kcol((CTX_LEN, GROUP)), kcol((CTX_LEN, GROUP)), kcol((8, GROUP)),
        ],
        out_specs=[
            pl.BlockSpec((1, 2, SEQ, GROUP), lambda b: (b, 0, 0, 0)),
            pl.BlockSpec((1, 2, CTX_LEN, GROUP), lambda b: (b, 0, 0, 0)),
            pl.BlockSpec((1, 8, GROUP), lambda b: (b, 0, 0)),
        ],
        out_shape=[
            jax.ShapeDtypeStruct((nb, 2, SEQ, GROUP), BF16),
            jax.ShapeDtypeStruct((nb, 2, CTX_LEN, GROUP), BF16),
            jax.ShapeDtypeStruct((nb, 8, GROUP), F32),
        ],
        compiler_params=_cparams("parallel"),
        name=f"hyena_fwd_{order}",
    )(x, g_l, g_c, *spec_l, *spec_c)


def _hyena_inv_kernel(pl_ref, pc_ref, pn_ref, gl_ref, gc_ref, u_ref, gate_ref, bias_ref, o_ref):
    bias = bias_ref[0]

    def inverse(p_ref, nyq, g_ref, n, blk, off):
        pr = p_ref[0, 0]
        pi = p_ref[0, 1]

        def body(i, c):
            r0 = pl.multiple_of(i * blk, blk)
            rows = pl.ds(r0, blk)
            orow = pl.ds(pl.multiple_of(off + r0, math.gcd(blk, CTX_LEN)), blk)
            t = lax.broadcasted_iota(jnp.int32, (blk, 1), 0)
            sign = jnp.where(t % 2 == 0, 1.0, -1.0)
            y = _dot(g_ref[rows, 0:n], pr) + _dot(g_ref[rows, n:2 * n], pi) + sign * nyq
            o_ref[orow, :] = gate_ref[orow, :] * (y + u_ref[orow, :] * bias)
            return c

        lax.fori_loop(0, n // blk, body, 0)

    inverse(pc_ref, pn_ref[0, 1:2, :], gc_ref, CTX_LEN, CTX_LEN, 0)
    inverse(pl_ref, pn_ref[0, 0:1, :], gl_ref, SEQ, HY_FBLK, CTX_LEN)


def hyena_inverse_transform(p_l, p_c, p_n, g_l, g_c, u, ucol, gate, gcol, bias):
    nb = p_l.shape[0]
    return pl.pallas_call(
        _hyena_inv_kernel,
        grid=(nb,),
        in_specs=[
            pl.BlockSpec((1, 2, SEQ, GROUP), lambda b: (b, 0, 0, 0)),
            pl.BlockSpec((1, 2, CTX_LEN, GROUP), lambda b: (b, 0, 0, 0)),
            pl.BlockSpec((1, 8, GROUP), lambda b: (b, 0, 0)),
            _resident((SEQ, 2 * SEQ)), _resident((CTX_LEN, 2 * CTX_LEN)),
            pl.BlockSpec((T_ALL, GROUP), lambda b: (b, ucol)),
            pl.BlockSpec((T_ALL, GROUP), lambda b: (b, gcol)),
            pl.BlockSpec((1, 1, GROUP), lambda b: (0, 0, 0)),
        ],
        out_specs=pl.BlockSpec((T_ALL, GROUP), lambda b: (b, 0)),
        out_shape=jax.ShapeDtypeStruct((nb * T_ALL, GROUP), F32),
        compiler_params=_cparams("parallel"),
        name="hyena_inv",
    )(p_l, p_c, p_n, g_l, g_c, u, gate, bias.reshape(1, 1, GROUP))


def hyena_mixer(slab, g_l, g_c, conv_w, f_w1, f_b1, f_w2, f_b2, f_w3, f_freq, bias):
    u = hyena_short_conv(slab, conv_w)
    spec_l = hyena_spectrum(SEQ, g_l, *hyena_filter_taps(SEQ, f_w1, f_b1, f_w2, f_b2, f_w3, f_freq))
    spec_c = hyena_spectrum(CTX_LEN, g_c, *hyena_filter_taps(CTX_LEN, f_w1, f_b1, f_w2, f_b2, f_w3, f_freq))
    p = hyena_forward_transform(u, 0, g_l, g_c, spec_l, spec_c, 0)
    z = hyena_inverse_transform(*p, g_l, g_c, u, 0, u, 1, bias[0])
    p = hyena_forward_transform(z, 0, g_l, g_c, spec_l, spec_c, 1)
    return hyena_inverse_transform(*p, g_l, g_c, z, 0, u, 2, bias[1])


def kernel(x, c, ctx, c_ctx, w_mod, b_mod, norm_w, ffn_w_gu, ffn_w_down, w_in, w_out,
           hy_conv, hy_f_w1, hy_f_b1, hy_f_w2, hy_f_b2, hy_f_w3, hy_f_freq, hy_bias,
           na_q_norm, na_k_norm, na_rpb, dn_conv, dn_a_log, dn_dt_bias, dn_norm,
           rw_mu, rw_w0, rw_w_up, rw_a0, rw_a_up, rw_g_up, rw_k_k, rw_k_a, rw_r_k, rw_ln_w, rw_ln_b):
    nb = x.shape[0]
    assert x.shape[1:] == (SEQ, D_MODEL) and ctx.shape[1:] == (CTX_LEN, D_MODEL) and nb + 1 <= 16
    s = None
    cond = jnp.concatenate([c_ctx[None], c, jnp.zeros((15 - nb, D_MODEL), F32)], axis=0)
    mod = modulation_all(cond, w_mod, b_mod).reshape(DEPTH, 16, N_MOD, D_MODEL)
    g_l = dft_matrix(SEQ)
    g_c = dft_matrix(CTX_LEN)
    w_gu = ffn_w_gu.astype(BF16)
    w_down = ffn_w_down.astype(BF16)
    w_out_b = w_out.astype(BF16)
    dn_end = 6 * GROUP + 4 * GROUP + 4 * GROUP_HEADS
    w_in_p = jnp.concatenate(
        [w_in[:, :, :dn_end], jnp.zeros((DEPTH, D_MODEL, 6 * GROUP + DN_W - dn_end), F32), w_in[:, :, dn_end:]],
        axis=2).astype(BF16)
    for l in range(DEPTH):
        need_ctx = l < DEPTH - 1
        modc = mod[l, 0:1]
        modb = mod[l, 1:1 + nb]
        s = ffn_half_step(s, modc, modb, norm_w[l, 0], w_gu, w_down, l, 0, 0, parts=(x, ctx) if l == 0 else None)
        hy_s, na_s, dn_s, rw_s = input_projection(s, modc, modb, norm_w[l, 1], w_in_p, l)
        hy_g = hyena_mixer(hy_s, g_l, g_c, hy_conv[l], hy_f_w1[l], hy_f_b1[l], hy_f_w2[l], hy_f_b2[l], hy_f_w3[l],
                           hy_f_freq[l], hy_bias[l])
        na_g = na_mixer(na_s, na_q_norm[l], na_k_norm[l], na_bias_table(na_rpb[l]), need_ctx)
        dn_parts = deltanet_mixer(dn_s, dn_conv[l], dn_a_log[l], dn_dt_bias[l]) + (dn_s, dn_norm[l])
        rw_parts = rwkv_mixer(rw_s, rw_mu[l], rw_w0[l], rw_w_up[l], rw_a0[l], rw_a_up[l], rw_g_up[l], rw_k_k[l],
                              rw_k_a[l], rw_r_k[l], rw_ln_w[l], rw_ln_b[l])
        s = mixer_output_ffn(s, modc, modb, norm_w[l, 2], hy_g, na_g, dn_parts, rw_parts, w_out_b, w_gu, w_down, l,
                             latent_only=not need_ctx)
    return s.reshape(nb, SEQ, D_MODEL)
```

```python
import functools
import math

import numpy as np
import jax
import jax.numpy as jnp
from jax import lax
from jax.experimental import pallas as pl
from jax.experimental.pallas import tpu as pltpu

D_MODEL = 1024
SEQ = 2048
DEPTH = 2
CTX_LEN = 256
T_ALL = CTX_LEN + SEQ
GRID_W = 64
GROUP = 256
HEAD_DIM = 64
GROUP_HEADS = 4
D_FF = 2816
N_MOD = 9
NORM_EPS = 1e-6

HY_ORDER = 2
HY_BANDS = 16
HY_TARGET = 1e-2
HY_SHORT_DECAY_PCT = 0.3
HY_LONG_DECAY_PCT = 1.5

NA_WIN_ROWS = 8
NA_WIN_COLS = 16

CHUNK = 64
RW_DECAY_RANK = 32
RW_AAA_RANK = 32
RW_GATE_RANK = 64
RW_LN_EPS = 64e-5

DN_W = 4 * GROUP + 128
RW_W = 3 * GROUP + 128
P_PAD = 3 * GROUP + 3 * GROUP + DN_W + RW_W

TM = 768
TF = 512
VMEM_LIMIT = 56 * 1024 * 1024

F32 = jnp.float32
BF16 = jnp.bfloat16


def _cparams(*sem):
    return pltpu.CompilerParams(dimension_semantics=sem, vmem_limit_bytes=VMEM_LIMIT)


def _silu(x):
    return x * (1.0 / (1.0 + jnp.exp(-x)))


def _sigmoid(x):
    return 1.0 / (1.0 + jnp.exp(-x))


def _softplus(x):
    return jnp.maximum(x, 0.0) + jnp.log(1.0 + jnp.exp(-jnp.abs(x)))


def _dot(a, b):
    return jnp.dot(a, b, preferred_element_type=F32)


def _dot_nt(a, b):
    return lax.dot_general(a, b, (((1,), (1,)), ((), ())), preferred_element_type=F32)


def _dot_tn(a, b):
    return lax.dot_general(a, b, (((0,), (0,)), ((), ())), preferred_element_type=F32)


def _dot_hi(a, b):
    return jnp.dot(a, b, preferred_element_type=F32, precision=lax.Precision.HIGHEST)


def _mod_kernel(cond_ref, w_ref, b_ref, o_ref):
    a = _silu(cond_ref[...]).astype(BF16)
    o_ref[0] = _dot(a, w_ref[0].astype(BF16)) + b_ref[0]


def modulation_all(cond, w_mod, b_mod):
    r = cond.shape[0]
    tn = 1024
    return pl.pallas_call(
        _mod_kernel,
        grid=(DEPTH, N_MOD * D_MODEL // tn),
        in_specs=[
            pl.BlockSpec((r, D_MODEL), lambda l, j: (0, 0)),
            pl.BlockSpec((1, D_MODEL, tn), lambda l, j: (l, 0, j)),
            pl.BlockSpec((1, 1, tn), lambda l, j: (l, 0, j)),
        ],
        out_specs=pl.BlockSpec((1, r, tn), lambda l, j: (l, 0, j)),
        out_shape=jax.ShapeDtypeStruct((DEPTH, r, N_MOD * D_MODEL), F32),
        compiler_params=_cparams("parallel", "parallel"),
        name="modulation",
    )(cond, w_mod, b_mod.reshape(DEPTH, 1, N_MOD * D_MODEL))


TL = 512


def _row_mod(modc_ref, modb_ref, tile, idx, latent_only=False):
    if latent_only:
        return modb_ref[0, idx:idx + 1, :]
    row = lax.broadcasted_iota(jnp.int32, (TM, 1), 0) + (tile % (T_ALL // TM)) * TM
    return jnp.where(row < CTX_LEN, modc_ref[0, idx:idx + 1, :], modb_ref[0, idx:idx + 1, :])


def _adaln(x, nw, shift, scale):
    y = x * lax.rsqrt(jnp.mean(x * x, axis=-1, keepdims=True) + NORM_EPS)
    return y * nw * (1.0 + scale) + shift


def _ffn_body(sub, latent_only, i, x, modc_ref, modb_ref, nw_ref, wgu_ref, wd_ref):
    shift = _row_mod(modc_ref, modb_ref, i, 3 * sub, latent_only)
    scale = _row_mod(modc_ref, modb_ref, i, 3 * sub + 1, latent_only)
    h = _adaln(x, nw_ref[...], shift, scale).astype(BF16)
    acc = None
    for c0 in range(0, D_FF, TF):
        c1 = min(c0 + TF, D_FF)
        a = (_silu(_dot(h, wgu_ref[:, c0:c1])) * _dot(h, wgu_ref[:, D_FF + c0:D_FF + c1])).astype(BF16)
        part = _dot(a, wd_ref[c0:c1, :])
        acc = part if acc is None else acc + part
    gate = _row_mod(modc_ref, modb_ref, i, 3 * sub + 2, latent_only)
    return x + 0.5 * gate * acc


def _ffn_kernel(sub, latent_only, x_ref, modc_ref, modb_ref, nw_ref, wgu_ref, wd_ref, o_ref):
    o_ref[...] = _ffn_body(sub, latent_only, pl.program_id(0), x_ref[...], modc_ref, modb_ref, nw_ref, wgu_ref, wd_ref)


def _ffn_parts_kernel(sub, ctx_ref, xa_ref, xb_ref, modc_ref, modb_ref, nw_ref, wgu_ref, wd_ref, o_ref):
    i = pl.program_id(0)
    first = jnp.concatenate([ctx_ref[...], xa_ref[...]], axis=0)
    x = jnp.where(i % (T_ALL // TM) == 0, first, xb_ref[...])
    o_ref[...] = _ffn_body(sub, False, i, x, modc_ref, modb_ref, nw_ref, wgu_ref, wd_ref)


def ffn_half_step(x, modc, modb, nw, w_gu, w_down, layer, which, sub, parts=None):
    tiles_per_b = T_ALL // TM
    once = pl.Buffered(1)
    common = [
        pl.BlockSpec((1, N_MOD, D_MODEL), lambda i: (0, 0, 0)),
        pl.BlockSpec((1, N_MOD, D_MODEL), lambda i: (i // tiles_per_b, 0, 0)),
        pl.BlockSpec((1, D_MODEL), lambda i: (0, 0)),
        pl.BlockSpec((None, None, D_MODEL, 2 * D_FF), lambda i: (layer, which, 0, 0), pipeline_mode=once),
        pl.BlockSpec((None, None, D_FF, D_MODEL), lambda i: (layer, which, 0, 0), pipeline_mode=once),
    ]
    if parts is None:
        n = x.shape[0]
        body = functools.partial(_ffn_kernel, sub, False)
        data_specs = [pl.BlockSpec((TM, D_MODEL), lambda i: (i, 0))]
        data = (x,)
    else:
        lat, ctx = parts
        nb = lat.shape[0]
        n = nb * T_ALL
        head = TM - CTX_LEN

        def window(rows, start):
            return pl.BlockSpec((pl.Element(rows), pl.Element(D_MODEL)),
                                lambda i: (pl.multiple_of(start(i // tiles_per_b, i % tiles_per_b), CTX_LEN), 0))

        body = functools.partial(_ffn_parts_kernel, sub)
        data_specs = [
            pl.BlockSpec((CTX_LEN, D_MODEL), lambda i: (i // tiles_per_b, 0)),
            window(head, lambda b, t: b * SEQ),
            window(TM, lambda b, t: b * SEQ + head + (jnp.maximum(t, 1) - 1) * TM),
        ]
        lat2 = lat.reshape(nb * SEQ, D_MODEL)
        data = (ctx.reshape(nb * CTX_LEN, D_MODEL), lat2, lat2)
    return pl.pallas_call(
        body,
        grid=(n // TM,),
        in_specs=data_specs + common,
        out_specs=pl.BlockSpec((TM, D_MODEL), lambda i: (i, 0)),
        out_shape=jax.ShapeDtypeStruct((n, D_MODEL), F32),
        compiler_params=_cparams("parallel"),
        name=f"ffn{sub}",
    )(*data, modc, modb, nw.reshape(1, D_MODEL), w_gu, w_down)


def _inproj_kernel(x_ref, modc_ref, modb_ref, nw_ref, w_ref, hy_ref, na_ref, dn_ref, rw_ref):
    i = pl.program_id(0)
    shift = _row_mod(modc_ref, modb_ref, i, 3)
    scale = _row_mod(modc_ref, modb_ref, i, 4)
    h = _adaln(x_ref[...], nw_ref[...], shift, scale).astype(BF16)
    o0 = 3 * GROUP
    o1 = 6 * GROUP
    o2 = o1 + DN_W
    y = _dot(h, w_ref[...])
    hy_ref[...] = y[:, 0:o0]
    na_ref[...] = y[:, o0:o1]
    dn_ref[...] = y[:, o1:o2]
    rw_ref[...] = y[:, o2:P_PAD]


def input_projection(x, modc, modb, nw, w_in_p, layer):
    n = x.shape[0]
    tiles_per_b = T_ALL // TM
    widths = (3 * GROUP, 3 * GROUP, DN_W, RW_W)
    return pl.pallas_call(
        _inproj_kernel,
        grid=(n // TM,),
        in_specs=[
            pl.BlockSpec((TM, D_MODEL), lambda i: (i, 0)),
            pl.BlockSpec((1, N_MOD, D_MODEL), lambda i: (0, 0, 0)),
            pl.BlockSpec((1, N_MOD, D_MODEL), lambda i: (i // tiles_per_b, 0, 0)),
            pl.BlockSpec((1, D_MODEL), lambda i: (0, 0)),
            pl.BlockSpec((None, D_MODEL, P_PAD), lambda i: (layer, 0, 0)),
        ],
        out_specs=[pl.BlockSpec((TM, w), lambda i: (i, 0)) for w in widths],
        out_shape=[jax.ShapeDtypeStruct((n, w), F32) for w in widths],
        compiler_params=_cparams("parallel"),
        name="inproj",
    )(x, modc, modb, nw.reshape(1, D_MODEL), w_in_p)


def _mix_ffn_kernel(latent_only, x_ref, modc_ref, modb_ref, nw_ref, hy_ref, na_ref,
                    of_ref, ob_ref, z_ref, dnw_ref, yf_ref, yb_ref, bonus_ref, g_ref, pv_ref,
                    wo_ref, wgu_ref, wd_ref, o_ref):
    i = pl.program_id(0)
    hmean = _head_mean_matrix(1.0 / HEAD_DIM)
    o = of_ref[...] + ob_ref[...]
    g_dn = o * lax.rsqrt(_dot_split(o * o, hmean) + NORM_EPS) * dnw_ref[...] * _silu(z_ref[...])
    y = yf_ref[...] + yb_ref[...]
    yc = y - _dot_split(y, hmean)
    yn = yc * lax.rsqrt(_dot_split(yc * yc, hmean) + RW_LN_EPS) * pv_ref[7:8, :] + pv_ref[8:9, :]
    g_rw = (yn + bonus_ref[...]) * g_ref[...]
    mix = _dot(hy_ref[...].astype(BF16), wo_ref[0:GROUP, :])
    mix += _dot(na_ref[...].astype(BF16), wo_ref[GROUP:2 * GROUP, :])
    mix += _dot(g_dn.astype(BF16), wo_ref[2 * GROUP:3 * GROUP, :])
    mix += _dot(g_rw.astype(BF16), wo_ref[3 * GROUP:4 * GROUP, :])
    x = x_ref[...] + _row_mod(modc_ref, modb_ref, i, 5, latent_only) * mix
    o_ref[...] = _ffn_body(2, latent_only, i, x, modc_ref, modb_ref, nw_ref, wgu_ref, wd_ref)


def mixer_output_ffn(x, modc, modb, nw, hy, na, dn, rw, w_out, w_gu, w_down, layer, latent_only):
    n = x.shape[0]
    nb = n // T_ALL
    o_f, o_b, dn_slab, dn_norm = dn
    y_f, y_b, bonus, gate, pv = rw
    flat = lambda t: t.reshape(n, GROUP)
    if latent_only:
        tm, tiles_per_b = TL, SEQ // TL
        rows = lambda i: pl.multiple_of((i // tiles_per_b) * T_ALL + CTX_LEN + (i % tiles_per_b) * TL,
                                        math.gcd(CTX_LEN, TL))
        win = lambda width, col=0: pl.BlockSpec((pl.Element(tm), pl.Element(width)), lambda i: (rows(i), col * width))
    else:
        tm, tiles_per_b = TM, T_ALL // TM
        win = lambda width, col=0: pl.BlockSpec((tm, width), lambda i: (i, col))
    n_out = nb * tiles_per_b * tm
    once = pl.Buffered(1)
    small = lambda shape: pl.BlockSpec(shape, lambda i: (0,) * len(shape))
    return pl.pallas_call(
        functools.partial(_mix_ffn_kernel, latent_only),
        grid=(n_out // tm,),
        in_specs=[
            win(D_MODEL), small((1, N_MOD, D_MODEL)),
            pl.BlockSpec((1, N_MOD, D_MODEL), lambda i: (i // tiles_per_b, 0, 0)), small((1, D_MODEL)),
            win(GROUP), win(GROUP),
            win(GROUP), win(GROUP), win(GROUP, 3), small((1, GROUP)),
            win(GROUP), win(GROUP), win(GROUP), win(GROUP), small((16, GROUP)),
            pl.BlockSpec((None, D_MODEL, D_MODEL), lambda i: (layer, 0, 0), pipeline_mode=once),
            pl.BlockSpec((None, None, D_MODEL, 2 * D_FF), lambda i: (layer, 1, 0, 0), pipeline_mode=once),
            pl.BlockSpec((None, None, D_FF, D_MODEL), lambda i: (layer, 1, 0, 0), pipeline_mode=once),
        ],
        out_specs=pl.BlockSpec((tm, D_MODEL), lambda i: (i, 0)),
        out_shape=jax.ShapeDtypeStruct((n_out, D_MODEL), F32),
        compiler_params=_cparams("parallel"),
        name="mix_ffn",
    )(x, modc, modb, nw.reshape(1, D_MODEL), hy, na, flat(o_f), flat(o_b), dn_slab,
      jnp.tile(dn_norm, GROUP_HEADS).reshape(1, GROUP), flat(y_f), flat(y_b), bonus, gate, pv,
      w_out, w_gu, w_down)


def _head_mean_matrix(scale):
    r = lax.broadcasted_iota(jnp.int32, (GROUP, GROUP), 0) // HEAD_DIM
    c = lax.broadcasted_iota(jnp.int32, (GROUP, GROUP), 1) // HEAD_DIM
    return jnp.where(r == c, scale, 0.0).astype(BF16)


def _dot_split(a, m_bf16):
    hi = a.astype(BF16)
    lo = (a - hi.astype(F32)).astype(BF16)
    return _dot(hi, m_bf16) + _dot(lo, m_bf16)


def _lane_head(width=GROUP):
    return lax.broadcasted_iota(jnp.int32, (1, width), 1) // HEAD_DIM


NA_ROWS = SEQ // GRID_W
NA_LOCAL = NA_WIN_ROWS * GRID_W
NA_NEG = -1e30
NA_BLK = 256
NA_PAIR = 8


def na_bias_table(rpb):
    n_dr = 2 * NA_WIN_ROWS
    rows = jnp.pad(rpb, ((0, 0), (0, 1), (0, 128 - rpb.shape[2]))).reshape(GROUP_HEADS * n_dr, 128)
    toep = pl.pallas_call(
        _na_bias_kernel,
        out_shape=jax.ShapeDtypeStruct((GROUP_HEADS * n_dr, GRID_W * GRID_W), F32),
        name="na_bias",
    )(rows).reshape(GROUP_HEADS, n_dr, GRID_W, GRID_W)
    tab = jnp.stack([toep[:, NA_WIN_ROWS - 1 - p:2 * NA_WIN_ROWS - 1 - p] for p in range(NA_WIN_ROWS)], axis=0)
    tab = jnp.transpose(tab, (0, 1, 3, 2, 4))
    return tab.reshape(NA_WIN_ROWS, GROUP_HEADS, GRID_W, NA_LOCAL)


def _na_bias_kernel(rpb_ref, o_ref):
    n = GRID_W * GRID_W
    d = lax.broadcasted_iota(jnp.int32, (128, n), 0)
    cj = lax.broadcasted_iota(jnp.int32, (128, n), 1)
    onehot = jnp.where((cj % GRID_W) - (cj // GRID_W) + NA_WIN_COLS - 1 == d, 1.0, 0.0).astype(BF16)
    cj1 = lax.broadcasted_iota(jnp.int32, (1, n), 1)
    c = cj1 // GRID_W
    j = cj1 % GRID_W
    start = jnp.clip(c - NA_WIN_COLS // 2, 0, GRID_W - NA_WIN_COLS)
    in_win = jnp.logical_and(j >= start, j < start + NA_WIN_COLS)
    o_ref[...] = jnp.where(in_win, _dot_exact_rhs(rpb_ref[...], onehot), NA_NEG)


def _na_kernel(need_ctx, slab_ref, qw_ref, kw_ref, bias_ref, o_ref, q_s, k_s, v_s):
    hm = _head_mean_matrix(1.0 / HEAD_DIM)
    qw = qw_ref[...] * (HEAD_DIM ** -0.5)
    kw = kw_ref[...]

    def prep(i, c):
        r0 = pl.multiple_of(i * NA_BLK, NA_BLK)
        q = slab_ref[pl.ds(r0, NA_BLK), 0:GROUP]
        k = slab_ref[pl.ds(r0, NA_BLK), GROUP:2 * GROUP]
        q_s[pl.ds(r0, NA_BLK), :] = (q * lax.rsqrt(_dot_split(q * q, hm) + NORM_EPS) * qw).astype(BF16)
        k_s[pl.ds(r0, NA_BLK), :] = (k * lax.rsqrt(_dot_split(k * k, hm) + NORM_EPS) * kw).astype(BF16)
        v_s[pl.ds(r0, NA_BLK), :] = slab_ref[pl.ds(r0, NA_BLK), 2 * GROUP:3 * GROUP].astype(BF16)
        return c

    lax.fori_loop(0, T_ALL // NA_BLK, prep, 0)

    lane_h = _lane_head()
    kc = k_s[0:CTX_LEN, :]
    vc = v_s[0:CTX_LEN, :]

    if need_ctx:
        qc = q_s[0:CTX_LEN, :]
        out = jnp.zeros((CTX_LEN, GROUP), F32)
        for h in range(GROUP_HEADS):
            mask = lane_h == h
            s = _dot_nt(jnp.where(mask, qc, jnp.zeros_like(qc)), kc)
            e = jnp.exp(s - jnp.max(s, axis=-1, keepdims=True))
            p = e * (1.0 / jnp.sum(e, axis=-1, keepdims=True))
            out = jnp.where(mask, _dot(p.astype(BF16), vc), out)
        o_ref[0:CTX_LEN, :] = out
    else:
        o_ref[0:CTX_LEN, :] = jnp.zeros((CTX_LEN, GROUP), F32)

    def pair_body(i, c):
        rows = [i * NA_PAIR + t for t in range(NA_PAIR)]
        start = [jnp.clip(r - NA_WIN_ROWS // 2, 0, NA_ROWS - NA_WIN_ROWS) for r in rows]
        q0 = [pl.multiple_of(CTX_LEN + r * GRID_W, GRID_W) for r in rows]
        k0 = [pl.multiple_of(CTX_LEN + s * GRID_W, GRID_W) for s in start]
        q = [_expand_heads(q_s[pl.ds(a, GRID_W), :]) for a in q0]
        kb = [k_s[pl.ds(a, NA_LOCAL), :] for a in k0]
        vb = [v_s[pl.ds(a, NA_LOCAL), :] for a in k0]
        bias = [bias_ref[r - s].reshape(GROUP_HEADS * GRID_W, NA_LOCAL) for r, s in zip(rows, start)]
        s_loc = [_dot_nt(q[t], kb[t]) + bias[t] for t in range(NA_PAIR)]
        s_ctx = [_dot_nt(q[t], kc) for t in range(NA_PAIR)]
        m = [jnp.maximum(jnp.max(a, axis=-1, keepdims=True), jnp.max(b, axis=-1, keepdims=True))
             for a, b in zip(s_loc, s_ctx)]
        e_loc = [jnp.exp(a - mm) for a, mm in zip(s_loc, m)]
        e_ctx = [jnp.exp(b - mm) for b, mm in zip(s_ctx, m)]
        inv = [1.0 / (jnp.sum(a, axis=-1, keepdims=True) + jnp.sum(b, axis=-1, keepdims=True))
               for a, b in zip(e_loc, e_ctx)]
        o = [_dot((e_loc[t] * inv[t]).astype(BF16), vb[t]) + _dot((e_ctx[t] * inv[t]).astype(BF16), vc)
             for t in range(NA_PAIR)]
        for t in range(NA_PAIR):
            out = o[t][0:GRID_W]
            for h in range(1, GROUP_HEADS):
                out = jnp.where(lane_h == h, o[t][h * GRID_W:(h + 1) * GRID_W], out)
            o_ref[pl.ds(q0[t], GRID_W), :] = out
        return c

    lax.fori_loop(0, NA_ROWS // NA_PAIR, pair_body, 0)


def na_mixer(slab, q_norm, k_norm, bias_tab, need_ctx):
    n = slab.shape[0]
    tile4 = lambda w: jnp.tile(w, GROUP_HEADS).reshape(1, GROUP)
    return pl.pallas_call(
        functools.partial(_na_kernel, need_ctx),
        grid=(n // T_ALL,),
        in_specs=[
            pl.BlockSpec((T_ALL, 3 * GROUP), lambda b: (b, 0)),
            pl.BlockSpec((1, GROUP), lambda b: (0, 0)),
            pl.BlockSpec((1, GROUP), lambda b: (0, 0)),
            pl.BlockSpec((NA_WIN_ROWS, GROUP_HEADS, GRID_W, NA_LOCAL), lambda b: (0, 0, 0, 0)),
        ],
        out_specs=pl.BlockSpec((T_ALL, GROUP), lambda b: (b, 0)),
        out_shape=jax.ShapeDtypeStruct((n, GROUP), F32),
        scratch_shapes=[pltpu.VMEM((T_ALL, GROUP), BF16)] * 3,
        compiler_params=_cparams("parallel"),
        name="na_mixer",
    )(slab, tile4(q_norm), tile4(k_norm), bias_tab)


SEQ_BLK = 256
N_BLK = T_ALL // SEQ_BLK
N_CHUNK = T_ALL // CHUNK
CTX_CHUNKS = CTX_LEN // CHUNK


def _prev_cur_next(ref, i, c0, c1):
    r0 = pl.multiple_of(i * SEQ_BLK, SEQ_BLK)
    cur = ref[pl.ds(r0, SEQ_BLK), c0:c1]
    up0 = pl.multiple_of(jnp.maximum(r0 - 8, 0), 8)
    dn0 = pl.multiple_of(jnp.minimum(r0 + SEQ_BLK, T_ALL - 8), 8)
    up = ref[pl.ds(up0, 8), c0:c1][7:8, :]
    dn = ref[pl.ds(dn0, 8), c0:c1][0:1, :]
    up = jnp.where(i >= 2, up, 0.0)
    dn = jnp.where(jnp.logical_and(i >= 1, i <= N_BLK - 2), dn, 0.0)
    row = lax.broadcasted_iota(jnp.int32, (SEQ_BLK, 1), 0)
    prev = jnp.where(row == 0, up, pltpu.roll(cur, 1, 0))
    nxt = jnp.where(row == SEQ_BLK - 1, dn, pltpu.roll(cur, SEQ_BLK - 1, 0))
    return prev, cur, nxt


def _chunk_cumsum(x, reverse):
    pos = lax.broadcasted_iota(jnp.int32, (SEQ_BLK, 1), 0) % CHUNK
    s = 1
    while s < CHUNK:
        if reverse:
            x = x + jnp.where(pos < CHUNK - s, pltpu.roll(x, SEQ_BLK - s, 0), 0.0)
        else:
            x = x + jnp.where(pos >= s, pltpu.roll(x, s, 0), 0.0)
        s *= 2
    return x


def _split3(a):
    hi = a.astype(BF16)
    r1 = a - hi.astype(F32)
    mid = r1.astype(BF16)
    lo = (r1 - mid.astype(F32)).astype(BF16)
    return hi, mid, lo


def _dot_exact_rhs(a, m_bf16):
    hi, mid, lo = _split3(a)
    return _dot(hi, m_bf16) + _dot(mid, m_bf16) + _dot(lo, m_bf16)


def _expand_heads(x):
    lane_h = _lane_head()
    return jnp.concatenate([jnp.where(lane_h == h, x, 0.0) for h in range(GROUP_HEADS)], axis=0)


def _chunk_of_step(n, reverse):
    if not reverse:
        return n
    return jnp.where(n < CTX_CHUNKS, CTX_CHUNKS - 1 - n, N_CHUNK + CTX_CHUNKS - 1 - n)


INV_BASE = 16


def _cat_dot(a, b):
    return _dot(a.astype(BF16), _expand_heads(b.astype(BF16)))


def _cat_index():
    i = lax.broadcasted_iota(jnp.int32, (CHUNK, GROUP_HEADS * CHUNK), 0)
    j = lax.broadcasted_iota(jnp.int32, (CHUNK, GROUP_HEADS * CHUNK), 1) % CHUNK
    return i, j


def _cat_masks(reverse):
    i, j = _cat_index()
    if reverse:
        return i <= j, i < j
    return i >= j, i > j


def _inverse_unit_triangular(mats):
    i, j = _cat_index()
    inner = (i // INV_BASE) == (j // INV_BASE)
    eye = jnp.where(i == j, 1.0, 0.0)
    nd = [jnp.where(inner, n, 0.0) for n in mats]
    x = [eye - n for n in nd]
    p = [_cat_dot(n, n) for n in nd]
    k = 2
    while k < INV_BASE:
        x = [xi + _cat_dot(pi, xi) for xi, pi in zip(x, p)]
        k *= 2
        if k < INV_BASE:
            p = [_cat_dot(pi, pi) for pi in p]
    width = INV_BASE
    while width < CHUNK:
        outer = (i // (2 * width)) == (j // (2 * width))
        sel = jnp.logical_and(outer, jnp.logical_not(inner))
        t = [_cat_dot(jnp.where(sel, n, 0.0), xi) for n, xi in zip(mats, x)]
        x = [xi - _cat_dot(xi, ti) for xi, ti in zip(x, t)]
        inner = outer
        width *= 2
    return x


def _head_rows(gc, lane_onehot):
    hi, mid, lo = _split3(gc)
    t = _dot_nt(lane_onehot, hi) + _dot_nt(lane_onehot, mid) + _dot_nt(lane_onehot, lo)
    return jnp.concatenate([t[h:h + 1, :] for h in range(GROUP_HEADS)], axis=1)


INTRA_CHUNKS = SEQ_BLK // CHUNK
INTER_BATCH = 8
INTER_CHUNKS = 2
GL_ROWS = 8


def _dn_prep_block(slab_ref, conv_ref, alog_ref, dt_ref, i):
    hsum = _head_mean_matrix(1.0)
    col = lax.broadcasted_iota(jnp.int32, (128, GROUP), 0)
    lane = lax.broadcasted_iota(jnp.int32, (128, GROUP), 1) // HEAD_DIM
    neg_a = -jnp.exp(alog_ref[...])
    dtb = dt_ref[...]
    rows = pl.ds(pl.multiple_of(i * SEQ_BLK, SEQ_BLK), SEQ_BLK)
    qkv = []
    for j in range(3):
        prev, cur, nxt = _prev_cur_next(slab_ref, i, j * GROUP, (j + 1) * GROUP)
        w = conv_ref[:, j * GROUP:(j + 1) * GROUP]
        u = _silu(prev * w[0:1] + cur * w[1:2] + nxt * w[2:3])
        if j == 0:
            u = u * lax.rsqrt(_dot_split(u * u, hsum) + 1e-6) * (HEAD_DIM ** -0.5)
        elif j == 1:
            u = u * lax.rsqrt(_dot_split(u * u, hsum) + 1e-6)
        qkv.append(u)
    ba = slab_ref[rows, 4 * GROUP:4 * GROUP + 128]
    gc, beta = [], []
    for d in range(2):
        e_b = jnp.where(col == 8 * d + lane, 1.0, 0.0).astype(BF16)
        e_a = jnp.where(col == 8 * d + 4 + lane, 1.0, 0.0).astype(BF16)
        beta.append(_sigmoid(_dot_exact_rhs(ba, e_b)))
        g = neg_a[d:d + 1] * _softplus(_dot_exact_rhs(ba, e_a) + dtb[d:d + 1])
        gc.append(_chunk_cumsum(g, reverse=(d == 1)))
    return qkv[0], qkv[1], qkv[2], gc, beta


def _dn_intra_kernel(slab_ref, conv_ref, alog_ref, dt_ref, u_ref, w_ref, attn_ref, qd_ref, kd_ref, gl_ref):
    q_blk, k_blk, v_blk, gc_blk, beta_blk = _dn_prep_block(slab_ref, conv_ref, alog_ref, dt_ref, pl.program_id(1))
    onehot = jnp.where(
        lax.broadcasted_iota(jnp.int32, (8, GROUP), 1) == HEAD_DIM * lax.broadcasted_iota(jnp.int32, (8, GROUP), 0),
        1.0, 0.0).astype(BF16)
    masks = (_cat_masks(False), _cat_masks(True))
    chains = [(j, d) for j in range(INTRA_CHUNKS) for d in range(2)]
    rows = [slice(j * CHUNK, (j + 1) * CHUNK) for j, d in chains]
    gc = [gc_blk[d][r] for (j, d), r in zip(chains, rows)]
    beta = [beta_blk[d][r] for (j, d), r in zip(chains, rows)]
    q = [q_blk[r] for r in rows]
    k = [k_blk[r] for r in rows]
    v = [v_blk[r] for r in rows]
    eg = [jnp.exp(g) for g in gc]
    g_last = [g[0:1, :] if d == 1 else g[CHUNK - 1:CHUNK, :] for (j, d), g in zip(chains, gc)]
    kb = [a * b for a, b in zip(k, beta)]
    k_e = [_expand_heads(a.astype(BF16)) for a in k]
    dec = []
    for (j, d), g in zip(chains, gc):
        incl = masks[d][0]
        dec.append(jnp.where(incl, jnp.exp(jnp.where(incl, g - _head_rows(g, onehot), 0.0)), 0.0))
    m = [jnp.where(masks[d][1], _dot_nt(a.astype(BF16), ke) * dc, 0.0)
         for (j, d), a, ke, dc in zip(chains, kb, k_e, dec)]
    attn = [_dot_nt(a.astype(BF16), ke) * dc for a, ke, dc in zip(q, k_e, dec)]
    rhs = [jnp.concatenate([_expand_heads((a * b).astype(BF16)), _expand_heads((c * e).astype(BF16))], axis=1)
           for a, b, c, e in zip(v, beta, kb, eg)]
    x = _inverse_unit_triangular(m)
    sol = [_dot(xi.astype(BF16), r) for xi, r in zip(x, rhs)]
    for i, ((j, d), r) in enumerate(zip(chains, rows)):
        u_ref[d, 0, r, :] = sol[i][:, 0:GROUP]
        w_ref[d, 0, r, :] = sol[i][:, GROUP:2 * GROUP].astype(BF16)
        attn_ref[d, 0, r, :] = attn[i].astype(BF16)
        qd_ref[d, 0, r, :] = (q[i] * eg[i]).astype(BF16)
        kd_ref[d, 0, r, :] = (k[i] * jnp.exp(g_last[i] - gc[i])).astype(BF16)
        gl_ref[d, 0, pl.ds(j * GL_ROWS, GL_ROWS), :] = jnp.broadcast_to(jnp.exp(g_last[i]), (GL_ROWS, GROUP))


def _same_head_mask():
    r = lax.broadcasted_iota(jnp.int32, (GROUP, GROUP), 0) // HEAD_DIM
    c = lax.broadcasted_iota(jnp.int32, (GROUP, GROUP), 1) // HEAD_DIM
    return r == c


def _dn_inter_kernel(uf, wf, af, qf, kf, gf, ub, wb, ab, qb, kb, gb, of_ref, ob_ref, s_ref):
    @pl.when(pl.program_id(1) == 0)
    def _():
        s_ref[...] = jnp.zeros_like(s_ref)

    same = _same_head_mask()
    ins = ((uf, wf, af, qf, kf, gf, of_ref), (ub, wb, ab, qb, kb, gb, ob_ref))
    chains = [(d, j) for j in range(INTER_BATCH) for d in range(2)]
    s = [s_ref[d * INTER_BATCH + j] for d, j in chains]
    for t in range(INTER_CHUNKS):
        rows = [_inter_rows(t, d == 1, CHUNK) for d, j in chains]
        grow = [_inter_rows(t, d == 1, GL_ROWS) for d, j in chains]
        s_b = [x.astype(BF16) for x in s]
        v_new = [(ins[d][0][0, j, r, :] - _dot(ins[d][1][0, j, r, :], sb)).astype(BF16)
                 for (d, j), r, sb in zip(chains, rows, s_b)]
        v_ne = [_expand_heads(x) for x in v_new]
        o = [_dot(ins[d][3][0, j, r, :], sb) + _dot(ins[d][2][0, j, r, :], ve)
             for (d, j), r, sb, ve in zip(chains, rows, s_b, v_ne)]
        upd = [_dot_tn(ins[d][4][0, j, r, :], x) for (d, j), r, x in zip(chains, rows, v_new)]
        for i, (d, j) in enumerate(chains):
            ins[d][6][j, rows[i], :] = o[i]
        s = [s[i] * ins[d][5][0, j, grow[i], :][0:1] + jnp.where(same, upd[i], 0.0) for i, (d, j) in enumerate(chains)]
    for i, (d, j) in enumerate(chains):
        s_ref[d * INTER_BATCH + j] = s[i]


def _inter_block(n, reverse):
    if not reverse:
        return n
    return _chunk_of_step(INTER_CHUNKS * n + INTER_CHUNKS - 1, True) // INTER_CHUNKS


def _inter_rows(t, reverse, rows_per_chunk):
    j = INTER_CHUNKS - 1 - t if reverse else t
    return slice(j * rows_per_chunk, (j + 1) * rows_per_chunk)


def deltanet_mixer(slab, conv_w, a_log, dt_bias):
    nb = slab.shape[0] // T_ALL
    assert nb % INTER_BATCH == 0 and CTX_CHUNKS % INTER_CHUNKS == 0 and N_CHUNK % INTER_CHUNKS == 0
    lanes = lambda t: jnp.repeat(t, HEAD_DIM, axis=-1)
    seq = lambda dt: jax.ShapeDtypeStruct((nb, T_ALL, GROUP), dt)
    seq2 = lambda dt: jax.ShapeDtypeStruct((2, nb, T_ALL, GROUP), dt)
    p2 = pl.BlockSpec((2, 1, SEQ_BLK, GROUP), lambda b, i: (0, b, i, 0))
    small = lambda shape: pl.BlockSpec(shape, lambda b, i: (0, 0))
    gl_shape = jax.ShapeDtypeStruct((2, nb, N_CHUNK * GL_ROWS, GROUP), F32)
    u, w, attn, qd, kd, gl = pl.pallas_call(
        _dn_intra_kernel,
        grid=(nb, N_BLK),
        in_specs=[pl.BlockSpec((T_ALL, DN_W), lambda b, i: (b, 0)), small((3, 3 * GROUP)),
                  small((2, GROUP)), small((2, GROUP))],
        out_specs=[p2, p2, p2, p2, p2,
                   pl.BlockSpec((2, 1, INTRA_CHUNKS * GL_ROWS, GROUP), lambda b, i: (0, b, i, 0))],
        out_shape=[seq2(F32), seq2(BF16), seq2(BF16), seq2(BF16), seq2(BF16), gl_shape],
        compiler_params=_cparams("parallel", "parallel"),
        name="deltanet_intra",
    )(slab, conv_w, lanes(a_log), lanes(dt_bias))

    def per_dir(d, rows_per_chunk):
        return pl.BlockSpec((1, INTER_BATCH, INTER_CHUNKS * rows_per_chunk, GROUP),
                            lambda b, n: (d, b, _inter_block(n, d == 1), 0))

    def out_dir(d):
        return pl.BlockSpec((INTER_BATCH, INTER_CHUNKS * CHUNK, GROUP), lambda b, n: (b, _inter_block(n, d == 1), 0))

    specs = [per_dir(d, r) for d in range(2) for r in (CHUNK,) * 5 + (GL_ROWS,)]
    o_f, o_b = pl.pallas_call(
        _dn_inter_kernel,
        grid=(nb // INTER_BATCH, N_CHUNK // INTER_CHUNKS),
        in_specs=specs,
        out_specs=[out_dir(0), out_dir(1)],
        out_shape=[seq(F32), seq(F32)],
        scratch_shapes=[pltpu.VMEM((2 * INTER_BATCH, GROUP, GROUP), F32)],
        compiler_params=_cparams("parallel", "arbitrary"),
        name="deltanet_inter",
    )(u, w, attn, qd, kd, gl, u, w, attn, qd, kd, gl)

    return o_f, o_b


RW_LR = RW_DECAY_RANK + RW_AAA_RANK + RW_GATE_RANK
RW_LR_OUT = 5 * GROUP


def _dot3(a, b_hi, b_lo):
    a_hi = a.astype(BF16)
    a_lo = (a - a_hi.astype(F32)).astype(BF16)
    return _dot(a_hi, b_hi) + (_dot(a_lo, b_hi) + _dot(a_hi, b_lo))


def rwkv_lowrank_weights(w_up, a_up, g_up):
    w = jnp.zeros((RW_LR, RW_LR_OUT), F32)
    o1 = RW_DECAY_RANK
    o2 = o1 + RW_AAA_RANK
    for d in range(2):
        w = w.at[0:o1, d * GROUP:(d + 1) * GROUP].set(w_up[d])
        w = w.at[o1:o2, (2 + d) * GROUP:(3 + d) * GROUP].set(a_up[d])
    return w.at[o2:RW_LR, 4 * GROUP:5 * GROUP].set(g_up)


def _rw_intra_kernel(slab_ref, mu_ref, pv_ref, wlr_ref,
                     at_ref, rt_ref, bg_ref, kg_ref, gl_ref, v_ref, bonus_ref, g_ref,
                     xc_ref, arb_ref, rhs0_ref, yk_ref):
    i = pl.program_id(1)
    hsum = _head_mean_matrix(1.0)
    wlr = wlr_ref[...]
    wlr_hi = wlr.astype(BF16)
    wlr_lo = (wlr - wlr_hi.astype(F32)).astype(BF16)
    pv = pv_ref[...]
    w0 = (pv[0:1], pv[1:2])
    a0 = (pv[2:3], pv[3:4])
    k_k, k_a, r_k = pv[4:5], pv[5:6], pv[6:7]
    lr_lane = lax.broadcasted_iota(jnp.int32, (1, RW_LR), 1)

    def shifted(c0, c1):
        prev, cur, nxt = _prev_cur_next(slab_ref, i, c0, c1)
        return cur + mu_ref[0:1, c0:c1] * (prev - cur) + mu_ref[1:2, c0:c1] * (nxt - cur)

    r = shifted(0, GROUP)
    k = shifted(GROUP, 2 * GROUP)
    v = shifted(2 * GROUP, 3 * GROUP)
    lr = shifted(3 * GROUP, 3 * GROUP + RW_LR)
    t = jnp.where(lr_lane < RW_DECAY_RANK, jnp.tanh(lr),
                  jnp.where(lr_lane < RW_DECAY_RANK + RW_AAA_RANK, lr, _sigmoid(lr)))
    proj = _dot3(t, wlr_hi, wlr_lo)
    kq = k * k_k
    kk = kq * lax.rsqrt(_dot_split(kq * kq, hsum) + 1e-6)
    v_blk = v.astype(BF16)
    v_ref[0] = v_blk
    g_ref[...] = proj[:, 4 * GROUP:5 * GROUP]
    ksum = jnp.zeros_like(k)
    at_blk, rt_blk, bh_blk, kh_blk = [], [], [], []
    for d in range(2):
        w_log = -_softplus(-(w0[d] + proj[:, d * GROUP:(d + 1) * GROUP])) - 0.5
        lw = -jnp.exp(w_log)
        a_gate = _sigmoid(a0[d] + proj[:, (2 + d) * GROUP:(3 + d) * GROUP])
        k_d = k * (1.0 + (a_gate - 1.0) * k_a)
        ksum = ksum + k_d
        cum = _chunk_cumsum(lw, reverse=(d == 1))
        ends = [cum[j * CHUNK:j * CHUNK + 1, :] if d == 1 else cum[(j + 1) * CHUNK - 1:(j + 1) * CHUNK, :]
                for j in range(INTRA_CHUNKS)]
        to_end = jnp.exp(jnp.concatenate([jnp.broadcast_to(e, (CHUNK, GROUP)) for e in ends], axis=0) - cum)
        inv = jnp.exp(-cum)
        b = kk * a_gate
        at_blk.append((-kk * jnp.exp(cum - lw)).astype(BF16))
        rt_blk.append((r * jnp.exp(cum)).astype(BF16))
        bh_blk.append((b * inv).astype(BF16))
        kh_blk.append((k_d * inv).astype(BF16))
        at_ref[d, 0] = at_blk[d]
        rt_ref[d, 0] = rt_blk[d]
        bg_ref[d, 0] = (b * to_end).astype(BF16)
        kg_ref[d, 0] = (k_d * to_end).astype(BF16)
        gl_ref[d, 0] = jnp.concatenate([jnp.broadcast_to(jnp.exp(e), (GL_ROWS, GROUP)) for e in ends], axis=0)
    bonus_ref[...] = _dot_split(r * ksum * r_k, hsum) * v

    masks = (_cat_masks(False), _cat_masks(True))
    chains = [(j, d) for j in range(INTRA_CHUNKS) for d in range(2)]
    rows = [slice(j * CHUNK, (j + 1) * CHUNK) for j, d in chains]
    at = [at_blk[d][r_] for (j, d), r_ in zip(chains, rows)]
    rt = [rt_blk[d][r_] for (j, d), r_ in zip(chains, rows)]
    bh_e = [_expand_heads(bh_blk[d][r_]) for (j, d), r_ in zip(chains, rows)]
    kh_e = [_expand_heads(kh_blk[d][r_]) for (j, d), r_ in zip(chains, rows)]
    v_e = [_expand_heads(v_blk[r_]) for r_ in rows]
    x = _inverse_unit_triangular(
        [jnp.where(masks[d][1], -_dot_nt(a, b_), 0.0) for (j, d), a, b_ in zip(chains, at, bh_e)])
    a_ak = [jnp.where(masks[d][1], _dot_nt(a, b_), 0.0).astype(BF16) for (j, d), a, b_ in zip(chains, at, kh_e)]
    a_rb = [jnp.where(masks[d][0], _dot_nt(a, b_), 0.0).astype(BF16) for (j, d), a, b_ in zip(chains, rt, bh_e)]
    a_rk = [jnp.where(masks[d][0], _dot_nt(a, b_), 0.0).astype(BF16) for (j, d), a, b_ in zip(chains, rt, kh_e)]
    rhs0 = [_dot(a, ve) for a, ve in zip(a_ak, v_e)]
    yk = [_dot(a, ve) for a, ve in zip(a_rk, v_e)]
    for n_, ((j, d), r_) in enumerate(zip(chains, rows)):
        xc_ref[d, 0, r_, :] = x[n_].astype(BF16)
        arb_ref[d, 0, r_, :] = a_rb[n_]
        rhs0_ref[d, 0, r_, :] = rhs0[n_]
        yk_ref[d, 0, r_, :] = yk[n_]


def _rw_inter_kernel(*refs):
    n_in = 10
    fwd, bwd = refs[0:n_in], refs[n_in:2 * n_in]
    yf_ref, yb_ref, s_ref = refs[2 * n_in:]

    @pl.when(pl.program_id(1) == 0)
    def _():
        s_ref[...] = jnp.zeros_like(s_ref)

    same = _same_head_mask()
    ins = (fwd, bwd)
    outs = (yf_ref, yb_ref)
    chains = [(d, j) for j in range(INTER_BATCH) for d in range(2)]

    s = [s_ref[d * INTER_BATCH + j] for d, j in chains]
    for t in range(INTER_CHUNKS):
        rows = [_inter_rows(t, d == 1, CHUNK) for d, j in chains]

        def arg(idx):
            return [ins[d][idx][0, j, r, :] for (d, j), r in zip(chains, rows)]

        at, rt, bg, kg, xc, arb, rhs0, yk = (arg(i) for i in range(8))
        gamma = [ins[d][8][0, j, _inter_rows(t, d == 1, GL_ROWS), :][0:1] for d, j in chains]
        v = [ins[d][9][j, r, :] for (d, j), r in zip(chains, rows)]
        s_b = [x.astype(BF16) for x in s]
        rhs = [_expand_heads((_dot_nt(a, sb) + r0).astype(BF16)) for a, sb, r0 in zip(at, s_b, rhs0)]
        sa = [_dot(x, r).astype(BF16) for x, r in zip(xc, rhs)]
        sa_e = [_expand_heads(x) for x in sa]
        y = [_dot_nt(r, sb) + _dot(a, se) + y0 for r, sb, a, se, y0 in zip(rt, s_b, arb, sa_e, yk)]
        upd = [_dot_tn(a, b) + _dot_tn(c, e) for a, b, c, e in zip(sa, bg, v, kg)]
        for i, (d, j) in enumerate(chains):
            outs[d][j, rows[i], :] = y[i]
        s = [s[i] * gamma[i] + jnp.where(same, upd[i], 0.0) for i in range(len(chains))]
    for i, (d, j) in enumerate(chains):
        s_ref[d * INTER_BATCH + j] = s[i]


def rwkv_mixer(slab, mu, w0, w_up, a0, a_up, g_up, k_k, k_a, r_k, ln_w, ln_b):
    n = slab.shape[0]
    nb = n // T_ALL
    assert nb % INTER_BATCH == 0 and CTX_CHUNKS % INTER_CHUNKS == 0 and N_CHUNK % INTER_CHUNKS == 0
    pv = jnp.concatenate([w0, a0, k_k[None], k_a[None], r_k.reshape(1, GROUP), ln_w[None], ln_b[None],
                          jnp.zeros((7, GROUP), F32)], axis=0)
    seq = lambda dt: jax.ShapeDtypeStruct((nb, T_ALL, GROUP), dt)
    seq2 = lambda dt: jax.ShapeDtypeStruct((2, nb, T_ALL, GROUP), dt)
    flat = jax.ShapeDtypeStruct((n, GROUP), F32)
    gl_shape = jax.ShapeDtypeStruct((2, nb, N_CHUNK * GL_ROWS, GROUP), F32)
    p1 = pl.BlockSpec((1, SEQ_BLK, GROUP), lambda b, i: (b, i, 0))
    p2 = pl.BlockSpec((2, 1, SEQ_BLK, GROUP), lambda b, i: (0, b, i, 0))
    pflat = pl.BlockSpec((SEQ_BLK, GROUP), lambda b, i: (b * N_BLK + i, 0))
    pgl = pl.BlockSpec((2, 1, INTRA_CHUNKS * GL_ROWS, GROUP), lambda b, i: (0, b, i, 0))
    small = lambda shape: pl.BlockSpec(shape, lambda b, i: (0, 0))
    at, rt, bg, kg, gl, v, bonus, g, xc, arb, rhs0, yk = pl.pallas_call(
        _rw_intra_kernel,
        grid=(nb, N_BLK),
        in_specs=[pl.BlockSpec((T_ALL, RW_W), lambda b, i: (b, 0)), small((2, RW_W)), small((16, GROUP)),
                  small((RW_LR, RW_LR_OUT))],
        out_specs=[p2] * 4 + [pgl, p1, pflat, pflat] + [p2] * 4,
        out_shape=[seq2(BF16)] * 4 + [gl_shape, seq(BF16), flat, flat, seq2(BF16), seq2(BF16), seq2(F32), seq2(F32)],
        compiler_params=_cparams("parallel", "parallel"),
        name="rwkv7_intra",
    )(slab, mu, pv, rwkv_lowrank_weights(w_up, a_up, g_up))

    def per_dir(d, rows_per_chunk):
        return pl.BlockSpec((1, INTER_BATCH, INTER_CHUNKS * rows_per_chunk, GROUP),
                            lambda b, n_: (d, b, _inter_block(n_, d == 1), 0))

    def shared(d):
        return pl.BlockSpec((INTER_BATCH, INTER_CHUNKS * CHUNK, GROUP), lambda b, n_: (b, _inter_block(n_, d == 1), 0))

    specs = [s for d in range(2) for s in [per_dir(d, CHUNK)] * 8 + [per_dir(d, GL_ROWS), shared(d)]]
    per = (at, rt, bg, kg, xc, arb, rhs0, yk, gl, v)
    y_f, y_b = pl.pallas_call(
        _rw_inter_kernel,
        grid=(nb // INTER_BATCH, N_CHUNK // INTER_CHUNKS),
        in_specs=specs,
        out_specs=[shared(0), shared(1)],
        out_shape=[seq(F32), seq(F32)],
        scratch_shapes=[pltpu.VMEM((2 * INTER_BATCH, GROUP, GROUP), F32)],
        compiler_params=_cparams("parallel", "arbitrary"),
        name="rwkv7_inter",
    )(*per, *per)

    return y_f, y_b, bonus, g, pv


DFT_SPLIT = 64
DFT_BLK = 256


def _dft_tables(n):
    big = 2 * n
    t = np.arange(n, dtype=np.int64)[:, None]
    k1 = np.arange(n // DFT_SPLIT, dtype=np.int64)[None, :]
    k2 = np.arange(DFT_SPLIT, dtype=np.int64)[None, :]
    alpha = 2.0 * np.pi * ((DFT_SPLIT * t * k1) % big) / big
    beta = 2.0 * np.pi * ((t * k2) % big) / big

    def pad(a):
        out = np.zeros((n, 128), np.float32)
        out[:, :a.shape[1]] = a
        return out

    return np.stack([pad(np.cos(alpha)), pad(np.sin(alpha)), pad(np.cos(beta)), pad(np.sin(beta))])


def _dft_gen_kernel(n, tab_ref, g_ref):
    k = lax.broadcasted_iota(jnp.int32, (128, n), 1)
    row = lax.broadcasted_iota(jnp.int32, (128, n), 0)
    e_a = jnp.where(k // DFT_SPLIT == row, 1.0, 0.0).astype(BF16)
    e_b = jnp.where(jnp.logical_and(k % DFT_SPLIT == row, row < DFT_SPLIT), 1.0, 0.0).astype(BF16)
    ca = _dot_split(tab_ref[0], e_a)
    sa = _dot_split(tab_ref[1], e_a)
    cb = _dot_split(tab_ref[2], e_b)
    sb = _dot_split(tab_ref[3], e_b)
    g_ref[:, 0:n] = (ca * cb - sa * sb).astype(BF16)
    g_ref[:, n:2 * n] = (-(sa * cb + ca * sb)).astype(BF16)


def dft_matrix(n):
    blk = min(DFT_BLK, n)
    return pl.pallas_call(
        functools.partial(_dft_gen_kernel, n),
        grid=(n // blk,),
        in_specs=[pl.BlockSpec((4, blk, 128), lambda i: (0, i, 0))],
        out_specs=pl.BlockSpec((blk, 2 * n), lambda i: (i, 0)),
        out_shape=jax.ShapeDtypeStruct((n, 2 * n), BF16),
        compiler_params=_cparams("parallel"),
        name=f"dft_matrix_{n}",
    )(jnp.asarray(_dft_tables(n)))


HY_COLS_F = 2 * HY_ORDER * GROUP
HY_OC = HY_ORDER * GROUP


def _hyena_filter_kernel(n, z_ref, w1_ref, b1_ref, w2_ref, b2_ref, w3_ref, freq_ref, dl_ref, hs_ref, hd_ref):
    blk = min(SEQ_BLK, n)
    freq = freq_ref[...]
    dl = dl_ref[...]

    def fill(i, norm):
        r0 = pl.multiple_of(i * blk, blk)
        z = z_ref[pl.ds(r0, blk), :]
        hid = jnp.sin(freq * (_dot_hi(z, w1_ref[...]) + b1_ref[...]))
        hid = jnp.sin(freq * (_dot_hi(hid, w2_ref[...]) + b2_ref[...]))
        t = z[:, 0:1]
        h = _dot_hi(hid, w3_ref[...]) * jnp.exp(-t * dl)
        lag = lax.broadcasted_iota(jnp.int32, (blk, 1), 0) + r0
        hf = h[:, 0:HY_OC]
        hb = jnp.where(lag == 0, 0.0, h[:, HY_OC:2 * HY_OC])
        hs_ref[pl.ds(r0, blk), :] = hf + hb
        hd_ref[pl.ds(r0, blk), :] = hf - hb
        return norm + jnp.sum(jnp.abs(hf) + jnp.abs(hb), axis=0, keepdims=True)

    norm = lax.fori_loop(0, n // blk, fill, jnp.zeros((1, HY_OC), F32))
    inv = 1.0 / norm

    def scale(i, c):
        rows = pl.ds(pl.multiple_of(i * blk, blk), blk)
        hs_ref[rows, :] = hs_ref[rows, :] * inv
        hd_ref[rows, :] = hd_ref[rows, :] * inv
        return c

    lax.fori_loop(0, n // blk, scale, 0)


def hyena_filter_taps(n, f_w1, f_b1, f_w2, f_b2, f_w3, f_freq):
    f32 = np.float32
    t = np.linspace(0.0, 1.0, n, dtype=f32)[:, None]
    ang = (f32(2.0 * math.pi) * np.arange(n, dtype=f32)[:, None] / f32(n)).astype(f32)
    bands = np.linspace(1e-4, HY_BANDS - 1, HY_BANDS, dtype=f32)[None]
    arg = (bands * ang).astype(f32)
    z = np.concatenate([t, np.cos(arg).astype(f32), -np.sin(arg).astype(f32)], axis=-1)
    emb = z.shape[1]
    z = jnp.asarray(np.pad(z, ((0, 0), (0, 128 - emb))))
    w1 = jnp.pad(f_w1, ((0, 128 - emb), (0, 0)))
    max_decay = math.log(HY_TARGET) / HY_SHORT_DECAY_PCT
    min_decay = math.log(HY_TARGET) / HY_LONG_DECAY_PCT
    deltas = np.abs(np.linspace(min_decay, max_decay, HY_OC, dtype=f32))
    dl = jnp.asarray(np.tile(deltas, 2).reshape(1, HY_COLS_F))
    hid = f_w2.shape[0]
    out = jax.ShapeDtypeStruct((n, HY_OC), F32)
    return pl.pallas_call(
        functools.partial(_hyena_filter_kernel, n),
        out_shape=[out, out],
        compiler_params=pltpu.CompilerParams(vmem_limit_bytes=VMEM_LIMIT),
        name=f"hyena_filter_{n}",
    )(z, w1, f_b1.reshape(1, hid), f_w2, f_b2.reshape(1, hid), f_w3, f_freq.reshape(1, hid), dl)


def _hyena_spectrum_kernel(n, g_ref, hs_ref, hd_ref, kr_ref, ki_ref, kn_ref):
    blk = min(2 * SEQ_BLK, n)
    big = 2.0 * n

    def split(ref):
        x = ref[...]
        hi = x.astype(BF16)
        return hi, (x - hi.astype(F32)).astype(BF16)

    s_hi, s_lo = split(hs_ref)
    d_hi, d_lo = split(hd_ref)

    def body(i, c):
        r0 = pl.multiple_of(i * blk, blk)
        rows = pl.ds(r0, blk)
        k = lax.broadcasted_iota(jnp.int32, (blk, 1), 0) + r0
        wgt = jnp.where(k == 0, 1.0 / big, 2.0 / big)
        gc = g_ref[rows, 0:n]
        gs = g_ref[rows, n:2 * n]
        kr_ref[rows, :] = (_dot(gc, s_hi) + _dot(gc, s_lo)) * wgt
        ki_ref[rows, :] = (_dot(gs, d_hi) + _dot(gs, d_lo)) * wgt
        return c

    lax.fori_loop(0, n // blk, body, 0)
    t = lax.broadcasted_iota(jnp.int32, (n, 1), 0)
    sign = jnp.where(t % 2 == 0, 1.0, -1.0)
    kn_ref[...] = jnp.broadcast_to(jnp.sum(sign * hs_ref[...], axis=0, keepdims=True) * (1.0 / big), (8, HY_OC))


def hyena_spectrum(n, g, hs, hd):
    out = jax.ShapeDtypeStruct((n, HY_OC), F32)
    return pl.pallas_call(
        functools.partial(_hyena_spectrum_kernel, n),
        out_shape=[out, out, jax.ShapeDtypeStruct((8, HY_OC), F32)],
        compiler_params=pltpu.CompilerParams(vmem_limit_bytes=VMEM_LIMIT),
        name=f"hyena_spectrum_{n}",
    )(g, hs, hd)


def _short_conv_into(dst_ref, slab_ref, w_ref, col):
    def body(i, c):
        rows = pl.ds(pl.multiple_of(i * SEQ_BLK, SEQ_BLK), SEQ_BLK)
        prev, cur, nxt = _prev_cur_next(slab_ref, i, col * GROUP, (col + 1) * GROUP)
        w = w_ref[:, col * GROUP:(col + 1) * GROUP]
        dst_ref[rows, :] = prev * w[0:1] + cur * w[1:2] + nxt * w[2:3]
        return c

    lax.fori_loop(0, N_BLK, body, 0)


HY_FBLK = 2048
HY_IBLK = 1024


def _alt_sign(n):
    t = lax.broadcasted_iota(jnp.int32, (n, 1), 0)
    return jnp.where(t % 2 == 0, 1.0, -1.0)


def _hyena_fwd_kernel(from_slab, *refs):
    if from_slab:
        slab_ref, w_ref = refs[0:2]
        refs = refs[2:]
        x_ref = refs[-1]
        _short_conv_into(x_ref, slab_ref, w_ref, 0)
    else:
        x_ref = refs[0]
        refs = refs[1:]
    gl_ref, gc_ref, krl_ref, kil_ref, knl_ref, krc_ref, kic_ref, knc_ref, pl_ref, pc_ref, pn_ref = refs[0:11]

    def transform(x, g_ref, kr_ref, ki_ref, kn_ref, p_ref, n, blk):
        xb = x.astype(BF16)

        def body(i, c):
            rows = pl.ds(pl.multiple_of(i * blk, blk), blk)
            zr = _dot(g_ref[rows, 0:n], xb)
            zi = _dot(g_ref[rows, n:2 * n], xb)
            kr = kr_ref[rows, :]
            ki = ki_ref[rows, :]
            p_ref[0, 0, rows, :] = (zr * kr - zi * ki).astype(BF16)
            p_ref[0, 1, rows, :] = (zr * ki + zi * kr).astype(BF16)
            return c

        lax.fori_loop(0, n // blk, body, 0)
        return jnp.sum(_alt_sign(n) * x, axis=0, keepdims=True) * kn_ref[0:1, :]

    nyq_c = transform(x_ref[0:CTX_LEN, :], gc_ref, krc_ref, kic_ref, knc_ref, pc_ref, CTX_LEN, CTX_LEN)
    nyq_l = transform(x_ref[CTX_LEN:T_ALL, :], gl_ref, krl_ref, kil_ref, knl_ref, pl_ref, SEQ, HY_FBLK)
    pn_ref[0] = jnp.concatenate([nyq_l, nyq_c, jnp.zeros((6, GROUP), F32)], axis=0)


def _resident(shape):
    return pl.BlockSpec(shape, lambda b: (0,) * len(shape), pipeline_mode=pl.Buffered(1))


def hyena_forward_transform(x, conv_w, g_l, g_c, spec_l, spec_c, order):
    n = x.shape[0]
    nb = n // T_ALL
    kcol = lambda shape: pl.BlockSpec(shape, lambda b: (0, order))
    if conv_w is None:
        data_specs = [pl.BlockSpec((T_ALL, GROUP), lambda b: (b, 0))]
        data, scratch = (x,), []
    else:
        data_specs = [pl.BlockSpec((T_ALL, 3 * GROUP), lambda b: (b, 0)), _resident((3, 3 * GROUP))]
        data, scratch = (x, conv_w), [pltpu.VMEM((T_ALL, GROUP), F32)]
    return pl.pallas_call(
        functools.partial(_hyena_fwd_kernel, conv_w is not None),
        grid=(nb,),
        in_specs=data_specs + [
            _resident((SEQ, 2 * SEQ)), _resident((CTX_LEN, 2 * CTX_LEN)),
            kcol((SEQ, GROUP)), kcol((SEQ, GROUP)), kcol((8, GROUP)),
            kcol((CTX_LEN, GROUP)), kcol((CTX_LEN, GROUP)), kcol((8, GROUP)),
        ],
        out_specs=[
            pl.BlockSpec((1, 2, SEQ, GROUP), lambda b: (b, 0, 0, 0)),
            pl.BlockSpec((1, 2, CTX_LEN, GROUP), lambda b: (b, 0, 0, 0)),
            pl.BlockSpec((1, 8, GROUP), lambda b: (b, 0, 0)),
        ],
        out_shape=[
            jax.ShapeDtypeStruct((nb, 2, SEQ, GROUP), BF16),
            jax.ShapeDtypeStruct((nb, 2, CTX_LEN, GROUP), BF16),
            jax.ShapeDtypeStruct((nb, 8, GROUP), F32),
        ],
        scratch_shapes=scratch,
        compiler_params=_cparams("parallel"),
        name=f"hyena_fwd_{order}",
    )(*data, g_l, g_c, *spec_l, *spec_c)


def _hyena_inv_kernel(gate_col, u_given, pl_ref, pc_ref, pn_ref, gl_ref, gc_ref, slab_ref, w_ref, *refs):
    if u_given:
        u_ref, bias_ref, o_ref, gate_s = refs
    else:
        bias_ref, o_ref, gate_s, u_ref = refs
        _short_conv_into(u_ref, slab_ref, w_ref, 0)
    _short_conv_into(gate_s, slab_ref, w_ref, gate_col)
    bias = bias_ref[0]

    def inverse(p_ref, nyq, g_ref, n, blk, off):
        pr = p_ref[0, 0]
        pi = p_ref[0, 1]

        def body(i, c):
            r0 = pl.multiple_of(i * blk, blk)
            rows = pl.ds(r0, blk)
            orow = pl.ds(pl.multiple_of(off + r0, math.gcd(blk, CTX_LEN)), blk)
            t = lax.broadcasted_iota(jnp.int32, (blk, 1), 0)
            sign = jnp.where(t % 2 == 0, 1.0, -1.0)
            y = _dot(g_ref[rows, 0:n], pr) + _dot(g_ref[rows, n:2 * n], pi) + sign * nyq
            o_ref[orow, :] = gate_s[orow, :] * (y + u_ref[orow, :] * bias)
            return c

        lax.fori_loop(0, n // blk, body, 0)

    inverse(pc_ref, pn_ref[0, 1:2, :], gc_ref, CTX_LEN, CTX_LEN, 0)
    inverse(pl_ref, pn_ref[0, 0:1, :], gl_ref, SEQ, HY_IBLK, CTX_LEN)


def hyena_inverse_transform(p_l, p_c, p_n, g_l, g_c, slab, conv_w, gate_col, u, bias):
    nb = p_l.shape[0]
    seq = pltpu.VMEM((T_ALL, GROUP), F32)
    u_specs = [] if u is None else [pl.BlockSpec((T_ALL, GROUP), lambda b: (b, 0))]
    u_data = () if u is None else (u,)
    return pl.pallas_call(
        functools.partial(_hyena_inv_kernel, gate_col, u is not None),
        grid=(nb,),
        in_specs=[
            pl.BlockSpec((1, 2, SEQ, GROUP), lambda b: (b, 0, 0, 0)),
            pl.BlockSpec((1, 2, CTX_LEN, GROUP), lambda b: (b, 0, 0, 0)),
            pl.BlockSpec((1, 8, GROUP), lambda b: (b, 0, 0)),
            _resident((SEQ, 2 * SEQ)), _resident((CTX_LEN, 2 * CTX_LEN)),
            pl.BlockSpec((T_ALL, 3 * GROUP), lambda b: (b, 0)), _resident((3, 3 * GROUP)),
        ] + u_specs + [pl.BlockSpec((1, 1, GROUP), lambda b: (0, 0, 0))],
        out_specs=pl.BlockSpec((T_ALL, GROUP), lambda b: (b, 0)),
        out_shape=jax.ShapeDtypeStruct((nb * T_ALL, GROUP), F32),
        scratch_shapes=[seq] if u is not None else [seq, seq],
        compiler_params=_cparams("parallel"),
        name="hyena_inv",
    )(p_l, p_c, p_n, g_l, g_c, slab, conv_w, *u_data, bias.reshape(1, 1, GROUP))


def hyena_mixer(slab, g_l, g_c, conv_w, f_w1, f_b1, f_w2, f_b2, f_w3, f_freq, bias):
    spec_l = hyena_spectrum(SEQ, g_l, *hyena_filter_taps(SEQ, f_w1, f_b1, f_w2, f_b2, f_w3, f_freq))
    spec_c = hyena_spectrum(CTX_LEN, g_c, *hyena_filter_taps(CTX_LEN, f_w1, f_b1, f_w2, f_b2, f_w3, f_freq))
    p = hyena_forward_transform(slab, conv_w, g_l, g_c, spec_l, spec_c, 0)
    z = hyena_inverse_transform(*p, g_l, g_c, slab, conv_w, 1, None, bias[0])
    p = hyena_forward_transform(z, None, g_l, g_c, spec_l, spec_c, 1)
    return hyena_inverse_transform(*p, g_l, g_c, slab, conv_w, 2, z, bias[1])


def kernel(x, c, ctx, c_ctx, w_mod, b_mod, norm_w, ffn_w_gu, ffn_w_down, w_in, w_out,
           hy_conv, hy_f_w1, hy_f_b1, hy_f_w2, hy_f_b2, hy_f_w3, hy_f_freq, hy_bias,
           na_q_norm, na_k_norm, na_rpb, dn_conv, dn_a_log, dn_dt_bias, dn_norm,
           rw_mu, rw_w0, rw_w_up, rw_a0, rw_a_up, rw_g_up, rw_k_k, rw_k_a, rw_r_k, rw_ln_w, rw_ln_b):
    nb = x.shape[0]
    assert x.shape[1:] == (SEQ, D_MODEL) and ctx.shape[1:] == (CTX_LEN, D_MODEL) and nb + 1 <= 16
    s = None
    cond = jnp.concatenate([c_ctx[None], c, jnp.zeros((15 - nb, D_MODEL), F32)], axis=0)
    mod = modulation_all(cond, w_mod, b_mod).reshape(DEPTH, 16, N_MOD, D_MODEL)
    g_l = dft_matrix(SEQ)
    g_c = dft_matrix(CTX_LEN)
    w_gu = ffn_w_gu.astype(BF16)
    w_down = ffn_w_down.astype(BF16)
    w_out_b = w_out.astype(BF16)
    dn_end = 6 * GROUP + 4 * GROUP + 4 * GROUP_HEADS
    w_in_p = jnp.concatenate(
        [w_in[:, :, :dn_end], jnp.zeros((DEPTH, D_MODEL, 6 * GROUP + DN_W - dn_end), F32), w_in[:, :, dn_end:]],
        axis=2).astype(BF16)
    for l in range(DEPTH):
        need_ctx = l < DEPTH - 1
        modc = mod[l, 0:1]
        modb = mod[l, 1:1 + nb]
        s = ffn_half_step(s, modc, modb, norm_w[l, 0], w_gu, w_down, l, 0, 0, parts=(x, ctx) if l == 0 else None)
        hy_s, na_s, dn_s, rw_s = input_projection(s, modc, modb, norm_w[l, 1], w_in_p, l)
        hy_g = hyena_mixer(hy_s, g_l, g_c, hy_conv[l], hy_f_w1[l], hy_f_b1[l], hy_f_w2[l], hy_f_b2[l], hy_f_w3[l],
                           hy_f_freq[l], hy_bias[l])
        na_g = na_mixer(na_s, na_q_norm[l], na_k_norm[l], na_bias_table(na_rpb[l]), need_ctx)
        dn_parts = deltanet_mixer(dn_s, dn_conv[l], dn_a_log[l], dn_dt_bias[l]) + (dn_s, dn_norm[l])
        rw_parts = rwkv_mixer(rw_s, rw_mu[l], rw_w0[l], rw_w_up[l], rw_a0[l], rw_a_up[l], rw_g_up[l], rw_k_k[l],
                              rw_k_a[l], rw_r_k[l], rw_ln_w[l], rw_ln_b[l])
        s = mixer_output_ffn(s, modc, modb, norm_w[l, 2], hy_g, na_g, dn_parts, rw_parts, w_out_b, w_gu, w_down, l,
                             latent_only=not need_ctx)
    return s.reshape(nb, SEQ, D_MODEL)
```

```python
import functools
import math

import numpy as np
import jax
import jax.numpy as jnp
from jax import lax
from jax.experimental import pallas as pl
from jax.experimental.pallas import tpu as pltpu

D_MODEL = 1024
SEQ = 2048
DEPTH = 2
CTX_LEN = 256
T_ALL = CTX_LEN + SEQ
GRID_W = 64
GROUP = 256
HEAD_DIM = 64
GROUP_HEADS = 4
D_FF = 2816
N_MOD = 9
NORM_EPS = 1e-6

HY_ORDER = 2
HY_BANDS = 16
HY_TARGET = 1e-2
HY_SHORT_DECAY_PCT = 0.3
HY_LONG_DECAY_PCT = 1.5

NA_WIN_ROWS = 8
NA_WIN_COLS = 16

CHUNK = 64
RW_DECAY_RANK = 32
RW_AAA_RANK = 32
RW_GATE_RANK = 64
RW_LN_EPS = 64e-5

DN_W = 4 * GROUP + 128
RW_W = 3 * GROUP + 128
P_PAD = 3 * GROUP + 3 * GROUP + DN_W + RW_W

TM = 768
TF = 512
VMEM_LIMIT = 56 * 1024 * 1024

F32 = jnp.float32
BF16 = jnp.bfloat16


def _cparams(*sem):
    return pltpu.CompilerParams(dimension_semantics=sem, vmem_limit_bytes=VMEM_LIMIT)


def _silu(x):
    return x * (1.0 / (1.0 + jnp.exp(-x)))


def _sigmoid(x):
    return 1.0 / (1.0 + jnp.exp(-x))


def _softplus(x):
    return jnp.maximum(x, 0.0) + jnp.log(1.0 + jnp.exp(-jnp.abs(x)))


def _dot(a, b):
    return jnp.dot(a, b, preferred_element_type=F32)


def _dot_nt(a, b):
    return lax.dot_general(a, b, (((1,), (1,)), ((), ())), preferred_element_type=F32)


def _dot_tn(a, b):
    return lax.dot_general(a, b, (((0,), (0,)), ((), ())), preferred_element_type=F32)


def _dot_hi(a, b):
    return jnp.dot(a, b, preferred_element_type=F32, precision=lax.Precision.HIGHEST)


def _mod_kernel(cond_ref, w_ref, b_ref, o_ref):
    a = _silu(cond_ref[...]).astype(BF16)
    o_ref[0] = _dot(a, w_ref[0].astype(BF16)) + b_ref[0]


def modulation_all(cond, w_mod, b_mod):
    r = cond.shape[0]
    tn = 1024
    return pl.pallas_call(
        _mod_kernel,
        grid=(DEPTH, N_MOD * D_MODEL // tn),
        in_specs=[
            pl.BlockSpec((r, D_MODEL), lambda l, j: (0, 0)),
            pl.BlockSpec((1, D_MODEL, tn), lambda l, j: (l, 0, j)),
            pl.BlockSpec((1, 1, tn), lambda l, j: (l, 0, j)),
        ],
        out_specs=pl.BlockSpec((1, r, tn), lambda l, j: (l, 0, j)),
        out_shape=jax.ShapeDtypeStruct((DEPTH, r, N_MOD * D_MODEL), F32),
        compiler_params=_cparams("parallel", "parallel"),
        name="modulation",
    )(cond, w_mod, b_mod.reshape(DEPTH, 1, N_MOD * D_MODEL))


TL = 512


def _row_mod(modc_ref, modb_ref, tile, idx, latent_only=False):
    if latent_only:
        return modb_ref[0, idx:idx + 1, :]
    row = lax.broadcasted_iota(jnp.int32, (TM, 1), 0) + (tile % (T_ALL // TM)) * TM
    return jnp.where(row < CTX_LEN, modc_ref[0, idx:idx + 1, :], modb_ref[0, idx:idx + 1, :])


def _adaln(x, nw, shift, scale):
    y = x * lax.rsqrt(jnp.mean(x * x, axis=-1, keepdims=True) + NORM_EPS)
    return y * nw * (1.0 + scale) + shift


def _ffn_body(sub, latent_only, i, x, modc_ref, modb_ref, nw_ref, wgu_ref, wd_ref):
    shift = _row_mod(modc_ref, modb_ref, i, 3 * sub, latent_only)
    scale = _row_mod(modc_ref, modb_ref, i, 3 * sub + 1, latent_only)
    h = _adaln(x, nw_ref[...], shift, scale).astype(BF16)
    acc = None
    for c0 in range(0, D_FF, TF):
        c1 = min(c0 + TF, D_FF)
        a = (_silu(_dot(h, wgu_ref[:, c0:c1])) * _dot(h, wgu_ref[:, D_FF + c0:D_FF + c1])).astype(BF16)
        part = _dot(a, wd_ref[c0:c1, :])
        acc = part if acc is None else acc + part
    gate = _row_mod(modc_ref, modb_ref, i, 3 * sub + 2, latent_only)
    return x + 0.5 * gate * acc


def _ffn_kernel(sub, latent_only, x_ref, modc_ref, modb_ref, nw_ref, wgu_ref, wd_ref, o_ref):
    o_ref[...] = _ffn_body(sub, latent_only, pl.program_id(0), x_ref[...], modc_ref, modb_ref, nw_ref, wgu_ref, wd_ref)


def _ffn_parts_kernel(sub, ctx_ref, xa_ref, xb_ref, modc_ref, modb_ref, nw_ref, wgu_ref, wd_ref, o_ref):
    i = pl.program_id(0)
    first = jnp.concatenate([ctx_ref[...], xa_ref[...]], axis=0)
    x = jnp.where(i % (T_ALL // TM) == 0, first, xb_ref[...])
    o_ref[...] = _ffn_body(sub, False, i, x, modc_ref, modb_ref, nw_ref, wgu_ref, wd_ref)


def ffn_half_step(x, modc, modb, nw, w_gu, w_down, layer, which, sub, parts=None):
    tiles_per_b = T_ALL // TM
    once = pl.Buffered(1)
    common = [
        pl.BlockSpec((1, N_MOD, D_MODEL), lambda i: (0, 0, 0)),
        pl.BlockSpec((1, N_MOD, D_MODEL), lambda i: (i // tiles_per_b, 0, 0)),
        pl.BlockSpec((1, D_MODEL), lambda i: (0, 0)),
        pl.BlockSpec((None, None, D_MODEL, 2 * D_FF), lambda i: (layer, which, 0, 0), pipeline_mode=once),
        pl.BlockSpec((None, None, D_FF, D_MODEL), lambda i: (layer, which, 0, 0), pipeline_mode=once),
    ]
    if parts is None:
        n = x.shape[0]
        body = functools.partial(_ffn_kernel, sub, False)
        data_specs = [pl.BlockSpec((TM, D_MODEL), lambda i: (i, 0))]
        data = (x,)
    else:
        lat, ctx = parts
        nb = lat.shape[0]
        n = nb * T_ALL
        head = TM - CTX_LEN

        def window(rows, start):
            return pl.BlockSpec((pl.Element(rows), pl.Element(D_MODEL)),
                                lambda i: (pl.multiple_of(start(i // tiles_per_b, i % tiles_per_b), CTX_LEN), 0))

        body = functools.partial(_ffn_parts_kernel, sub)
        data_specs = [
            pl.BlockSpec((CTX_LEN, D_MODEL), lambda i: (i // tiles_per_b, 0)),
            window(head, lambda b, t: b * SEQ),
            window(TM, lambda b, t: b * SEQ + head + (jnp.maximum(t, 1) - 1) * TM),
        ]
        lat2 = lat.reshape(nb * SEQ, D_MODEL)
        data = (ctx.reshape(nb * CTX_LEN, D_MODEL), lat2, lat2)
    return pl.pallas_call(
        body,
        grid=(n // TM,),
        in_specs=data_specs + common,
        out_specs=pl.BlockSpec((TM, D_MODEL), lambda i: (i, 0)),
        out_shape=jax.ShapeDtypeStruct((n, D_MODEL), F32),
        compiler_params=_cparams("parallel"),
        name=f"ffn{sub}",
    )(*data, modc, modb, nw.reshape(1, D_MODEL), w_gu, w_down)


def _inproj_kernel(x_ref, modc_ref, modb_ref, nw_ref, w_ref, hy_ref, na_ref, dn_ref, rw_ref):
    i = pl.program_id(0)
    shift = _row_mod(modc_ref, modb_ref, i, 3)
    scale = _row_mod(modc_ref, modb_ref, i, 4)
    h = _adaln(x_ref[...], nw_ref[...], shift, scale).astype(BF16)
    o0 = 3 * GROUP
    o1 = 6 * GROUP
    o2 = o1 + DN_W
    y = _dot(h, w_ref[...])
    hy_ref[...] = y[:, 0:o0]
    na_ref[...] = y[:, o0:o1]
    dn_ref[...] = y[:, o1:o2]
    rw_ref[...] = y[:, o2:P_PAD]


def input_projection(x, modc, modb, nw, w_in_p, layer):
    n = x.shape[0]
    tiles_per_b = T_ALL // TM
    widths = (3 * GROUP, 3 * GROUP, DN_W, RW_W)
    return pl.pallas_call(
        _inproj_kernel,
        grid=(n // TM,),
        in_specs=[
            pl.BlockSpec((TM, D_MODEL), lambda i: (i, 0)),
            pl.BlockSpec((1, N_MOD, D_MODEL), lambda i: (0, 0, 0)),
            pl.BlockSpec((1, N_MOD, D_MODEL), lambda i: (i // tiles_per_b, 0, 0)),
            pl.BlockSpec((1, D_MODEL), lambda i: (0, 0)),
            pl.BlockSpec((None, D_MODEL, P_PAD), lambda i: (layer, 0, 0)),
        ],
        out_specs=[pl.BlockSpec((TM, w), lambda i: (i, 0)) for w in widths],
        out_shape=[jax.ShapeDtypeStruct((n, w), F32) for w in widths],
        compiler_params=_cparams("parallel"),
        name="inproj",
    )(x, modc, modb, nw.reshape(1, D_MODEL), w_in_p)


def _mix_ffn_kernel(latent_only, x_ref, modc_ref, modb_ref, nw_ref, hy_ref, na_ref,
                    of_ref, ob_ref, z_ref, dnw_ref, yf_ref, yb_ref, bonus_ref, g_ref, pv_ref,
                    wo_ref, wgu_ref, wd_ref, o_ref):
    i = pl.program_id(0)
    hmean = _head_mean_matrix(1.0 / HEAD_DIM)
    o = of_ref[...] + ob_ref[...]
    g_dn = o * lax.rsqrt(_dot_split(o * o, hmean) + NORM_EPS) * dnw_ref[...] * _silu(z_ref[...])
    y = yf_ref[...] + yb_ref[...]
    yc = y - _dot_split(y, hmean)
    yn = yc * lax.rsqrt(_dot_split(yc * yc, hmean) + RW_LN_EPS) * pv_ref[7:8, :] + pv_ref[8:9, :]
    g_rw = (yn + bonus_ref[...]) * g_ref[...]
    mix = _dot(hy_ref[...].astype(BF16), wo_ref[0:GROUP, :])
    mix += _dot(na_ref[...].astype(BF16), wo_ref[GROUP:2 * GROUP, :])
    mix += _dot(g_dn.astype(BF16), wo_ref[2 * GROUP:3 * GROUP, :])
    mix += _dot(g_rw.astype(BF16), wo_ref[3 * GROUP:4 * GROUP, :])
    x = x_ref[...] + _row_mod(modc_ref, modb_ref, i, 5, latent_only) * mix
    o_ref[...] = _ffn_body(2, latent_only, i, x, modc_ref, modb_ref, nw_ref, wgu_ref, wd_ref)


def mixer_output_ffn(x, modc, modb, nw, hy, na, dn, rw, w_out, w_gu, w_down, layer, latent_only):
    n = x.shape[0]
    nb = n // T_ALL
    o_f, o_b, dn_slab, dn_norm = dn
    y_f, y_b, bonus, gate, pv = rw
    flat = lambda t: t.reshape(n, GROUP)
    if latent_only:
        tm, tiles_per_b = TL, SEQ // TL
        rows = lambda i: pl.multiple_of((i // tiles_per_b) * T_ALL + CTX_LEN + (i % tiles_per_b) * TL,
                                        math.gcd(CTX_LEN, TL))
        win = lambda width, col=0: pl.BlockSpec((pl.Element(tm), pl.Element(width)), lambda i: (rows(i), col * width))
    else:
        tm, tiles_per_b = TM, T_ALL // TM
        win = lambda width, col=0: pl.BlockSpec((tm, width), lambda i: (i, col))
    n_out = nb * tiles_per_b * tm
    once = pl.Buffered(1)
    small = lambda shape: pl.BlockSpec(shape, lambda i: (0,) * len(shape))
    return pl.pallas_call(
        functools.partial(_mix_ffn_kernel, latent_only),
        grid=(n_out // tm,),
        in_specs=[
            win(D_MODEL), small((1, N_MOD, D_MODEL)),
            pl.BlockSpec((1, N_MOD, D_MODEL), lambda i: (i // tiles_per_b, 0, 0)), small((1, D_MODEL)),
            win(GROUP), win(GROUP),
            win(GROUP), win(GROUP), win(GROUP, 3), small((1, GROUP)),
            win(GROUP), win(GROUP), win(GROUP), win(GROUP), small((16, GROUP)),
            pl.BlockSpec((None, D_MODEL, D_MODEL), lambda i: (layer, 0, 0), pipeline_mode=once),
            pl.BlockSpec((None, None, D_MODEL, 2 * D_FF), lambda i: (layer, 1, 0, 0), pipeline_mode=once),
            pl.BlockSpec((None, None, D_FF, D_MODEL), lambda i: (layer, 1, 0, 0), pipeline_mode=once),
        ],
        out_specs=pl.BlockSpec((tm, D_MODEL), lambda i: (i, 0)),
        out_shape=jax.ShapeDtypeStruct((n_out, D_MODEL), F32),
        compiler_params=_cparams("parallel"),
        name="mix_ffn",
    )(x, modc, modb, nw.reshape(1, D_MODEL), hy, na, flat(o_f), flat(o_b), dn_slab,
      jnp.tile(dn_norm, GROUP_HEADS).reshape(1, GROUP), flat(y_f), flat(y_b), bonus, gate, pv,
      w_out, w_gu, w_down)


def _head_mean_matrix(scale):
    r = lax.broadcasted_iota(jnp.int32, (GROUP, GROUP), 0) // HEAD_DIM
    c = lax.broadcasted_iota(jnp.int32, (GROUP, GROUP), 1) // HEAD_DIM
    return jnp.where(r == c, scale, 0.0).astype(BF16)


def _dot_split(a, m_bf16):
    hi = a.astype(BF16)
    lo = (a - hi.astype(F32)).astype(BF16)
    return _dot(hi, m_bf16) + _dot(lo, m_bf16)


def _lane_head(width=GROUP):
    return lax.broadcasted_iota(jnp.int32, (1, width), 1) // HEAD_DIM


NA_ROWS = SEQ // GRID_W
NA_LOCAL = NA_WIN_ROWS * GRID_W
NA_NEG = -1e30
NA_BLK = 256
NA_PAIR = 8


def na_bias_table(rpb):
    n_dr = 2 * NA_WIN_ROWS
    rows = jnp.pad(rpb, ((0, 0), (0, 1), (0, 128 - rpb.shape[2]))).reshape(GROUP_HEADS * n_dr, 128)
    toep = pl.pallas_call(
        _na_bias_kernel,
        out_shape=jax.ShapeDtypeStruct((GROUP_HEADS * n_dr, GRID_W * GRID_W), F32),
        name="na_bias",
    )(rows).reshape(GROUP_HEADS, n_dr, GRID_W, GRID_W)
    tab = jnp.stack([toep[:, NA_WIN_ROWS - 1 - p:2 * NA_WIN_ROWS - 1 - p] for p in range(NA_WIN_ROWS)], axis=0)
    tab = jnp.transpose(tab, (0, 1, 3, 2, 4))
    return tab.reshape(NA_WIN_ROWS, GROUP_HEADS, GRID_W, NA_LOCAL)


def _na_bias_kernel(rpb_ref, o_ref):
    n = GRID_W * GRID_W
    d = lax.broadcasted_iota(jnp.int32, (128, n), 0)
    cj = lax.broadcasted_iota(jnp.int32, (128, n), 1)
    onehot = jnp.where((cj % GRID_W) - (cj // GRID_W) + NA_WIN_COLS - 1 == d, 1.0, 0.0).astype(BF16)
    cj1 = lax.broadcasted_iota(jnp.int32, (1, n), 1)
    c = cj1 // GRID_W
    j = cj1 % GRID_W
    start = jnp.clip(c - NA_WIN_COLS // 2, 0, GRID_W - NA_WIN_COLS)
    in_win = jnp.logical_and(j >= start, j < start + NA_WIN_COLS)
    o_ref[...] = jnp.where(in_win, _dot_exact_rhs(rpb_ref[...], onehot), NA_NEG)


def _na_kernel(need_ctx, slab_ref, qw_ref, kw_ref, bias_ref, o_ref, q_s, k_s, v_s):
    hm = _head_mean_matrix(1.0 / HEAD_DIM)
    qw = qw_ref[...] * (HEAD_DIM ** -0.5)
    kw = kw_ref[...]

    def prep(i, c):
        r0 = pl.multiple_of(i * NA_BLK, NA_BLK)
        q = slab_ref[pl.ds(r0, NA_BLK), 0:GROUP]
        k = slab_ref[pl.ds(r0, NA_BLK), GROUP:2 * GROUP]
        q_s[pl.ds(r0, NA_BLK), :] = (q * lax.rsqrt(_dot_split(q * q, hm) + NORM_EPS) * qw).astype(BF16)
        k_s[pl.ds(r0, NA_BLK), :] = (k * lax.rsqrt(_dot_split(k * k, hm) + NORM_EPS) * kw).astype(BF16)
        v_s[pl.ds(r0, NA_BLK), :] = slab_ref[pl.ds(r0, NA_BLK), 2 * GROUP:3 * GROUP].astype(BF16)
        return c

    lax.fori_loop(0, T_ALL // NA_BLK, prep, 0)

    lane_h = _lane_head()
    kc = k_s[0:CTX_LEN, :]
    vc = v_s[0:CTX_LEN, :]

    if need_ctx:
        qc = q_s[0:CTX_LEN, :]
        out = jnp.zeros((CTX_LEN, GROUP), F32)
        for h in range(GROUP_HEADS):
            mask = lane_h == h
            s = _dot_nt(jnp.where(mask, qc, jnp.zeros_like(qc)), kc)
            e = jnp.exp(s - jnp.max(s, axis=-1, keepdims=True))
            p = e * (1.0 / jnp.sum(e, axis=-1, keepdims=True))
            out = jnp.where(mask, _dot(p.astype(BF16), vc), out)
        o_ref[0:CTX_LEN, :] = out
    else:
        o_ref[0:CTX_LEN, :] = jnp.zeros((CTX_LEN, GROUP), F32)

    def pair_body(i, c):
        rows = [i * NA_PAIR + t for t in range(NA_PAIR)]
        start = [jnp.clip(r - NA_WIN_ROWS // 2, 0, NA_ROWS - NA_WIN_ROWS) for r in rows]
        q0 = [pl.multiple_of(CTX_LEN + r * GRID_W, GRID_W) for r in rows]
        k0 = [pl.multiple_of(CTX_LEN + s * GRID_W, GRID_W) for s in start]
        q = [_expand_heads(q_s[pl.ds(a, GRID_W), :]) for a in q0]
        kb = [k_s[pl.ds(a, NA_LOCAL), :] for a in k0]
        vb = [v_s[pl.ds(a, NA_LOCAL), :] for a in k0]
        bias = [bias_ref[r - s].reshape(GROUP_HEADS * GRID_W, NA_LOCAL) for r, s in zip(rows, start)]
        s_loc = [_dot_nt(q[t], kb[t]) + bias[t] for t in range(NA_PAIR)]
        s_ctx = [_dot_nt(q[t], kc) for t in range(NA_PAIR)]
        m = [jnp.maximum(jnp.max(a, axis=-1, keepdims=True), jnp.max(b, axis=-1, keepdims=True))
             for a, b in zip(s_loc, s_ctx)]
        e_loc = [jnp.exp(a - mm) for a, mm in zip(s_loc, m)]
        e_ctx = [jnp.exp(b - mm) for b, mm in zip(s_ctx, m)]
        inv = [1.0 / (jnp.sum(a, axis=-1, keepdims=True) + jnp.sum(b, axis=-1, keepdims=True))
               for a, b in zip(e_loc, e_ctx)]
        o = [_dot((e_loc[t] * inv[t]).astype(BF16), vb[t]) + _dot((e_ctx[t] * inv[t]).astype(BF16), vc)
             for t in range(NA_PAIR)]
        for t in range(NA_PAIR):
            out = o[t][0:GRID_W]
            for h in range(1, GROUP_HEADS):
                out = jnp.where(lane_h == h, o[t][h * GRID_W:(h + 1) * GRID_W], out)
            o_ref[pl.ds(q0[t], GRID_W), :] = out
        return c

    lax.fori_loop(0, NA_ROWS // NA_PAIR, pair_body, 0)


def na_mixer(slab, q_norm, k_norm, bias_tab, need_ctx):
    n = slab.shape[0]
    tile4 = lambda w: jnp.tile(w, GROUP_HEADS).reshape(1, GROUP)
    return pl.pallas_call(
        functools.partial(_na_kernel, need_ctx),
        grid=(n // T_ALL,),
        in_specs=[
            pl.BlockSpec((T_ALL, 3 * GROUP), lambda b: (b, 0)),
            pl.BlockSpec((1, GROUP), lambda b: (0, 0)),
            pl.BlockSpec((1, GROUP), lambda b: (0, 0)),
            pl.BlockSpec((NA_WIN_ROWS, GROUP_HEADS, GRID_W, NA_LOCAL), lambda b: (0, 0, 0, 0)),
        ],
        out_specs=pl.BlockSpec((T_ALL, GROUP), lambda b: (b, 0)),
        out_shape=jax.ShapeDtypeStruct((n, GROUP), F32),
        scratch_shapes=[pltpu.VMEM((T_ALL, GROUP), BF16)] * 3,
        compiler_params=_cparams("parallel"),
        name="na_mixer",
    )(slab, tile4(q_norm), tile4(k_norm), bias_tab)


SEQ_BLK = 256
N_BLK = T_ALL // SEQ_BLK
N_CHUNK = T_ALL // CHUNK
CTX_CHUNKS = CTX_LEN // CHUNK


def _prev_cur_next(ref, i, c0, c1):
    r0 = pl.multiple_of(i * SEQ_BLK, SEQ_BLK)
    cur = ref[pl.ds(r0, SEQ_BLK), c0:c1]
    up0 = pl.multiple_of(jnp.maximum(r0 - 8, 0), 8)
    dn0 = pl.multiple_of(jnp.minimum(r0 + SEQ_BLK, T_ALL - 8), 8)
    up = ref[pl.ds(up0, 8), c0:c1][7:8, :]
    dn = ref[pl.ds(dn0, 8), c0:c1][0:1, :]
    up = jnp.where(i >= 2, up, 0.0)
    dn = jnp.where(jnp.logical_and(i >= 1, i <= N_BLK - 2), dn, 0.0)
    row = lax.broadcasted_iota(jnp.int32, (SEQ_BLK, 1), 0)
    prev = jnp.where(row == 0, up, pltpu.roll(cur, 1, 0))
    nxt = jnp.where(row == SEQ_BLK - 1, dn, pltpu.roll(cur, SEQ_BLK - 1, 0))
    return prev, cur, nxt


def _chunk_cumsum(x, reverse):
    pos = lax.broadcasted_iota(jnp.int32, (SEQ_BLK, 1), 0) % CHUNK
    s = 1
    while s < CHUNK:
        if reverse:
            x = x + jnp.where(pos < CHUNK - s, pltpu.roll(x, SEQ_BLK - s, 0), 0.0)
        else:
            x = x + jnp.where(pos >= s, pltpu.roll(x, s, 0), 0.0)
        s *= 2
    return x


def _split3(a):
    hi = a.astype(BF16)
    r1 = a - hi.astype(F32)
    mid = r1.astype(BF16)
    lo = (r1 - mid.astype(F32)).astype(BF16)
    return hi, mid, lo


def _dot_exact_rhs(a, m_bf16):
    hi, mid, lo = _split3(a)
    return _dot(hi, m_bf16) + _dot(mid, m_bf16) + _dot(lo, m_bf16)


def _expand_heads(x):
    lane_h = _lane_head()
    return jnp.concatenate([jnp.where(lane_h == h, x, 0.0) for h in range(GROUP_HEADS)], axis=0)


def _chunk_of_step(n, reverse):
    if not reverse:
        return n
    return jnp.where(n < CTX_CHUNKS, CTX_CHUNKS - 1 - n, N_CHUNK + CTX_CHUNKS - 1 - n)


INV_BASE = 16


def _cat_dot(a, b):
    return _dot(a.astype(BF16), _expand_heads(b.astype(BF16)))


def _cat_index():
    i = lax.broadcasted_iota(jnp.int32, (CHUNK, GROUP_HEADS * CHUNK), 0)
    j = lax.broadcasted_iota(jnp.int32, (CHUNK, GROUP_HEADS * CHUNK), 1) % CHUNK
    return i, j


def _cat_masks(reverse):
    i, j = _cat_index()
    if reverse:
        return i <= j, i < j
    return i >= j, i > j


def _inverse_unit_triangular(mats):
    i, j = _cat_index()
    inner = (i // INV_BASE) == (j // INV_BASE)
    eye = jnp.where(i == j, 1.0, 0.0)
    nd = [jnp.where(inner, n, 0.0) for n in mats]
    x = [eye - n for n in nd]
    p = [_cat_dot(n, n) for n in nd]
    k = 2
    while k < INV_BASE:
        x = [xi + _cat_dot(pi, xi) for xi, pi in zip(x, p)]
        k *= 2
        if k < INV_BASE:
            p = [_cat_dot(pi, pi) for pi in p]
    width = INV_BASE
    while width < CHUNK:
        outer = (i // (2 * width)) == (j // (2 * width))
        sel = jnp.logical_and(outer, jnp.logical_not(inner))
        t = [_cat_dot(jnp.where(sel, n, 0.0), xi) for n, xi in zip(mats, x)]
        x = [xi - _cat_dot(xi, ti) for xi, ti in zip(x, t)]
        inner = outer
        width *= 2
    return x


def _head_rows(gc, lane_onehot):
    hi, mid, lo = _split3(gc)
    t = _dot_nt(lane_onehot, hi) + _dot_nt(lane_onehot, mid) + _dot_nt(lane_onehot, lo)
    return jnp.concatenate([t[h:h + 1, :] for h in range(GROUP_HEADS)], axis=1)


INTRA_CHUNKS = SEQ_BLK // CHUNK
INTER_BATCH = 8
INTER_CHUNKS = 2
GL_ROWS = 8


def _dn_prep_block(slab_ref, conv_ref, alog_ref, dt_ref, i):
    hsum = _head_mean_matrix(1.0)
    col = lax.broadcasted_iota(jnp.int32, (128, GROUP), 0)
    lane = lax.broadcasted_iota(jnp.int32, (128, GROUP), 1) // HEAD_DIM
    neg_a = -jnp.exp(alog_ref[...])
    dtb = dt_ref[...]
    rows = pl.ds(pl.multiple_of(i * SEQ_BLK, SEQ_BLK), SEQ_BLK)
    qkv = []
    for j in range(3):
        prev, cur, nxt = _prev_cur_next(slab_ref, i, j * GROUP, (j + 1) * GROUP)
        w = conv_ref[:, j * GROUP:(j + 1) * GROUP]
        u = _silu(prev * w[0:1] + cur * w[1:2] + nxt * w[2:3])
        if j == 0:
            u = u * lax.rsqrt(_dot_split(u * u, hsum) + 1e-6) * (HEAD_DIM ** -0.5)
        elif j == 1:
            u = u * lax.rsqrt(_dot_split(u * u, hsum) + 1e-6)
        qkv.append(u)
    ba = slab_ref[rows, 4 * GROUP:4 * GROUP + 128]
    gc, beta = [], []
    for d in range(2):
        e_b = jnp.where(col == 8 * d + lane, 1.0, 0.0).astype(BF16)
        e_a = jnp.where(col == 8 * d + 4 + lane, 1.0, 0.0).astype(BF16)
        beta.append(_sigmoid(_dot_exact_rhs(ba, e_b)))
        g = neg_a[d:d + 1] * _softplus(_dot_exact_rhs(ba, e_a) + dtb[d:d + 1])
        gc.append(_chunk_cumsum(g, reverse=(d == 1)))
    return qkv[0], qkv[1], qkv[2], gc, beta


def _dn_intra_kernel(slab_ref, conv_ref, alog_ref, dt_ref, u_ref, w_ref, attn_ref, qd_ref, kd_ref, gl_ref):
    q_blk, k_blk, v_blk, gc_blk, beta_blk = _dn_prep_block(slab_ref, conv_ref, alog_ref, dt_ref, pl.program_id(1))
    onehot = jnp.where(
        lax.broadcasted_iota(jnp.int32, (8, GROUP), 1) == HEAD_DIM * lax.broadcasted_iota(jnp.int32, (8, GROUP), 0),
        1.0, 0.0).astype(BF16)
    masks = (_cat_masks(False), _cat_masks(True))
    chains = [(j, d) for j in range(INTRA_CHUNKS) for d in range(2)]
    rows = [slice(j * CHUNK, (j + 1) * CHUNK) for j, d in chains]
    gc = [gc_blk[d][r] for (j, d), r in zip(chains, rows)]
    beta = [beta_blk[d][r] for (j, d), r in zip(chains, rows)]
    q = [q_blk[r] for r in rows]
    k = [k_blk[r] for r in rows]
    v = [v_blk[r] for r in rows]
    eg = [jnp.exp(g) for g in gc]
    g_last = [g[0:1, :] if d == 1 else g[CHUNK - 1:CHUNK, :] for (j, d), g in zip(chains, gc)]
    kb = [a * b for a, b in zip(k, beta)]
    k_e = [_expand_heads(a.astype(BF16)) for a in k]
    dec = []
    for (j, d), g in zip(chains, gc):
        incl = masks[d][0]
        dec.append(jnp.where(incl, jnp.exp(jnp.where(incl, g - _head_rows(g, onehot), 0.0)), 0.0))
    m = [jnp.where(masks[d][1], _dot_nt(a.astype(BF16), ke) * dc, 0.0)
         for (j, d), a, ke, dc in zip(chains, kb, k_e, dec)]
    attn = [_dot_nt(a.astype(BF16), ke) * dc for a, ke, dc in zip(q, k_e, dec)]
    rhs = [jnp.concatenate([_expand_heads((a * b).astype(BF16)), _expand_heads((c * e).astype(BF16))], axis=1)
           for a, b, c, e in zip(v, beta, kb, eg)]
    x = _inverse_unit_triangular(m)
    sol = [_dot(xi.astype(BF16), r) for xi, r in zip(x, rhs)]
    for i, ((j, d), r) in enumerate(zip(chains, rows)):
        u_ref[d, 0, r, :] = sol[i][:, 0:GROUP]
        w_ref[d, 0, r, :] = sol[i][:, GROUP:2 * GROUP].astype(BF16)
        attn_ref[d, 0, r, :] = attn[i].astype(BF16)
        qd_ref[d, 0, r, :] = (q[i] * eg[i]).astype(BF16)
        kd_ref[d, 0, r, :] = (k[i] * jnp.exp(g_last[i] - gc[i])).astype(BF16)
        gl_ref[d, 0, pl.ds(j * GL_ROWS, GL_ROWS), :] = jnp.broadcast_to(jnp.exp(g_last[i]), (GL_ROWS, GROUP))


def _same_head_mask():
    r = lax.broadcasted_iota(jnp.int32, (GROUP, GROUP), 0) // HEAD_DIM
    c = lax.broadcasted_iota(jnp.int32, (GROUP, GROUP), 1) // HEAD_DIM
    return r == c


def _dn_inter_kernel(uf, wf, af, qf, kf, gf, ub, wb, ab, qb, kb, gb, of_ref, ob_ref, s_ref):
    @pl.when(pl.program_id(1) == 0)
    def _():
        s_ref[...] = jnp.zeros_like(s_ref)

    same = _same_head_mask()
    ins = ((uf, wf, af, qf, kf, gf, of_ref), (ub, wb, ab, qb, kb, gb, ob_ref))
    chains = [(d, j) for j in range(INTER_BATCH) for d in range(2)]
    s = [s_ref[d * INTER_BATCH + j] for d, j in chains]
    for t in range(INTER_CHUNKS):
        rows = [_inter_rows(t, d == 1, CHUNK) for d, j in chains]
        grow = [_inter_rows(t, d == 1, GL_ROWS) for d, j in chains]
        s_b = [x.astype(BF16) for x in s]
        v_new = [(ins[d][0][0, j, r, :] - _dot(ins[d][1][0, j, r, :], sb)).astype(BF16)
                 for (d, j), r, sb in zip(chains, rows, s_b)]
        v_ne = [_expand_heads(x) for x in v_new]
        o = [_dot(ins[d][3][0, j, r, :], sb) + _dot(ins[d][2][0, j, r, :], ve)
             for (d, j), r, sb, ve in zip(chains, rows, s_b, v_ne)]
        upd = [_dot_tn(ins[d][4][0, j, r, :], x) for (d, j), r, x in zip(chains, rows, v_new)]
        for i, (d, j) in enumerate(chains):
            ins[d][6][j, rows[i], :] = o[i]
        s = [s[i] * ins[d][5][0, j, grow[i], :][0:1] + jnp.where(same, upd[i], 0.0) for i, (d, j) in enumerate(chains)]
    for i, (d, j) in enumerate(chains):
        s_ref[d * INTER_BATCH + j] = s[i]


def _inter_block(n, reverse):
    if not reverse:
        return n
    return _chunk_of_step(INTER_CHUNKS * n + INTER_CHUNKS - 1, True) // INTER_CHUNKS


def _inter_rows(t, reverse, rows_per_chunk):
    j = INTER_CHUNKS - 1 - t if reverse else t
    return slice(j * rows_per_chunk, (j + 1) * rows_per_chunk)


def deltanet_mixer(slab, conv_w, a_log, dt_bias):
    nb = slab.shape[0] // T_ALL
    assert nb % INTER_BATCH == 0 and CTX_CHUNKS % INTER_CHUNKS == 0 and N_CHUNK % INTER_CHUNKS == 0
    lanes = lambda t: jnp.repeat(t, HEAD_DIM, axis=-1)
    seq = lambda dt: jax.ShapeDtypeStruct((nb, T_ALL, GROUP), dt)
    seq2 = lambda dt: jax.ShapeDtypeStruct((2, nb, T_ALL, GROUP), dt)
    p2 = pl.BlockSpec((2, 1, SEQ_BLK, GROUP), lambda b, i: (0, b, i, 0))
    small = lambda shape: pl.BlockSpec(shape, lambda b, i: (0, 0))
    gl_shape = jax.ShapeDtypeStruct((2, nb, N_CHUNK * GL_ROWS, GROUP), F32)
    u, w, attn, qd, kd, gl = pl.pallas_call(
        _dn_intra_kernel,
        grid=(nb, N_BLK),
        in_specs=[pl.BlockSpec((T_ALL, DN_W), lambda b, i: (b, 0)), small((3, 3 * GROUP)),
                  small((2, GROUP)), small((2, GROUP))],
        out_specs=[p2, p2, p2, p2, p2,
                   pl.BlockSpec((2, 1, INTRA_CHUNKS * GL_ROWS, GROUP), lambda b, i: (0, b, i, 0))],
        out_shape=[seq2(F32), seq2(BF16), seq2(BF16), seq2(BF16), seq2(BF16), gl_shape],
        compiler_params=_cparams("parallel", "parallel"),
        name="deltanet_intra",
    )(slab, conv_w, lanes(a_log), lanes(dt_bias))

    def per_dir(d, rows_per_chunk):
        return pl.BlockSpec((1, INTER_BATCH, INTER_CHUNKS * rows_per_chunk, GROUP),
                            lambda b, n: (d, b, _inter_block(n, d == 1), 0))

    def out_dir(d):
        return pl.BlockSpec((INTER_BATCH, INTER_CHUNKS * CHUNK, GROUP), lambda b, n: (b, _inter_block(n, d == 1), 0))

    specs = [per_dir(d, r) for d in range(2) for r in (CHUNK,) * 5 + (GL_ROWS,)]
    o_f, o_b = pl.pallas_call(
        _dn_inter_kernel,
        grid=(nb // INTER_BATCH, N_CHUNK // INTER_CHUNKS),
        in_specs=specs,
        out_specs=[out_dir(0), out_dir(1)],
        out_shape=[seq(F32), seq(F32)],
        scratch_shapes=[pltpu.VMEM((2 * INTER_BATCH, GROUP, GROUP), F32)],
        compiler_params=_cparams("parallel", "arbitrary"),
        name="deltanet_inter",
    )(u, w, attn, qd, kd, gl, u, w, attn, qd, kd, gl)

    return o_f, o_b


RW_LR = RW_DECAY_RANK + RW_AAA_RANK + RW_GATE_RANK
RW_LR_OUT = 5 * GROUP


def _dot3(a, b_hi, b_lo):
    a_hi = a.astype(BF16)
    a_lo = (a - a_hi.astype(F32)).astype(BF16)
    return _dot(a_hi, b_hi) + (_dot(a_lo, b_hi) + _dot(a_hi, b_lo))


def rwkv_lowrank_weights(w_up, a_up, g_up):
    w = jnp.zeros((RW_LR, RW_LR_OUT), F32)
    o1 = RW_DECAY_RANK
    o2 = o1 + RW_AAA_RANK
    for d in range(2):
        w = w.at[0:o1, d * GROUP:(d + 1) * GROUP].set(w_up[d])
        w = w.at[o1:o2, (2 + d) * GROUP:(3 + d) * GROUP].set(a_up[d])
    return w.at[o2:RW_LR, 4 * GROUP:5 * GROUP].set(g_up)


def _rw_intra_kernel(slab_ref, mu_ref, pv_ref, wlr_ref,
                     at_ref, rt_ref, bg_ref, kg_ref, gl_ref, v_ref, bonus_ref, g_ref,
                     xc_ref, arb_ref, rhs0_ref, yk_ref):
    i = pl.program_id(1)
    hsum = _head_mean_matrix(1.0)
    wlr = wlr_ref[...]
    wlr_hi = wlr.astype(BF16)
    wlr_lo = (wlr - wlr_hi.astype(F32)).astype(BF16)
    pv = pv_ref[...]
    w0 = (pv[0:1], pv[1:2])
    a0 = (pv[2:3], pv[3:4])
    k_k, k_a, r_k = pv[4:5], pv[5:6], pv[6:7]
    lr_lane = lax.broadcasted_iota(jnp.int32, (1, RW_LR), 1)

    def shifted(c0, c1):
        prev, cur, nxt = _prev_cur_next(slab_ref, i, c0, c1)
        return cur + mu_ref[0:1, c0:c1] * (prev - cur) + mu_ref[1:2, c0:c1] * (nxt - cur)

    r = shifted(0, GROUP)
    k = shifted(GROUP, 2 * GROUP)
    v = shifted(2 * GROUP, 3 * GROUP)
    lr = shifted(3 * GROUP, 3 * GROUP + RW_LR)
    t = jnp.where(lr_lane < RW_DECAY_RANK, jnp.tanh(lr),
                  jnp.where(lr_lane < RW_DECAY_RANK + RW_AAA_RANK, lr, _sigmoid(lr)))
    proj = _dot3(t, wlr_hi, wlr_lo)
    kq = k * k_k
    kk = kq * lax.rsqrt(_dot_split(kq * kq, hsum) + 1e-6)
    v_blk = v.astype(BF16)
    v_ref[0] = v_blk
    g_ref[...] = proj[:, 4 * GROUP:5 * GROUP]
    ksum = jnp.zeros_like(k)
    at_blk, rt_blk, bh_blk, kh_blk = [], [], [], []
    for d in range(2):
        w_log = -_softplus(-(w0[d] + proj[:, d * GROUP:(d + 1) * GROUP])) - 0.5
        lw = -jnp.exp(w_log)
        a_gate = _sigmoid(a0[d] + proj[:, (2 + d) * GROUP:(3 + d) * GROUP])
        k_d = k * (1.0 + (a_gate - 1.0) * k_a)
        ksum = ksum + k_d
        cum = _chunk_cumsum(lw, reverse=(d == 1))
        ends = [cum[j * CHUNK:j * CHUNK + 1, :] if d == 1 else cum[(j + 1) * CHUNK - 1:(j + 1) * CHUNK, :]
                for j in range(INTRA_CHUNKS)]
        to_end = jnp.exp(jnp.concatenate([jnp.broadcast_to(e, (CHUNK, GROUP)) for e in ends], axis=0) - cum)
        inv = jnp.exp(-cum)
        b = kk * a_gate
        at_blk.append((-kk * jnp.exp(cum - lw)).astype(BF16))
        rt_blk.append((r * jnp.exp(cum)).astype(BF16))
        bh_blk.append((b * inv).astype(BF16))
        kh_blk.append((k_d * inv).astype(BF16))
        at_ref[d, 0] = at_blk[d]
        rt_ref[d, 0] = rt_blk[d]
        bg_ref[d, 0] = (b * to_end).astype(BF16)
        kg_ref[d, 0] = (k_d * to_end).astype(BF16)
        gl_ref[d, 0] = jnp.concatenate([jnp.broadcast_to(jnp.exp(e), (GL_ROWS, GROUP)) for e in ends], axis=0)
    bonus_ref[...] = _dot_split(r * ksum * r_k, hsum) * v

    masks = (_cat_masks(False), _cat_masks(True))
    chains = [(j, d) for j in range(INTRA_CHUNKS) for d in range(2)]
    rows = [slice(j * CHUNK, (j + 1) * CHUNK) for j, d in chains]
    at = [at_blk[d][r_] for (j, d), r_ in zip(chains, rows)]
    rt = [rt_blk[d][r_] for (j, d), r_ in zip(chains, rows)]
    bh_e = [_expand_heads(bh_blk[d][r_]) for (j, d), r_ in zip(chains, rows)]
    kh_e = [_expand_heads(kh_blk[d][r_]) for (j, d), r_ in zip(chains, rows)]
    v_e = [_expand_heads(v_blk[r_]) for r_ in rows]
    ar = [jnp.concatenate([a, r_], axis=0) for a, r_ in zip(at, rt)]
    s_b = [_dot_nt(x_, b_) for x_, b_ in zip(ar, bh_e)]
    s_k = [_dot_nt(x_, b_) for x_, b_ in zip(ar, kh_e)]
    x = _inverse_unit_triangular([jnp.where(masks[d][1], -t[0:CHUNK], 0.0) for (j, d), t in zip(chains, s_b)])
    a_ak = [jnp.where(masks[d][1], t[0:CHUNK], 0.0).astype(BF16) for (j, d), t in zip(chains, s_k)]
    a_rb = [jnp.where(masks[d][0], t[CHUNK:2 * CHUNK], 0.0).astype(BF16) for (j, d), t in zip(chains, s_b)]
    a_rk = [jnp.where(masks[d][0], t[CHUNK:2 * CHUNK], 0.0).astype(BF16) for (j, d), t in zip(chains, s_k)]
    kv = [_dot(jnp.concatenate([a, b_], axis=0), ve) for a, b_, ve in zip(a_ak, a_rk, v_e)]
    rhs0 = [t[0:CHUNK] for t in kv]
    yk = [t[CHUNK:2 * CHUNK] for t in kv]
    for n_, ((j, d), r_) in enumerate(zip(chains, rows)):
        xc_ref[d, 0, r_, :] = x[n_].astype(BF16)
        arb_ref[d, 0, r_, :] = a_rb[n_]
        rhs0_ref[d, 0, r_, :] = rhs0[n_]
        yk_ref[d, 0, r_, :] = yk[n_]


def _rw_inter_kernel(*refs):
    n_in = 10
    fwd, bwd = refs[0:n_in], refs[n_in:2 * n_in]
    yf_ref, yb_ref, s_ref = refs[2 * n_in:]

    @pl.when(pl.program_id(1) == 0)
    def _():
        s_ref[...] = jnp.zeros_like(s_ref)

    same = _same_head_mask()
    ins = (fwd, bwd)
    outs = (yf_ref, yb_ref)
    chains = [(d, j) for j in range(INTER_BATCH) for d in range(2)]

    s = [s_ref[d * INTER_BATCH + j] for d, j in chains]
    for t in range(INTER_CHUNKS):
        rows = [_inter_rows(t, d == 1, CHUNK) for d, j in chains]

        def arg(idx):
            return [ins[d][idx][0, j, r, :] for (d, j), r in zip(chains, rows)]

        at, rt, bg, kg, xc, arb, rhs0, yk = (arg(i) for i in range(8))
        gamma = [ins[d][8][0, j, _inter_rows(t, d == 1, GL_ROWS), :][0:1] for d, j in chains]
        v = [ins[d][9][j, r, :] for (d, j), r in zip(chains, rows)]
        s_b = [x.astype(BF16) for x in s]
        rhs = [_expand_heads((_dot_nt(a, sb) + r0).astype(BF16)) for a, sb, r0 in zip(at, s_b, rhs0)]
        sa = [_dot(x, r).astype(BF16) for x, r in zip(xc, rhs)]
        sa_e = [_expand_heads(x) for x in sa]
        y = [_dot_nt(r, sb) + _dot(a, se) + y0 for r, sb, a, se, y0 in zip(rt, s_b, arb, sa_e, yk)]
        upd = [_dot_tn(a, b) + _dot_tn(c, e) for a, b, c, e in zip(sa, bg, v, kg)]
        for i, (d, j) in enumerate(chains):
            outs[d][j, rows[i], :] = y[i]
        s = [s[i] * gamma[i] + jnp.where(same, upd[i], 0.0) for i in range(len(chains))]
    for i, (d, j) in enumerate(chains):
        s_ref[d * INTER_BATCH + j] = s[i]


def rwkv_mixer(slab, mu, w0, w_up, a0, a_up, g_up, k_k, k_a, r_k, ln_w, ln_b):
    n = slab.shape[0]
    nb = n // T_ALL
    assert nb % INTER_BATCH == 0 and CTX_CHUNKS % INTER_CHUNKS == 0 and N_CHUNK % INTER_CHUNKS == 0
    pv = jnp.concatenate([w0, a0, k_k[None], k_a[None], r_k.reshape(1, GROUP), ln_w[None], ln_b[None],
                          jnp.zeros((7, GROUP), F32)], axis=0)
    seq = lambda dt: jax.ShapeDtypeStruct((nb, T_ALL, GROUP), dt)
    seq2 = lambda dt: jax.ShapeDtypeStruct((2, nb, T_ALL, GROUP), dt)
    flat = jax.ShapeDtypeStruct((n, GROUP), F32)
    gl_shape = jax.ShapeDtypeStruct((2, nb, N_CHUNK * GL_ROWS, GROUP), F32)
    p1 = pl.BlockSpec((1, SEQ_BLK, GROUP), lambda b, i: (b, i, 0))
    p2 = pl.BlockSpec((2, 1, SEQ_BLK, GROUP), lambda b, i: (0, b, i, 0))
    pflat = pl.BlockSpec((SEQ_BLK, GROUP), lambda b, i: (b * N_BLK + i, 0))
    pgl = pl.BlockSpec((2, 1, INTRA_CHUNKS * GL_ROWS, GROUP), lambda b, i: (0, b, i, 0))
    small = lambda shape: pl.BlockSpec(shape, lambda b, i: (0, 0))
    at, rt, bg, kg, gl, v, bonus, g, xc, arb, rhs0, yk = pl.pallas_call(
        _rw_intra_kernel,
        grid=(nb, N_BLK),
        in_specs=[pl.BlockSpec((T_ALL, RW_W), lambda b, i: (b, 0)), small((2, RW_W)), small((16, GROUP)),
                  small((RW_LR, RW_LR_OUT))],
        out_specs=[p2] * 4 + [pgl, p1, pflat, pflat] + [p2] * 4,
        out_shape=[seq2(BF16)] * 4 + [gl_shape, seq(BF16), flat, flat, seq2(BF16), seq2(BF16), seq2(F32), seq2(F32)],
        compiler_params=_cparams("parallel", "parallel"),
        name="rwkv7_intra",
    )(slab, mu, pv, rwkv_lowrank_weights(w_up, a_up, g_up))

    def per_dir(d, rows_per_chunk):
        return pl.BlockSpec((1, INTER_BATCH, INTER_CHUNKS * rows_per_chunk, GROUP),
                            lambda b, n_: (d, b, _inter_block(n_, d == 1), 0))

    def shared(d):
        return pl.BlockSpec((INTER_BATCH, INTER_CHUNKS * CHUNK, GROUP), lambda b, n_: (b, _inter_block(n_, d == 1), 0))

    specs = [s for d in range(2) for s in [per_dir(d, CHUNK)] * 8 + [per_dir(d, GL_ROWS), shared(d)]]
    per = (at, rt, bg, kg, xc, arb, rhs0, yk, gl, v)
    y_f, y_b = pl.pallas_call(
        _rw_inter_kernel,
        grid=(nb // INTER_BATCH, N_CHUNK // INTER_CHUNKS),
        in_specs=specs,
        out_specs=[shared(0), shared(1)],
        out_shape=[seq(F32), seq(F32)],
        scratch_shapes=[pltpu.VMEM((2 * INTER_BATCH, GROUP, GROUP), F32)],
        compiler_params=_cparams("parallel", "arbitrary"),
        name="rwkv7_inter",
    )(*per, *per)

    return y_f, y_b, bonus, g, pv


DFT_SPLIT = 64
DFT_BLK = 256


def _dft_tables(n):
    big = 2 * n
    t = np.arange(n, dtype=np.int64)[:, None]
    k1 = np.arange(n // DFT_SPLIT, dtype=np.int64)[None, :]
    k2 = np.arange(DFT_SPLIT, dtype=np.int64)[None, :]
    alpha = 2.0 * np.pi * ((DFT_SPLIT * t * k1) % big) / big
    beta = 2.0 * np.pi * ((t * k2) % big) / big

    def pad(a):
        out = np.zeros((n, 128), np.float32)
        out[:, :a.shape[1]] = a
        return out

    return np.stack([pad(np.cos(alpha)), pad(np.sin(alpha)), pad(np.cos(beta)), pad(np.sin(beta))])


def _dft_gen_kernel(n, tab_ref, g_ref):
    k = lax.broadcasted_iota(jnp.int32, (128, n), 1)
    row = lax.broadcasted_iota(jnp.int32, (128, n), 0)
    e_a = jnp.where(k // DFT_SPLIT == row, 1.0, 0.0).astype(BF16)
    e_b = jnp.where(jnp.logical_and(k % DFT_SPLIT == row, row < DFT_SPLIT), 1.0, 0.0).astype(BF16)
    ca = _dot_split(tab_ref[0], e_a)
    sa = _dot_split(tab_ref[1], e_a)
    cb = _dot_split(tab_ref[2], e_b)
    sb = _dot_split(tab_ref[3], e_b)
    g_ref[:, 0:n] = (ca * cb - sa * sb).astype(BF16)
    g_ref[:, n:2 * n] = (-(sa * cb + ca * sb)).astype(BF16)


def dft_matrix(n):
    blk = min(DFT_BLK, n)
    return pl.pallas_call(
        functools.partial(_dft_gen_kernel, n),
        grid=(n // blk,),
        in_specs=[pl.BlockSpec((4, blk, 128), lambda i: (0, i, 0))],
        out_specs=pl.BlockSpec((blk, 2 * n), lambda i: (i, 0)),
        out_shape=jax.ShapeDtypeStruct((n, 2 * n), BF16),
        compiler_params=_cparams("parallel"),
        name=f"dft_matrix_{n}",
    )(jnp.asarray(_dft_tables(n)))


HY_COLS_F = 2 * HY_ORDER * GROUP
HY_OC = HY_ORDER * GROUP


def _hyena_filter_kernel(n, z_ref, w1_ref, b1_ref, w2_ref, b2_ref, w3_ref, freq_ref, dl_ref, hs_ref, hd_ref):
    blk = min(SEQ_BLK, n)
    freq = freq_ref[...]
    dl = dl_ref[...]

    def fill(i, norm):
        r0 = pl.multiple_of(i * blk, blk)
        z = z_ref[pl.ds(r0, blk), :]
        hid = jnp.sin(freq * (_dot_hi(z, w1_ref[...]) + b1_ref[...]))
        hid = jnp.sin(freq * (_dot_hi(hid, w2_ref[...]) + b2_ref[...]))
        t = z[:, 0:1]
        h = _dot_hi(hid, w3_ref[...]) * jnp.exp(-t * dl)
        lag = lax.broadcasted_iota(jnp.int32, (blk, 1), 0) + r0
        hf = h[:, 0:HY_OC]
        hb = jnp.where(lag == 0, 0.0, h[:, HY_OC:2 * HY_OC])
        hs_ref[pl.ds(r0, blk), :] = hf + hb
        hd_ref[pl.ds(r0, blk), :] = hf - hb
        return norm + jnp.sum(jnp.abs(hf) + jnp.abs(hb), axis=0, keepdims=True)

    norm = lax.fori_loop(0, n // blk, fill, jnp.zeros((1, HY_OC), F32))
    inv = 1.0 / norm

    def scale(i, c):
        rows = pl.ds(pl.multiple_of(i * blk, blk), blk)
        hs_ref[rows, :] = hs_ref[rows, :] * inv
        hd_ref[rows, :] = hd_ref[rows, :] * inv
        return c

    lax.fori_loop(0, n // blk, scale, 0)


def hyena_filter_taps(n, f_w1, f_b1, f_w2, f_b2, f_w3, f_freq):
    f32 = np.float32
    t = np.linspace(0.0, 1.0, n, dtype=f32)[:, None]
    ang = (f32(2.0 * math.pi) * np.arange(n, dtype=f32)[:, None] / f32(n)).astype(f32)
    bands = np.linspace(1e-4, HY_BANDS - 1, HY_BANDS, dtype=f32)[None]
    arg = (bands * ang).astype(f32)
    z = np.concatenate([t, np.cos(arg).astype(f32), -np.sin(arg).astype(f32)], axis=-1)
    emb = z.shape[1]
    z = jnp.asarray(np.pad(z, ((0, 0), (0, 128 - emb))))
    w1 = jnp.pad(f_w1, ((0, 128 - emb), (0, 0)))
    max_decay = math.log(HY_TARGET) / HY_SHORT_DECAY_PCT
    min_decay = math.log(HY_TARGET) / HY_LONG_DECAY_PCT
    deltas = np.abs(np.linspace(min_decay, max_decay, HY_OC, dtype=f32))
    dl = jnp.asarray(np.tile(deltas, 2).reshape(1, HY_COLS_F))
    hid = f_w2.shape[0]
    out = jax.ShapeDtypeStruct((n, HY_OC), F32)
    return pl.pallas_call(
        functools.partial(_hyena_filter_kernel, n),
        out_shape=[out, out],
        compiler_params=pltpu.CompilerParams(vmem_limit_bytes=VMEM_LIMIT),
        name=f"hyena_filter_{n}",
    )(z, w1, f_b1.reshape(1, hid), f_w2, f_b2.reshape(1, hid), f_w3, f_freq.reshape(1, hid), dl)


def _hyena_spectrum_kernel(n, g_ref, hs_ref, hd_ref, kr_ref, ki_ref, kn_ref):
    blk = min(2 * SEQ_BLK, n)
    big = 2.0 * n

    def split(ref):
        x = ref[...]
        hi = x.astype(BF16)
        return hi, (x - hi.astype(F32)).astype(BF16)

    s_hi, s_lo = split(hs_ref)
    d_hi, d_lo = split(hd_ref)

    def body(i, c):
        r0 = pl.multiple_of(i * blk, blk)
        rows = pl.ds(r0, blk)
        k = lax.broadcasted_iota(jnp.int32, (blk, 1), 0) + r0
        wgt = jnp.where(k == 0, 1.0 / big, 2.0 / big)
        gc = g_ref[rows, 0:n]
        gs = g_ref[rows, n:2 * n]
        kr_ref[rows, :] = (_dot(gc, s_hi) + _dot(gc, s_lo)) * wgt
        ki_ref[rows, :] = (_dot(gs, d_hi) + _dot(gs, d_lo)) * wgt
        return c

    lax.fori_loop(0, n // blk, body, 0)
    t = lax.broadcasted_iota(jnp.int32, (n, 1), 0)
    sign = jnp.where(t % 2 == 0, 1.0, -1.0)
    kn_ref[...] = jnp.broadcast_to(jnp.sum(sign * hs_ref[...], axis=0, keepdims=True) * (1.0 / big), (8, HY_OC))


def hyena_spectrum(n, g, hs, hd):
    out = jax.ShapeDtypeStruct((n, HY_OC), F32)
    return pl.pallas_call(
        functools.partial(_hyena_spectrum_kernel, n),
        out_shape=[out, out, jax.ShapeDtypeStruct((8, HY_OC), F32)],
        compiler_params=pltpu.CompilerParams(vmem_limit_bytes=VMEM_LIMIT),
        name=f"hyena_spectrum_{n}",
    )(g, hs, hd)


def _short_conv_into(dst_ref, slab_ref, w_ref, col):
    def body(i, c):
        rows = pl.ds(pl.multiple_of(i * SEQ_BLK, SEQ_BLK), SEQ_BLK)
        prev, cur, nxt = _prev_cur_next(slab_ref, i, col * GROUP, (col + 1) * GROUP)
        w = w_ref[:, col * GROUP:(col + 1) * GROUP]
        dst_ref[rows, :] = prev * w[0:1] + cur * w[1:2] + nxt * w[2:3]
        return c

    lax.fori_loop(0, N_BLK, body, 0)


HY_FBLK = 2048
HY_IBLK = 1024


def _alt_sign(n):
    t = lax.broadcasted_iota(jnp.int32, (n, 1), 0)
    return jnp.where(t % 2 == 0, 1.0, -1.0)


def _hyena_fwd_kernel(from_slab, *refs):
    if from_slab:
        slab_ref, w_ref = refs[0:2]
        refs = refs[2:]
        x_ref = refs[-1]
        _short_conv_into(x_ref, slab_ref, w_ref, 0)
    else:
        x_ref = refs[0]
        refs = refs[1:]
    gl_ref, gc_ref, krl_ref, kil_ref, knl_ref, krc_ref, kic_ref, knc_ref, pl_ref, pc_ref, pn_ref = refs[0:11]

    def transform(x, g_ref, kr_ref, ki_ref, kn_ref, p_ref, n, blk):
        xb = x.astype(BF16)

        def body(i, c):
            rows = pl.ds(pl.multiple_of(i * blk, blk), blk)
            zr = _dot(g_ref[rows, 0:n], xb)
            zi = _dot(g_ref[rows, n:2 * n], xb)
            kr = kr_ref[rows, :]
            ki = ki_ref[rows, :]
            p_ref[0, 0, rows, :] = (zr * kr - zi * ki).astype(BF16)
            p_ref[0, 1, rows, :] = (zr * ki + zi * kr).astype(BF16)
            return c

        lax.fori_loop(0, n // blk, body, 0)
        return jnp.sum(_alt_sign(n) * x, axis=0, keepdims=True) * kn_ref[0:1, :]

    nyq_c = transform(x_ref[0:CTX_LEN, :], gc_ref, krc_ref, kic_ref, knc_ref, pc_ref, CTX_LEN, CTX_LEN)
    nyq_l = transform(x_ref[CTX_LEN:T_ALL, :], gl_ref, krl_ref, kil_ref, knl_ref, pl_ref, SEQ, HY_FBLK)
    pn_ref[0] = jnp.concatenate([nyq_l, nyq_c, jnp.zeros((6, GROUP), F32)], axis=0)


def _resident(shape):
    return pl.BlockSpec(shape, lambda b: (0,) * len(shape), pipeline_mode=pl.Buffered(1))


def hyena_forward_transform(x, conv_w, g_l, g_c, spec_l, spec_c, order):
    n = x.shape[0]
    nb = n // T_ALL
    kcol = lambda shape: pl.BlockSpec(shape, lambda b: (0, order))
    if conv_w is None:
        data_specs = [pl.BlockSpec((T_ALL, GROUP), lambda b: (b, 0))]
        data, scratch = (x,), []
    else:
        data_specs = [pl.BlockSpec((T_ALL, 3 * GROUP), lambda b: (b, 0)), _resident((3, 3 * GROUP))]
        data, scratch = (x, conv_w), [pltpu.VMEM((T_ALL, GROUP), F32)]
    return pl.pallas_call(
        functools.partial(_hyena_fwd_kernel, conv_w is not None),
        grid=(nb,),
        in_specs=data_specs + [
            _resident((SEQ, 2 * SEQ)), _resident((CTX_LEN, 2 * CTX_LEN)),
            kcol((SEQ, GROUP)), kcol((SEQ, GROUP)), kcol((8, GROUP)),
            kcol((CTX_LEN, GROUP)), kcol((CTX_LEN, GROUP)), kcol((8, GROUP)),
        ],
        out_specs=[
            pl.BlockSpec((1, 2, SEQ, GROUP), lambda b: (b, 0, 0, 0)),
            pl.BlockSpec((1, 2, CTX_LEN, GROUP), lambda b: (b, 0, 0, 0)),
            pl.BlockSpec((1, 8, GROUP), lambda b: (b, 0, 0)),
        ],
        out_shape=[
            jax.ShapeDtypeStruct((nb, 2, SEQ, GROUP), BF16),
            jax.ShapeDtypeStruct((nb, 2, CTX_LEN, GROUP), BF16),
            jax.ShapeDtypeStruct((nb, 8, GROUP), F32),
        ],
        scratch_shapes=scratch,
        compiler_params=_cparams("parallel"),
        name=f"hyena_fwd_{order}",
    )(*data, g_l, g_c, *spec_l, *spec_c)


def _hyena_inv_kernel(gate_col, u_given, pl_ref, pc_ref, pn_ref, gl_ref, gc_ref, slab_ref, w_ref, *refs):
    if u_given:
        u_ref, bias_ref, o_ref, gate_s = refs
    else:
        bias_ref, o_ref, gate_s, u_ref = refs
        _short_conv_into(u_ref, slab_ref, w_ref, 0)
    _short_conv_into(gate_s, slab_ref, w_ref, gate_col)
    bias = bias_ref[0]

    def inverse(p_ref, nyq, g_ref, n, blk, off):
        pr = p_ref[0, 0]
        pi = p_ref[0, 1]

        def body(i, c):
            r0 = pl.multiple_of(i * blk, blk)
            rows = pl.ds(r0, blk)
            orow = pl.ds(pl.multiple_of(off + r0, math.gcd(blk, CTX_LEN)), blk)
            t = lax.broadcasted_iota(jnp.int32, (blk, 1), 0)
            sign = jnp.where(t % 2 == 0, 1.0, -1.0)
            y = _dot(g_ref[rows, 0:n], pr) + _dot(g_ref[rows, n:2 * n], pi) + sign * nyq
            o_ref[orow, :] = gate_s[orow, :] * (y + u_ref[orow, :] * bias)
            return c

        lax.fori_loop(0, n // blk, body, 0)

    inverse(pc_ref, pn_ref[0, 1:2, :], gc_ref, CTX_LEN, CTX_LEN, 0)
    inverse(pl_ref, pn_ref[0, 0:1, :], gl_ref, SEQ, HY_IBLK, CTX_LEN)


def hyena_inverse_transform(p_l, p_c, p_n, g_l, g_c, slab, conv_w, gate_col, u, bias):
    nb = p_l.shape[0]
    seq = pltpu.VMEM((T_ALL, GROUP), F32)
    u_specs = [] if u is None else [pl.BlockSpec((T_ALL, GROUP), lambda b: (b, 0))]
    u_data = () if u is None else (u,)
    return pl.pallas_call(
        functools.partial(_hyena_inv_kernel, gate_col, u is not None),
        grid=(nb,),
        in_specs=[
            pl.BlockSpec((1, 2, SEQ, GROUP), lambda b: (b, 0, 0, 0)),
            pl.BlockSpec((1, 2, CTX_LEN, GROUP), lambda b: (b, 0, 0, 0)),
            pl.BlockSpec((1, 8, GROUP), lambda b: (b, 0, 0)),
            _resident((SEQ, 2 * SEQ)), _resident((CTX_LEN, 2 * CTX_LEN)),
            pl.BlockSpec((T_ALL, 3 * GROUP), lambda b: (b, 0)), _resident((3, 3 * GROUP)),
        ] + u_specs + [pl.BlockSpec((1, 1, GROUP), lambda b: (0, 0, 0))],
        out_specs=pl.BlockSpec((T_ALL, GROUP), lambda b: (b, 0)),
        out_shape=jax.ShapeDtypeStruct((nb * T_ALL, GROUP), F32),
        scratch_shapes=[seq] if u is not None else [seq, seq],
        compiler_params=_cparams("parallel"),
        name="hyena_inv",
    )(p_l, p_c, p_n, g_l, g_c, slab, conv_w, *u_data, bias.reshape(1, 1, GROUP))


def hyena_mixer(slab, g_l, g_c, conv_w, f_w1, f_b1, f_w2, f_b2, f_w3, f_freq, bias):
    spec_l = hyena_spectrum(SEQ, g_l, *hyena_filter_taps(SEQ, f_w1, f_b1, f_w2, f_b2, f_w3, f_freq))
    spec_c = hyena_spectrum(CTX_LEN, g_c, *hyena_filter_taps(CTX_LEN, f_w1, f_b1, f_w2, f_b2, f_w3, f_freq))
    p = hyena_forward_transform(slab, conv_w, g_l, g_c, spec_l, spec_c, 0)
    z = hyena_inverse_transform(*p, g_l, g_c, slab, conv_w, 1, None, bias[0])
    p = hyena_forward_transform(z, None, g_l, g_c, spec_l, spec_c, 1)
    return hyena_inverse_transform(*p, g_l, g_c, slab, conv_w, 2, z, bias[1])


def kernel(x, c, ctx, c_ctx, w_mod, b_mod, norm_w, ffn_w_gu, ffn_w_down, w_in, w_out,
           hy_conv, hy_f_w1, hy_f_b1, hy_f_w2, hy_f_b2, hy_f_w3, hy_f_freq, hy_bias,
           na_q_norm, na_k_norm, na_rpb, dn_conv, dn_a_log, dn_dt_bias, dn_norm,
           rw_mu, rw_w0, rw_w_up, rw_a0, rw_a_up, rw_g_up, rw_k_k, rw_k_a, rw_r_k, rw_ln_w, rw_ln_b):
    nb = x.shape[0]
    assert x.shape[1:] == (SEQ, D_MODEL) and ctx.shape[1:] == (CTX_LEN, D_MODEL) and nb + 1 <= 16
    s = None
    cond = jnp.concatenate([c_ctx[None], c, jnp.zeros((15 - nb, D_MODEL), F32)], axis=0)
    mod = modulation_all(cond, w_mod, b_mod).reshape(DEPTH, 16, N_MOD, D_MODEL)
    g_l = dft_matrix(SEQ)
    g_c = dft_matrix(CTX_LEN)
    w_gu = ffn_w_gu.astype(BF16)
    w_down = ffn_w_down.astype(BF16)
    w_out_b = w_out.astype(BF16)
    dn_end = 6 * GROUP + 4 * GROUP + 4 * GROUP_HEADS
    w_in_p = jnp.concatenate(
        [w_in[:, :, :dn_end], jnp.zeros((DEPTH, D_MODEL, 6 * GROUP + DN_W - dn_end), F32), w_in[:, :, dn_end:]],
        axis=2).astype(BF16)
    for l in range(DEPTH):
        need_ctx = l < DEPTH - 1
        modc = mod[l, 0:1]
        modb = mod[l, 1:1 + nb]
        s = ffn_half_step(s, modc, modb, norm_w[l, 0], w_gu, w_down, l, 0, 0, parts=(x, ctx) if l == 0 else None)
        hy_s, na_s, dn_s, rw_s = input_projection(s, modc, modb, norm_w[l, 1], w_in_p, l)
        hy_g = hyena_mixer(hy_s, g_l, g_c, hy_conv[l], hy_f_w1[l], hy_f_b1[l], hy_f_w2[l], hy_f_b2[l], hy_f_w3[l],
                           hy_f_freq[l], hy_bias[l])
        na_g = na_mixer(na_s, na_q_norm[l], na_k_norm[l], na_bias_table(na_rpb[l]), need_ctx)
        dn_parts = deltanet_mixer(dn_s, dn_conv[l], dn_a_log[l], dn_dt_bias[l]) + (dn_s, dn_norm[l])
        rw_parts = rwkv_mixer(rw_s, rw_mu[l], rw_w0[l], rw_w_up[l], rw_a0[l], rw_a_up[l], rw_g_up[l], rw_k_k[l],
                              rw_k_a[l], rw_r_k[l], rw_ln_w[l], rw_ln_b[l])
        s = mixer_output_ffn(s, modc, modb, norm_w[l, 2], hy_g, na_g, dn_parts, rw_parts, w_out_b, w_gu, w_down, l,
                             latent_only=not need_ctx)
    return s.reshape(nb, SEQ, D_MODEL)
```
